```python
import math
import jax, jax.numpy as jnp
from jax import lax
import numpy as np

D_MODEL = 2048
BATCH = 4
SEQ = 4096
DEPTH = 1

HEAD_DIM = 64
DILATION_PATTERNS = ((128, 1), (512, 4), (2048, 16))
N_ATT_GROUPS = len(DILATION_PATTERNS)
HEADS_PER_GROUP = 6
N_Q_HEADS = N_ATT_GROUPS * HEADS_PER_GROUP
KV_HEADS = HEADS_PER_GROUP
ATT_Q_WIDTH = N_Q_HEADS * HEAD_DIM
KV_WIDTH = KV_HEADS * HEAD_DIM
ROT_DIM = HEAD_DIM // 4
ROPE_THETA = 500000.0
BLK = 128
SSM_WIDTH = D_MODEL - ATT_Q_WIDTH
SSM_GROUP_CH = 16
SSM_GROUPS = SSM_WIDTH // SSM_GROUP_CH
SSM_STATE = 64
IN_WIDTH = ATT_Q_WIDTH + 2 * KV_WIDTH + SSM_WIDTH
OUT_IN_WIDTH = KV_WIDTH + SSM_WIDTH
D_FF = 4 * D_MODEL
N_MOD = 6
EPS = 1e-6

kernel_name = "hymba_s5_longnet_sandwich_adaln_block"


def rms_norm(x, g):
    xf = x.astype(jnp.float32)
    y = xf * lax.rsqrt(jnp.mean(xf * xf, axis=-1, keepdims=True) + EPS)
    return (y * g.astype(jnp.float32)).astype(x.dtype)


def rope_partial(t, positions):
    freqs = ROPE_THETA ** (-jnp.arange(0, ROT_DIM, 2, dtype=jnp.float32) / ROT_DIM)
    ang = positions.astype(jnp.float32)[..., None] * freqs
    cos = jnp.cos(ang)[:, :, None, :]
    sin = jnp.sin(ang)[:, :, None, :]
    tr = t[..., :ROT_DIM].astype(jnp.float32)
    x1, x2 = tr[..., :ROT_DIM // 2], tr[..., ROT_DIM // 2:]
    rot = jnp.concatenate([x1 * cos - x2 * sin, x2 * cos + x1 * sin], axis=-1).astype(t.dtype)
    return jnp.concatenate([rot, t[..., ROT_DIM:]], axis=-1)


def dilated_window_attention(q, k, v, dilation, span):
    assert span <= BLK
    B, L, H, Dh = q.shape
    n = L // dilation

    def to_sub(t):
        return t.reshape(B, n, dilation, H, Dh).transpose(0, 2, 3, 1, 4)

    qs, ks, vs = to_sub(q), to_sub(k), to_sub(v)
    n_pad = -(-n // BLK) * BLK
    pad = ((0, 0), (0, 0), (0, 0), (0, n_pad - n), (0, 0))
    qs, ks, vs = (jnp.pad(t, pad) for t in (qs, ks, vs))
    nb = n_pad // BLK
    qb = qs.reshape(B, dilation, H, nb, BLK, Dh)
    kb = ks.reshape(B, dilation, H, nb, BLK, Dh)
    vb = vs.reshape(B, dilation, H, nb, BLK, Dh)

    def with_prev(t):
        prev = jnp.concatenate([jnp.zeros_like(t[:, :, :, :1]), t[:, :, :, :-1]], axis=3)
        return jnp.concatenate([prev, t], axis=4)

    kc, vc = with_prev(kb), with_prev(vb)
    scores = jnp.einsum('brhnqe,brhnke->brhnqk', qb, kc).astype(jnp.float32) / math.sqrt(Dh)
    blk = jnp.arange(nb)[:, None, None]
    qi = jnp.arange(BLK)[None, :, None]
    kj = jnp.arange(2 * BLK)[None, None, :]
    dist = qi + BLK - kj
    valid = (dist >= 0) & (dist <= span) & (blk * BLK - BLK + kj >= 0)
    scores = jnp.where(valid, scores, -jnp.inf)
    lse = jax.nn.logsumexp(scores, axis=-1)
    p = jnp.exp(scores - lse[..., None]).astype(v.dtype)
    out = jnp.einsum('brhnqk,brhnke->brhnqe', p, vc)
    out = out.reshape(B, dilation, H, n_pad, Dh)[:, :, :, :n]
    out = out.transpose(0, 3, 1, 2, 4).reshape(B, L, H, Dh)
    lse = lse.reshape(B, dilation, H, n_pad)[:, :, :, :n].transpose(0, 3, 1, 2).reshape(B, L, H)
    return out, lse


def _scan_op(e1, e2):
    a1, b1 = e1
    a2, b2 = e2
    return a1 * a2, a2 * b1 + b2


def s5_mixer(u, a_re, a_im, log_dt, b_re, b_im, c_re, c_im, d_skip, w_glu, b_glu):
    B, L, G, P = u.shape
    u32 = u.astype(jnp.float32)
    lam = lax.complex(a_re.astype(jnp.float32), a_im.astype(jnp.float32))
    dt = jnp.exp(log_dt.astype(jnp.float32))[:, None]
    a_bar = jnp.exp(lam * dt)
    b_mat = lax.complex(b_re.astype(jnp.float32), b_im.astype(jnp.float32))
    b_bar = ((a_bar - 1.0) / lam)[..., None] * b_mat
    c_mat = lax.complex(c_re.astype(jnp.float32), c_im.astype(jnp.float32))
    bu = jnp.einsum('blgp,gnp->blgn', u32.astype(jnp.complex64), b_bar)
    a_all = jnp.broadcast_to(a_bar, bu.shape)
    _, state = lax.associative_scan(_scan_op, (a_all, bu), axis=1)
    y = jnp.einsum('blgn,gpn->blgp', state, c_mat).real + d_skip.astype(jnp.float32) * u32
    y = y.reshape(B, L, G * P)
    y = jax.nn.gelu(y)
    y = y * jax.nn.sigmoid(y @ w_glu.astype(jnp.float32) + b_glu.astype(jnp.float32))
    return y.astype(u.dtype)


def setup_inputs(seed: int = 0) -> dict:
    key = jax.random.key(seed)
    ks = jax.random.split(key, 32)
    f32 = jnp.float32
    nrm = lambda k, shape, s: jax.random.normal(k, shape, f32) * s
    x = jax.random.normal(ks[0], (BATCH, SEQ, D_MODEL), f32)
    c = jax.random.normal(ks[1], (BATCH, D_MODEL), f32)
    offset = jax.random.randint(ks[2], (BATCH, 1), 0, 1024, dtype=jnp.int32)
    positions = (offset + jnp.arange(SEQ, dtype=jnp.int32)[None, :]).astype(jnp.int32)
    n_idx = jnp.arange(SSM_STATE, dtype=f32)
    return {
        "x": x,
        "c": c,
        "positions": positions,
        "w_ada": nrm(ks[3], (DEPTH, D_MODEL, N_MOD * D_MODEL), 0.5 * D_MODEL ** -0.5),
        "b_ada": nrm(ks[4], (DEPTH, N_MOD * D_MODEL), 0.01),
        "g_pre_mix": 1.0 + nrm(ks[5], (DEPTH, D_MODEL), 0.02),
        "g_post_mix": 1.0 + nrm(ks[6], (DEPTH, D_MODEL), 0.02),
        "w_in": nrm(ks[7], (DEPTH, D_MODEL, IN_WIDTH), D_MODEL ** -0.5),
        "ssm_a_re": -0.5 + nrm(ks[8], (DEPTH, SSM_GROUPS, SSM_STATE), 0.01),
        "ssm_a_im": math.pi * n_idx + nrm(ks[9], (DEPTH, SSM_GROUPS, SSM_STATE), 0.01),
        "ssm_log_dt": jax.random.uniform(ks[10], (DEPTH, SSM_GROUPS), f32, math.log(1e-3), math.log(1e-1)),
        "ssm_b_re": nrm(ks[11], (DEPTH, SSM_GROUPS, SSM_STATE, SSM_GROUP_CH), (2 * SSM_GROUP_CH) ** -0.5),
        "ssm_b_im": nrm(ks[12], (DEPTH, SSM_GROUPS, SSM_STATE, SSM_GROUP_CH), (2 * SSM_GROUP_CH) ** -0.5),
        "ssm_c_re": nrm(ks[13], (DEPTH, SSM_GROUPS, SSM_GROUP_CH, SSM_STATE), (2 * SSM_STATE) ** -0.5),
        "ssm_c_im": nrm(ks[14], (DEPTH, SSM_GROUPS, SSM_GROUP_CH, SSM_STATE), (2 * SSM_STATE) ** -0.5),
        "ssm_d": nrm(ks[15], (DEPTH, SSM_GROUPS, SSM_GROUP_CH), 1.0),
        "w_glu": nrm(ks[16], (DEPTH, SSM_WIDTH, SSM_WIDTH), SSM_WIDTH ** -0.5),
        "b_glu": nrm(ks[17], (DEPTH, SSM_WIDTH), 0.01),
        "g_attn_out": 1.0 + nrm(ks[18], (DEPTH, KV_WIDTH), 0.02),
        "g_ssm_out": 1.0 + nrm(ks[19], (DEPTH, SSM_WIDTH), 0.02),
        "w_out": nrm(ks[20], (DEPTH, OUT_IN_WIDTH, D_MODEL), OUT_IN_WIDTH ** -0.5),
        "g_pre_mlp": 1.0 + nrm(ks[21], (DEPTH, D_MODEL), 0.02),
        "g_post_mlp": 1.0 + nrm(ks[22], (DEPTH, D_MODEL), 0.02),
        "w_mlp_in": nrm(ks[23], (DEPTH, D_MODEL, D_FF), D_MODEL ** -0.5),
        "w_mlp_out": nrm(ks[24], (DEPTH, D_FF, D_MODEL), D_FF ** -0.5),
    }


def reference(x, c, positions, w_ada, b_ada, g_pre_mix, g_post_mix, w_in,
              ssm_a_re, ssm_a_im, ssm_log_dt, ssm_b_re, ssm_b_im, ssm_c_re, ssm_c_im,
              ssm_d, w_glu, b_glu, g_attn_out, g_ssm_out, w_out,
              g_pre_mlp, g_post_mlp, w_mlp_in, w_mlp_out):
    B, L, _ = x.shape
    for l in range(DEPTH):
        mod = jax.nn.silu(c) @ w_ada[l] + b_ada[l]
        sh1, sc1, gt1, sh2, sc2, gt2 = (m[:, None, :] for m in jnp.split(mod, N_MOD, axis=-1))

        h = rms_norm(x, g_pre_mix[l]) * (1.0 + sc1) + sh1
        proj = h @ w_in[l]
        q = proj[..., :ATT_Q_WIDTH].reshape(B, L, N_Q_HEADS, HEAD_DIM)
        k = proj[..., ATT_Q_WIDTH:ATT_Q_WIDTH + KV_WIDTH].reshape(B, L, KV_HEADS, HEAD_DIM)
        v = proj[..., ATT_Q_WIDTH + KV_WIDTH:ATT_Q_WIDTH + 2 * KV_WIDTH].reshape(B, L, KV_HEADS, HEAD_DIM)
        u = proj[..., ATT_Q_WIDTH + 2 * KV_WIDTH:].reshape(B, L, SSM_GROUPS, SSM_GROUP_CH)

        q = rope_partial(q, positions).reshape(B, L, N_ATT_GROUPS, HEADS_PER_GROUP, HEAD_DIM)
        k = rope_partial(k, positions)
        outs, lses = [], []
        for gi, (window, dilation) in enumerate(DILATION_PATTERNS):
            o_g, lse_g = dilated_window_attention(q[:, :, gi], k, v, dilation, window // dilation)
            outs.append(o_g)
            lses.append(lse_g)
        wts = jax.nn.softmax(jnp.stack(lses, axis=0), axis=0)
        att = jnp.sum(wts[..., None].astype(x.dtype) * jnp.stack(outs, axis=0), axis=0)
        att = rms_norm(att.reshape(B, L, KV_WIDTH), g_attn_out[l])

        ssm = s5_mixer(u, ssm_a_re[l], ssm_a_im[l], ssm_log_dt[l], ssm_b_re[l], ssm_b_im[l],
                       ssm_c_re[l], ssm_c_im[l], ssm_d[l], w_glu[l], b_glu[l])
        ssm = rms_norm(ssm, g_ssm_out[l])

        mix = jnp.concatenate([att, ssm], axis=-1) @ w_out[l]
        x = x + gt1 * rms_norm(mix, g_post_mix[l])

        h = rms_norm(x, g_pre_mlp[l]) * (1.0 + sc2) + sh2
        y = jnp.square(jax.nn.relu(h @ w_mlp_in[l])) @ w_mlp_out[l]
        x = x + gt2 * rms_norm(y, g_post_mlp[l])
    return x
```

```python
import functools
import math

import jax
import jax.numpy as jnp
from jax import lax
from jax.experimental import pallas as pl
from jax.experimental.pallas import tpu as pltpu

F32 = jnp.float32
BF16 = jnp.bfloat16

LANES = 128
SUBLANES = 8

HEAD_DIM = 64
ROT_DIM = HEAD_DIM // 4
ROPE_THETA = 500000.0
DILATIONS = (1, 4, 16)
HEADS_PER_GROUP = 6
GROUP_WIDTH = HEADS_PER_GROUP * HEAD_DIM
ATT_BLK = 128
SSM_P = 16
SSM_N = 64
SSM_CHUNK = 16
CHUNK_W = SSM_CHUNK * SSM_P
N_MOD = 6
EPS = 1e-6
NEG = -1e30


def _cparams(sem, vmem_mb):
    return pltpu.CompilerParams(dimension_semantics=sem, vmem_limit_bytes=vmem_mb * 1024 * 1024)


def _resident(shape):
    nd = len(shape)
    return pl.BlockSpec(shape, lambda *_: (0,) * nd, pipeline_mode=pl.Buffered(1))


def _adaln_kernel(s_ref, w_ref, b_ref, o_ref):
    o_ref[...] = jnp.dot(s_ref[...], w_ref[...].astype(BF16),
                         preferred_element_type=F32) + b_ref[...]


def _adaln(c, w_ada, b_ada):
    bsz, d = c.shape
    n = w_ada.shape[1]
    tn = 1024
    s = jnp.zeros((SUBLANES, d), BF16).at[:bsz].set(jax.nn.silu(c).astype(BF16))
    out = pl.pallas_call(
        _adaln_kernel,
        grid=(n // tn,),
        in_specs=[pl.BlockSpec((SUBLANES, d), lambda j: (0, 0)),
                  pl.BlockSpec((d, tn), lambda j: (0, j)),
                  pl.BlockSpec((1, tn), lambda j: (0, j))],
        out_specs=pl.BlockSpec((SUBLANES, tn), lambda j: (0, j)),
        out_shape=jax.ShapeDtypeStruct((SUBLANES, n), F32),
        compiler_params=_cparams(("arbitrary",), 40),
        name="adaln",
    )(s, w_ada, b_ada.reshape(1, n))
    return out[:bsz].reshape(bsz, N_MOD, d)


def _inproj_kernel(x_ref, mod_ref, g_ref, w_ref, ct_ref, st_ref,
                   q1_ref, q2_ref, q3_ref, k_ref, v_ref, u_ref, *, qk_width):
    x = x_ref[...]
    ms = jnp.mean(x * x, axis=-1, keepdims=True)
    scale = g_ref[...] * (1.0 + mod_ref[1:2, :])
    h = (x * lax.rsqrt(ms + EPS)) * scale + mod_ref[0:1, :]
    hb = h.astype(BF16)

    pqk = jnp.dot(hb, w_ref[:, :qk_width], preferred_element_type=F32)
    ct = ct_ref[...]
    st = st_ref[...]
    lane = lax.broadcasted_iota(jnp.int32, ct.shape, 1) % HEAD_DIM
    second_half = (lane >= ROT_DIM // 2) & (lane < ROT_DIM)
    q_blocks = 3 * GROUP_WIDTH // LANES
    outs = (q1_ref, q2_ref, q3_ref, k_ref)
    per_out = GROUP_WIDTH // LANES
    for j in range(qk_width // LANES):
        t = pqk[:, j * LANES:(j + 1) * LANES]
        partner = jnp.where(second_half,
                            pltpu.roll(t, ROT_DIM // 2, axis=1),
                            pltpu.roll(t, LANES - ROT_DIM // 2, axis=1))
        r = t * ct + partner * st
        if j < q_blocks:
            r = r * (1.0 / math.sqrt(HEAD_DIM))
        dst = outs[j // per_out]
        jj = j % per_out
        dst[:, jj * LANES:(jj + 1) * LANES] = r.astype(BF16)

    pvu = jnp.dot(hb, w_ref[:, qk_width:], preferred_element_type=F32)
    v_ref[...] = pvu[:, :GROUP_WIDTH].astype(BF16)
    u_ref[...] = pvu[:, GROUP_WIDTH:].astype(BF16)


def _inproj(x2, mod, g_pre, w_in_b, ctab, stab, seq):
    t_tok, d = x2.shape
    in_w = w_in_b.shape[1]
    qk_width = 4 * GROUP_WIDTH
    ssm_w = in_w - qk_width - GROUP_WIDTH
    tm = 512
    per_b = seq // tm
    tok = lambda w: pl.BlockSpec((tm, w), lambda i: (i, 0))
    out_shape = [jax.ShapeDtypeStruct((t_tok, GROUP_WIDTH), BF16)] * 5 + \
                [jax.ShapeDtypeStruct((t_tok, ssm_w), BF16)]
    return pl.pallas_call(
        functools.partial(_inproj_kernel, qk_width=qk_width),
        grid=(t_tok // tm,),
        in_specs=[tok(d),
                  pl.BlockSpec((None, N_MOD, d), lambda i: (i // per_b, 0, 0)),
                  _resident((1, d)),
                  _resident((d, in_w)),
                  tok(LANES), tok(LANES)],
        out_specs=[tok(GROUP_WIDTH)] * 5 + [tok(ssm_w)],
        out_shape=out_shape,
        compiler_params=_cparams(("parallel",), 52),
        name="inproj",
    )(x2, mod, g_pre, w_in_b, ctab, stab)


def _attn_kernel(*refs, n, first, last):
    if first:
        q_ref, k_ref, v_ref, o_ref, l_ref = refs
        op_ref = lp_ref = None
    elif last:
        q_ref, k_ref, v_ref, op_ref, lp_ref, o_ref = refs
        l_ref = None
    else:
        q_ref, k_ref, v_ref, op_ref, lp_ref, o_ref, l_ref = refs

    qi = lax.broadcasted_iota(jnp.int32, (ATT_BLK, 2 * ATT_BLK), 0)
    kj = lax.broadcasted_iota(jnp.int32, (ATT_BLK, 2 * ATT_BLK), 1)
    rel = qi - kj
    lane = lax.broadcasted_iota(jnp.int32, (ATT_BLK, LANES), 1)

    def body(i, carry):
        qs = pl.multiple_of(i * ATT_BLK, ATT_BLK)
        ks = pl.multiple_of(jnp.maximum(i - 1, 0) * ATT_BLK, ATT_BLK)
        dist = rel + (qs - ks)
        valid = (dist >= 0) & (dist <= ATT_BLK)
        q = q_ref[pl.ds(qs, ATT_BLK), :]
        kw = k_ref[pl.ds(ks, 2 * ATT_BLK), :]
        vw = v_ref[pl.ds(ks, 2 * ATT_BLK), :]
        if not first:
            o_prev = op_ref[pl.ds(qs, ATT_BLK), :].astype(F32)
            l_prev = lp_ref[pl.ds(qs, ATT_BLK), :]
        o_parts = []
        lse_tile = jnp.zeros((ATT_BLK, LANES), F32)
        for h in range(HEADS_PER_GROUP):
            sl = slice(h * HEAD_DIM, (h + 1) * HEAD_DIM)
            s = lax.dot_general(q[:, sl], kw[:, sl], (((1,), (1,)), ((), ())),
                                preferred_element_type=F32)
            s = jnp.where(valid, s, NEG)
            m = jnp.max(s, axis=1, keepdims=True)
            p = jnp.exp(s - m)
            l = jnp.sum(p, axis=1, keepdims=True)
            o = jnp.dot(p.astype(BF16), vw[:, sl], preferred_element_type=F32)
            o = o * (1.0 / l)
            lse = m + jnp.log(l)
            if not first:
                lp = l_prev[:, h:h + 1]
                mx = jnp.maximum(lp, lse)
                wp = jnp.exp(lp - mx)
                wc = jnp.exp(lse - mx)
                tot = wp + wc
                inv = 1.0 / tot
                o = o_prev[:, sl] * (wp * inv) + o * (wc * inv)
                lse = mx + jnp.log(tot)
            o_parts.append(o)
            if not last:
                lse_tile = jnp.where(lane == h, lse, lse_tile)
        o_ref[pl.ds(qs, ATT_BLK), :] = jnp.concatenate(o_parts, axis=1).astype(o_ref.dtype)
        if not last:
            l_ref[pl.ds(qs, ATT_BLK), :] = lse_tile
        return carry

    lax.fori_loop(0, n // ATT_BLK, body, 0)


def _attn_group(q, k, v, o_prev, l_prev, dilation, bsz, seq):
    first = o_prev is None
    last = dilation == DILATIONS[-1]
    n = seq // dilation
    view = lambda a, w: a.reshape(bsz, n, dilation * w)
    spec = lambda w: pl.BlockSpec((None, n, w), lambda b, r: (b, 0, r))
    ins = [view(q, GROUP_WIDTH), view(k, GROUP_WIDTH), view(v, GROUP_WIDTH)]
    in_specs = [spec(GROUP_WIDTH)] * 3
    if not first:
        ins += [view(o_prev, GROUP_WIDTH), view(l_prev, LANES)]
        in_specs += [spec(GROUP_WIDTH), spec(LANES)]
    out_shape = [jax.ShapeDtypeStruct((bsz, n, dilation * GROUP_WIDTH), BF16)]
    out_specs = [spec(GROUP_WIDTH)]
    if not last:
        out_shape.append(jax.ShapeDtypeStruct((bsz, n, dilation * LANES), F32))
        out_specs.append(spec(LANES))
    res = pl.pallas_call(
        functools.partial(_attn_kernel, n=n, first=first, last=last),
        grid=(bsz, dilation),
        in_specs=in_specs,
        out_specs=out_specs,
        out_shape=out_shape,
        compiler_params=_cparams(("parallel", "parallel"), 48),
        name=f"attn_d{dilation}",
    )(*ins)
    o = res[0].reshape(bsz * seq, GROUP_WIDTH)
    l = None if last else res[1].reshape(bsz * seq, LANES)
    return o, l


def _ssm_tables(a_re, a_im, log_dt, b_re, b_im, c_re, c_im, d_skip):
    hp = lax.Precision.HIGHEST
    lam = lax.complex(a_re.astype(F32), a_im.astype(F32))
    dt = jnp.exp(log_dt.astype(F32))[:, None]
    a_bar = jnp.exp(lam * dt)
    b_bar = ((a_bar - 1.0) / lam)[..., None] * lax.complex(b_re.astype(F32), b_im.astype(F32))
    c_mat = lax.complex(c_re.astype(F32), c_im.astype(F32))
    g = a_re.shape[0]
    taus = jnp.arange(2 * SSM_CHUNK + 1, dtype=F32)
    apow = jnp.exp((lam * dt)[:, None, :] * taus[None, :, None])

    ca = c_mat[:, None, :, :] * apow[:, :SSM_CHUNK, None, :]
    kern = jnp.einsum('gtpn,gnq->gtpq', ca, b_bar, precision=hp).real
    kern = kern.at[:, 0].add(d_skip.astype(F32)[:, :, None] * jnp.eye(SSM_P, dtype=F32))
    s_idx = jnp.arange(SSM_CHUNK)[:, None]
    t_idx = jnp.arange(SSM_CHUNK)[None, :]
    tau = t_idx - s_idx
    m5 = kern[:, jnp.clip(tau, 0, SSM_CHUNK - 1)]
    m5 = jnp.where((tau >= 0)[None, :, :, None, None], m5, 0.0)
    m_mat = m5.transpose(0, 1, 4, 2, 3).reshape(g, CHUNK_W, CHUNK_W)

    w4 = apow[:, SSM_CHUNK - 1 - jnp.arange(SSM_CHUNK), :][:, :, None, :] * \
        b_bar.transpose(0, 2, 1)[:, None, :, :]
    w4 = w4.reshape(g, CHUNK_W, SSM_N)
    w_mat = jnp.concatenate([w4.real, w4.imag, w4.imag, w4.real], axis=-1)

    v4 = c_mat.transpose(0, 2, 1)[:, :, None, :] * \
        apow[:, 1:SSM_CHUNK + 1, :].transpose(0, 2, 1)[:, :, :, None]
    v4 = v4.reshape(g, SSM_N, CHUNK_W)
    v_mat = jnp.concatenate([v4.real, -v4.imag], axis=1)

    def mul_rows(al):
        re, im = al.real, al.imag
        return [jnp.concatenate([re, re], -1), jnp.concatenate([-im, im], -1),
                jnp.concatenate([im, -im], -1)]
    ptab = jnp.stack(mul_rows(apow[:, SSM_CHUNK]) + mul_rows(apow[:, 2 * SSM_CHUNK]) +
                     [jnp.zeros((g, LANES), F32)] * 2, axis=1)
    return m_mat.astype(BF16), w_mat.astype(BF16), v_mat.astype(BF16), ptab.astype(F32)


def _ssm_kernel(u_ref, m_ref, w_ref, v_ref, p_ref, y_ref, gx_ref, gz_ref, h_ref, *, groups, rows, bsz):
    row = lax.broadcasted_iota(jnp.int32, (rows, LANES), 0)
    head = row < bsz

    def shift(a):
        return jnp.where(head, 0.0, pltpu.roll(a, bsz, axis=0))

    for g in range(groups):
        e = jnp.dot(u_ref[g], w_ref[g], preferred_element_type=F32)
        ex, ez = e[:, :LANES], e[:, LANES:]
        exs, ezs = shift(ex), shift(ez)
        p1, p2, p3 = p_ref[g, 0:1, :], p_ref[g, 1:2, :], p_ref[g, 2:3, :]
        gx_ref[g] = ex + p1 * exs + p2 * ezs
        gz_ref[g] = ez + p1 * ezs + p3 * exs

    def step(j, carry):
        r0 = pl.multiple_of(j * SUBLANES, SUBLANES)
        new = []
        for g in range(groups):
            x, z = carry[g]
            q1, q2, q3 = p_ref[g, 3:4, :], p_ref[g, 4:5, :], p_ref[g, 5:6, :]
            xn = q1 * x + q2 * z + gx_ref[g, pl.ds(r0, SUBLANES), :]
            zn = q1 * z + q3 * x + gz_ref[g, pl.ds(r0, SUBLANES), :]
            h_ref[g, pl.ds(r0, SUBLANES), :] = xn
            new.append((xn, zn))
        return tuple(new)

    zero = jnp.zeros((SUBLANES, LANES), F32)
    lax.fori_loop(0, rows // SUBLANES, step, tuple((zero, zero) for _ in range(groups)))

    for g in range(groups):
        h_in = shift(h_ref[g]).astype(BF16)
        y = jnp.dot(u_ref[g], m_ref[g], preferred_element_type=F32)
        y = y + jnp.dot(h_in, v_ref[g], preferred_element_type=F32)
        y_ref[g] = y.astype(y_ref.dtype)


def _ssm(u, tables, bsz, seq):
    m_mat, w_mat, v_mat, ptab = tables
    g = m_mat.shape[0]
    nchunk = seq // SSM_CHUNK
    rows = nchunk * bsz
    assert SUBLANES % bsz == 0 and rows % SUBLANES == 0
    uc = u.reshape(bsz, nchunk, SSM_CHUNK, g, SSM_P).transpose(3, 1, 0, 2, 4).reshape(g, rows, CHUNK_W)
    gs = 4
    blk = lambda a, b: pl.BlockSpec((gs, a, b), lambda i: (i, 0, 0))
    y = pl.pallas_call(
        functools.partial(_ssm_kernel, groups=gs, rows=rows, bsz=bsz),
        grid=(g // gs,),
        in_specs=[blk(rows, CHUNK_W), blk(CHUNK_W, CHUNK_W), blk(CHUNK_W, CHUNK_W),
                  blk(LANES, CHUNK_W), blk(SUBLANES, LANES)],
        out_specs=blk(rows, CHUNK_W),
        out_shape=jax.ShapeDtypeStruct((g, rows, CHUNK_W), BF16),
        scratch_shapes=[pltpu.VMEM((gs, rows, LANES), F32)] * 3,
        compiler_params=_cparams(("parallel",), 48),
        name="ssm",
    )(uc, m_mat, w_mat, v_mat, ptab)
    y = y.reshape(g, nchunk, bsz, SSM_CHUNK, SSM_P).transpose(2, 1, 3, 0, 4)
    return y.reshape(bsz * seq, g * SSM_P)


def _rms(x):
    return x * lax.rsqrt(jnp.mean(x * x, axis=-1, keepdims=True) + EPS)


def _mix_kernel(att_ref, y_ref, x_ref, mod_ref, ga_ref, gs_ref, gp_ref, wg_ref, bg_ref, wo_ref, o_ref):
    att = _rms(att_ref[...].astype(F32)) * ga_ref[...]
    s = jax.nn.gelu(y_ref[...].astype(F32))
    z = jnp.dot(s.astype(BF16), wg_ref[...], preferred_element_type=F32) + bg_ref[...]
    s = s * jax.nn.sigmoid(z)
    ssm = _rms(s) * gs_ref[...]
    aw = att_ref.shape[1]
    mix = jnp.dot(att.astype(BF16), wo_ref[:aw, :], preferred_element_type=F32)
    mix = mix + jnp.dot(ssm.astype(BF16), wo_ref[aw:, :], preferred_element_type=F32)
    o_ref[...] = x_ref[...] + mod_ref[2:3, :] * (_rms(mix) * gp_ref[...])


def _mix(att, y, x2, mod, g_attn, g_ssm, g_post, w_glu_b, b_glu, w_out_b, seq):
    t_tok, d = x2.shape
    aw, sw = att.shape[1], y.shape[1]
    tm = 512
    per_b = seq // tm
    tok = lambda w: pl.BlockSpec((tm, w), lambda i: (i, 0))
    return pl.pallas_call(
        _mix_kernel,
        grid=(t_tok // tm,),
        in_specs=[tok(aw), tok(sw), tok(d),
                  pl.BlockSpec((None, N_MOD, d), lambda i: (i // per_b, 0, 0)),
                  _resident((1, aw)), _resident((1, sw)), _resident((1, d)),
                  _resident((sw, sw)), _resident((1, sw)), _resident((aw + sw, d))],
        out_specs=tok(d),
        out_shape=jax.ShapeDtypeStruct((t_tok, d), F32),
        compiler_params=_cparams(("parallel",), 48),
        name="mix",
    )(att, y, x2, mod, g_attn, g_ssm, g_post, w_glu_b, b_glu, w_out_b)


def _mlp_kernel(x_ref, mod_ref, gpre_ref, gpost_ref, w1_ref, w2_ref, o_ref, h_ref, acc_ref):
    f = pl.program_id(1)

    @pl.when(f == 0)
    def _():
        scale = gpre_ref[...] * (1.0 + mod_ref[4:5, :])
        h_ref[...] = (_rms(x_ref[...]) * scale + mod_ref[3:4, :]).astype(BF16)
        acc_ref[...] = jnp.zeros_like(acc_ref)

    t = jnp.dot(h_ref[...], w1_ref[...], preferred_element_type=F32)
    t = jnp.square(jnp.maximum(t, 0.0)).astype(BF16)
    acc_ref[...] += jnp.dot(t, w2_ref[...], preferred_element_type=F32)

    @pl.when(f == pl.num_programs(1) - 1)
    def _():
        o_ref[...] = x_ref[...] + mod_ref[5:6, :] * (_rms(acc_ref[...]) * gpost_ref[...])


def _mlp(x2, mod, g_pre, g_post, w1_b, w2_b, seq):
    t_tok, d = x2.shape
    d_ff = w1_b.shape[1]
    tm, tf = 512, 512
    per_b = seq // tm
    return pl.pallas_call(
        _mlp_kernel,
        grid=(t_tok // tm, d_ff // tf),
        in_specs=[pl.BlockSpec((tm, d), lambda i, f: (i, 0)),
                  pl.BlockSpec((None, N_MOD, d), lambda i, f: (i // per_b, 0, 0)),
                  _resident((1, d)), _resident((1, d)),
                  pl.BlockSpec((d, tf), lambda i, f: (0, f)),
                  pl.BlockSpec((tf, d), lambda i, f: (f, 0))],
        out_specs=pl.BlockSpec((tm, d), lambda i, f: (i, 0)),
        out_shape=jax.ShapeDtypeStruct((t_tok, d), F32),
        scratch_shapes=[pltpu.VMEM((tm, d), BF16), pltpu.VMEM((tm, d), F32)],
        compiler_params=_cparams(("parallel", "arbitrary"), 52),
        name="mlp",
    )(x2, mod, g_pre, g_post, w1_b, w2_b)


def _rope_tables(positions):
    half = ROT_DIM // 2
    freqs = ROPE_THETA ** (-jnp.arange(0, ROT_DIM, 2, dtype=F32) / ROT_DIM)
    ang = positions.reshape(-1).astype(F32)[:, None] * freqs
    cos, sin = jnp.cos(ang), jnp.sin(ang)
    t_tok = ang.shape[0]
    pad1 = jnp.ones((t_tok, HEAD_DIM - ROT_DIM), F32)
    pad0 = jnp.zeros((t_tok, HEAD_DIM - ROT_DIM), F32)
    assert cos.shape[1] == half
    ct = jnp.concatenate([cos, cos, pad1], axis=1)
    st = jnp.concatenate([-sin, sin, pad0], axis=1)
    return jnp.tile(ct, (1, LANES // HEAD_DIM)), jnp.tile(st, (1, LANES // HEAD_DIM))


def kernel(x, c, positions, w_ada, b_ada, g_pre_mix, g_post_mix, w_in, ssm_a_re, ssm_a_im, ssm_log_dt,
           ssm_b_re, ssm_b_im, ssm_c_re, ssm_c_im, ssm_d, w_glu, b_glu, g_attn_out, g_ssm_out, w_out,
           g_pre_mlp, g_post_mlp, w_mlp_in, w_mlp_out):
    bsz, seq, d = x.shape
    depth = w_ada.shape[0]
    x2 = x.reshape(bsz * seq, d)
    ctab, stab = _rope_tables(positions)
    row = lambda a: a.reshape(1, -1).astype(F32)
    for l in range(depth):
        mod = _adaln(c, w_ada[l], b_ada[l])
        q1, q2, q3, k, v, u = _inproj(x2, mod, row(g_pre_mix[l]), w_in[l].astype(BF16), ctab, stab, seq)

        o, lse = None, None
        for dil, q in zip(DILATIONS, (q1, q2, q3)):
            o, lse = _attn_group(q, k, v, o, lse, dil, bsz, seq)

        tables = _ssm_tables(ssm_a_re[l], ssm_a_im[l], ssm_log_dt[l], ssm_b_re[l], ssm_b_im[l],
                             ssm_c_re[l], ssm_c_im[l], ssm_d[l])
        y = _ssm(u, tables, bsz, seq)

        x2 = _mix(o, y, x2, mod, row(g_attn_out[l]), row(g_ssm_out[l]), row(g_post_mix[l]),
                  w_glu[l].astype(BF16), row(b_glu[l]), w_out[l].astype(BF16), seq)
        x2 = _mlp(x2, mod, row(g_pre_mlp[l]), row(g_post_mlp[l]),
                  w_mlp_in[l].astype(BF16), w_mlp_out[l].astype(BF16), seq)
    return x2.reshape(bsz, seq, d)
```

```python
import functools
import math

import jax
import jax.numpy as jnp
from jax import lax
from jax.experimental import pallas as pl
from jax.experimental.pallas import tpu as pltpu

F32 = jnp.float32
BF16 = jnp.bfloat16

LANES = 128
SUBLANES = 8

HEAD_DIM = 64
ROT_DIM = HEAD_DIM // 4
ROPE_THETA = 500000.0
DILATIONS = (1, 4, 16)
HEADS_PER_GROUP = 6
GROUP_WIDTH = HEADS_PER_GROUP * HEAD_DIM
GROUP_SLABS = GROUP_WIDTH // LANES
ATT_BLK = 128
SSM_P = 16
SSM_N = 64
SSM_CHUNK = 16
CHUNK_W = SSM_CHUNK * SSM_P
SLOTS = LANES // SSM_P
N_MOD = 6
EPS = 1e-6
NEG = -1e30
TOK_TILE = 512


def _cparams(sem, vmem_mb):
    return pltpu.CompilerParams(dimension_semantics=sem, vmem_limit_bytes=vmem_mb * 1024 * 1024)


def _resident(shape):
    nd = len(shape)
    return pl.BlockSpec(shape, lambda *_: (0,) * nd, pipeline_mode=pl.Buffered(1))


def _slab(j):
    return slice(j * LANES, (j + 1) * LANES)


def _slot_transpose(vs):
    slot = lax.broadcasted_iota(jnp.int32, vs[0].shape, 1) // SSM_P
    vs = list(vs)
    for dist in (4, 2, 1):
        low = (slot & dist) == 0
        nxt = list(vs)
        for i in range(SLOTS):
            if i & dist == 0:
                a, b = vs[i], vs[i + dist]
                nxt[i] = jnp.where(low, a, pltpu.roll(b, dist * SSM_P, axis=1))
                nxt[i + dist] = jnp.where(low, pltpu.roll(a, LANES - dist * SSM_P, axis=1), b)
        vs = nxt
    return vs


def _adaln_kernel(s_ref, w_ref, b_ref, o_ref):
    o_ref[...] = jnp.dot(s_ref[...], w_ref[...].astype(BF16),
                         preferred_element_type=F32) + b_ref[...]


def _adaln(c, w_ada, b_ada):
    bsz, d = c.shape
    n = w_ada.shape[1]
    tn = 1024
    s = jnp.zeros((SUBLANES, d), BF16).at[:bsz].set(jax.nn.silu(c).astype(BF16))
    out = pl.pallas_call(
        _adaln_kernel,
        grid=(n // tn,),
        in_specs=[pl.BlockSpec((SUBLANES, d), lambda j: (0, 0)),
                  pl.BlockSpec((d, tn), lambda j: (0, j)),
                  pl.BlockSpec((1, tn), lambda j: (0, j))],
        out_specs=pl.BlockSpec((SUBLANES, tn), lambda j: (0, j)),
        out_shape=jax.ShapeDtypeStruct((SUBLANES, n), F32),
        compiler_params=_cparams(("arbitrary",), 40),
        name="adaln",
    )(s, w_ada, b_ada.reshape(1, n))
    return out[:bsz].reshape(bsz, N_MOD, d)


def _inproj_kernel(x_ref, mod_ref, g_ref, w_ref, ct_ref, st_ref,
                   q1_ref, q4_ref, q16_ref, k1_ref, k4_ref, k16_ref, v1_ref, v4_ref, v16_ref,
                   uc_ref, sc_ref, *, tm):
    x = x_ref[...]
    ms = jnp.mean(x * x, axis=-1, keepdims=True)
    scale = g_ref[...] * (1.0 + mod_ref[1:2, :])
    h = (x * lax.rsqrt(ms + EPS)) * scale + mod_ref[0:1, :]
    hb = h.astype(BF16)

    qk_slabs = 4 * GROUP_SLABS
    pqk = jnp.dot(hb, w_ref[:, :qk_slabs * LANES], preferred_element_type=F32)
    ct = ct_ref[...]
    st = st_ref[...]
    lane = lax.broadcasted_iota(jnp.int32, ct.shape, 1) % HEAD_DIM
    second_half = (lane >= ROT_DIM // 2) & (lane < ROT_DIM)

    def scatter(j, jj, refs):
        for dil, ref in refs:
            if dil == 1:
                ref[:, _slab(jj)] = sc_ref[j].astype(BF16)
            else:
                for r in range(dil):
                    ref[r, :, _slab(jj)] = sc_ref[j, pl.ds(r, tm // dil, stride=dil), :].astype(BF16)

    q_refs = ((1, q1_ref),), ((4, q4_ref),), ((16, q16_ref),)
    for j in range(qk_slabs):
        t = pqk[:, _slab(j)]
        partner = jnp.where(second_half,
                            pltpu.roll(t, ROT_DIM // 2, axis=1),
                            pltpu.roll(t, LANES - ROT_DIM // 2, axis=1))
        r = t * ct + partner * st
        grp, jj = divmod(j, GROUP_SLABS)
        if grp < 3:
            r = r * (1.0 / math.sqrt(HEAD_DIM))
        sc_ref[j] = r
        scatter(j, jj, q_refs[grp] if grp < 3 else ((1, k1_ref), (4, k4_ref), (16, k16_ref)))

    pvu = jnp.dot(hb, w_ref[:, qk_slabs * LANES:], preferred_element_type=F32)
    for j in range(pvu.shape[1] // LANES):
        sc_ref[qk_slabs + j] = pvu[:, _slab(j)]
    for jj in range(GROUP_SLABS):
        scatter(qk_slabs + jj, jj, ((1, v1_ref), (4, v4_ref), (16, v16_ref)))

    u0 = qk_slabs + GROUP_SLABS
    for gb in range(uc_ref.shape[0] // SLOTS):
        pieces = [[[None, None] for _ in range(tm // (SSM_CHUNK * SUBLANES))] for _ in range(SLOTS)]
        for cb in range(tm // (SSM_CHUNK * SUBLANES)):
            for half in range(2):
                vs = [sc_ref[u0 + gb, pl.ds(cb * SSM_CHUNK * SUBLANES + half * SLOTS + s, SUBLANES,
                                            stride=SSM_CHUNK), :] for s in range(SLOTS)]
                for g, o in enumerate(_slot_transpose(vs)):
                    pieces[g][cb][half] = o
        for g in range(SLOTS):
            full = jnp.concatenate([jnp.concatenate(p, axis=1) for p in pieces[g]], axis=0)
            uc_ref[gb * SLOTS + g] = full.astype(BF16)


def _inproj(x2, mod, g_pre, w_in_b, ctab, stab, bsz, seq):
    t_tok, d = x2.shape
    in_w = w_in_b.shape[1]
    ssm_w = in_w - 5 * GROUP_WIDTH
    groups = ssm_w // SSM_P
    tm = TOK_TILE
    per_b = seq // tm
    tok = lambda w: pl.BlockSpec((tm, w), lambda i: (i, 0))

    def res_spec(dil):
        return pl.BlockSpec((None, dil, tm // dil, GROUP_WIDTH), lambda i: (i // per_b, 0, i % per_b, 0))

    def res_shape(dil):
        return jax.ShapeDtypeStruct((bsz, dil, seq // dil, GROUP_WIDTH), BF16)

    nat = jax.ShapeDtypeStruct((t_tok, GROUP_WIDTH), BF16)
    out_shape = [nat, res_shape(4), res_shape(16)] * 3 + \
                [jax.ShapeDtypeStruct((groups, t_tok // SSM_CHUNK, CHUNK_W), BF16)]
    out_specs = [tok(GROUP_WIDTH), res_spec(4), res_spec(16)] * 3 + \
                [pl.BlockSpec((groups, tm // SSM_CHUNK, CHUNK_W), lambda i: (0, i, 0))]
    return pl.pallas_call(
        functools.partial(_inproj_kernel, tm=tm),
        grid=(t_tok // tm,),
        in_specs=[tok(d),
                  pl.BlockSpec((None, N_MOD, d), lambda i: (i // per_b, 0, 0)),
                  _resident((1, d)),
                  _resident((d, in_w)),
                  tok(LANES), tok(LANES)],
        out_specs=out_specs,
        out_shape=out_shape,
        scratch_shapes=[pltpu.VMEM((in_w // LANES, tm, LANES), F32)],
        compiler_params=_cparams(("parallel",), 56),
        name="inproj",
    )(x2, mod, g_pre, w_in_b, ctab, stab)


def _attn_kernel(q_ref, k_ref, v_ref, o_ref, l_ref, *, n):
    qi = lax.broadcasted_iota(jnp.int32, (ATT_BLK, 2 * ATT_BLK), 0)
    kj = lax.broadcasted_iota(jnp.int32, (ATT_BLK, 2 * ATT_BLK), 1)
    rel = qi - kj
    lane = lax.broadcasted_iota(jnp.int32, (ATT_BLK, LANES), 1)

    def body(i, carry):
        qs = pl.multiple_of(i * ATT_BLK, ATT_BLK)
        ks = pl.multiple_of(jnp.maximum(i - 1, 0) * ATT_BLK, ATT_BLK)
        dist = rel + (qs - ks)
        valid = (dist >= 0) & (dist <= ATT_BLK)
        q = q_ref[pl.ds(qs, ATT_BLK), :]
        kw = k_ref[pl.ds(ks, 2 * ATT_BLK), :]
        vw = v_ref[pl.ds(ks, 2 * ATT_BLK), :]
        o_parts = []
        lse_tile = jnp.zeros((ATT_BLK, LANES), F32)
        for h in range(HEADS_PER_GROUP):
            sl = slice(h * HEAD_DIM, (h + 1) * HEAD_DIM)
            s = lax.dot_general(q[:, sl], kw[:, sl], (((1,), (1,)), ((), ())),
                                preferred_element_type=F32)
            s = jnp.where(valid, s, NEG)
            m = jnp.max(s, axis=1, keepdims=True)
            p = jnp.exp(s - m)
            l = jnp.sum(p, axis=1, keepdims=True)
            o = jnp.dot(p.astype(BF16), vw[:, sl], preferred_element_type=F32)
            o_parts.append(o * (1.0 / l))
            lse_tile = jnp.where(lane == h, m + jnp.log(l), lse_tile)
        o_ref[pl.ds(qs, ATT_BLK), :] = jnp.concatenate(o_parts, axis=1).astype(o_ref.dtype)
        l_ref[pl.ds(qs, ATT_BLK), :] = lse_tile
        return carry

    lax.fori_loop(0, n // ATT_BLK, body, 0)


def _attn_group(q, k, v, dilation, bsz, seq):
    n = seq // dilation
    spec = lambda w: pl.BlockSpec((None, None, n, w), lambda b, r: (b, r, 0, 0))
    return pl.pallas_call(
        functools.partial(_attn_kernel, n=n),
        grid=(bsz, dilation),
        in_specs=[spec(GROUP_WIDTH)] * 3,
        out_specs=[spec(GROUP_WIDTH), spec(LANES)],
        out_shape=[jax.ShapeDtypeStruct((bsz, dilation, n, GROUP_WIDTH), BF16),
                   jax.ShapeDtypeStruct((bsz, dilation, n, LANES), F32)],
        compiler_params=_cparams(("parallel", "parallel"), 48),
        name=f"attn_d{dilation}",
    )(q, k, v)


def _cmul(a, b):
    return a[0] * b[0] - a[1] * b[1], a[0] * b[1] + a[1] * b[0]


def _ssm_tables(a_re, a_im, log_dt, b_re, b_im, c_re, c_im, d_skip):
    hp = lax.Precision.HIGHEST
    f = lambda t: t.astype(F32)
    lam = (f(a_re), f(a_im))
    dt = jnp.exp(f(log_dt))[:, None]
    g = a_re.shape[0]
    taus = jnp.arange(2 * SSM_CHUNK + 1, dtype=F32)[None, :, None]
    mag = jnp.exp(lam[0][:, None, :] * dt[:, None, :] * taus)
    ang = lam[1][:, None, :] * dt[:, None, :] * taus
    apow = (mag * jnp.cos(ang), mag * jnp.sin(ang))
    a_bar = (apow[0][:, 1], apow[1][:, 1])
    den = lam[0] * lam[0] + lam[1] * lam[1]
    coef = _cmul((a_bar[0] - 1.0, a_bar[1]), (lam[0] / den, -lam[1] / den))
    b_bar = _cmul((coef[0][..., None], coef[1][..., None]), (f(b_re), f(b_im)))
    c_mat = (f(c_re), f(c_im))

    ca = _cmul((c_mat[0][:, None], c_mat[1][:, None]),
               (apow[0][:, :SSM_CHUNK, None, :], apow[1][:, :SSM_CHUNK, None, :]))
    kern = jnp.einsum('gtpn,gnq->gtpq', ca[0], b_bar[0], precision=hp) - \
        jnp.einsum('gtpn,gnq->gtpq', ca[1], b_bar[1], precision=hp)
    kern = kern.at[:, 0].add(f(d_skip)[:, :, None] * jnp.eye(SSM_P, dtype=F32))
    kqp = kern.transpose(0, 1, 3, 2)
    rows = [jnp.pad(kqp[:, :SSM_CHUNK - s], ((0, 0), (s, 0), (0, 0), (0, 0))) for s in range(SSM_CHUNK)]
    m_mat = jnp.stack(rows, axis=1)
    m_mat = m_mat.transpose(0, 1, 3, 2, 4).reshape(g, CHUNK_W, CHUNK_W)

    rev = (apow[0][:, SSM_CHUNK - 1::-1][:, :SSM_CHUNK], apow[1][:, SSM_CHUNK - 1::-1][:, :SSM_CHUNK])
    bt = (b_bar[0].transpose(0, 2, 1)[:, None], b_bar[1].transpose(0, 2, 1)[:, None])
    w4 = _cmul((rev[0][:, :, None, :], rev[1][:, :, None, :]), bt)
    w_re, w_im = (t.reshape(g, CHUNK_W, SSM_N) for t in w4)
    w_mat = jnp.concatenate([w_re, w_im, w_im, w_re], axis=-1)

    ct = (c_mat[0].transpose(0, 2, 1)[:, :, None, :], c_mat[1].transpose(0, 2, 1)[:, :, None, :])
    at = (apow[0][:, 1:SSM_CHUNK + 1].transpose(0, 2, 1)[..., None],
          apow[1][:, 1:SSM_CHUNK + 1].transpose(0, 2, 1)[..., None])
    v4 = _cmul(ct, at)
    v_re, v_im = (t.reshape(g, SSM_N, CHUNK_W) for t in v4)
    v_mat = jnp.concatenate([v_re, -v_im], axis=1)

    def mul_rows(tau):
        re, im = apow[0][:, tau], apow[1][:, tau]
        return [jnp.concatenate([re, re], -1), jnp.concatenate([-im, im], -1),
                jnp.concatenate([im, -im], -1)]
    ptab = jnp.stack(mul_rows(SSM_CHUNK) + mul_rows(2 * SSM_CHUNK) +
                     [jnp.zeros((g, LANES), F32)] * 2, axis=1)
    return m_mat.astype(BF16), w_mat.astype(BF16), v_mat.astype(BF16), ptab


def _ssm_kernel(u_ref, m_ref, w_ref, v_ref, p_ref, y_ref, gx_ref, gz_ref, h_ref, *, groups, nchunk, bsz):
    rows = nchunk * bsz
    row = lax.broadcasted_iota(jnp.int32, (rows, LANES), 0)
    first_chunk = (row % nchunk) == 0

    def prev_chunk(a):
        return jnp.where(first_chunk, 0.0, pltpu.roll(a, 1, axis=0))

    for g in range(groups):
        e = jnp.dot(u_ref[g], w_ref[g], preferred_element_type=F32)
        ex, ez = e[:, :LANES], e[:, LANES:]
        exs, ezs = prev_chunk(ex), prev_chunk(ez)
        p1, p2, p3 = p_ref[g, 0:1, :], p_ref[g, 1:2, :], p_ref[g, 2:3, :]
        gx = ex + p1 * exs + p2 * ezs
        gz = ez + p1 * ezs + p3 * exs
        for b in range(bsz):
            gx_ref[g, pl.ds(b, nchunk, stride=bsz), :] = gx[b * nchunk:(b + 1) * nchunk]
            gz_ref[g, pl.ds(b, nchunk, stride=bsz), :] = gz[b * nchunk:(b + 1) * nchunk]

    def step(j, carry):
        r0 = pl.multiple_of(j * SUBLANES, SUBLANES)
        new = []
        for g in range(groups):
            x, z = carry[g]
            q1, q2, q3 = p_ref[g, 3:4, :], p_ref[g, 4:5, :], p_ref[g, 5:6, :]
            xn = q1 * x + q2 * z + gx_ref[g, pl.ds(r0, SUBLANES), :]
            zn = q1 * z + q3 * x + gz_ref[g, pl.ds(r0, SUBLANES), :]
            h_ref[g, pl.ds(r0, SUBLANES), :] = xn
            new.append((xn, zn))
        return tuple(new)

    zero = jnp.zeros((SUBLANES, LANES), F32)
    lax.fori_loop(0, rows // SUBLANES, step, tuple((zero, zero) for _ in range(groups)))

    for g in range(groups):
        h_end = jnp.concatenate([h_ref[g, pl.ds(b, nchunk, stride=bsz), :] for b in range(bsz)], axis=0)
        h_in = prev_chunk(h_end).astype(BF16)
        y = jnp.dot(u_ref[g], m_ref[g], preferred_element_type=F32)
        y = y + jnp.dot(h_in, v_ref[g], preferred_element_type=F32)
        y_ref[g] = y.astype(y_ref.dtype)


def _ssm(uc, tables, bsz, seq):
    m_mat, w_mat, v_mat, ptab = tables
    g, rows, _ = uc.shape
    nchunk = seq // SSM_CHUNK
    assert rows == nchunk * bsz and SUBLANES % bsz == 0 and SUBLANES // bsz == 2
    gs = 4
    blk = lambda a, b: pl.BlockSpec((gs, a, b), lambda i: (i, 0, 0))
    return pl.pallas_call(
        functools.partial(_ssm_kernel, groups=gs, nchunk=nchunk, bsz=bsz),
        grid=(g // gs,),
        in_specs=[blk(rows, CHUNK_W), blk(CHUNK_W, CHUNK_W), blk(CHUNK_W, CHUNK_W),
                  blk(LANES, CHUNK_W), blk(SUBLANES, LANES)],
        out_specs=blk(rows, CHUNK_W),
        out_shape=jax.ShapeDtypeStruct((g, rows, CHUNK_W), BF16),
        scratch_shapes=[pltpu.VMEM((gs, rows, LANES), F32)] * 3,
        compiler_params=_cparams(("parallel",), 48),
        name="ssm",
    )(uc, m_mat, w_mat, v_mat, ptab)


def _rms(x):
    return x * lax.rsqrt(jnp.mean(x * x, axis=-1, keepdims=True) + EPS)


def _mix_kernel(o1_ref, l1_ref, o4_ref, l4_ref, o16_ref, l16_ref, y_ref, x_ref, mod_ref,
                ga_ref, gs_ref, gp_ref, ex_ref, wg_ref, bg_ref, wo_ref, out_ref,
                so_ref, sl_ref, sy_ref, *, tm):
    for gi, (dil, o_ref, l_ref) in enumerate(((4, o4_ref, l4_ref), (16, o16_ref, l16_ref))):
        for r in range(dil):
            rows = pl.ds(r, tm // dil, stride=dil)
            for jj in range(GROUP_SLABS):
                so_ref[gi * GROUP_SLABS + jj, rows, :] = o_ref[r, :, _slab(jj)].astype(F32)
            sl_ref[gi, rows, :] = l_ref[r]

    lses = [l1_ref[...], sl_ref[0], sl_ref[1]]
    mx = jnp.maximum(jnp.maximum(lses[0], lses[1]), lses[2])
    es = [jnp.exp(l - mx) for l in lses]
    inv = 1.0 / (es[0] + es[1] + es[2])
    wexp = [jnp.dot((e * inv).astype(BF16), ex_ref[...], preferred_element_type=F32) for e in es]
    att_slabs = []
    for jj in range(GROUP_SLABS):
        a = wexp[0][:, _slab(jj)] * o1_ref[:, _slab(jj)].astype(F32)
        a = a + wexp[1][:, _slab(jj)] * so_ref[jj]
        a = a + wexp[2][:, _slab(jj)] * so_ref[GROUP_SLABS + jj]
        att_slabs.append(a)
    att = _rms(jnp.concatenate(att_slabs, axis=1)) * ga_ref[...]

    for gb in range(y_ref.shape[0] // SLOTS):
        ygs = [y_ref[gb * SLOTS + g].astype(F32) for g in range(SLOTS)]
        for cb in range(tm // (SSM_CHUNK * SUBLANES)):
            for half in range(2):
                vs = [yg[cb * SUBLANES:(cb + 1) * SUBLANES, _slab(half)] for yg in ygs]
                for s, o in enumerate(_slot_transpose(vs)):
                    sy_ref[gb, pl.ds(cb * SSM_CHUNK * SUBLANES + half * SLOTS + s, SUBLANES,
                                     stride=SSM_CHUNK), :] = o
    yv = jnp.concatenate([sy_ref[gb] for gb in range(sy_ref.shape[0])], axis=1)

    s = jax.nn.gelu(yv)
    z = jnp.dot(s.astype(BF16), wg_ref[...], preferred_element_type=F32) + bg_ref[...]
    s = s * jax.nn.sigmoid(z)
    ssm = _rms(s) * gs_ref[...]
    aw = att.shape[1]
    mix = jnp.dot(att.astype(BF16), wo_ref[:aw, :], preferred_element_type=F32)
    mix = mix + jnp.dot(ssm.astype(BF16), wo_ref[aw:, :], preferred_element_type=F32)
    out_ref[...] = x_ref[...] + mod_ref[2:3, :] * (_rms(mix) * gp_ref[...])


def _mix(o1, l1, o4, l4, o16, l16, yc, x2, mod, g_attn, g_ssm, g_post, w_glu_b, b_glu, w_out_b, bsz, seq):
    t_tok, d = x2.shape
    groups = yc.shape[0]
    sw = groups * SSM_P
    aw = GROUP_WIDTH
    tm = TOK_TILE
    per_b = seq // tm
    tok = lambda w: pl.BlockSpec((tm, w), lambda i: (i, 0))
    res = lambda dil, w: pl.BlockSpec((None, dil, tm // dil, w), lambda i: (i // per_b, 0, i % per_b, 0))
    head_of_lane = jnp.arange(aw)[None, :] // HEAD_DIM
    expand = (jnp.arange(LANES)[:, None] == head_of_lane).astype(BF16)
    return pl.pallas_call(
        functools.partial(_mix_kernel, tm=tm),
        grid=(t_tok // tm,),
        in_specs=[tok(aw), tok(LANES), res(4, aw), res(4, LANES), res(16, aw), res(16, LANES),
                  pl.BlockSpec((groups, tm // SSM_CHUNK, CHUNK_W), lambda i: (0, i, 0)),
                  tok(d),
                  pl.BlockSpec((None, N_MOD, d), lambda i: (i // per_b, 0, 0)),
                  _resident((1, aw)), _resident((1, sw)), _resident((1, d)),
                  _resident((LANES, aw)),
                  _resident((sw, sw)), _resident((1, sw)), _resident((aw + sw, d))],
        out_specs=tok(d),
        out_shape=jax.ShapeDtypeStruct((t_tok, d), F32),
        scratch_shapes=[pltpu.VMEM((2 * GROUP_SLABS, tm, LANES), F32),
                        pltpu.VMEM((2, tm, LANES), F32),
                        pltpu.VMEM((sw // LANES, tm, LANES), F32)],
        compiler_params=_cparams(("parallel",), 52),
        name="mix",
    )(o1, l1, o4, l4, o16, l16, yc, x2, mod, g_attn, g_ssm, g_post, expand, w_glu_b, b_glu, w_out_b)


def _mlp_kernel(x_ref, mod_ref, gpre_ref, gpost_ref, w1_ref, w2_ref, o_ref, h_ref, acc_ref):
    f = pl.program_id(1)

    @pl.when(f == 0)
    def _():
        scale = gpre_ref[...] * (1.0 + mod_ref[4:5, :])
        h_ref[...] = (_rms(x_ref[...]) * scale + mod_ref[3:4, :]).astype(BF16)
        acc_ref[...] = jnp.zeros_like(acc_ref)

    t = jnp.dot(h_ref[...], w1_ref[...], preferred_element_type=F32)
    t = jnp.square(jnp.maximum(t, 0.0)).astype(BF16)
    acc_ref[...] += jnp.dot(t, w2_ref[...], preferred_element_type=F32)

    @pl.when(f == pl.num_programs(1) - 1)
    def _():
        o_ref[...] = x_ref[...] + mod_ref[5:6, :] * (_rms(acc_ref[...]) * gpost_ref[...])


def _mlp(x2, mod, g_pre, g_post, w1_b, w2_b, seq):
    t_tok, d = x2.shape
    d_ff = w1_b.shape[1]
    tm, tf = TOK_TILE, 512
    per_b = seq // tm
    return pl.pallas_call(
        _mlp_kernel,
        grid=(t_tok // tm, d_ff // tf),
        in_specs=[pl.BlockSpec((tm, d), lambda i, f: (i, 0)),
                  pl.BlockSpec((None, N_MOD, d), lambda i, f: (i // per_b, 0, 0)),
                  _resident((1, d)), _resident((1, d)),
                  pl.BlockSpec((d, tf), lambda i, f: (0, f)),
                  pl.BlockSpec((tf, d), lambda i, f: (f, 0))],
        out_specs=pl.BlockSpec((tm, d), lambda i, f: (i, 0)),
        out_shape=jax.ShapeDtypeStruct((t_tok, d), F32),
        scratch_shapes=[pltpu.VMEM((tm, d), BF16), pltpu.VMEM((tm, d), F32)],
        compiler_params=_cparams(("parallel", "arbitrary"), 52),
        name="mlp",
    )(x2, mod, g_pre, g_post, w1_b, w2_b)


def _rope_tables(positions):
    half = ROT_DIM // 2
    freqs = ROPE_THETA ** (-jnp.arange(0, ROT_DIM, 2, dtype=F32) / ROT_DIM)
    ang = positions.reshape(-1).astype(F32)[:, None] * freqs
    cos, sin = jnp.cos(ang), jnp.sin(ang)
    t_tok = ang.shape[0]
    pad1 = jnp.ones((t_tok, HEAD_DIM - ROT_DIM), F32)
    pad0 = jnp.zeros((t_tok, HEAD_DIM - ROT_DIM), F32)
    assert cos.shape[1] == half
    ct = jnp.concatenate([cos, cos, pad1], axis=1)
    st = jnp.concatenate([-sin, sin, pad0], axis=1)
    return jnp.tile(ct, (1, LANES // HEAD_DIM)), jnp.tile(st, (1, LANES // HEAD_DIM))


def kernel(x, c, positions, w_ada, b_ada, g_pre_mix, g_post_mix, w_in, ssm_a_re, ssm_a_im, ssm_log_dt,
           ssm_b_re, ssm_b_im, ssm_c_re, ssm_c_im, ssm_d, w_glu, b_glu, g_attn_out, g_ssm_out, w_out,
           g_pre_mlp, g_post_mlp, w_mlp_in, w_mlp_out):
    bsz, seq, d = x.shape
    depth = w_ada.shape[0]
    x2 = x.reshape(bsz * seq, d)
    ctab, stab = _rope_tables(positions)
    row = lambda a: a.reshape(1, -1).astype(F32)
    for l in range(depth):
        mod = _adaln(c, w_ada[l], b_ada[l])
        q1, q4, q16, k1, k4, k16, v1, v4, v16, uc = _inproj(
            x2, mod, row(g_pre_mix[l]), w_in[l].astype(BF16), ctab, stab, bsz, seq)

        nat = lambda a: a.reshape(bsz, 1, seq, a.shape[-1])
        o1, l1 = _attn_group(nat(q1), nat(k1), nat(v1), 1, bsz, seq)
        o4, l4 = _attn_group(q4, k4, v4, 4, bsz, seq)
        o16, l16 = _attn_group(q16, k16, v16, 16, bsz, seq)

        tables = _ssm_tables(ssm_a_re[l], ssm_a_im[l], ssm_log_dt[l], ssm_b_re[l], ssm_b_im[l],
                             ssm_c_re[l], ssm_c_im[l], ssm_d[l])
        yc = _ssm(uc, tables, bsz, seq)

        x2 = _mix(o1.reshape(bsz * seq, -1), l1.reshape(bsz * seq, -1), o4, l4, o16, l16, yc, x2, mod,
                  row(g_attn_out[l]), row(g_ssm_out[l]), row(g_post_mix[l]),
                  w_glu[l].astype(BF16), row(b_glu[l]), w_out[l].astype(BF16), bsz, seq)
        x2 = _mlp(x2, mod, row(g_pre_mlp[l]), row(g_post_mlp[l]),
                  w_mlp_in[l].astype(BF16), w_mlp_out[l].astype(BF16), seq)
    return x2.reshape(bsz, seq, d)
```

```python
import functools
import math

import jax
import jax.numpy as jnp
from jax import lax
from jax.experimental import pallas as pl
from jax.experimental.pallas import tpu as pltpu

F32 = jnp.float32
BF16 = jnp.bfloat16

LANES = 128
SUBLANES = 8

HEAD_DIM = 64
ROT_DIM = HEAD_DIM // 4
ROPE_THETA = 500000.0
DILATIONS = (1, 4, 16)
HEADS_PER_GROUP = 6
GROUP_WIDTH = HEADS_PER_GROUP * HEAD_DIM
GROUP_SLABS = GROUP_WIDTH // LANES
ATT_BLK = 128
SSM_P = 16
SSM_N = 64
SSM_CHUNK = 16
CHUNK_W = SSM_CHUNK * SSM_P
SLOTS = LANES // SSM_P
N_MOD = 6
EPS = 1e-6
NEG = -1e30
TOK_TILE = 512


def _cparams(sem, vmem_mb):
    return pltpu.CompilerParams(dimension_semantics=sem, vmem_limit_bytes=vmem_mb * 1024 * 1024)


def _resident(shape):
    nd = len(shape)
    return pl.BlockSpec(shape, lambda *_: (0,) * nd, pipeline_mode=pl.Buffered(1))


def _slab(j):
    return slice(j * LANES, (j + 1) * LANES)


def _slot_transpose(vs):
    slot = lax.broadcasted_iota(jnp.int32, vs[0].shape, 1) // SSM_P
    vs = list(vs)
    for dist in (4, 2, 1):
        low = (slot & dist) == 0
        nxt = list(vs)
        for i in range(SLOTS):
            if i & dist == 0:
                a, b = vs[i], vs[i + dist]
                nxt[i] = jnp.where(low, a, pltpu.roll(b, dist * SSM_P, axis=1))
                nxt[i + dist] = jnp.where(low, pltpu.roll(a, LANES - dist * SSM_P, axis=1), b)
        vs = nxt
    return vs


def _adaln_kernel(s_ref, w_ref, b_ref, o_ref):
    o_ref[...] = jnp.dot(s_ref[...], w_ref[...].astype(BF16),
                         preferred_element_type=F32) + b_ref[...]


def _adaln(c, w_ada, b_ada):
    bsz, d = c.shape
    n = w_ada.shape[1]
    tn = 1024
    s = jnp.zeros((SUBLANES, d), BF16).at[:bsz].set(jax.nn.silu(c).astype(BF16))
    out = pl.pallas_call(
        _adaln_kernel,
        grid=(n // tn,),
        in_specs=[pl.BlockSpec((SUBLANES, d), lambda j: (0, 0)),
                  pl.BlockSpec((d, tn), lambda j: (0, j)),
                  pl.BlockSpec((1, tn), lambda j: (0, j))],
        out_specs=pl.BlockSpec((SUBLANES, tn), lambda j: (0, j)),
        out_shape=jax.ShapeDtypeStruct((SUBLANES, n), F32),
        compiler_params=_cparams(("arbitrary",), 40),
        name="adaln",
    )(s, w_ada, b_ada.reshape(1, n))
    return out[:bsz].reshape(bsz, N_MOD, d)


def _inproj_kernel(x_ref, mod_ref, g_ref, w_ref, ct_ref, st_ref,
                   q1_ref, q4_ref, q16_ref, k1_ref, k4_ref, k16_ref, v1_ref, v4_ref, v16_ref,
                   uc_ref, sc_ref, *, tm):
    x = x_ref[...]
    ms = jnp.mean(x * x, axis=-1, keepdims=True)
    scale = g_ref[...] * (1.0 + mod_ref[1:2, :])
    h = (x * lax.rsqrt(ms + EPS)) * scale + mod_ref[0:1, :]
    hb = h.astype(BF16)

    qk_slabs = 4 * GROUP_SLABS
    pqk = jnp.dot(hb, w_ref[:, :qk_slabs * LANES], preferred_element_type=F32)
    ct = ct_ref[...]
    st = st_ref[...]
    lane = lax.broadcasted_iota(jnp.int32, ct.shape, 1) % HEAD_DIM
    second_half = (lane >= ROT_DIM // 2) & (lane < ROT_DIM)

    def scatter(j, jj, refs):
        for dil, ref in refs:
            if dil == 1:
                ref[:, _slab(jj)] = sc_ref[j].astype(BF16)
            else:
                for r in range(dil):
                    ref[r, :, _slab(jj)] = sc_ref[j, pl.ds(r, tm // dil, stride=dil), :].astype(BF16)

    q_refs = ((1, q1_ref),), ((4, q4_ref),), ((16, q16_ref),)
    for j in range(qk_slabs):
        t = pqk[:, _slab(j)]
        partner = jnp.where(second_half,
                            pltpu.roll(t, ROT_DIM // 2, axis=1),
                            pltpu.roll(t, LANES - ROT_DIM // 2, axis=1))
        r = t * ct + partner * st
        grp, jj = divmod(j, GROUP_SLABS)
        if grp < 3:
            r = r * (math.log2(math.e) / math.sqrt(HEAD_DIM))
        sc_ref[j] = r
        scatter(j, jj, q_refs[grp] if grp < 3 else ((1, k1_ref), (4, k4_ref), (16, k16_ref)))

    pvu = jnp.dot(hb, w_ref[:, qk_slabs * LANES:], preferred_element_type=F32)
    for j in range(pvu.shape[1] // LANES):
        sc_ref[qk_slabs + j] = pvu[:, _slab(j)]
    for jj in range(GROUP_SLABS):
        scatter(qk_slabs + jj, jj, ((1, v1_ref), (4, v4_ref), (16, v16_ref)))

    u0 = qk_slabs + GROUP_SLABS
    for gb in range(uc_ref.shape[0] // SLOTS):
        pieces = [[[None, None] for _ in range(tm // (SSM_CHUNK * SUBLANES))] for _ in range(SLOTS)]
        for cb in range(tm // (SSM_CHUNK * SUBLANES)):
            for half in range(2):
                vs = [sc_ref[u0 + gb, pl.ds(cb * SSM_CHUNK * SUBLANES + half * SLOTS + s, SUBLANES,
                                            stride=SSM_CHUNK), :] for s in range(SLOTS)]
                for g, o in enumerate(_slot_transpose(vs)):
                    pieces[g][cb][half] = o
        for g in range(SLOTS):
            full = jnp.concatenate([jnp.concatenate(p, axis=1) for p in pieces[g]], axis=0)
            uc_ref[gb * SLOTS + g] = full.astype(BF16)


def _inproj(x2, mod, g_pre, w_in_b, ctab, stab, bsz, seq):
    t_tok, d = x2.shape
    in_w = w_in_b.shape[1]
    ssm_w = in_w - 5 * GROUP_WIDTH
    groups = ssm_w // SSM_P
    tm = TOK_TILE
    per_b = seq // tm
    tok = lambda w: pl.BlockSpec((tm, w), lambda i: (i, 0))

    def res_spec(dil):
        return pl.BlockSpec((None, dil, tm // dil, GROUP_WIDTH), lambda i: (i // per_b, 0, i % per_b, 0))

    def res_shape(dil):
        return jax.ShapeDtypeStruct((bsz, dil, seq // dil, GROUP_WIDTH), BF16)

    nat = jax.ShapeDtypeStruct((t_tok, GROUP_WIDTH), BF16)
    out_shape = [nat, res_shape(4), res_shape(16)] * 3 + \
                [jax.ShapeDtypeStruct((groups, t_tok // SSM_CHUNK, CHUNK_W), BF16)]
    out_specs = [tok(GROUP_WIDTH), res_spec(4), res_spec(16)] * 3 + \
                [pl.BlockSpec((groups, tm // SSM_CHUNK, CHUNK_W), lambda i: (0, i, 0))]
    return pl.pallas_call(
        functools.partial(_inproj_kernel, tm=tm),
        grid=(t_tok // tm,),
        in_specs=[tok(d),
                  pl.BlockSpec((None, N_MOD, d), lambda i: (i // per_b, 0, 0)),
                  _resident((1, d)),
                  _resident((d, in_w)),
                  tok(LANES), tok(LANES)],
        out_specs=out_specs,
        out_shape=out_shape,
        scratch_shapes=[pltpu.VMEM((in_w // LANES, tm, LANES), F32)],
        compiler_params=_cparams(("parallel",), 56),
        name="inproj",
    )(x2, mod, g_pre, w_in_b, ctab, stab)


def _stat_lanes(h):
    a, b = slice(h, h + 1), slice(HEAD_DIM + h, HEAD_DIM + h + 1)
    return (a, b) if h % 2 == 0 else (b, a)


def _attn_kernel(q_ref, k_ref, v_ref, o_ref, s_ref, sc_ref, pr_ref, mk_ref, *, nblk, units):
    qi = lax.broadcasted_iota(jnp.int32, (ATT_BLK, 2 * ATT_BLK), 0)
    kj = lax.broadcasted_iota(jnp.int32, (ATT_BLK, 2 * ATT_BLK), 1)
    rel = qi - kj
    mk_ref[0] = jnp.where((rel >= 0) & (rel <= ATT_BLK), 0.0, NEG)
    mk_ref[1] = jnp.where((rel + ATT_BLK >= 0) & (rel <= 0), 0.0, NEG)
    low = lax.broadcasted_iota(jnp.int32, (ATT_BLK, LANES), 1) < HEAD_DIM
    low_k = lax.broadcasted_iota(jnp.int32, (2 * ATT_BLK, LANES), 1) < HEAD_DIM

    def coords(u):
        u = jnp.minimum(u, units - 1)
        r, i = u // nblk, u % nblk
        rows = pl.ds(pl.multiple_of(i * ATT_BLK, ATT_BLK), ATT_BLK)
        krows = pl.ds(pl.multiple_of(jnp.maximum(i - 1, 0) * ATT_BLK, ATT_BLK), 2 * ATT_BLK)
        return r, i, rows, krows

    def scores(u, slot):
        r, i, rows, krows = coords(u)
        bias = mk_ref[jnp.minimum(i, 1)]
        for h in range(HEADS_PER_GROUP):
            jj, par = divmod(h, 2)
            q2 = q_ref[r, rows, _slab(jj)]
            mine = low if par == 0 else jnp.logical_not(low)
            qm = jnp.where(mine, q2, jnp.zeros_like(q2))
            s = lax.dot_general(qm, k_ref[r, krows, _slab(jj)], (((1,), (1,)), ((), ())),
                                preferred_element_type=F32)
            sc_ref[slot, h] = s + bias

    def softmax(u, slot):
        r, _, rows, _ = coords(u)
        s_ref[r, rows, :] = jnp.zeros((ATT_BLK, LANES), F32)
        for h in range(HEADS_PER_GROUP):
            s = sc_ref[slot, h]
            m = jnp.max(s, axis=1, keepdims=True)
            pr_ref[slot, h] = jnp.exp2(s - m).astype(BF16)
            s_ref[r, rows, _stat_lanes(h)[0]] = m

    def values(u, slot):
        r, _, rows, krows = coords(u)
        for jj in range(GROUP_SLABS):
            v2 = v_ref[r, krows, _slab(jj)]
            outs = []
            for par in range(2):
                h = 2 * jj + par
                mine_k = low_k if par == 0 else jnp.logical_not(low_k)
                v_ext = jnp.where(mine_k, v2, jnp.ones_like(v2))
                oe = jnp.dot(pr_ref[slot, h], v_ext, preferred_element_type=F32)
                outs.append(oe)
                l_sl = _stat_lanes(h)[1]
                s_ref[r, rows, l_sl] = oe[:, l_sl]
            o_ref[r, rows, _slab(jj)] = jnp.where(low, outs[0], outs[1]).astype(o_ref.dtype)

    scores(0, 0)
    softmax(0, 0)
    scores(1, 1)

    def body(t, carry):
        u = 2 * t
        scores(u + 2, 0)
        softmax(u + 1, 1)
        values(u, 0)
        scores(u + 3, 1)
        softmax(u + 2, 0)
        values(u + 1, 1)
        return carry

    assert units % 2 == 0
    lax.fori_loop(0, units // 2, body, 0)


def _attn_group(q, k, v, dilation, bsz, seq):
    n = seq // dilation
    nblk = n // ATT_BLK
    spec = lambda w: pl.BlockSpec((None, dilation, n, w), lambda b: (b, 0, 0, 0))
    win = (ATT_BLK, 2 * ATT_BLK)
    return pl.pallas_call(
        functools.partial(_attn_kernel, nblk=nblk, units=dilation * nblk),
        grid=(bsz,),
        in_specs=[spec(GROUP_WIDTH)] * 3,
        out_specs=[spec(GROUP_WIDTH), spec(LANES)],
        out_shape=[jax.ShapeDtypeStruct((bsz, dilation, n, GROUP_WIDTH), BF16),
                   jax.ShapeDtypeStruct((bsz, dilation, n, LANES), F32)],
        scratch_shapes=[pltpu.VMEM((2, HEADS_PER_GROUP) + win, F32),
                        pltpu.VMEM((2, HEADS_PER_GROUP) + win, BF16),
                        pltpu.VMEM((2,) + win, F32)],
        compiler_params=_cparams(("parallel",), 48),
        name=f"attn_d{dilation}",
    )(q, k, v)


def _cmul(a, b):
    return a[0] * b[0] - a[1] * b[1], a[0] * b[1] + a[1] * b[0]


def _ssm_tables(a_re, a_im, log_dt, b_re, b_im, c_re, c_im, d_skip):
    hp = lax.Precision.HIGHEST
    f = lambda t: t.astype(F32)
    lam = (f(a_re), f(a_im))
    dt = jnp.exp(f(log_dt))[:, None]
    g = a_re.shape[0]
    taus = jnp.arange(2 * SSM_CHUNK + 1, dtype=F32)[None, :, None]
    mag = jnp.exp(lam[0][:, None, :] * dt[:, None, :] * taus)
    ang = lam[1][:, None, :] * dt[:, None, :] * taus
    apow = (mag * jnp.cos(ang), mag * jnp.sin(ang))
    a_bar = (apow[0][:, 1], apow[1][:, 1])
    den = lam[0] * lam[0] + lam[1] * lam[1]
    coef = _cmul((a_bar[0] - 1.0, a_bar[1]), (lam[0] / den, -lam[1] / den))
    b_bar = _cmul((coef[0][..., None], coef[1][..., None]), (f(b_re), f(b_im)))
    c_mat = (f(c_re), f(c_im))

    ca = _cmul((c_mat[0][:, None], c_mat[1][:, None]),
               (apow[0][:, :SSM_CHUNK, None, :], apow[1][:, :SSM_CHUNK, None, :]))
    kern = jnp.einsum('gtpn,gnq->gtpq', ca[0], b_bar[0], precision=hp) - \
        jnp.einsum('gtpn,gnq->gtpq', ca[1], b_bar[1], precision=hp)
    kern = kern.at[:, 0].add(f(d_skip)[:, :, None] * jnp.eye(SSM_P, dtype=F32))
    m_row = kern.transpose(0, 3, 1, 2).reshape(g, SSM_P, CHUNK_W)

    rev = (apow[0][:, SSM_CHUNK - 1::-1][:, :SSM_CHUNK], apow[1][:, SSM_CHUNK - 1::-1][:, :SSM_CHUNK])
    bt = (b_bar[0].transpose(0, 2, 1)[:, None], b_bar[1].transpose(0, 2, 1)[:, None])
    w4 = _cmul((rev[0][:, :, None, :], rev[1][:, :, None, :]), bt)
    w_re, w_im = (t.reshape(g, CHUNK_W, SSM_N) for t in w4)
    w_mat = jnp.concatenate([w_re, w_im, w_im, w_re], axis=-1)

    ct = (c_mat[0].transpose(0, 2, 1)[:, :, None, :], c_mat[1].transpose(0, 2, 1)[:, :, None, :])
    at = (apow[0][:, 1:SSM_CHUNK + 1].transpose(0, 2, 1)[..., None],
          apow[1][:, 1:SSM_CHUNK + 1].transpose(0, 2, 1)[..., None])
    v4 = _cmul(ct, at)
    v_re, v_im = (t.reshape(g, SSM_N, CHUNK_W) for t in v4)
    v_mat = jnp.concatenate([v_re, -v_im], axis=1)

    def mul_rows(tau):
        re, im = apow[0][:, tau], apow[1][:, tau]
        return [jnp.concatenate([re, re], -1), jnp.concatenate([-im, im], -1),
                jnp.concatenate([im, -im], -1)]
    ptab = jnp.stack(mul_rows(SSM_CHUNK) + mul_rows(2 * SSM_CHUNK) +
                     [jnp.zeros((g, LANES), F32)] * 2, axis=1)
    return m_row, w_mat.astype(BF16), v_mat.astype(BF16), ptab


def _toeplitz(m_row):
    lane = lax.broadcasted_iota(jnp.int32, m_row.shape, 1)
    blocks = [m_row]
    for s in range(1, SSM_CHUNK):
        blocks.append(jnp.where(lane >= s * SSM_P, pltpu.roll(m_row, s * SSM_P, axis=1), 0.0))
    return jnp.concatenate(blocks, axis=0).astype(BF16)


def _ssm_kernel(u_ref, m_ref, w_ref, v_ref, p_ref, y_ref, gx_ref, gz_ref, h_ref, *, groups, nchunk, bsz):
    rows = nchunk * bsz
    row = lax.broadcasted_iota(jnp.int32, (rows, LANES), 0)
    first_chunk = (row % nchunk) == 0

    def prev_chunk(a):
        return jnp.where(first_chunk, 0.0, pltpu.roll(a, 1, axis=0))

    for g in range(groups):
        e = jnp.dot(u_ref[g], w_ref[g], preferred_element_type=F32)
        ex, ez = e[:, :LANES], e[:, LANES:]
        exs, ezs = prev_chunk(ex), prev_chunk(ez)
        p1, p2, p3 = p_ref[g, 0:1, :], p_ref[g, 1:2, :], p_ref[g, 2:3, :]
        gx = ex + p1 * exs + p2 * ezs
        gz = ez + p1 * ezs + p3 * exs
        for b in range(bsz):
            gx_ref[g, pl.ds(b, nchunk, stride=bsz), :] = gx[b * nchunk:(b + 1) * nchunk]
            gz_ref[g, pl.ds(b, nchunk, stride=bsz), :] = gz[b * nchunk:(b + 1) * nchunk]

    def step(j, carry):
        r0 = pl.multiple_of(j * SUBLANES, SUBLANES)
        new = []
        for g in range(groups):
            x, z = carry[g]
            q1, q2, q3 = p_ref[g, 3:4, :], p_ref[g, 4:5, :], p_ref[g, 5:6, :]
            xn = q1 * x + q2 * z + gx_ref[g, pl.ds(r0, SUBLANES), :]
            zn = q1 * z + q3 * x + gz_ref[g, pl.ds(r0, SUBLANES), :]
            h_ref[g, pl.ds(r0, SUBLANES), :] = xn
            new.append((xn, zn))
        return tuple(new)

    zero = jnp.zeros((SUBLANES, LANES), F32)
    lax.fori_loop(0, rows // SUBLANES, step, tuple((zero, zero) for _ in range(groups)))

    for g in range(groups):
        h_end = jnp.concatenate([h_ref[g, pl.ds(b, nchunk, stride=bsz), :] for b in range(bsz)], axis=0)
        h_in = prev_chunk(h_end).astype(BF16)
        y = jnp.dot(u_ref[g], _toeplitz(m_ref[g]), preferred_element_type=F32)
        y = y + jnp.dot(h_in, v_ref[g], preferred_element_type=F32)
        y_ref[g] = y.astype(y_ref.dtype)


def _ssm(uc, tables, bsz, seq):
    m_row, w_mat, v_mat, ptab = tables
    g, rows, _ = uc.shape
    nchunk = seq // SSM_CHUNK
    assert rows == nchunk * bsz and SUBLANES % bsz == 0 and SUBLANES // bsz == 2
    gs = 4
    blk = lambda a, b: pl.BlockSpec((gs, a, b), lambda i: (i, 0, 0))
    return pl.pallas_call(
        functools.partial(_ssm_kernel, groups=gs, nchunk=nchunk, bsz=bsz),
        grid=(g // gs,),
        in_specs=[blk(rows, CHUNK_W), blk(SSM_P, CHUNK_W), blk(CHUNK_W, CHUNK_W),
                  blk(LANES, CHUNK_W), blk(SUBLANES, LANES)],
        out_specs=blk(rows, CHUNK_W),
        out_shape=jax.ShapeDtypeStruct((g, rows, CHUNK_W), BF16),
        scratch_shapes=[pltpu.VMEM((gs, rows, LANES), F32)] * 3,
        compiler_params=_cparams(("parallel",), 48),
        name="ssm",
    )(uc, m_row, w_mat, v_mat, ptab)


def _rms(x):
    return x * lax.rsqrt(jnp.mean(x * x, axis=-1, keepdims=True) + EPS)


def _mix_kernel(o1_ref, l1_ref, o4_ref, l4_ref, o16_ref, l16_ref, y_ref, x_ref, mod_ref,
                ga_ref, gs_ref, gp_ref, ex_ref, wg_ref, bg_ref, wo_ref, out_ref,
                so_ref, sl_ref, sy_ref, *, tm):
    for gi, (dil, o_ref, l_ref) in enumerate(((4, o4_ref, l4_ref), (16, o16_ref, l16_ref))):
        for r in range(dil):
            rows = pl.ds(r, tm // dil, stride=dil)
            for jj in range(GROUP_SLABS):
                so_ref[gi * GROUP_SLABS + jj, rows, :] = o_ref[r, :, _slab(jj)].astype(F32)
            sl_ref[gi, rows, :] = l_ref[r]

    lane = lax.broadcasted_iota(jnp.int32, (tm, LANES), 1)
    even = (lane & 1) == 0
    ms, ls = [], []
    for st in (l1_ref[...], sl_ref[0], sl_ref[1]):
        sw = pltpu.roll(st, HEAD_DIM, axis=1)
        ms.append(jnp.where(even, st, sw))
        ls.append(jnp.where(even, sw, st))
    mx = jnp.maximum(jnp.maximum(ms[0], ms[1]), ms[2])
    es = [jnp.exp2(m - mx) for m in ms]
    inv = 1.0 / (es[0] * ls[0] + es[1] * ls[1] + es[2] * ls[2])
    head_lane = lane < HEADS_PER_GROUP
    wexp = [jnp.dot(jnp.where(head_lane, e * inv, 0.0).astype(BF16), ex_ref[...],
                    preferred_element_type=F32) for e in es]
    att_slabs = []
    for jj in range(GROUP_SLABS):
        a = wexp[0][:, _slab(jj)] * o1_ref[:, _slab(jj)].astype(F32)
        a = a + wexp[1][:, _slab(jj)] * so_ref[jj]
        a = a + wexp[2][:, _slab(jj)] * so_ref[GROUP_SLABS + jj]
        att_slabs.append(a)
    att = _rms(jnp.concatenate(att_slabs, axis=1)) * ga_ref[...]

    for gb in range(y_ref.shape[0] // SLOTS):
        ygs = [y_ref[gb * SLOTS + g].astype(F32) for g in range(SLOTS)]
        for cb in range(tm // (SSM_CHUNK * SUBLANES)):
            for half in range(2):
                vs = [yg[cb * SUBLANES:(cb + 1) * SUBLANES, _slab(half)] for yg in ygs]
                for s, o in enumerate(_slot_transpose(vs)):
                    sy_ref[gb, pl.ds(cb * SSM_CHUNK * SUBLANES + half * SLOTS + s, SUBLANES,
                                     stride=SSM_CHUNK), :] = o
    yv = jnp.concatenate([sy_ref[gb] for gb in range(sy_ref.shape[0])], axis=1)

    s = jax.nn.gelu(yv)
    z = jnp.dot(s.astype(BF16), wg_ref[...], preferred_element_type=F32) + bg_ref[...]
    s = s * jax.nn.sigmoid(z)
    ssm = _rms(s) * gs_ref[...]
    aw = att.shape[1]
    mix = jnp.dot(att.astype(BF16), wo_ref[:aw, :], preferred_element_type=F32)
    mix = mix + jnp.dot(ssm.astype(BF16), wo_ref[aw:, :], preferred_element_type=F32)
    out_ref[...] = x_ref[...] + mod_ref[2:3, :] * (_rms(mix) * gp_ref[...])


def _mix(o1, l1, o4, l4, o16, l16, yc, x2, mod, g_attn, g_ssm, g_post, w_glu_b, b_glu, w_out_b, bsz, seq):
    t_tok, d = x2.shape
    groups = yc.shape[0]
    sw = groups * SSM_P
    aw = GROUP_WIDTH
    tm = TOK_TILE
    per_b = seq // tm
    tok = lambda w: pl.BlockSpec((tm, w), lambda i: (i, 0))
    res = lambda dil, w: pl.BlockSpec((None, dil, tm // dil, w), lambda i: (i // per_b, 0, i % per_b, 0))
    head_of_lane = jnp.arange(aw)[None, :] // HEAD_DIM
    expand = (jnp.arange(LANES)[:, None] == head_of_lane).astype(BF16)
    return pl.pallas_call(
        functools.partial(_mix_kernel, tm=tm),
        grid=(t_tok // tm,),
        in_specs=[tok(aw), tok(LANES), res(4, aw), res(4, LANES), res(16, aw), res(16, LANES),
                  pl.BlockSpec((groups, tm // SSM_CHUNK, CHUNK_W), lambda i: (0, i, 0)),
                  tok(d),
                  pl.BlockSpec((None, N_MOD, d), lambda i: (i // per_b, 0, 0)),
                  _resident((1, aw)), _resident((1, sw)), _resident((1, d)),
                  _resident((LANES, aw)),
                  _resident((sw, sw)), _resident((1, sw)), _resident((aw + sw, d))],
        out_specs=tok(d),
        out_shape=jax.ShapeDtypeStruct((t_tok, d), F32),
        scratch_shapes=[pltpu.VMEM((2 * GROUP_SLABS, tm, LANES), F32),
                        pltpu.VMEM((2, tm, LANES), F32),
                        pltpu.VMEM((sw // LANES, tm, LANES), F32)],
        compiler_params=_cparams(("parallel",), 52),
        name="mix",
    )(o1, l1, o4, l4, o16, l16, yc, x2, mod, g_attn, g_ssm, g_post, expand, w_glu_b, b_glu, w_out_b)


def _mlp_kernel(x_ref, mod_ref, gpre_ref, gpost_ref, w1_ref, w2_ref, o_ref, h_ref):
    f = pl.program_id(1)

    @pl.when(f == 0)
    def _():
        scale = gpre_ref[...] * (1.0 + mod_ref[4:5, :])
        h_ref[...] = (_rms(x_ref[...]) * scale + mod_ref[3:4, :]).astype(BF16)
        o_ref[...] = jnp.zeros_like(o_ref)

    t = jnp.dot(h_ref[...], w1_ref[...], preferred_element_type=F32)
    t = jnp.square(jnp.maximum(t, 0.0)).astype(BF16)
    o_ref[...] += jnp.dot(t, w2_ref[...], preferred_element_type=F32)

    @pl.when(f == pl.num_programs(1) - 1)
    def _():
        o_ref[...] = x_ref[...] + mod_ref[5:6, :] * (_rms(o_ref[...]) * gpost_ref[...])


def _mlp(x2, mod, g_pre, g_post, w1_b, w2_b, seq):
    t_tok, d = x2.shape
    d_ff = w1_b.shape[1]
    tm, tf = 1024, 512
    per_b = seq // tm
    return pl.pallas_call(
        _mlp_kernel,
        grid=(t_tok // tm, d_ff // tf),
        in_specs=[pl.BlockSpec((tm, d), lambda i, f: (i, 0)),
                  pl.BlockSpec((None, N_MOD, d), lambda i, f: (i // per_b, 0, 0)),
                  _resident((1, d)), _resident((1, d)),
                  pl.BlockSpec((d, tf), lambda i, f: (0, f)),
                  pl.BlockSpec((tf, d), lambda i, f: (f, 0))],
        out_specs=pl.BlockSpec((tm, d), lambda i, f: (i, 0)),
        out_shape=jax.ShapeDtypeStruct((t_tok, d), F32),
        scratch_shapes=[pltpu.VMEM((tm, d), BF16)],
        compiler_params=_cparams(("parallel", "arbitrary"), 56),
        name="mlp",
    )(x2, mod, g_pre, g_post, w1_b, w2_b)


def _rope_tables(positions):
    half = ROT_DIM // 2
    freqs = ROPE_THETA ** (-jnp.arange(0, ROT_DIM, 2, dtype=F32) / ROT_DIM)
    ang = positions.reshape(-1).astype(F32)[:, None] * freqs
    cos, sin = jnp.cos(ang), jnp.sin(ang)
    t_tok = ang.shape[0]
    pad1 = jnp.ones((t_tok, HEAD_DIM - ROT_DIM), F32)
    pad0 = jnp.zeros((t_tok, HEAD_DIM - ROT_DIM), F32)
    assert cos.shape[1] == half
    ct = jnp.concatenate([cos, cos, pad1], axis=1)
    st = jnp.concatenate([-sin, sin, pad0], axis=1)
    return jnp.tile(ct, (1, LANES // HEAD_DIM)), jnp.tile(st, (1, LANES // HEAD_DIM))


def kernel(x, c, positions, w_ada, b_ada, g_pre_mix, g_post_mix, w_in, ssm_a_re, ssm_a_im, ssm_log_dt,
           ssm_b_re, ssm_b_im, ssm_c_re, ssm_c_im, ssm_d, w_glu, b_glu, g_attn_out, g_ssm_out, w_out,
           g_pre_mlp, g_post_mlp, w_mlp_in, w_mlp_out):
    bsz, seq, d = x.shape
    depth = w_ada.shape[0]
    x2 = x.reshape(bsz * seq, d)
    ctab, stab = _rope_tables(positions)
    row = lambda a: a.reshape(1, -1).astype(F32)
    for l in range(depth):
        mod = _adaln(c, w_ada[l], b_ada[l])
        q1, q4, q16, k1, k4, k16, v1, v4, v16, uc = _inproj(
            x2, mod, row(g_pre_mix[l]), w_in[l].astype(BF16), ctab, stab, bsz, seq)

        nat = lambda a: a.reshape(bsz, 1, seq, a.shape[-1])
        o1, l1 = _attn_group(nat(q1), nat(k1), nat(v1), 1, bsz, seq)
        o4, l4 = _attn_group(q4, k4, v4, 4, bsz, seq)
        o16, l16 = _attn_group(q16, k16, v16, 16, bsz, seq)

        tables = _ssm_tables(ssm_a_re[l], ssm_a_im[l], ssm_log_dt[l], ssm_b_re[l], ssm_b_im[l],
                             ssm_c_re[l], ssm_c_im[l], ssm_d[l])
        yc = _ssm(uc, tables, bsz, seq)

        x2 = _mix(o1.reshape(bsz * seq, -1), l1.reshape(bsz * seq, -1), o4, l4, o16, l16, yc, x2, mod,
                  row(g_attn_out[l]), row(g_ssm_out[l]), row(g_post_mix[l]),
                  w_glu[l].astype(BF16), row(b_glu[l]), w_out[l].astype(BF16), bsz, seq)
        x2 = _mlp(x2, mod, row(g_pre_mlp[l]), row(g_post_mlp[l]),
                  w_mlp_in[l].astype(BF16), w_mlp_out[l].astype(BF16), seq)
    return x2.reshape(bsz, seq, d)
```

```python
import functools
import math

import jax
import jax.numpy as jnp
from jax import lax
from jax.experimental import pallas as pl
from jax.experimental.pallas import tpu as pltpu

F32 = jnp.float32
BF16 = jnp.bfloat16

LANES = 128
SUBLANES = 8

HEAD_DIM = 64
ROT_DIM = HEAD_DIM // 4
ROPE_THETA = 500000.0
DILATIONS = (1, 4, 16)
HEADS_PER_GROUP = 6
GROUP_WIDTH = HEADS_PER_GROUP * HEAD_DIM
GROUP_SLABS = GROUP_WIDTH // LANES
ATT_BLK = 128
SSM_P = 16
SSM_N = 64
SSM_CHUNK = 16
CHUNK_W = SSM_CHUNK * SSM_P
SLOTS = LANES // SSM_P
N_MOD = 6
EPS = 1e-6
NEG = -1e30
TOK_TILE = 512
ROW_CHUNK = 128


def _cparams(sem, vmem_mb):
    return pltpu.CompilerParams(dimension_semantics=sem, vmem_limit_bytes=vmem_mb * 1024 * 1024)


def _resident(shape):
    nd = len(shape)
    return pl.BlockSpec(shape, lambda *_: (0,) * nd, pipeline_mode=pl.Buffered(1))


def _slab(j):
    return slice(j * LANES, (j + 1) * LANES)


def _slot_transpose(vs):
    slot = lax.broadcasted_iota(jnp.int32, vs[0].shape, 1) // SSM_P
    vs = list(vs)
    for dist in (4, 2, 1):
        low = (slot & dist) == 0
        nxt = list(vs)
        for i in range(SLOTS):
            if i & dist == 0:
                a, b = vs[i], vs[i + dist]
                nxt[i] = jnp.where(low, a, pltpu.roll(b, dist * SSM_P, axis=1))
                nxt[i + dist] = jnp.where(low, pltpu.roll(a, LANES - dist * SSM_P, axis=1), b)
        vs = nxt
    return vs


def _adaln_kernel(s_ref, w_ref, b_ref, o_ref):
    o_ref[...] = jnp.dot(s_ref[...], w_ref[...].astype(BF16),
                         preferred_element_type=F32) + b_ref[...]


def _adaln(c, w_ada, b_ada):
    bsz, d = c.shape
    n = w_ada.shape[1]
    tn = 1024
    s = jnp.zeros((SUBLANES, d), BF16).at[:bsz].set(jax.nn.silu(c).astype(BF16))
    out = pl.pallas_call(
        _adaln_kernel,
        grid=(n // tn,),
        in_specs=[pl.BlockSpec((SUBLANES, d), lambda j: (0, 0)),
                  pl.BlockSpec((d, tn), lambda j: (0, j)),
                  pl.BlockSpec((1, tn), lambda j: (0, j))],
        out_specs=pl.BlockSpec((SUBLANES, tn), lambda j: (0, j)),
        out_shape=jax.ShapeDtypeStruct((SUBLANES, n), F32),
        compiler_params=_cparams(("arbitrary",), 40),
        name="adaln",
    )(s, w_ada, b_ada.reshape(1, n))
    return out[:bsz].reshape(bsz, N_MOD, d)


def _inproj_kernel(x_ref, mod_ref, g_ref, w_ref, ct_ref, st_ref,
                   q1_ref, q4_ref, q16_ref, k1_ref, k4_ref, k16_ref, v1_ref, v4_ref, v16_ref,
                   uc_ref, sc_ref, *, tm):
    x = x_ref[...]
    ms = jnp.mean(x * x, axis=-1, keepdims=True)
    scale = g_ref[...] * (1.0 + mod_ref[1:2, :])
    h = (x * lax.rsqrt(ms + EPS)) * scale + mod_ref[0:1, :]
    hb = h.astype(BF16)

    qk_slabs = 4 * GROUP_SLABS

    def scatter(j, jj, refs):
        for dil, ref in refs:
            if dil == 1:
                ref[:, _slab(jj)] = sc_ref[j].astype(BF16)
            else:
                for r in range(dil):
                    ref[r, :, _slab(jj)] = sc_ref[j, pl.ds(r, tm // dil, stride=dil), :].astype(BF16)

    pvu = jnp.dot(hb, w_ref[:, qk_slabs * LANES:], preferred_element_type=F32)
    for j in range(pvu.shape[1] // LANES):
        sc_ref[qk_slabs + j] = pvu[:, _slab(j)]
    for jj in range(GROUP_SLABS):
        scatter(qk_slabs + jj, jj, ((1, v1_ref), (4, v4_ref), (16, v16_ref)))

    u0 = qk_slabs + GROUP_SLABS
    for gb in range(uc_ref.shape[0] // SLOTS):
        pieces = [[[None, None] for _ in range(tm // (SSM_CHUNK * SUBLANES))] for _ in range(SLOTS)]
        for cb in range(tm // (SSM_CHUNK * SUBLANES)):
            for half in range(2):
                vs = [sc_ref[u0 + gb, pl.ds(cb * SSM_CHUNK * SUBLANES + half * SLOTS + s, SUBLANES,
                                            stride=SSM_CHUNK), :] for s in range(SLOTS)]
                for g, o in enumerate(_slot_transpose(vs)):
                    pieces[g][cb][half] = o
        for g in range(SLOTS):
            full = jnp.concatenate([jnp.concatenate(p, axis=1) for p in pieces[g]], axis=0)
            uc_ref[gb * SLOTS + g] = full.astype(BF16)

    ct = ct_ref[...]
    st = st_ref[...]
    lane = lax.broadcasted_iota(jnp.int32, ct.shape, 1) % HEAD_DIM
    second_half = (lane >= ROT_DIM // 2) & (lane < ROT_DIM)
    q_refs = ((1, q1_ref),), ((4, q4_ref),), ((16, q16_ref),)
    chunk_slabs = 4
    for c0 in reversed(range(0, qk_slabs, chunk_slabs)):
        pc = jnp.dot(hb, w_ref[:, c0 * LANES:(c0 + chunk_slabs) * LANES], preferred_element_type=F32)
        for s in range(chunk_slabs):
            j = c0 + s
            grp, jj = divmod(j, GROUP_SLABS)
            t = pc[:, _slab(s)]
            partner = jnp.where(second_half,
                                pltpu.roll(t, ROT_DIM // 2, axis=1),
                                pltpu.roll(t, LANES - ROT_DIM // 2, axis=1))
            r = t * ct + partner * st
            if grp < 3:
                r = r * (math.log2(math.e) / math.sqrt(HEAD_DIM))
            sc_ref[j] = r
            scatter(j, jj, q_refs[grp] if grp < 3 else ((1, k1_ref), (4, k4_ref), (16, k16_ref)))


def _inproj(x2, mod, g_pre, w_in_b, ctab, stab, bsz, seq):
    t_tok, d = x2.shape
    in_w = w_in_b.shape[1]
    ssm_w = in_w - 5 * GROUP_WIDTH
    groups = ssm_w // SSM_P
    tm = TOK_TILE
    per_b = seq // tm
    tok = lambda w: pl.BlockSpec((tm, w), lambda i: (i, 0))

    def res_spec(dil):
        return pl.BlockSpec((None, dil, tm // dil, GROUP_WIDTH), lambda i: (i // per_b, 0, i % per_b, 0))

    def res_shape(dil):
        return jax.ShapeDtypeStruct((bsz, dil, seq // dil, GROUP_WIDTH), BF16)

    nat = jax.ShapeDtypeStruct((t_tok, GROUP_WIDTH), BF16)
    out_shape = [nat, res_shape(4), res_shape(16)] * 3 + \
                [jax.ShapeDtypeStruct((groups, t_tok // SSM_CHUNK, CHUNK_W), BF16)]
    out_specs = [tok(GROUP_WIDTH), res_spec(4), res_spec(16)] * 3 + \
                [pl.BlockSpec((groups, tm // SSM_CHUNK, CHUNK_W), lambda i: (0, i, 0))]
    return pl.pallas_call(
        functools.partial(_inproj_kernel, tm=tm),
        grid=(t_tok // tm,),
        in_specs=[tok(d),
                  pl.BlockSpec((None, N_MOD, d), lambda i: (i // per_b, 0, 0)),
                  _resident((1, d)),
                  _resident((d, in_w)),
                  tok(LANES), tok(LANES)],
        out_specs=out_specs,
        out_shape=out_shape,
        scratch_shapes=[pltpu.VMEM((in_w // LANES, tm, LANES), F32)],
        compiler_params=_cparams(("parallel",), 56),
        name="inproj",
    )(x2, mod, g_pre, w_in_b, ctab, stab)


def _stat_lanes(h):
    a, b = slice(h, h + 1), slice(HEAD_DIM + h, HEAD_DIM + h + 1)
    return (a, b) if h % 2 == 0 else (b, a)


def _attn_kernel(q_ref, k_ref, v_ref, o_ref, s_ref, sc_ref, pr_ref, mk_ref, *, nblk, units):
    qi = lax.broadcasted_iota(jnp.int32, (ATT_BLK, 2 * ATT_BLK), 0)
    kj = lax.broadcasted_iota(jnp.int32, (ATT_BLK, 2 * ATT_BLK), 1)
    rel = qi - kj
    mk_ref[0] = jnp.where((rel >= 0) & (rel <= ATT_BLK), 0.0, NEG)
    mk_ref[1] = jnp.where((rel + ATT_BLK >= 0) & (rel <= 0), 0.0, NEG)
    low = lax.broadcasted_iota(jnp.int32, (ATT_BLK, LANES), 1) < HEAD_DIM
    low_k = lax.broadcasted_iota(jnp.int32, (2 * ATT_BLK, LANES), 1) < HEAD_DIM

    def coords(u):
        u = jnp.minimum(u, units - 1)
        r, i = u // nblk, u % nblk
        rows = pl.ds(pl.multiple_of(i * ATT_BLK, ATT_BLK), ATT_BLK)
        krows = pl.ds(pl.multiple_of(jnp.maximum(i - 1, 0) * ATT_BLK, ATT_BLK), 2 * ATT_BLK)
        return r, i, rows, krows

    def scores(u, slot):
        r, i, rows, krows = coords(u)
        bias = mk_ref[jnp.minimum(i, 1)]
        for h in range(HEADS_PER_GROUP):
            jj, par = divmod(h, 2)
            q2 = q_ref[r, rows, _slab(jj)]
            mine = low if par == 0 else jnp.logical_not(low)
            qm = jnp.where(mine, q2, jnp.zeros_like(q2))
            s = lax.dot_general(qm, k_ref[r, krows, _slab(jj)], (((1,), (1,)), ((), ())),
                                preferred_element_type=F32)
            sc_ref[slot, h] = s + bias

    def softmax(u, slot):
        r, _, rows, _ = coords(u)
        s_ref[r, rows, :] = jnp.zeros((ATT_BLK, LANES), F32)
        for h in range(HEADS_PER_GROUP):
            s = sc_ref[slot, h]
            m = jnp.max(s, axis=1, keepdims=True)
            pr_ref[slot, h] = jnp.exp2(s - m).astype(BF16)
            s_ref[r, rows, _stat_lanes(h)[0]] = m

    def values(u, slot):
        r, _, rows, krows = coords(u)
        for jj in range(GROUP_SLABS):
            v2 = v_ref[r, krows, _slab(jj)]
            outs = []
            for par in range(2):
                h = 2 * jj + par
                mine_k = low_k if par == 0 else jnp.logical_not(low_k)
                v_ext = jnp.where(mine_k, v2, jnp.ones_like(v2))
                oe = jnp.dot(pr_ref[slot, h], v_ext, preferred_element_type=F32)
                outs.append(oe)
                l_sl = _stat_lanes(h)[1]
                s_ref[r, rows, l_sl] = oe[:, l_sl]
            o_ref[r, rows, _slab(jj)] = jnp.where(low, outs[0], outs[1]).astype(o_ref.dtype)

    scores(0, 0)
    softmax(0, 0)
    scores(1, 1)

    def body(t, carry):
        u = 2 * t
        scores(u + 2, 0)
        softmax(u + 1, 1)
        values(u, 0)
        scores(u + 3, 1)
        softmax(u + 2, 0)
        values(u + 1, 1)
        return carry

    assert units % 2 == 0
    lax.fori_loop(0, units // 2, body, 0)


def _attn_group(q, k, v, dilation, bsz, seq):
    n = seq // dilation
    nblk = n // ATT_BLK
    spec = lambda w: pl.BlockSpec((None, dilation, n, w), lambda b: (b, 0, 0, 0))
    win = (ATT_BLK, 2 * ATT_BLK)
    return pl.pallas_call(
        functools.partial(_attn_kernel, nblk=nblk, units=dilation * nblk),
        grid=(bsz,),
        in_specs=[spec(GROUP_WIDTH)] * 3,
        out_specs=[spec(GROUP_WIDTH), spec(LANES)],
        out_shape=[jax.ShapeDtypeStruct((bsz, dilation, n, GROUP_WIDTH), BF16),
                   jax.ShapeDtypeStruct((bsz, dilation, n, LANES), F32)],
        scratch_shapes=[pltpu.VMEM((2, HEADS_PER_GROUP) + win, F32),
                        pltpu.VMEM((2, HEADS_PER_GROUP) + win, BF16),
                        pltpu.VMEM((2,) + win, F32)],
        compiler_params=_cparams(("parallel",), 48),
        name=f"attn_d{dilation}",
    )(q, k, v)


def _cmul(a, b):
    return a[0] * b[0] - a[1] * b[1], a[0] * b[1] + a[1] * b[0]


def _ssm_tables(a_re, a_im, log_dt, b_re, b_im, c_re, c_im, d_skip):
    hp = lax.Precision.HIGHEST
    f = lambda t: t.astype(F32)
    lam = (f(a_re), f(a_im))
    dt = jnp.exp(f(log_dt))[:, None]
    g = a_re.shape[0]
    taus = jnp.arange(2 * SSM_CHUNK + 1, dtype=F32)[None, :, None]
    mag = jnp.exp(lam[0][:, None, :] * dt[:, None, :] * taus)
    ang = lam[1][:, None, :] * dt[:, None, :] * taus
    apow = (mag * jnp.cos(ang), mag * jnp.sin(ang))
    a_bar = (apow[0][:, 1], apow[1][:, 1])
    den = lam[0] * lam[0] + lam[1] * lam[1]
    coef = _cmul((a_bar[0] - 1.0, a_bar[1]), (lam[0] / den, -lam[1] / den))
    b_bar = _cmul((coef[0][..., None], coef[1][..., None]), (f(b_re), f(b_im)))
    c_mat = (f(c_re), f(c_im))

    ca = _cmul((c_mat[0][:, None], c_mat[1][:, None]),
               (apow[0][:, :SSM_CHUNK, None, :], apow[1][:, :SSM_CHUNK, None, :]))
    kern = jnp.einsum('gtpn,gnq->gtpq', ca[0], b_bar[0], precision=hp) - \
        jnp.einsum('gtpn,gnq->gtpq', ca[1], b_bar[1], precision=hp)
    kern = kern.at[:, 0].add(f(d_skip)[:, :, None] * jnp.eye(SSM_P, dtype=F32))
    m_row = kern.transpose(0, 3, 1, 2).reshape(g, SSM_P, CHUNK_W)

    rev = (apow[0][:, SSM_CHUNK - 1::-1][:, :SSM_CHUNK], apow[1][:, SSM_CHUNK - 1::-1][:, :SSM_CHUNK])
    bt = (b_bar[0].transpose(0, 2, 1)[:, None], b_bar[1].transpose(0, 2, 1)[:, None])
    w4 = _cmul((rev[0][:, :, None, :], rev[1][:, :, None, :]), bt)
    w_re, w_im = (t.reshape(g, CHUNK_W, SSM_N) for t in w4)
    w_mat = jnp.concatenate([w_re, w_im, w_im, w_re], axis=-1)

    ct = (c_mat[0].transpose(0, 2, 1)[:, :, None, :], c_mat[1].transpose(0, 2, 1)[:, :, None, :])
    at = (apow[0][:, 1:SSM_CHUNK + 1].transpose(0, 2, 1)[..., None],
          apow[1][:, 1:SSM_CHUNK + 1].transpose(0, 2, 1)[..., None])
    v4 = _cmul(ct, at)
    v_re, v_im = (t.reshape(g, SSM_N, CHUNK_W) for t in v4)
    v_mat = jnp.concatenate([v_re, -v_im], axis=1)

    def mul_rows(tau):
        re, im = apow[0][:, tau], apow[1][:, tau]
        return [jnp.concatenate([re, re], -1), jnp.concatenate([-im, im], -1),
                jnp.concatenate([im, -im], -1)]
    ptab = jnp.stack(mul_rows(SSM_CHUNK) + mul_rows(2 * SSM_CHUNK) +
                     [jnp.zeros((g, LANES), F32)] * 2, axis=1)
    return m_row, w_mat.astype(BF16), v_mat.astype(BF16), ptab


def _toeplitz(m_row):
    lane = lax.broadcasted_iota(jnp.int32, m_row.shape, 1)
    blocks = [m_row]
    for s in range(1, SSM_CHUNK):
        blocks.append(jnp.where(lane >= s * SSM_P, pltpu.roll(m_row, s * SSM_P, axis=1), 0.0))
    return jnp.concatenate(blocks, axis=0).astype(BF16)


def _ssm_kernel(u_ref, m_ref, w_ref, v_ref, p_ref, y_ref, gx_ref, gz_ref, h_ref, *, groups, nchunk, bsz):
    rows = nchunk * bsz
    row = lax.broadcasted_iota(jnp.int32, (rows, LANES), 0)
    first_chunk = (row % nchunk) == 0

    def prev_chunk(a):
        return jnp.where(first_chunk, 0.0, pltpu.roll(a, 1, axis=0))

    for g in range(groups):
        e = jnp.dot(u_ref[g], w_ref[g], preferred_element_type=F32)
        ex, ez = e[:, :LANES], e[:, LANES:]
        exs, ezs = prev_chunk(ex), prev_chunk(ez)
        p1, p2, p3 = p_ref[g, 0:1, :], p_ref[g, 1:2, :], p_ref[g, 2:3, :]
        gx = ex + p1 * exs + p2 * ezs
        gz = ez + p1 * ezs + p3 * exs
        for b in range(bsz):
            gx_ref[g, pl.ds(b, nchunk, stride=bsz), :] = gx[b * nchunk:(b + 1) * nchunk]
            gz_ref[g, pl.ds(b, nchunk, stride=bsz), :] = gz[b * nchunk:(b + 1) * nchunk]

    def step(j, carry):
        r0 = pl.multiple_of(j * SUBLANES, SUBLANES)
        new = []
        for g in range(groups):
            x, z = carry[g]
            q1, q2, q3 = p_ref[g, 3:4, :], p_ref[g, 4:5, :], p_ref[g, 5:6, :]
            xn = q1 * x + q2 * z + gx_ref[g, pl.ds(r0, SUBLANES), :]
            zn = q1 * z + q3 * x + gz_ref[g, pl.ds(r0, SUBLANES), :]
            h_ref[g, pl.ds(r0, SUBLANES), :] = xn
            new.append((xn, zn))
        return tuple(new)

    zero = jnp.zeros((SUBLANES, LANES), F32)
    lax.fori_loop(0, rows // SUBLANES, step, tuple((zero, zero) for _ in range(groups)))

    for g in range(groups):
        h_end = jnp.concatenate([h_ref[g, pl.ds(b, nchunk, stride=bsz), :] for b in range(bsz)], axis=0)
        h_in = prev_chunk(h_end).astype(BF16)
        y = jnp.dot(u_ref[g], _toeplitz(m_ref[g]), preferred_element_type=F32)
        y = y + jnp.dot(h_in, v_ref[g], preferred_element_type=F32)
        y_ref[g] = y.astype(y_ref.dtype)


def _ssm(uc, tables, bsz, seq):
    m_row, w_mat, v_mat, ptab = tables
    g, rows, _ = uc.shape
    nchunk = seq // SSM_CHUNK
    assert rows == nchunk * bsz and SUBLANES % bsz == 0 and SUBLANES // bsz == 2
    gs = 4
    blk = lambda a, b: pl.BlockSpec((gs, a, b), lambda i: (i, 0, 0))
    return pl.pallas_call(
        functools.partial(_ssm_kernel, groups=gs, nchunk=nchunk, bsz=bsz),
        grid=(g // gs,),
        in_specs=[blk(rows, CHUNK_W), blk(SSM_P, CHUNK_W), blk(CHUNK_W, CHUNK_W),
                  blk(LANES, CHUNK_W), blk(SUBLANES, LANES)],
        out_specs=blk(rows, CHUNK_W),
        out_shape=jax.ShapeDtypeStruct((g, rows, CHUNK_W), BF16),
        scratch_shapes=[pltpu.VMEM((gs, rows, LANES), F32)] * 3,
        compiler_params=_cparams(("parallel",), 48),
        name="ssm",
    )(uc, m_row, w_mat, v_mat, ptab)


def _rms(x):
    return x * lax.rsqrt(jnp.mean(x * x, axis=-1, keepdims=True) + EPS)


def _mix_kernel(o1_ref, l1_ref, o4_ref, l4_ref, o16_ref, l16_ref, y_ref, x_ref, mod_ref,
                ga_ref, gs_ref, gp_ref, ex_ref, wg_ref, bg_ref, wo_ref, out_ref,
                so_ref, sl_ref, sy_ref, *, tm):
    for gi, (dil, o_ref, l_ref) in enumerate(((4, o4_ref, l4_ref), (16, o16_ref, l16_ref))):
        for r in range(dil):
            rows = pl.ds(r, tm // dil, stride=dil)
            for jj in range(GROUP_SLABS):
                so_ref[gi * GROUP_SLABS + jj, rows, :] = o_ref[r, :, _slab(jj)].astype(F32)
            sl_ref[gi, rows, :] = l_ref[r]

    for gb in range(y_ref.shape[0] // SLOTS):
        ygs = [y_ref[gb * SLOTS + g].astype(F32) for g in range(SLOTS)]
        for cb in range(tm // (SSM_CHUNK * SUBLANES)):
            for half in range(2):
                vs = [yg[cb * SUBLANES:(cb + 1) * SUBLANES, _slab(half)] for yg in ygs]
                for s, o in enumerate(_slot_transpose(vs)):
                    sy_ref[gb, pl.ds(cb * SSM_CHUNK * SUBLANES + half * SLOTS + s, SUBLANES,
                                     stride=SSM_CHUNK), :] = o

    hm = tm // 2
    lane = lax.broadcasted_iota(jnp.int32, (hm, LANES), 1)
    even = (lane & 1) == 0
    head_lane = lane < HEADS_PER_GROUP
    aw = o1_ref.shape[1]

    def glu_in(rows):
        yv = jnp.concatenate([sy_ref[gb, rows, :] for gb in range(sy_ref.shape[0])], axis=1)
        s = jax.nn.gelu(yv)
        return s, jnp.dot(s.astype(BF16), wg_ref[...], preferred_element_type=F32)

    def attention(rows):
        ms, ls = [], []
        for st in (l1_ref[rows, :], sl_ref[0, rows, :], sl_ref[1, rows, :]):
            sw = pltpu.roll(st, HEAD_DIM, axis=1)
            ms.append(jnp.where(even, st, sw))
            ls.append(jnp.where(even, sw, st))
        mx = jnp.maximum(jnp.maximum(ms[0], ms[1]), ms[2])
        es = [jnp.exp2(m - mx) for m in ms]
        inv = 1.0 / (es[0] * ls[0] + es[1] * ls[1] + es[2] * ls[2])
        wexp = [jnp.dot(jnp.where(head_lane, e * inv, 0.0).astype(BF16), ex_ref[...],
                        preferred_element_type=F32) for e in es]
        att_slabs = []
        for jj in range(GROUP_SLABS):
            a = wexp[0][:, _slab(jj)] * o1_ref[rows, _slab(jj)].astype(F32)
            a = a + wexp[1][:, _slab(jj)] * so_ref[jj, rows, :]
            a = a + wexp[2][:, _slab(jj)] * so_ref[GROUP_SLABS + jj, rows, :]
            att_slabs.append(a)
        att = _rms(jnp.concatenate(att_slabs, axis=1)) * ga_ref[...]
        return jnp.dot(att.astype(BF16), wo_ref[:aw, :], preferred_element_type=F32)

    def glu_out(s, z, mix_att):
        s = s * jax.nn.sigmoid(z + bg_ref[...])
        ssm = _rms(s) * gs_ref[...]
        return mix_att + jnp.dot(ssm.astype(BF16), wo_ref[aw:, :], preferred_element_type=F32)

    def residual(rows, mix):
        out_ref[rows, :] = x_ref[rows, :] + mod_ref[2:3, :] * (_rms(mix) * gp_ref[...])

    halves = (slice(0, hm), slice(hm, tm))
    sz = [glu_in(r) for r in halves]
    ma = [attention(r) for r in halves]
    mixes = [glu_out(s, z, m) for (s, z), m in zip(sz, ma)]
    for r, mix in zip(halves, mixes):
        residual(r, mix)


def _mix(o1, l1, o4, l4, o16, l16, yc, x2, mod, g_attn, g_ssm, g_post, w_glu_b, b_glu, w_out_b, bsz, seq):
    t_tok, d = x2.shape
    groups = yc.shape[0]
    sw = groups * SSM_P
    aw = GROUP_WIDTH
    tm = TOK_TILE
    per_b = seq // tm
    tok = lambda w: pl.BlockSpec((tm, w), lambda i: (i, 0))
    res = lambda dil, w: pl.BlockSpec((None, dil, tm // dil, w), lambda i: (i // per_b, 0, i % per_b, 0))
    head_of_lane = jnp.arange(aw)[None, :] // HEAD_DIM
    expand = (jnp.arange(LANES)[:, None] == head_of_lane).astype(BF16)
    return pl.pallas_call(
        functools.partial(_mix_kernel, tm=tm),
        grid=(t_tok // tm,),
        in_specs=[tok(aw), tok(LANES), res(4, aw), res(4, LANES), res(16, aw), res(16, LANES),
                  pl.BlockSpec((groups, tm // SSM_CHUNK, CHUNK_W), lambda i: (0, i, 0)),
                  tok(d),
                  pl.BlockSpec((None, N_MOD, d), lambda i: (i // per_b, 0, 0)),
                  _resident((1, aw)), _resident((1, sw)), _resident((1, d)),
                  _resident((LANES, aw)),
                  _resident((sw, sw)), _resident((1, sw)), _resident((aw + sw, d))],
        out_specs=tok(d),
        out_shape=jax.ShapeDtypeStruct((t_tok, d), F32),
        scratch_shapes=[pltpu.VMEM((2 * GROUP_SLABS, tm, LANES), F32),
                        pltpu.VMEM((2, tm, LANES), F32),
                        pltpu.VMEM((sw // LANES, tm, LANES), F32)],
        compiler_params=_cparams(("parallel",), 52),
        name="mix",
    )(o1, l1, o4, l4, o16, l16, yc, x2, mod, g_attn, g_ssm, g_post, expand, w_glu_b, b_glu, w_out_b)


def _mlp_kernel(x_ref, mod_ref, gpre_ref, gpost_ref, w1_ref, w2_ref, o_ref, h_ref):
    f = pl.program_id(1)

    def row_chunks(fn):
        def body(c, carry):
            fn(pl.ds(pl.multiple_of(c * ROW_CHUNK, ROW_CHUNK), ROW_CHUNK))
            return carry
        lax.fori_loop(0, x_ref.shape[0] // ROW_CHUNK, body, 0)

    @pl.when(f == 0)
    def _():
        scale = gpre_ref[...] * (1.0 + mod_ref[4:5, :])
        shift = mod_ref[3:4, :]

        def pre(rows):
            h_ref[rows, :] = (_rms(x_ref[rows, :]) * scale + shift).astype(BF16)
            o_ref[rows, :] = jnp.zeros((ROW_CHUNK, o_ref.shape[1]), F32)
        row_chunks(pre)

    t = jnp.dot(h_ref[...], w1_ref[...], preferred_element_type=F32)
    t = jnp.square(jnp.maximum(t, 0.0)).astype(BF16)
    o_ref[...] += jnp.dot(t, w2_ref[...], preferred_element_type=F32)

    @pl.when(f == pl.num_programs(1) - 1)
    def _():
        gate = mod_ref[5:6, :] * gpost_ref[...]

        def post(rows):
            o_ref[rows, :] = x_ref[rows, :] + gate * _rms(o_ref[rows, :])
        row_chunks(post)


def _mlp(x2, mod, g_pre, g_post, w1_b, w2_b, seq):
    t_tok, d = x2.shape
    d_ff = w1_b.shape[1]
    tm, tf = 1024, 512
    per_b = seq // tm
    return pl.pallas_call(
        _mlp_kernel,
        grid=(t_tok // tm, d_ff // tf),
        in_specs=[pl.BlockSpec((tm, d), lambda i, f: (i, 0)),
                  pl.BlockSpec((None, N_MOD, d), lambda i, f: (i // per_b, 0, 0)),
                  _resident((1, d)), _resident((1, d)),
                  pl.BlockSpec((d, tf), lambda i, f: (0, f)),
                  pl.BlockSpec((tf, d), lambda i, f: (f, 0))],
        out_specs=pl.BlockSpec((tm, d), lambda i, f: (i, 0)),
        out_shape=jax.ShapeDtypeStruct((t_tok, d), F32),
        scratch_shapes=[pltpu.VMEM((tm, d), BF16)],
        compiler_params=_cparams(("parallel", "arbitrary"), 56),
        name="mlp",
    )(x2, mod, g_pre, g_post, w1_b, w2_b)


def _rope_tables(positions):
    half = ROT_DIM // 2
    freqs = ROPE_THETA ** (-jnp.arange(0, ROT_DIM, 2, dtype=F32) / ROT_DIM)
    ang = positions.reshape(-1).astype(F32)[:, None] * freqs
    cos, sin = jnp.cos(ang), jnp.sin(ang)
    t_tok = ang.shape[0]
    pad1 = jnp.ones((t_tok, HEAD_DIM - ROT_DIM), F32)
    pad0 = jnp.zeros((t_tok, HEAD_DIM - ROT_DIM), F32)
    assert cos.shape[1] == half
    ct = jnp.concatenate([cos, cos, pad1], axis=1)
    st = jnp.concatenate([-sin, sin, pad0], axis=1)
    return jnp.tile(ct, (1, LANES // HEAD_DIM)), jnp.tile(st, (1, LANES // HEAD_DIM))


def kernel(x, c, positions, w_ada, b_ada, g_pre_mix, g_post_mix, w_in, ssm_a_re, ssm_a_im, ssm_log_dt,
           ssm_b_re, ssm_b_im, ssm_c_re, ssm_c_im, ssm_d, w_glu, b_glu, g_attn_out, g_ssm_out, w_out,
           g_pre_mlp, g_post_mlp, w_mlp_in, w_mlp_out):
    bsz, seq, d = x.shape
    depth = w_ada.shape[0]
    x2 = x.reshape(bsz * seq, d)
    ctab, stab = _rope_tables(positions)
    row = lambda a: a.reshape(1, -1).astype(F32)
    for l in range(depth):
        mod = _adaln(c, w_ada[l], b_ada[l])
        q1, q4, q16, k1, k4, k16, v1, v4, v16, uc = _inproj(
            x2, mod, row(g_pre_mix[l]), w_in[l].astype(BF16), ctab, stab, bsz, seq)

        nat = lambda a: a.reshape(bsz, 1, seq, a.shape[-1])
        o1, l1 = _attn_group(nat(q1), nat(k1), nat(v1), 1, bsz, seq)
        o4, l4 = _attn_group(q4, k4, v4, 4, bsz, seq)
        o16, l16 = _attn_group(q16, k16, v16, 16, bsz, seq)

        tables = _ssm_tables(ssm_a_re[l], ssm_a_im[l], ssm_log_dt[l], ssm_b_re[l], ssm_b_im[l],
                             ssm_c_re[l], ssm_c_im[l], ssm_d[l])
        yc = _ssm(uc, tables, bsz, seq)

        x2 = _mix(o1.reshape(bsz * seq, -1), l1.reshape(bsz * seq, -1), o4, l4, o16, l16, yc, x2, mod,
                  row(g_attn_out[l]), row(g_ssm_out[l]), row(g_post_mix[l]),
                  w_glu[l].astype(BF16), row(b_glu[l]), w_out[l].astype(BF16), bsz, seq)
        x2 = _mlp(x2, mod, row(g_pre_mlp[l]), row(g_post_mlp[l]),
                  w_mlp_in[l].astype(BF16), w_mlp_out[l].astype(BF16), seq)
    return x2.reshape(bsz, seq, d)
```

```python
import functools
import math

import jax
import jax.numpy as jnp
from jax import lax
from jax.experimental import pallas as pl
from jax.experimental.pallas import tpu as pltpu

F32 = jnp.float32
BF16 = jnp.bfloat16

LANES = 128
SUBLANES = 8

HEAD_DIM = 64
ROT_DIM = HEAD_DIM // 4
ROPE_THETA = 500000.0
DILATIONS = (1, 4, 16)
HEADS_PER_GROUP = 6
GROUP_WIDTH = HEADS_PER_GROUP * HEAD_DIM
GROUP_SLABS = GROUP_WIDTH // LANES
ATT_BLK = 128
SSM_P = 16
SSM_N = 64
SSM_CHUNK = 16
CHUNK_W = SSM_CHUNK * SSM_P
SLOTS = LANES // SSM_P
N_MOD = 6
EPS = 1e-6
NEG = -1e30
TOK_TILE = 512
ROW_CHUNK = 128


def _cparams(sem, vmem_mb):
    return pltpu.CompilerParams(dimension_semantics=sem, vmem_limit_bytes=vmem_mb * 1024 * 1024)


def _resident(shape):
    nd = len(shape)
    return pl.BlockSpec(shape, lambda *_: (0,) * nd, pipeline_mode=pl.Buffered(1))


def _slab(j):
    return slice(j * LANES, (j + 1) * LANES)


def _slot_transpose(vs):
    slot = lax.broadcasted_iota(jnp.int32, vs[0].shape, 1) // SSM_P
    vs = list(vs)
    for dist in (4, 2, 1):
        low = (slot & dist) == 0
        nxt = list(vs)
        for i in range(SLOTS):
            if i & dist == 0:
                a, b = vs[i], vs[i + dist]
                nxt[i] = jnp.where(low, a, pltpu.roll(b, dist * SSM_P, axis=1))
                nxt[i + dist] = jnp.where(low, pltpu.roll(a, LANES - dist * SSM_P, axis=1), b)
        vs = nxt
    return vs


def _adaln_kernel(s_ref, w_ref, b_ref, o_ref):
    o_ref[...] = jnp.dot(s_ref[...], w_ref[...].astype(BF16),
                         preferred_element_type=F32) + b_ref[...]


def _adaln(c, w_ada, b_ada):
    bsz, d = c.shape
    n = w_ada.shape[1]
    tn = 1024
    s = jnp.zeros((SUBLANES, d), BF16).at[:bsz].set(jax.nn.silu(c).astype(BF16))
    out = pl.pallas_call(
        _adaln_kernel,
        grid=(n // tn,),
        in_specs=[pl.BlockSpec((SUBLANES, d), lambda j: (0, 0)),
                  pl.BlockSpec((d, tn), lambda j: (0, j)),
                  pl.BlockSpec((1, tn), lambda j: (0, j))],
        out_specs=pl.BlockSpec((SUBLANES, tn), lambda j: (0, j)),
        out_shape=jax.ShapeDtypeStruct((SUBLANES, n), F32),
        compiler_params=_cparams(("arbitrary",), 40),
        name="adaln",
    )(s, w_ada, b_ada.reshape(1, n))
    return out[:bsz].reshape(bsz, N_MOD, d)


def _inproj_kernel(x_ref, mod_ref, g_ref, w_ref, cs_ref, rx_ref, r1_ref,
                   q1_ref, q4_ref, q16_ref, k1_ref, k4_ref, k16_ref, v1_ref, v4_ref, v16_ref,
                   uc_ref, sc_ref, *, tm):
    x = x_ref[...]
    ms = jnp.mean(x * x, axis=-1, keepdims=True)
    scale = g_ref[...] * (1.0 + mod_ref[1:2, :])
    h = (x * lax.rsqrt(ms + EPS)) * scale + mod_ref[0:1, :]
    hb = h.astype(BF16)

    qk_slabs = 4 * GROUP_SLABS

    def scatter(j, jj, refs):
        for dil, ref in refs:
            if dil == 1:
                ref[:, _slab(jj)] = sc_ref[j].astype(BF16)
            else:
                for r in range(dil):
                    ref[r, :, _slab(jj)] = sc_ref[j, pl.ds(r, tm // dil, stride=dil), :].astype(BF16)

    pvu = jnp.dot(hb, w_ref[:, qk_slabs * LANES:], preferred_element_type=F32)
    for j in range(pvu.shape[1] // LANES):
        sc_ref[qk_slabs + j] = pvu[:, _slab(j)]
    for jj in range(GROUP_SLABS):
        scatter(qk_slabs + jj, jj, ((1, v1_ref), (4, v4_ref), (16, v16_ref)))

    u0 = qk_slabs + GROUP_SLABS
    for gb in range(uc_ref.shape[0] // SLOTS):
        pieces = [[[None, None] for _ in range(tm // (SSM_CHUNK * SUBLANES))] for _ in range(SLOTS)]
        for cb in range(tm // (SSM_CHUNK * SUBLANES)):
            for half in range(2):
                vs = [sc_ref[u0 + gb, pl.ds(cb * SSM_CHUNK * SUBLANES + half * SLOTS + s, SUBLANES,
                                            stride=SSM_CHUNK), :] for s in range(SLOTS)]
                for g, o in enumerate(_slot_transpose(vs)):
                    pieces[g][cb][half] = o
        for g in range(SLOTS):
            full = jnp.concatenate([jnp.concatenate(p, axis=1) for p in pieces[g]], axis=0)
            uc_ref[gb * SLOTS + g] = full.astype(BF16)

    cs = cs_ref[...]
    cs_hi = cs.astype(BF16)
    cs_lo = (cs - cs_hi.astype(F32)).astype(BF16)
    tab = jnp.dot(cs_hi, rx_ref[...], preferred_element_type=F32) + \
        jnp.dot(cs_lo, rx_ref[...], preferred_element_type=F32)
    ct = tab[:, :LANES] + r1_ref[...]
    st = tab[:, LANES:]
    lane = lax.broadcasted_iota(jnp.int32, ct.shape, 1) % HEAD_DIM
    second_half = (lane >= ROT_DIM // 2) & (lane < ROT_DIM)
    q_refs = ((1, q1_ref),), ((4, q4_ref),), ((16, q16_ref),)
    chunk_slabs = 4
    for c0 in reversed(range(0, qk_slabs, chunk_slabs)):
        pc = jnp.dot(hb, w_ref[:, c0 * LANES:(c0 + chunk_slabs) * LANES], preferred_element_type=F32)
        for s in range(chunk_slabs):
            j = c0 + s
            grp, jj = divmod(j, GROUP_SLABS)
            t = pc[:, _slab(s)]
            partner = jnp.where(second_half,
                                pltpu.roll(t, ROT_DIM // 2, axis=1),
                                pltpu.roll(t, LANES - ROT_DIM // 2, axis=1))
            r = t * ct + partner * st
            if grp < 3:
                r = r * (math.log2(math.e) / math.sqrt(HEAD_DIM))
            sc_ref[j] = r
            scatter(j, jj, q_refs[grp] if grp < 3 else ((1, k1_ref), (4, k4_ref), (16, k16_ref)))


def _rope_spread():
    half = ROT_DIM // 2
    dim = jnp.arange(LANES)[None, :] % HEAD_DIM
    i = jnp.arange(half)[:, None]
    first, second = dim == i, dim == i + half
    ct_rows = (first | second).astype(F32)
    st_rows = second.astype(F32) - first.astype(F32)
    zero = jnp.zeros_like(ct_rows)
    spread = jnp.concatenate([jnp.concatenate([ct_rows, zero], axis=1),
                              jnp.concatenate([zero, st_rows], axis=1)], axis=0)
    return spread.astype(BF16), (dim >= ROT_DIM).astype(F32)


def _inproj(x2, mod, g_pre, w_in_b, cos_sin, bsz, seq):
    t_tok, d = x2.shape
    in_w = w_in_b.shape[1]
    ssm_w = in_w - 5 * GROUP_WIDTH
    groups = ssm_w // SSM_P
    tm = TOK_TILE
    per_b = seq // tm
    tok = lambda w: pl.BlockSpec((tm, w), lambda i: (i, 0))

    def res_spec(dil):
        return pl.BlockSpec((None, dil, tm // dil, GROUP_WIDTH), lambda i: (i // per_b, 0, i % per_b, 0))

    def res_shape(dil):
        return jax.ShapeDtypeStruct((bsz, dil, seq // dil, GROUP_WIDTH), BF16)

    nat = jax.ShapeDtypeStruct((t_tok, GROUP_WIDTH), BF16)
    out_shape = [nat, res_shape(4), res_shape(16)] * 3 + \
                [jax.ShapeDtypeStruct((groups, t_tok // SSM_CHUNK, CHUNK_W), BF16)]
    out_specs = [tok(GROUP_WIDTH), res_spec(4), res_spec(16)] * 3 + \
                [pl.BlockSpec((groups, tm // SSM_CHUNK, CHUNK_W), lambda i: (0, i, 0))]
    return pl.pallas_call(
        functools.partial(_inproj_kernel, tm=tm),
        grid=(t_tok // tm,),
        in_specs=[tok(d),
                  pl.BlockSpec((None, N_MOD, d), lambda i: (i // per_b, 0, 0)),
                  _resident((1, d)),
                  _resident((d, in_w)),
                  tok(ROT_DIM), _resident((ROT_DIM, 2 * LANES)), _resident((1, LANES))],
        out_specs=out_specs,
        out_shape=out_shape,
        scratch_shapes=[pltpu.VMEM((in_w // LANES, tm, LANES), F32)],
        compiler_params=_cparams(("parallel",), 56),
        name="inproj",
    )(x2, mod, g_pre, w_in_b, cos_sin, *_rope_spread())


def _stat_lanes(h):
    a, b = slice(h, h + 1), slice(HEAD_DIM + h, HEAD_DIM + h + 1)
    return (a, b) if h % 2 == 0 else (b, a)


def _attn_kernel(q_ref, k_ref, v_ref, o_ref, s_ref, sc_ref, pr_ref, mk_ref, *, nblk, units):
    pair = 2 * ATT_BLK
    qi = lax.broadcasted_iota(jnp.int32, (pair, 2 * ATT_BLK), 0) % ATT_BLK
    kj = lax.broadcasted_iota(jnp.int32, (pair, 2 * ATT_BLK), 1)
    rel = qi - kj
    mk_ref[0] = jnp.where((rel >= 0) & (rel <= ATT_BLK), 0.0, NEG)
    mk_ref[1] = jnp.where((rel + ATT_BLK >= 0) & (rel <= 0), 0.0, NEG)
    low = lax.broadcasted_iota(jnp.int32, (ATT_BLK, LANES), 1) < HEAD_DIM
    top, bot = slice(0, ATT_BLK), slice(ATT_BLK, pair)
    ones = jnp.ones((2 * ATT_BLK, LANES), BF16)

    def coords(u):
        u = jnp.minimum(u, units - 1)
        r, i = u // nblk, u % nblk
        rows = pl.ds(pl.multiple_of(i * ATT_BLK, ATT_BLK), ATT_BLK)
        krows = pl.ds(pl.multiple_of(jnp.maximum(i - 1, 0) * ATT_BLK, ATT_BLK), 2 * ATT_BLK)
        return r, i, rows, krows

    def scores(u, slot):
        r, i, rows, krows = coords(u)
        bias = mk_ref[jnp.minimum(i, 1)]
        for jj in range(GROUP_SLABS):
            q2 = q_ref[r, rows, _slab(jj)]
            zero = jnp.zeros_like(q2)
            qp = jnp.concatenate([jnp.where(low, q2, zero), jnp.where(low, zero, q2)], axis=0)
            s = lax.dot_general(qp, k_ref[r, krows, _slab(jj)], (((1,), (1,)), ((), ())),
                                preferred_element_type=F32)
            sc_ref[slot, jj] = s + bias

    def softmax(u, slot):
        r, _, rows, _ = coords(u)
        s_ref[r, rows, :] = jnp.zeros((ATT_BLK, LANES), F32)
        for jj in range(GROUP_SLABS):
            s = sc_ref[slot, jj]
            m = jnp.max(s, axis=1, keepdims=True)
            pr_ref[slot, jj] = jnp.exp2(s - m).astype(BF16)
            s_ref[r, rows, _stat_lanes(2 * jj)[0]] = m[top]
            s_ref[r, rows, _stat_lanes(2 * jj + 1)[0]] = m[bot]

    def values(u, slot):
        r, _, rows, krows = coords(u)
        for jj in range(GROUP_SLABS):
            v_ext = jnp.concatenate([v_ref[r, krows, _slab(jj)], ones], axis=1)
            oe = jnp.dot(pr_ref[slot, jj], v_ext, preferred_element_type=F32)
            l_even, l_odd = _stat_lanes(2 * jj)[1], _stat_lanes(2 * jj + 1)[1]
            s_ref[r, rows, l_even] = oe[top, LANES:][:, l_even]
            s_ref[r, rows, l_odd] = oe[bot, LANES:][:, l_odd]
            o_ref[r, rows, _slab(jj)] = jnp.where(low, oe[top, :LANES], oe[bot, :LANES]).astype(o_ref.dtype)

    scores(0, 0)
    softmax(0, 0)
    scores(1, 1)

    def body(t, carry):
        u = 2 * t
        scores(u + 2, 0)
        softmax(u + 1, 1)
        values(u, 0)
        scores(u + 3, 1)
        softmax(u + 2, 0)
        values(u + 1, 1)
        return carry

    assert units % 2 == 0
    lax.fori_loop(0, units // 2, body, 0)


def _attn_group(q, k, v, dilation, bsz, seq):
    n = seq // dilation
    nblk = n // ATT_BLK
    spec = lambda w: pl.BlockSpec((None, dilation, n, w), lambda b: (b, 0, 0, 0))
    win = (2 * ATT_BLK, 2 * ATT_BLK)
    return pl.pallas_call(
        functools.partial(_attn_kernel, nblk=nblk, units=dilation * nblk),
        grid=(bsz,),
        in_specs=[spec(GROUP_WIDTH)] * 3,
        out_specs=[spec(GROUP_WIDTH), spec(LANES)],
        out_shape=[jax.ShapeDtypeStruct((bsz, dilation, n, GROUP_WIDTH), BF16),
                   jax.ShapeDtypeStruct((bsz, dilation, n, LANES), F32)],
        scratch_shapes=[pltpu.VMEM((2, GROUP_SLABS) + win, F32),
                        pltpu.VMEM((2, GROUP_SLABS) + win, BF16),
                        pltpu.VMEM((2,) + win, F32)],
        compiler_params=_cparams(("parallel",), 48),
        name=f"attn_d{dilation}",
    )(q, k, v)


def _cmul(a, b):
    return a[0] * b[0] - a[1] * b[1], a[0] * b[1] + a[1] * b[0]


def _ssm_tables(a_re, a_im, log_dt, b_re, b_im, c_re, c_im, d_skip):
    hp = lax.Precision.HIGHEST
    f = lambda t: t.astype(F32)
    lam = (f(a_re), f(a_im))
    dt = jnp.exp(f(log_dt))[:, None]
    g = a_re.shape[0]
    taus = jnp.arange(2 * SSM_CHUNK + 1, dtype=F32)[None, :, None]
    mag = jnp.exp(lam[0][:, None, :] * dt[:, None, :] * taus)
    ang = lam[1][:, None, :] * dt[:, None, :] * taus
    apow = (mag * jnp.cos(ang), mag * jnp.sin(ang))
    a_bar = (apow[0][:, 1], apow[1][:, 1])
    den = lam[0] * lam[0] + lam[1] * lam[1]
    coef = _cmul((a_bar[0] - 1.0, a_bar[1]), (lam[0] / den, -lam[1] / den))
    b_bar = _cmul((coef[0][..., None], coef[1][..., None]), (f(b_re), f(b_im)))
    c_mat = (f(c_re), f(c_im))

    ca = _cmul((c_mat[0][:, None], c_mat[1][:, None]),
               (apow[0][:, :SSM_CHUNK, None, :], apow[1][:, :SSM_CHUNK, None, :]))
    kern = jnp.einsum('gtpn,gnq->gtpq', ca[0], b_bar[0], precision=hp) - \
        jnp.einsum('gtpn,gnq->gtpq', ca[1], b_bar[1], precision=hp)
    kern = kern.at[:, 0].add(f(d_skip)[:, :, None] * jnp.eye(SSM_P, dtype=F32))
    m_row = kern.transpose(0, 3, 1, 2).reshape(g, SSM_P, CHUNK_W)

    rev = (apow[0][:, SSM_CHUNK - 1::-1][:, :SSM_CHUNK], apow[1][:, SSM_CHUNK - 1::-1][:, :SSM_CHUNK])
    bt = (b_bar[0].transpose(0, 2, 1)[:, None], b_bar[1].transpose(0, 2, 1)[:, None])
    w4 = _cmul((rev[0][:, :, None, :], rev[1][:, :, None, :]), bt)
    w_re, w_im = (t.reshape(g, CHUNK_W, SSM_N) for t in w4)
    w_mat = jnp.concatenate([w_re, w_im, w_im, w_re], axis=-1)

    ct = (c_mat[0].transpose(0, 2, 1)[:, :, None, :], c_mat[1].transpose(0, 2, 1)[:, :, None, :])
    at = (apow[0][:, 1:SSM_CHUNK + 1].transpose(0, 2, 1)[..., None],
          apow[1][:, 1:SSM_CHUNK + 1].transpose(0, 2, 1)[..., None])
    v4 = _cmul(ct, at)
    v_re, v_im = (t.reshape(g, SSM_N, CHUNK_W) for t in v4)
    v_mat = jnp.concatenate([v_re, -v_im], axis=1)

    def mul_rows(tau):
        re, im = apow[0][:, tau], apow[1][:, tau]
        return [jnp.concatenate([re, re], -1), jnp.concatenate([-im, im], -1),
                jnp.concatenate([im, -im], -1)]
    ptab = jnp.stack(mul_rows(SSM_CHUNK) + mul_rows(2 * SSM_CHUNK) +
                     [jnp.zeros((g, LANES), F32)] * 2, axis=1)
    return m_row, w_mat.astype(BF16), v_mat.astype(BF16), ptab


def _toeplitz(m_row):
    lane = lax.broadcasted_iota(jnp.int32, m_row.shape, 1)
    blocks = [m_row]
    for s in range(1, SSM_CHUNK):
        blocks.append(jnp.where(lane >= s * SSM_P, pltpu.roll(m_row, s * SSM_P, axis=1), 0.0))
    return jnp.concatenate(blocks, axis=0).astype(BF16)


def _ssm_kernel(u_ref, m_ref, w_ref, v_ref, p_ref, y_ref, gx_ref, gz_ref, h_ref, *, groups, nchunk, bsz):
    rows = nchunk * bsz
    row = lax.broadcasted_iota(jnp.int32, (rows, LANES), 0)
    first_chunk = (row % nchunk) == 0

    def prev_chunk(a):
        return jnp.where(first_chunk, 0.0, pltpu.roll(a, 1, axis=0))

    for g in range(groups):
        e = jnp.dot(u_ref[g], w_ref[g], preferred_element_type=F32)
        ex, ez = e[:, :LANES], e[:, LANES:]
        exs, ezs = prev_chunk(ex), prev_chunk(ez)
        p1, p2, p3 = p_ref[g, 0:1, :], p_ref[g, 1:2, :], p_ref[g, 2:3, :]
        gx = ex + p1 * exs + p2 * ezs
        gz = ez + p1 * ezs + p3 * exs
        for b in range(bsz):
            gx_ref[g, pl.ds(b, nchunk, stride=bsz), :] = gx[b * nchunk:(b + 1) * nchunk]
            gz_ref[g, pl.ds(b, nchunk, stride=bsz), :] = gz[b * nchunk:(b + 1) * nchunk]

    def step(j, carry):
        r0 = pl.multiple_of(j * SUBLANES, SUBLANES)
        new = []
        for g in range(groups):
            x, z = carry[g]
            q1, q2, q3 = p_ref[g, 3:4, :], p_ref[g, 4:5, :], p_ref[g, 5:6, :]
            xn = q1 * x + q2 * z + gx_ref[g, pl.ds(r0, SUBLANES), :]
            zn = q1 * z + q3 * x + gz_ref[g, pl.ds(r0, SUBLANES), :]
            h_ref[g, pl.ds(r0, SUBLANES), :] = xn
            new.append((xn, zn))
        return tuple(new)

    zero = jnp.zeros((SUBLANES, LANES), F32)
    lax.fori_loop(0, rows // SUBLANES, step, tuple((zero, zero) for _ in range(groups)))

    for g in range(groups):
        h_end = jnp.concatenate([h_ref[g, pl.ds(b, nchunk, stride=bsz), :] for b in range(bsz)], axis=0)
        h_in = prev_chunk(h_end).astype(BF16)
        y = jnp.dot(u_ref[g], _toeplitz(m_ref[g]), preferred_element_type=F32)
        y = y + jnp.dot(h_in, v_ref[g], preferred_element_type=F32)
        y_ref[g] = y.astype(y_ref.dtype)


def _ssm(uc, tables, bsz, seq):
    m_row, w_mat, v_mat, ptab = tables
    g, rows, _ = uc.shape
    nchunk = seq // SSM_CHUNK
    assert rows == nchunk * bsz and SUBLANES % bsz == 0 and SUBLANES // bsz == 2
    gs = 4
    blk = lambda a, b: pl.BlockSpec((gs, a, b), lambda i: (i, 0, 0))
    return pl.pallas_call(
        functools.partial(_ssm_kernel, groups=gs, nchunk=nchunk, bsz=bsz),
        grid=(g // gs,),
        in_specs=[blk(rows, CHUNK_W), blk(SSM_P, CHUNK_W), blk(CHUNK_W, CHUNK_W),
                  blk(LANES, CHUNK_W), blk(SUBLANES, LANES)],
        out_specs=blk(rows, CHUNK_W),
        out_shape=jax.ShapeDtypeStruct((g, rows, CHUNK_W), BF16),
        scratch_shapes=[pltpu.VMEM((gs, rows, LANES), F32)] * 3,
        compiler_params=_cparams(("parallel",), 48),
        name="ssm",
    )(uc, m_row, w_mat, v_mat, ptab)


def _rms(x):
    return x * lax.rsqrt(jnp.mean(x * x, axis=-1, keepdims=True) + EPS)


def _mix_kernel(o1_ref, l1_ref, o4_ref, l4_ref, o16_ref, l16_ref, y_ref, x_ref, mod_ref,
                ga_ref, gs_ref, gp_ref, ex_ref, wg_ref, bg_ref, wo_ref, out_ref,
                so_ref, sl_ref, sy_ref, *, tm):
    for gi, (dil, o_ref, l_ref) in enumerate(((4, o4_ref, l4_ref), (16, o16_ref, l16_ref))):
        for r in range(dil):
            rows = pl.ds(r, tm // dil, stride=dil)
            for jj in range(GROUP_SLABS):
                so_ref[gi * GROUP_SLABS + jj, rows, :] = o_ref[r, :, _slab(jj)].astype(F32)
            sl_ref[gi, rows, :] = l_ref[r]

    for gb in range(y_ref.shape[0] // SLOTS):
        ygs = [y_ref[gb * SLOTS + g].astype(F32) for g in range(SLOTS)]
        for cb in range(tm // (SSM_CHUNK * SUBLANES)):
            for half in range(2):
                vs = [yg[cb * SUBLANES:(cb + 1) * SUBLANES, _slab(half)] for yg in ygs]
                for s, o in enumerate(_slot_transpose(vs)):
                    sy_ref[gb, pl.ds(cb * SSM_CHUNK * SUBLANES + half * SLOTS + s, SUBLANES,
                                     stride=SSM_CHUNK), :] = o

    lane = lax.broadcasted_iota(jnp.int32, (tm, LANES), 1)
    even = (lane & 1) == 0
    ms, ls = [], []
    for st in (l1_ref[...], sl_ref[0], sl_ref[1]):
        sw = pltpu.roll(st, HEAD_DIM, axis=1)
        ms.append(jnp.where(even, st, sw))
        ls.append(jnp.where(even, sw, st))
    mx = jnp.maximum(jnp.maximum(ms[0], ms[1]), ms[2])
    es = [jnp.exp2(m - mx) for m in ms]
    inv = 1.0 / (es[0] * ls[0] + es[1] * ls[1] + es[2] * ls[2])
    head_lane = lane < HEADS_PER_GROUP
    wexp = [jnp.dot(jnp.where(head_lane, e * inv, 0.0).astype(BF16), ex_ref[...],
                    preferred_element_type=F32) for e in es]
    att_slabs = []
    for jj in range(GROUP_SLABS):
        a = wexp[0][:, _slab(jj)] * o1_ref[:, _slab(jj)].astype(F32)
        a = a + wexp[1][:, _slab(jj)] * so_ref[jj]
        a = a + wexp[2][:, _slab(jj)] * so_ref[GROUP_SLABS + jj]
        att_slabs.append(a)
    att = _rms(jnp.concatenate(att_slabs, axis=1)) * ga_ref[...]

    yv = jnp.concatenate([sy_ref[gb] for gb in range(sy_ref.shape[0])], axis=1)
    s = jax.nn.gelu(yv)
    z = jnp.dot(s.astype(BF16), wg_ref[...], preferred_element_type=F32) + bg_ref[...]
    s = s * jax.nn.sigmoid(z)
    ssm = _rms(s) * gs_ref[...]
    aw = att.shape[1]
    mix = jnp.dot(att.astype(BF16), wo_ref[:aw, :], preferred_element_type=F32)
    mix = mix + jnp.dot(ssm.astype(BF16), wo_ref[aw:, :], preferred_element_type=F32)
    out_ref[...] = x_ref[...] + mod_ref[2:3, :] * (_rms(mix) * gp_ref[...])


def _mix(o1, l1, o4, l4, o16, l16, yc, x2, mod, g_attn, g_ssm, g_post, w_glu_b, b_glu, w_out_b, bsz, seq):
    t_tok, d = x2.shape
    groups = yc.shape[0]
    sw = groups * SSM_P
    aw = GROUP_WIDTH
    tm = TOK_TILE
    per_b = seq // tm
    tok = lambda w: pl.BlockSpec((tm, w), lambda i: (i, 0))
    res = lambda dil, w: pl.BlockSpec((None, dil, tm // dil, w), lambda i: (i // per_b, 0, i % per_b, 0))
    head_of_lane = jnp.arange(aw)[None, :] // HEAD_DIM
    expand = (jnp.arange(LANES)[:, None] == head_of_lane).astype(BF16)
    return pl.pallas_call(
        functools.partial(_mix_kernel, tm=tm),
        grid=(t_tok // tm,),
        in_specs=[tok(aw), tok(LANES), res(4, aw), res(4, LANES), res(16, aw), res(16, LANES),
                  pl.BlockSpec((groups, tm // SSM_CHUNK, CHUNK_W), lambda i: (0, i, 0)),
                  tok(d),
                  pl.BlockSpec((None, N_MOD, d), lambda i: (i // per_b, 0, 0)),
                  _resident((1, aw)), _resident((1, sw)), _resident((1, d)),
                  _resident((LANES, aw)),
                  _resident((sw, sw)), _resident((1, sw)), _resident((aw + sw, d))],
        out_specs=tok(d),
        out_shape=jax.ShapeDtypeStruct((t_tok, d), F32),
        scratch_shapes=[pltpu.VMEM((2 * GROUP_SLABS, tm, LANES), F32),
                        pltpu.VMEM((2, tm, LANES), F32),
                        pltpu.VMEM((sw // LANES, tm, LANES), F32)],
        compiler_params=_cparams(("parallel",), 52),
        name="mix",
    )(o1, l1, o4, l4, o16, l16, yc, x2, mod, g_attn, g_ssm, g_post, expand, w_glu_b, b_glu, w_out_b)


def _mlp_kernel(x_ref, mod_ref, gpre_ref, gpost_ref, w1_ref, w2_ref, o_ref, h_ref):
    f = pl.program_id(1)

    def row_chunks(fn):
        def body(c, carry):
            fn(pl.ds(pl.multiple_of(c * ROW_CHUNK, ROW_CHUNK), ROW_CHUNK))
            return carry
        lax.fori_loop(0, x_ref.shape[0] // ROW_CHUNK, body, 0)

    @pl.when(f == 0)
    def _():
        scale = gpre_ref[...] * (1.0 + mod_ref[4:5, :])
        shift = mod_ref[3:4, :]

        def pre(rows):
            h_ref[rows, :] = (_rms(x_ref[rows, :]) * scale + shift).astype(BF16)
            o_ref[rows, :] = jnp.zeros((ROW_CHUNK, o_ref.shape[1]), F32)
        row_chunks(pre)

    t = jnp.dot(h_ref[...], w1_ref[...], preferred_element_type=F32)
    t = jnp.square(jnp.maximum(t, 0.0)).astype(BF16)
    o_ref[...] += jnp.dot(t, w2_ref[...], preferred_element_type=F32)

    @pl.when(f == pl.num_programs(1) - 1)
    def _():
        gate = mod_ref[5:6, :] * gpost_ref[...]

        def post(rows):
            o_ref[rows, :] = x_ref[rows, :] + gate * _rms(o_ref[rows, :])
        row_chunks(post)


def _mlp(x2, mod, g_pre, g_post, w1_b, w2_b, seq):
    t_tok, d = x2.shape
    d_ff = w1_b.shape[1]
    tm, tf = 1024, 512
    per_b = seq // tm
    return pl.pallas_call(
        _mlp_kernel,
        grid=(t_tok // tm, d_ff // tf),
        in_specs=[pl.BlockSpec((tm, d), lambda i, f: (i, 0)),
                  pl.BlockSpec((None, N_MOD, d), lambda i, f: (i // per_b, 0, 0)),
                  _resident((1, d)), _resident((1, d)),
                  pl.BlockSpec((d, tf), lambda i, f: (0, f)),
                  pl.BlockSpec((tf, d), lambda i, f: (f, 0))],
        out_specs=pl.BlockSpec((tm, d), lambda i, f: (i, 0)),
        out_shape=jax.ShapeDtypeStruct((t_tok, d), F32),
        scratch_shapes=[pltpu.VMEM((tm, d), BF16)],
        compiler_params=_cparams(("parallel", "arbitrary"), 56),
        name="mlp",
    )(x2, mod, g_pre, g_post, w1_b, w2_b)


def _rope_cos_sin(positions):
    freqs = ROPE_THETA ** (-jnp.arange(0, ROT_DIM, 2, dtype=F32) / ROT_DIM)
    ang = positions.reshape(-1).astype(F32)[:, None] * freqs
    return jnp.concatenate([jnp.cos(ang), jnp.sin(ang)], axis=1)


def kernel(x, c, positions, w_ada, b_ada, g_pre_mix, g_post_mix, w_in, ssm_a_re, ssm_a_im, ssm_log_dt,
           ssm_b_re, ssm_b_im, ssm_c_re, ssm_c_im, ssm_d, w_glu, b_glu, g_attn_out, g_ssm_out, w_out,
           g_pre_mlp, g_post_mlp, w_mlp_in, w_mlp_out):
    bsz, seq, d = x.shape
    depth = w_ada.shape[0]
    x2 = x.reshape(bsz * seq, d)
    cos_sin = _rope_cos_sin(positions)
    row = lambda a: a.reshape(1, -1).astype(F32)
    for l in range(depth):
        mod = _adaln(c, w_ada[l], b_ada[l])
        q1, q4, q16, k1, k4, k16, v1, v4, v16, uc = _inproj(
            x2, mod, row(g_pre_mix[l]), w_in[l].astype(BF16), cos_sin, bsz, seq)

        nat = lambda a: a.reshape(bsz, 1, seq, a.shape[-1])
        o1, l1 = _attn_group(nat(q1), nat(k1), nat(v1), 1, bsz, seq)
        o4, l4 = _attn_group(q4, k4, v4, 4, bsz, seq)
        o16, l16 = _attn_group(q16, k16, v16, 16, bsz, seq)

        tables = _ssm_tables(ssm_a_re[l], ssm_a_im[l], ssm_log_dt[l], ssm_b_re[l], ssm_b_im[l],
                             ssm_c_re[l], ssm_c_im[l], ssm_d[l])
        yc = _ssm(uc, tables, bsz, seq)

        x2 = _mix(o1.reshape(bsz * seq, -1), l1.reshape(bsz * seq, -1), o4, l4, o16, l16, yc, x2, mod,
                  row(g_attn_out[l]), row(g_ssm_out[l]), row(g_post_mix[l]),
                  w_glu[l].astype(BF16), row(b_glu[l]), w_out[l].astype(BF16), bsz, seq)
        x2 = _mlp(x2, mod, row(g_pre_mlp[l]), row(g_post_mlp[l]),
                  w_mlp_in[l].astype(BF16), w_mlp_out[l].astype(BF16), seq)
    return x2.reshape(bsz, seq, d)
```

```python
import functools
import math

import jax
import jax.numpy as jnp
from jax import lax
from jax.experimental import pallas as pl
from jax.experimental.pallas import tpu as pltpu

F32 = jnp.float32
BF16 = jnp.bfloat16

LANES = 128
SUBLANES = 8

HEAD_DIM = 64
ROT_DIM = HEAD_DIM // 4
ROPE_THETA = 500000.0
DILATIONS = (1, 4, 16)
HEADS_PER_GROUP = 6
GROUP_WIDTH = HEADS_PER_GROUP * HEAD_DIM
GROUP_SLABS = GROUP_WIDTH // LANES
ATT_BLK = 128
SSM_P = 16
SSM_N = 64
SSM_CHUNK = 16
CHUNK_W = SSM_CHUNK * SSM_P
SLOTS = LANES // SSM_P
N_MOD = 6
EPS = 1e-6
NEG = -1e30
TOK_TILE = 512
ROW_CHUNK = 128


def _cparams(sem, vmem_mb):
    return pltpu.CompilerParams(dimension_semantics=sem, vmem_limit_bytes=vmem_mb * 1024 * 1024)


def _resident(shape):
    nd = len(shape)
    return pl.BlockSpec(shape, lambda *_: (0,) * nd, pipeline_mode=pl.Buffered(1))


def _slab(j):
    return slice(j * LANES, (j + 1) * LANES)


def _slot_transpose(vs):
    slot = lax.broadcasted_iota(jnp.int32, vs[0].shape, 1) // SSM_P
    vs = list(vs)
    for dist in (4, 2, 1):
        low = (slot & dist) == 0
        nxt = list(vs)
        for i in range(SLOTS):
            if i & dist == 0:
                a, b = vs[i], vs[i + dist]
                nxt[i] = jnp.where(low, a, pltpu.roll(b, dist * SSM_P, axis=1))
                nxt[i + dist] = jnp.where(low, pltpu.roll(a, LANES - dist * SSM_P, axis=1), b)
        vs = nxt
    return vs


def _adaln_kernel(s_ref, w_ref, b_ref, o_ref):
    o_ref[...] = jnp.dot(s_ref[...], w_ref[...].astype(BF16),
                         preferred_element_type=F32) + b_ref[...]


def _adaln(c, w_ada, b_ada):
    bsz, d = c.shape
    n = w_ada.shape[1]
    tn = 1024
    s = jnp.zeros((SUBLANES, d), BF16).at[:bsz].set(jax.nn.silu(c).astype(BF16))
    out = pl.pallas_call(
        _adaln_kernel,
        grid=(n // tn,),
        in_specs=[pl.BlockSpec((SUBLANES, d), lambda j: (0, 0)),
                  pl.BlockSpec((d, tn), lambda j: (0, j)),
                  pl.BlockSpec((1, tn), lambda j: (0, j))],
        out_specs=pl.BlockSpec((SUBLANES, tn), lambda j: (0, j)),
        out_shape=jax.ShapeDtypeStruct((SUBLANES, n), F32),
        compiler_params=_cparams(("arbitrary",), 40),
        name="adaln",
    )(s, w_ada, b_ada.reshape(1, n))
    return out[:bsz].reshape(bsz, N_MOD, d)


def _inproj_kernel(x_ref, mod_ref, g_ref, w_ref, cs_ref, rx_ref, r1_ref,
                   q1_ref, q4_ref, q16_ref, k1_ref, k4_ref, k16_ref, v1_ref, v4_ref, v16_ref,
                   uc_ref, sc_ref, *, tm):
    x = x_ref[...]
    ms = jnp.mean(x * x, axis=-1, keepdims=True)
    scale = g_ref[...] * (1.0 + mod_ref[1:2, :])
    h = (x * lax.rsqrt(ms + EPS)) * scale + mod_ref[0:1, :]
    hb = h.astype(BF16)

    qk_slabs = 4 * GROUP_SLABS

    def scatter(j, jj, refs):
        for dil, ref in refs:
            if dil == 1:
                ref[:, _slab(jj)] = sc_ref[j].astype(BF16)
            else:
                for r in range(dil):
                    ref[r, :, _slab(jj)] = sc_ref[j, pl.ds(r, tm // dil, stride=dil), :].astype(BF16)

    pvu = jnp.dot(hb, w_ref[:, qk_slabs * LANES:], preferred_element_type=F32)
    for j in range(pvu.shape[1] // LANES):
        sc_ref[qk_slabs + j] = pvu[:, _slab(j)]
    for jj in range(GROUP_SLABS):
        scatter(qk_slabs + jj, jj, ((1, v1_ref), (4, v4_ref), (16, v16_ref)))

    u0 = qk_slabs + GROUP_SLABS
    for gb in range(uc_ref.shape[0] // SLOTS):
        pieces = [[[None, None] for _ in range(tm // (SSM_CHUNK * SUBLANES))] for _ in range(SLOTS)]
        for cb in range(tm // (SSM_CHUNK * SUBLANES)):
            for half in range(2):
                vs = [sc_ref[u0 + gb, pl.ds(cb * SSM_CHUNK * SUBLANES + half * SLOTS + s, SUBLANES,
                                            stride=SSM_CHUNK), :] for s in range(SLOTS)]
                for g, o in enumerate(_slot_transpose(vs)):
                    pieces[g][cb][half] = o
        for g in range(SLOTS):
            full = jnp.concatenate([jnp.concatenate(p, axis=1) for p in pieces[g]], axis=0)
            uc_ref[gb * SLOTS + g] = full.astype(BF16)

    cs = cs_ref[...]
    cs_hi = cs.astype(BF16)
    cs_lo = (cs - cs_hi.astype(F32)).astype(BF16)
    tab = jnp.dot(cs_hi, rx_ref[...], preferred_element_type=F32) + \
        jnp.dot(cs_lo, rx_ref[...], preferred_element_type=F32)
    ct = tab[:, :LANES] + r1_ref[...]
    st = tab[:, LANES:]
    lane = lax.broadcasted_iota(jnp.int32, ct.shape, 1) % HEAD_DIM
    second_half = (lane >= ROT_DIM // 2) & (lane < ROT_DIM)
    q_refs = ((1, q1_ref),), ((4, q4_ref),), ((16, q16_ref),)
    chunk_slabs = 4
    for c0 in reversed(range(0, qk_slabs, chunk_slabs)):
        pc = jnp.dot(hb, w_ref[:, c0 * LANES:(c0 + chunk_slabs) * LANES], preferred_element_type=F32)
        for s in range(chunk_slabs):
            j = c0 + s
            grp, jj = divmod(j, GROUP_SLABS)
            t = pc[:, _slab(s)]
            partner = jnp.where(second_half,
                                pltpu.roll(t, ROT_DIM // 2, axis=1),
                                pltpu.roll(t, LANES - ROT_DIM // 2, axis=1))
            r = t * ct + partner * st
            if grp < 3:
                r = r * (math.log2(math.e) / math.sqrt(HEAD_DIM))
            sc_ref[j] = r
            scatter(j, jj, q_refs[grp] if grp < 3 else ((1, k1_ref), (4, k4_ref), (16, k16_ref)))


def _rope_spread():
    half = ROT_DIM // 2
    dim = jnp.arange(LANES)[None, :] % HEAD_DIM
    i = jnp.arange(half)[:, None]
    first, second = dim == i, dim == i + half
    ct_rows = (first | second).astype(F32)
    st_rows = second.astype(F32) - first.astype(F32)
    zero = jnp.zeros_like(ct_rows)
    spread = jnp.concatenate([jnp.concatenate([ct_rows, zero], axis=1),
                              jnp.concatenate([zero, st_rows], axis=1)], axis=0)
    return spread.astype(BF16), (dim >= ROT_DIM).astype(F32)


def _inproj(x2, mod, g_pre, w_in_b, cos_sin, bsz, seq):
    t_tok, d = x2.shape
    in_w = w_in_b.shape[1]
    ssm_w = in_w - 5 * GROUP_WIDTH
    groups = ssm_w // SSM_P
    tm = TOK_TILE
    per_b = seq // tm
    tok = lambda w: pl.BlockSpec((tm, w), lambda i: (i, 0))

    def res_spec(dil):
        return pl.BlockSpec((None, dil, tm // dil, GROUP_WIDTH), lambda i: (i // per_b, 0, i % per_b, 0))

    def res_shape(dil):
        return jax.ShapeDtypeStruct((bsz, dil, seq // dil, GROUP_WIDTH), BF16)

    nat = jax.ShapeDtypeStruct((t_tok, GROUP_WIDTH), BF16)
    out_shape = [nat, res_shape(4), res_shape(16)] * 3 + \
                [jax.ShapeDtypeStruct((groups, t_tok // SSM_CHUNK, CHUNK_W), BF16)]
    out_specs = [tok(GROUP_WIDTH), res_spec(4), res_spec(16)] * 3 + \
                [pl.BlockSpec((groups, tm // SSM_CHUNK, CHUNK_W), lambda i: (0, i, 0))]
    return pl.pallas_call(
        functools.partial(_inproj_kernel, tm=tm),
        grid=(t_tok // tm,),
        in_specs=[tok(d),
                  pl.BlockSpec((None, N_MOD, d), lambda i: (i // per_b, 0, 0)),
                  _resident((1, d)),
                  _resident((d, in_w)),
                  tok(ROT_DIM), _resident((ROT_DIM, 2 * LANES)), _resident((1, LANES))],
        out_specs=out_specs,
        out_shape=out_shape,
        scratch_shapes=[pltpu.VMEM((in_w // LANES, tm, LANES), F32)],
        compiler_params=_cparams(("parallel",), 56),
        name="inproj",
    )(x2, mod, g_pre, w_in_b, cos_sin, *_rope_spread())


def _stat_lanes(h):
    a, b = slice(h, h + 1), slice(HEAD_DIM + h, HEAD_DIM + h + 1)
    return (a, b) if h % 2 == 0 else (b, a)


def _attn_kernel(q_ref, k_ref, v_ref, o_ref, s_ref, sc_ref, pr_ref, mk_ref, *, nblk, units):
    pair = 2 * ATT_BLK
    qi = lax.broadcasted_iota(jnp.int32, (pair, 2 * ATT_BLK), 0) % ATT_BLK
    kj = lax.broadcasted_iota(jnp.int32, (pair, 2 * ATT_BLK), 1)
    rel = qi - kj
    mk_ref[0] = jnp.where((rel >= 0) & (rel <= ATT_BLK), 0.0, NEG)
    mk_ref[1] = jnp.where((rel + ATT_BLK >= 0) & (rel <= 0), 0.0, NEG)
    low = lax.broadcasted_iota(jnp.int32, (ATT_BLK, LANES), 1) < HEAD_DIM
    top, bot = slice(0, ATT_BLK), slice(ATT_BLK, pair)
    ones = jnp.ones((2 * ATT_BLK, LANES), BF16)

    def coords(u):
        u = jnp.minimum(u, units - 1)
        r, i = u // nblk, u % nblk
        rows = pl.ds(pl.multiple_of(i * ATT_BLK, ATT_BLK), ATT_BLK)
        krows = pl.ds(pl.multiple_of(jnp.maximum(i - 1, 0) * ATT_BLK, ATT_BLK), 2 * ATT_BLK)
        return r, i, rows, krows

    def scores(u, slot):
        r, i, rows, krows = coords(u)
        bias = mk_ref[jnp.minimum(i, 1)]
        for jj in range(GROUP_SLABS):
            q2 = q_ref[r, rows, _slab(jj)]
            zero = jnp.zeros_like(q2)
            qp = jnp.concatenate([jnp.where(low, q2, zero), jnp.where(low, zero, q2)], axis=0)
            s = lax.dot_general(qp, k_ref[r, krows, _slab(jj)], (((1,), (1,)), ((), ())),
                                preferred_element_type=F32)
            sc_ref[slot, jj] = s + bias

    def softmax(u, slot):
        r, _, rows, _ = coords(u)
        s_ref[r, rows, :] = jnp.zeros((ATT_BLK, LANES), F32)
        for jj in range(GROUP_SLABS):
            s = sc_ref[slot, jj]
            m = jnp.max(s, axis=1, keepdims=True)
            pr_ref[slot, jj] = jnp.exp2(s - m).astype(BF16)
            s_ref[r, rows, _stat_lanes(2 * jj)[0]] = m[top]
            s_ref[r, rows, _stat_lanes(2 * jj + 1)[0]] = m[bot]

    def values(u, slot):
        r, _, rows, krows = coords(u)
        for jj in range(GROUP_SLABS):
            v_ext = jnp.concatenate([v_ref[r, krows, _slab(jj)], ones], axis=1)
            oe = jnp.dot(pr_ref[slot, jj], v_ext, preferred_element_type=F32)
            l_even, l_odd = _stat_lanes(2 * jj)[1], _stat_lanes(2 * jj + 1)[1]
            s_ref[r, rows, l_even] = oe[top, LANES:][:, l_even]
            s_ref[r, rows, l_odd] = oe[bot, LANES:][:, l_odd]
            o_ref[r, rows, _slab(jj)] = jnp.where(low, oe[top, :LANES], oe[bot, :LANES]).astype(o_ref.dtype)

    scores(0, 0)
    softmax(0, 0)
    scores(1, 1)

    def body(t, carry):
        u = 2 * t
        scores(u + 2, 0)
        softmax(u + 1, 1)
        values(u, 0)
        scores(u + 3, 1)
        softmax(u + 2, 0)
        values(u + 1, 1)
        return carry

    assert units % 2 == 0
    lax.fori_loop(0, units // 2, body, 0)


def _attn_group(q, k, v, dilation, bsz, seq):
    n = seq // dilation
    nblk = n // ATT_BLK
    spec = lambda w: pl.BlockSpec((None, dilation, n, w), lambda b: (b, 0, 0, 0))
    win = (2 * ATT_BLK, 2 * ATT_BLK)
    return pl.pallas_call(
        functools.partial(_attn_kernel, nblk=nblk, units=dilation * nblk),
        grid=(bsz,),
        in_specs=[spec(GROUP_WIDTH)] * 3,
        out_specs=[spec(GROUP_WIDTH), spec(LANES)],
        out_shape=[jax.ShapeDtypeStruct((bsz, dilation, n, GROUP_WIDTH), BF16),
                   jax.ShapeDtypeStruct((bsz, dilation, n, LANES), F32)],
        scratch_shapes=[pltpu.VMEM((2, GROUP_SLABS) + win, F32),
                        pltpu.VMEM((2, GROUP_SLABS) + win, BF16),
                        pltpu.VMEM((2,) + win, F32)],
        compiler_params=_cparams(("parallel",), 48),
        name=f"attn_d{dilation}",
    )(q, k, v)


def _cmul(a, b):
    return a[0] * b[0] - a[1] * b[1], a[0] * b[1] + a[1] * b[0]


def _ssm_tables(a_re, a_im, log_dt, b_re, b_im, c_re, c_im, d_skip):
    hp = lax.Precision.HIGH
    f = lambda t: t.astype(F32)
    lam = (f(a_re), f(a_im))
    dt = jnp.exp(f(log_dt))[:, None]
    g = a_re.shape[0]
    taus = jnp.arange(2 * SSM_CHUNK + 1, dtype=F32)[None, :, None]
    mag = jnp.exp(lam[0][:, None, :] * dt[:, None, :] * taus)
    ang = lam[1][:, None, :] * dt[:, None, :] * taus
    apow = (mag * jnp.cos(ang), mag * jnp.sin(ang))
    a_bar = (apow[0][:, 1], apow[1][:, 1])
    den = lam[0] * lam[0] + lam[1] * lam[1]
    coef = _cmul((a_bar[0] - 1.0, a_bar[1]), (lam[0] / den, -lam[1] / den))
    b_bar = _cmul((coef[0][..., None], coef[1][..., None]), (f(b_re), f(b_im)))
    c_mat = (f(c_re), f(c_im))

    ca = _cmul((c_mat[0][:, None], c_mat[1][:, None]),
               (apow[0][:, :SSM_CHUNK, None, :], apow[1][:, :SSM_CHUNK, None, :]))
    kern = jnp.einsum('gtpn,gnq->gtpq', ca[0], b_bar[0], precision=hp) - \
        jnp.einsum('gtpn,gnq->gtpq', ca[1], b_bar[1], precision=hp)
    kern = kern.at[:, 0].add(f(d_skip)[:, :, None] * jnp.eye(SSM_P, dtype=F32))
    m_row = kern.transpose(0, 3, 1, 2).reshape(g, SSM_P, CHUNK_W)

    rev = (apow[0][:, SSM_CHUNK - 1::-1][:, :SSM_CHUNK], apow[1][:, SSM_CHUNK - 1::-1][:, :SSM_CHUNK])
    bt = (b_bar[0].transpose(0, 2, 1)[:, None], b_bar[1].transpose(0, 2, 1)[:, None])
    w4 = _cmul((rev[0][:, :, None, :], rev[1][:, :, None, :]), bt)
    w_re, w_im = (t.reshape(g, CHUNK_W, SSM_N) for t in w4)
    w_mat = jnp.concatenate([w_re, w_im, w_im, w_re], axis=-1)

    ct = (c_mat[0].transpose(0, 2, 1)[:, :, None, :], c_mat[1].transpose(0, 2, 1)[:, :, None, :])
    at = (apow[0][:, 1:SSM_CHUNK + 1].transpose(0, 2, 1)[..., None],
          apow[1][:, 1:SSM_CHUNK + 1].transpose(0, 2, 1)[..., None])
    v4 = _cmul(ct, at)
    v_re, v_im = (t.reshape(g, SSM_N, CHUNK_W) for t in v4)
    v_mat = jnp.concatenate([v_re, -v_im], axis=1)

    def mul_rows(tau):
        re, im = apow[0][:, tau], apow[1][:, tau]
        return [jnp.concatenate([re, re], -1), jnp.concatenate([-im, im], -1),
                jnp.concatenate([im, -im], -1)]
    ptab = jnp.stack(mul_rows(SSM_CHUNK) + mul_rows(2 * SSM_CHUNK) +
                     [jnp.zeros((g, LANES), F32)] * 2, axis=1)
    return m_row, w_mat.astype(BF16), v_mat.astype(BF16), ptab


def _toeplitz(m_row):
    lane = lax.broadcasted_iota(jnp.int32, m_row.shape, 1)
    blocks = [m_row]
    for s in range(1, SSM_CHUNK):
        blocks.append(jnp.where(lane >= s * SSM_P, pltpu.roll(m_row, s * SSM_P, axis=1), 0.0))
    return jnp.concatenate(blocks, axis=0).astype(BF16)


def _ssm_kernel(u_ref, m_ref, w_ref, v_ref, p_ref, y_ref, gx_ref, gz_ref, h_ref, *, groups, nchunk, bsz):
    rows = nchunk * bsz
    row = lax.broadcasted_iota(jnp.int32, (rows, LANES), 0)
    first_chunk = (row % nchunk) == 0

    def prev_chunk(a):
        return jnp.where(first_chunk, 0.0, pltpu.roll(a, 1, axis=0))

    for g in range(groups):
        e = jnp.dot(u_ref[g], w_ref[g], preferred_element_type=F32)
        ex, ez = e[:, :LANES], e[:, LANES:]
        exs, ezs = prev_chunk(ex), prev_chunk(ez)
        p1, p2, p3 = p_ref[g, 0:1, :], p_ref[g, 1:2, :], p_ref[g, 2:3, :]
        gx = ex + p1 * exs + p2 * ezs
        gz = ez + p1 * ezs + p3 * exs
        for b in range(bsz):
            gx_ref[g, pl.ds(b, nchunk, stride=bsz), :] = gx[b * nchunk:(b + 1) * nchunk]
            gz_ref[g, pl.ds(b, nchunk, stride=bsz), :] = gz[b * nchunk:(b + 1) * nchunk]

    def step(j, carry):
        r0 = pl.multiple_of(j * SUBLANES, SUBLANES)
        new = []
        for g in range(groups):
            x, z = carry[g]
            q1, q2, q3 = p_ref[g, 3:4, :], p_ref[g, 4:5, :], p_ref[g, 5:6, :]
            xn = q1 * x + q2 * z + gx_ref[g, pl.ds(r0, SUBLANES), :]
            zn = q1 * z + q3 * x + gz_ref[g, pl.ds(r0, SUBLANES), :]
            h_ref[g, pl.ds(r0, SUBLANES), :] = xn
            new.append((xn, zn))
        return tuple(new)

    zero = jnp.zeros((SUBLANES, LANES), F32)
    lax.fori_loop(0, rows // SUBLANES, step, tuple((zero, zero) for _ in range(groups)))

    for g in range(groups):
        h_end = jnp.concatenate([h_ref[g, pl.ds(b, nchunk, stride=bsz), :] for b in range(bsz)], axis=0)
        h_in = prev_chunk(h_end).astype(BF16)
        y = jnp.dot(u_ref[g], _toeplitz(m_ref[g]), preferred_element_type=F32)
        y = y + jnp.dot(h_in, v_ref[g], preferred_element_type=F32)
        y_ref[g] = y.astype(y_ref.dtype)


def _ssm(uc, tables, bsz, seq):
    m_row, w_mat, v_mat, ptab = tables
    g, rows, _ = uc.shape
    nchunk = seq // SSM_CHUNK
    assert rows == nchunk * bsz and SUBLANES % bsz == 0 and SUBLANES // bsz == 2
    gs = 4
    blk = lambda a, b: pl.BlockSpec((gs, a, b), lambda i: (i, 0, 0))
    return pl.pallas_call(
        functools.partial(_ssm_kernel, groups=gs, nchunk=nchunk, bsz=bsz),
        grid=(g // gs,),
        in_specs=[blk(rows, CHUNK_W), blk(SSM_P, CHUNK_W), blk(CHUNK_W, CHUNK_W),
                  blk(LANES, CHUNK_W), blk(SUBLANES, LANES)],
        out_specs=blk(rows, CHUNK_W),
        out_shape=jax.ShapeDtypeStruct((g, rows, CHUNK_W), BF16),
        scratch_shapes=[pltpu.VMEM((gs, rows, LANES), F32)] * 3,
        compiler_params=_cparams(("parallel",), 48),
        name="ssm",
    )(uc, m_row, w_mat, v_mat, ptab)


def _rms(x):
    return x * lax.rsqrt(jnp.mean(x * x, axis=-1, keepdims=True) + EPS)


def _mix_kernel(o1_ref, l1_ref, o4_ref, l4_ref, o16_ref, l16_ref, y_ref, x_ref, mod_ref,
                ga_ref, gs_ref, gp_ref, ex_ref, wg_ref, bg_ref, wo_ref, out_ref,
                so_ref, sl_ref, sy_ref, *, tm):
    for gi, (dil, o_ref, l_ref) in enumerate(((4, o4_ref, l4_ref), (16, o16_ref, l16_ref))):
        for r in range(dil):
            rows = pl.ds(r, tm // dil, stride=dil)
            for jj in range(GROUP_SLABS):
                so_ref[gi * GROUP_SLABS + jj, rows, :] = o_ref[r, :, _slab(jj)].astype(F32)
            sl_ref[gi, rows, :] = l_ref[r]

    for gb in range(y_ref.shape[0] // SLOTS):
        ygs = [y_ref[gb * SLOTS + g].astype(F32) for g in range(SLOTS)]
        for cb in range(tm // (SSM_CHUNK * SUBLANES)):
            for half in range(2):
                vs = [yg[cb * SUBLANES:(cb + 1) * SUBLANES, _slab(half)] for yg in ygs]
                for s, o in enumerate(_slot_transpose(vs)):
                    sy_ref[gb, pl.ds(cb * SSM_CHUNK * SUBLANES + half * SLOTS + s, SUBLANES,
                                     stride=SSM_CHUNK), :] = o

    lane = lax.broadcasted_iota(jnp.int32, (tm, LANES), 1)
    even = (lane & 1) == 0
    ms, ls = [], []
    for st in (l1_ref[...], sl_ref[0], sl_ref[1]):
        sw = pltpu.roll(st, HEAD_DIM, axis=1)
        ms.append(jnp.where(even, st, sw))
        ls.append(jnp.where(even, sw, st))
    mx = jnp.maximum(jnp.maximum(ms[0], ms[1]), ms[2])
    es = [jnp.exp2(m - mx) for m in ms]
    inv = 1.0 / (es[0] * ls[0] + es[1] * ls[1] + es[2] * ls[2])
    head_lane = lane < HEADS_PER_GROUP
    wexp = [jnp.dot(jnp.where(head_lane, e * inv, 0.0).astype(BF16), ex_ref[...],
                    preferred_element_type=F32) for e in es]
    att_slabs = []
    for jj in range(GROUP_SLABS):
        a = wexp[0][:, _slab(jj)] * o1_ref[:, _slab(jj)].astype(F32)
        a = a + wexp[1][:, _slab(jj)] * so_ref[jj]
        a = a + wexp[2][:, _slab(jj)] * so_ref[GROUP_SLABS + jj]
        att_slabs.append(a)
    att = _rms(jnp.concatenate(att_slabs, axis=1)) * ga_ref[...]

    yv = jnp.concatenate([sy_ref[gb] for gb in range(sy_ref.shape[0])], axis=1)
    s = jax.nn.gelu(yv)
    z = jnp.dot(s.astype(BF16), wg_ref[...], preferred_element_type=F32) + bg_ref[...]
    s = s * jax.nn.sigmoid(z)
    ssm = _rms(s) * gs_ref[...]
    aw = att.shape[1]
    mix = jnp.dot(att.astype(BF16), wo_ref[:aw, :], preferred_element_type=F32)
    mix = mix + jnp.dot(ssm.astype(BF16), wo_ref[aw:, :], preferred_element_type=F32)
    out_ref[...] = x_ref[...] + mod_ref[2:3, :] * (_rms(mix) * gp_ref[...])


def _mix(o1, l1, o4, l4, o16, l16, yc, x2, mod, g_attn, g_ssm, g_post, w_glu_b, b_glu, w_out_b, bsz, seq):
    t_tok, d = x2.shape
    groups = yc.shape[0]
    sw = groups * SSM_P
    aw = GROUP_WIDTH
    tm = TOK_TILE
    per_b = seq // tm
    tok = lambda w: pl.BlockSpec((tm, w), lambda i: (i, 0))
    res = lambda dil, w: pl.BlockSpec((None, dil, tm // dil, w), lambda i: (i // per_b, 0, i % per_b, 0))
    head_of_lane = jnp.arange(aw)[None, :] // HEAD_DIM
    expand = (jnp.arange(LANES)[:, None] == head_of_lane).astype(BF16)
    return pl.pallas_call(
        functools.partial(_mix_kernel, tm=tm),
        grid=(t_tok // tm,),
        in_specs=[tok(aw), tok(LANES), res(4, aw), res(4, LANES), res(16, aw), res(16, LANES),
                  pl.BlockSpec((groups, tm // SSM_CHUNK, CHUNK_W), lambda i: (0, i, 0)),
                  tok(d),
                  pl.BlockSpec((None, N_MOD, d), lambda i: (i // per_b, 0, 0)),
                  _resident((1, aw)), _resident((1, sw)), _resident((1, d)),
                  _resident((LANES, aw)),
                  _resident((sw, sw)), _resident((1, sw)), _resident((aw + sw, d))],
        out_specs=tok(d),
        out_shape=jax.ShapeDtypeStruct((t_tok, d), F32),
        scratch_shapes=[pltpu.VMEM((2 * GROUP_SLABS, tm, LANES), F32),
                        pltpu.VMEM((2, tm, LANES), F32),
                        pltpu.VMEM((sw // LANES, tm, LANES), F32)],
        compiler_params=_cparams(("parallel",), 52),
        name="mix",
    )(o1, l1, o4, l4, o16, l16, yc, x2, mod, g_attn, g_ssm, g_post, expand, w_glu_b, b_glu, w_out_b)


def _mlp_kernel(x_ref, mod_ref, gpre_ref, gpost_ref, w1_ref, w2_ref, o_ref, h_ref):
    f = pl.program_id(1)

    def row_chunks(fn):
        def body(c, carry):
            fn(pl.ds(pl.multiple_of(c * ROW_CHUNK, ROW_CHUNK), ROW_CHUNK))
            return carry
        lax.fori_loop(0, x_ref.shape[0] // ROW_CHUNK, body, 0)

    @pl.when(f == 0)
    def _():
        scale = gpre_ref[...] * (1.0 + mod_ref[4:5, :])
        shift = mod_ref[3:4, :]

        def pre(rows):
            h_ref[rows, :] = (_rms(x_ref[rows, :]) * scale + shift).astype(BF16)
            o_ref[rows, :] = jnp.zeros((ROW_CHUNK, o_ref.shape[1]), F32)
        row_chunks(pre)

    t = jnp.dot(h_ref[...], w1_ref[...], preferred_element_type=F32)
    t = jnp.square(jnp.maximum(t, 0.0)).astype(BF16)
    o_ref[...] += jnp.dot(t, w2_ref[...], preferred_element_type=F32)

    @pl.when(f == pl.num_programs(1) - 1)
    def _():
        gate = mod_ref[5:6, :] * gpost_ref[...]

        def post(rows):
            o_ref[rows, :] = x_ref[rows, :] + gate * _rms(o_ref[rows, :])
        row_chunks(post)


def _mlp(x2, mod, g_pre, g_post, w1_b, w2_b, seq):
    t_tok, d = x2.shape
    d_ff = w1_b.shape[1]
    tm, tf = 512, 2048
    per_b = seq // tm
    return pl.pallas_call(
        _mlp_kernel,
        grid=(t_tok // tm, d_ff // tf),
        in_specs=[pl.BlockSpec((tm, d), lambda i, f: (i, 0)),
                  pl.BlockSpec((None, N_MOD, d), lambda i, f: (i // per_b, 0, 0)),
                  _resident((1, d)), _resident((1, d)),
                  pl.BlockSpec((d, tf), lambda i, f: (0, f)),
                  pl.BlockSpec((tf, d), lambda i, f: (f, 0))],
        out_specs=pl.BlockSpec((tm, d), lambda i, f: (i, 0)),
        out_shape=jax.ShapeDtypeStruct((t_tok, d), F32),
        scratch_shapes=[pltpu.VMEM((tm, d), BF16)],
        compiler_params=_cparams(("parallel", "arbitrary"), 60),
        name="mlp",
    )(x2, mod, g_pre, g_post, w1_b, w2_b)


def _rope_cos_sin(positions):
    freqs = ROPE_THETA ** (-jnp.arange(0, ROT_DIM, 2, dtype=F32) / ROT_DIM)
    ang = positions.reshape(-1).astype(F32)[:, None] * freqs
    return jnp.concatenate([jnp.cos(ang), jnp.sin(ang)], axis=1)


def kernel(x, c, positions, w_ada, b_ada, g_pre_mix, g_post_mix, w_in, ssm_a_re, ssm_a_im, ssm_log_dt,
           ssm_b_re, ssm_b_im, ssm_c_re, ssm_c_im, ssm_d, w_glu, b_glu, g_attn_out, g_ssm_out, w_out,
           g_pre_mlp, g_post_mlp, w_mlp_in, w_mlp_out):
    bsz, seq, d = x.shape
    depth = w_ada.shape[0]
    x2 = x.reshape(bsz * seq, d)
    cos_sin = _rope_cos_sin(positions)
    row = lambda a: a.reshape(1, -1).astype(F32)
    for l in range(depth):
        mod = _adaln(c, w_ada[l], b_ada[l])
        q1, q4, q16, k1, k4, k16, v1, v4, v16, uc = _inproj(
            x2, mod, row(g_pre_mix[l]), w_in[l].astype(BF16), cos_sin, bsz, seq)

        nat = lambda a: a.reshape(bsz, 1, seq, a.shape[-1])
        o1, l1 = _attn_group(nat(q1), nat(k1), nat(v1), 1, bsz, seq)
        o4, l4 = _attn_group(q4, k4, v4, 4, bsz, seq)
        o16, l16 = _attn_group(q16, k16, v16, 16, bsz, seq)

        tables = _ssm_tables(ssm_a_re[l], ssm_a_im[l], ssm_log_dt[l], ssm_b_re[l], ssm_b_im[l],
                             ssm_c_re[l], ssm_c_im[l], ssm_d[l])
        yc = _ssm(uc, tables, bsz, seq)

        x2 = _mix(o1.reshape(bsz * seq, -1), l1.reshape(bsz * seq, -1), o4, l4, o16, l16, yc, x2, mod,
                  row(g_attn_out[l]), row(g_ssm_out[l]), row(g_post_mix[l]),
                  w_glu[l].astype(BF16), row(b_glu[l]), w_out[l].astype(BF16), bsz, seq)
        x2 = _mlp(x2, mod, row(g_pre_mlp[l]), row(g_post_mlp[l]),
                  w_mlp_in[l].astype(BF16), w_mlp_out[l].astype(BF16), seq)
    return x2.reshape(bsz, seq, d)
```

```python
import functools
import math

import jax
import jax.numpy as jnp
from jax import lax
from jax.experimental import pallas as pl
from jax.experimental.pallas import tpu as pltpu

F32 = jnp.float32
BF16 = jnp.bfloat16

LANES = 128
SUBLANES = 8

HEAD_DIM = 64
ROT_DIM = HEAD_DIM // 4
ROPE_THETA = 500000.0
DILATIONS = (1, 4, 16)
HEADS_PER_GROUP = 6
GROUP_WIDTH = HEADS_PER_GROUP * HEAD_DIM
GROUP_SLABS = GROUP_WIDTH // LANES
ATT_BLK = 128
SSM_P = 16
SSM_N = 64
SSM_CHUNK = 16
CHUNK_W = SSM_CHUNK * SSM_P
SLOTS = LANES // SSM_P
N_MOD = 6
EPS = 1e-6
NEG = -1e30
TOK_TILE = 512
ROW_CHUNK = 128


def _cparams(sem, vmem_mb):
    return pltpu.CompilerParams(dimension_semantics=sem, vmem_limit_bytes=vmem_mb * 1024 * 1024)


def _resident(shape):
    nd = len(shape)
    return pl.BlockSpec(shape, lambda *_: (0,) * nd, pipeline_mode=pl.Buffered(1))


def _slab(j):
    return slice(j * LANES, (j + 1) * LANES)


def _slot_transpose(vs):
    slot = lax.broadcasted_iota(jnp.int32, vs[0].shape, 1) // SSM_P
    vs = list(vs)
    for dist in (4, 2, 1):
        low = (slot & dist) == 0
        nxt = list(vs)
        for i in range(SLOTS):
            if i & dist == 0:
                a, b = vs[i], vs[i + dist]
                nxt[i] = jnp.where(low, a, pltpu.roll(b, dist * SSM_P, axis=1))
                nxt[i + dist] = jnp.where(low, pltpu.roll(a, LANES - dist * SSM_P, axis=1), b)
        vs = nxt
    return vs


def _adaln_kernel(s_ref, w_ref, b_ref, o_ref):
    o_ref[...] = jnp.dot(s_ref[...], w_ref[...].astype(BF16),
                         preferred_element_type=F32) + b_ref[...]


def _adaln(c, w_ada, b_ada):
    bsz, d = c.shape
    n = w_ada.shape[1]
    tn = 1024
    s = jnp.zeros((SUBLANES, d), BF16).at[:bsz].set(jax.nn.silu(c).astype(BF16))
    out = pl.pallas_call(
        _adaln_kernel,
        grid=(n // tn,),
        in_specs=[pl.BlockSpec((SUBLANES, d), lambda j: (0, 0)),
                  pl.BlockSpec((d, tn), lambda j: (0, j)),
                  pl.BlockSpec((1, tn), lambda j: (0, j))],
        out_specs=pl.BlockSpec((SUBLANES, tn), lambda j: (0, j)),
        out_shape=jax.ShapeDtypeStruct((SUBLANES, n), F32),
        compiler_params=_cparams(("arbitrary",), 40),
        name="adaln",
    )(s, w_ada, b_ada.reshape(1, n))
    return out[:bsz].reshape(bsz, N_MOD, d)


def _inproj_kernel(x_ref, mod_ref, g_ref, w_ref, cs_ref, rx_ref, r1_ref,
                   q1_ref, q4_ref, q16_ref, k1_ref, k4_ref, k16_ref, v1_ref, v4_ref, v16_ref,
                   uc_ref, sc_ref, *, tm):
    x = x_ref[...]
    ms = jnp.mean(x * x, axis=-1, keepdims=True)
    scale = g_ref[...] * (1.0 + mod_ref[1:2, :])
    h = (x * lax.rsqrt(ms + EPS)) * scale + mod_ref[0:1, :]
    hb = h.astype(BF16)

    qk_slabs = 4 * GROUP_SLABS

    def scatter(j, jj, refs):
        for dil, ref in refs:
            if dil == 1:
                ref[:, _slab(jj)] = sc_ref[j].astype(BF16)
            else:
                for r in range(dil):
                    ref[r, :, _slab(jj)] = sc_ref[j, pl.ds(r, tm // dil, stride=dil), :].astype(BF16)

    pvu = jnp.dot(hb, w_ref[:, qk_slabs * LANES:], preferred_element_type=F32)
    for j in range(pvu.shape[1] // LANES):
        sc_ref[qk_slabs + j] = pvu[:, _slab(j)]
    for jj in range(GROUP_SLABS):
        scatter(qk_slabs + jj, jj, ((1, v1_ref), (4, v4_ref), (16, v16_ref)))

    u0 = qk_slabs + GROUP_SLABS
    ngb, nc = uc_ref.shape[0] // SLOTS, tm // SSM_CHUNK
    vs = [jnp.concatenate([sc_ref[u0 + gb, pl.ds(cb * SSM_CHUNK * SUBLANES + half * SLOTS + s, SUBLANES,
                                                 stride=SSM_CHUNK), :]
                           for gb in range(ngb) for half in range(2) for cb in range(nc // SUBLANES)],
                          axis=0) for s in range(SLOTS)]
    for g, o in enumerate(_slot_transpose(vs)):
        for gb in range(ngb):
            r0 = gb * 2 * nc
            full = jnp.concatenate([o[r0:r0 + nc], o[r0 + nc:r0 + 2 * nc]], axis=1)
            uc_ref[gb * SLOTS + g] = full.astype(BF16)

    cs = cs_ref[...]
    cs_hi = cs.astype(BF16)
    cs_lo = (cs - cs_hi.astype(F32)).astype(BF16)
    tab = jnp.dot(cs_hi, rx_ref[...], preferred_element_type=F32) + \
        jnp.dot(cs_lo, rx_ref[...], preferred_element_type=F32)
    ct = tab[:, :LANES] + r1_ref[...]
    st = tab[:, LANES:]
    lane = lax.broadcasted_iota(jnp.int32, ct.shape, 1) % HEAD_DIM
    second_half = (lane >= ROT_DIM // 2) & (lane < ROT_DIM)
    q_refs = ((1, q1_ref),), ((4, q4_ref),), ((16, q16_ref),)
    chunk_slabs = 4
    for c0 in reversed(range(0, qk_slabs, chunk_slabs)):
        pc = jnp.dot(hb, w_ref[:, c0 * LANES:(c0 + chunk_slabs) * LANES], preferred_element_type=F32)
        for s in range(chunk_slabs):
            j = c0 + s
            grp, jj = divmod(j, GROUP_SLABS)
            t = pc[:, _slab(s)]
            partner = jnp.where(second_half,
                                pltpu.roll(t, ROT_DIM // 2, axis=1),
                                pltpu.roll(t, LANES - ROT_DIM // 2, axis=1))
            r = t * ct + partner * st
            if grp < 3:
                r = r * (math.log2(math.e) / math.sqrt(HEAD_DIM))
            sc_ref[j] = r
            scatter(j, jj, q_refs[grp] if grp < 3 else ((1, k1_ref), (4, k4_ref), (16, k16_ref)))


def _rope_spread():
    half = ROT_DIM // 2
    dim = jnp.arange(LANES)[None, :] % HEAD_DIM
    i = jnp.arange(half)[:, None]
    first, second = dim == i, dim == i + half
    ct_rows = (first | second).astype(F32)
    st_rows = second.astype(F32) - first.astype(F32)
    zero = jnp.zeros_like(ct_rows)
    spread = jnp.concatenate([jnp.concatenate([ct_rows, zero], axis=1),
                              jnp.concatenate([zero, st_rows], axis=1)], axis=0)
    return spread.astype(BF16), (dim >= ROT_DIM).astype(F32)


def _inproj(x2, mod, g_pre, w_in_b, cos_sin, bsz, seq):
    t_tok, d = x2.shape
    in_w = w_in_b.shape[1]
    ssm_w = in_w - 5 * GROUP_WIDTH
    groups = ssm_w // SSM_P
    tm = TOK_TILE
    per_b = seq // tm
    tok = lambda w: pl.BlockSpec((tm, w), lambda i: (i, 0))

    def res_spec(dil):
        return pl.BlockSpec((None, dil, tm // dil, GROUP_WIDTH), lambda i: (i // per_b, 0, i % per_b, 0))

    def res_shape(dil):
        return jax.ShapeDtypeStruct((bsz, dil, seq // dil, GROUP_WIDTH), BF16)

    nat = jax.ShapeDtypeStruct((t_tok, GROUP_WIDTH), BF16)
    out_shape = [nat, res_shape(4), res_shape(16)] * 3 + \
                [jax.ShapeDtypeStruct((groups, t_tok // SSM_CHUNK, CHUNK_W), BF16)]
    out_specs = [tok(GROUP_WIDTH), res_spec(4), res_spec(16)] * 3 + \
                [pl.BlockSpec((groups, tm // SSM_CHUNK, CHUNK_W), lambda i: (0, i, 0))]
    return pl.pallas_call(
        functools.partial(_inproj_kernel, tm=tm),
        grid=(t_tok // tm,),
        in_specs=[tok(d),
                  pl.BlockSpec((None, N_MOD, d), lambda i: (i // per_b, 0, 0)),
                  _resident((1, d)),
                  _resident((d, in_w)),
                  tok(ROT_DIM), _resident((ROT_DIM, 2 * LANES)), _resident((1, LANES))],
        out_specs=out_specs,
        out_shape=out_shape,
        scratch_shapes=[pltpu.VMEM((in_w // LANES, tm, LANES), F32)],
        compiler_params=_cparams(("parallel",), 56),
        name="inproj",
    )(x2, mod, g_pre, w_in_b, cos_sin, *_rope_spread())


def _stat_lanes(h):
    a, b = slice(h, h + 1), slice(HEAD_DIM + h, HEAD_DIM + h + 1)
    return (a, b) if h % 2 == 0 else (b, a)


def _attn_kernel(q_ref, k_ref, v_ref, o_ref, s_ref, sc_ref, pr_ref, mk_ref, *, nblk, units):
    pair = 2 * ATT_BLK
    qi = lax.broadcasted_iota(jnp.int32, (pair, 2 * ATT_BLK), 0) % ATT_BLK
    kj = lax.broadcasted_iota(jnp.int32, (pair, 2 * ATT_BLK), 1)
    rel = qi - kj
    mk_ref[0] = jnp.where((rel >= 0) & (rel <= ATT_BLK), 0.0, NEG)
    mk_ref[1] = jnp.where((rel + ATT_BLK >= 0) & (rel <= 0), 0.0, NEG)
    low = lax.broadcasted_iota(jnp.int32, (ATT_BLK, LANES), 1) < HEAD_DIM
    top, bot = slice(0, ATT_BLK), slice(ATT_BLK, pair)
    ones = jnp.ones((2 * ATT_BLK, LANES), BF16)

    def coords(u):
        u = jnp.minimum(u, units - 1)
        r, i = u // nblk, u % nblk
        rows = pl.ds(pl.multiple_of(i * ATT_BLK, ATT_BLK), ATT_BLK)
        krows = pl.ds(pl.multiple_of(jnp.maximum(i - 1, 0) * ATT_BLK, ATT_BLK), 2 * ATT_BLK)
        return r, i, rows, krows

    def scores(u, slot):
        r, i, rows, krows = coords(u)
        bias = mk_ref[jnp.minimum(i, 1)]
        for jj in range(GROUP_SLABS):
            q2 = q_ref[r, rows, _slab(jj)]
            zero = jnp.zeros_like(q2)
            qp = jnp.concatenate([jnp.where(low, q2, zero), jnp.where(low, zero, q2)], axis=0)
            s = lax.dot_general(qp, k_ref[r, krows, _slab(jj)], (((1,), (1,)), ((), ())),
                                preferred_element_type=F32)
            sc_ref[slot, jj] = s + bias

    def softmax(u, slot):
        r, _, rows, _ = coords(u)
        s_ref[r, rows, :] = jnp.zeros((ATT_BLK, LANES), F32)
        for jj in range(GROUP_SLABS):
            s = sc_ref[slot, jj]
            m = jnp.max(s, axis=1, keepdims=True)
            pr_ref[slot, jj] = jnp.exp2(s - m).astype(BF16)
            s_ref[r, rows, _stat_lanes(2 * jj)[0]] = m[top]
            s_ref[r, rows, _stat_lanes(2 * jj + 1)[0]] = m[bot]

    def values(u, slot):
        r, _, rows, krows = coords(u)
        for jj in range(GROUP_SLABS):
            v_ext = jnp.concatenate([v_ref[r, krows, _slab(jj)], ones], axis=1)
            oe = jnp.dot(pr_ref[slot, jj], v_ext, preferred_element_type=F32)
            l_even, l_odd = _stat_lanes(2 * jj)[1], _stat_lanes(2 * jj + 1)[1]
            s_ref[r, rows, l_even] = oe[top, LANES:][:, l_even]
            s_ref[r, rows, l_odd] = oe[bot, LANES:][:, l_odd]
            o_ref[r, rows, _slab(jj)] = jnp.where(low, oe[top, :LANES], oe[bot, :LANES]).astype(o_ref.dtype)

    scores(0, 0)
    softmax(0, 0)
    scores(1, 1)

    def body(t, carry):
        u = 2 * t
        scores(u + 2, 0)
        softmax(u + 1, 1)
        values(u, 0)
        scores(u + 3, 1)
        softmax(u + 2, 0)
        values(u + 1, 1)
        return carry

    assert units % 2 == 0
    lax.fori_loop(0, units // 2, body, 0)


def _attn_group(q, k, v, dilation, bsz, seq):
    n = seq // dilation
    nblk = n // ATT_BLK
    spec = lambda w: pl.BlockSpec((None, dilation, n, w), lambda b: (b, 0, 0, 0))
    win = (2 * ATT_BLK, 2 * ATT_BLK)
    return pl.pallas_call(
        functools.partial(_attn_kernel, nblk=nblk, units=dilation * nblk),
        grid=(bsz,),
        in_specs=[spec(GROUP_WIDTH)] * 3,
        out_specs=[spec(GROUP_WIDTH), spec(LANES)],
        out_shape=[jax.ShapeDtypeStruct((bsz, dilation, n, GROUP_WIDTH), BF16),
                   jax.ShapeDtypeStruct((bsz, dilation, n, LANES), F32)],
        scratch_shapes=[pltpu.VMEM((2, GROUP_SLABS) + win, F32),
                        pltpu.VMEM((2, GROUP_SLABS) + win, BF16),
                        pltpu.VMEM((2,) + win, F32)],
        compiler_params=_cparams(("parallel",), 48),
        name=f"attn_d{dilation}",
    )(q, k, v)


def _cmul(a, b):
    return a[0] * b[0] - a[1] * b[1], a[0] * b[1] + a[1] * b[0]


def _ssm_tables(a_re, a_im, log_dt, b_re, b_im, c_re, c_im, d_skip):
    hp = lax.Precision.HIGH
    f = lambda t: t.astype(F32)
    lam = (f(a_re), f(a_im))
    dt = jnp.exp(f(log_dt))[:, None]
    g = a_re.shape[0]
    taus = jnp.arange(2 * SSM_CHUNK + 1, dtype=F32)[None, :, None]
    mag = jnp.exp(lam[0][:, None, :] * dt[:, None, :] * taus)
    ang = lam[1][:, None, :] * dt[:, None, :] * taus
    apow = (mag * jnp.cos(ang), mag * jnp.sin(ang))
    a_bar = (apow[0][:, 1], apow[1][:, 1])
    den = lam[0] * lam[0] + lam[1] * lam[1]
    coef = _cmul((a_bar[0] - 1.0, a_bar[1]), (lam[0] / den, -lam[1] / den))
    b_bar = _cmul((coef[0][..., None], coef[1][..., None]), (f(b_re), f(b_im)))
    c_mat = (f(c_re), f(c_im))

    ca = _cmul((c_mat[0][:, None], c_mat[1][:, None]),
               (apow[0][:, :SSM_CHUNK, None, :], apow[1][:, :SSM_CHUNK, None, :]))
    kern = jnp.einsum('gtpn,gnq->gtpq', ca[0], b_bar[0], precision=hp) - \
        jnp.einsum('gtpn,gnq->gtpq', ca[1], b_bar[1], precision=hp)
    kern = kern.at[:, 0].add(f(d_skip)[:, :, None] * jnp.eye(SSM_P, dtype=F32))
    m_row = kern.transpose(0, 3, 1, 2).reshape(g, SSM_P, CHUNK_W)

    rev = (apow[0][:, SSM_CHUNK - 1::-1][:, :SSM_CHUNK], apow[1][:, SSM_CHUNK - 1::-1][:, :SSM_CHUNK])
    bt = (b_bar[0].transpose(0, 2, 1)[:, None], b_bar[1].transpose(0, 2, 1)[:, None])
    w4 = _cmul((rev[0][:, :, None, :], rev[1][:, :, None, :]), bt)
    w_re, w_im = (t.reshape(g, CHUNK_W, SSM_N) for t in w4)
    w_mat = jnp.concatenate([w_re, w_im, w_im, w_re], axis=-1)

    ct = (c_mat[0].transpose(0, 2, 1)[:, :, None, :], c_mat[1].transpose(0, 2, 1)[:, :, None, :])
    at = (apow[0][:, 1:SSM_CHUNK + 1].transpose(0, 2, 1)[..., None],
          apow[1][:, 1:SSM_CHUNK + 1].transpose(0, 2, 1)[..., None])
    v4 = _cmul(ct, at)
    v_re, v_im = (t.reshape(g, SSM_N, CHUNK_W) for t in v4)
    v_mat = jnp.concatenate([v_re, -v_im], axis=1)

    def mul_rows(tau):
        re, im = apow[0][:, tau], apow[1][:, tau]
        return [jnp.concatenate([re, re], -1), jnp.concatenate([-im, im], -1),
                jnp.concatenate([im, -im], -1)]
    ptab = jnp.stack(mul_rows(SSM_CHUNK) + mul_rows(2 * SSM_CHUNK) +
                     [jnp.zeros((g, LANES), F32)] * 2, axis=1)
    return m_row, w_mat.astype(BF16), v_mat.astype(BF16), ptab


def _toeplitz(m_row):
    lane = lax.broadcasted_iota(jnp.int32, m_row.shape, 1)
    blocks = [m_row]
    for s in range(1, SSM_CHUNK):
        blocks.append(jnp.where(lane >= s * SSM_P, pltpu.roll(m_row, s * SSM_P, axis=1), 0.0))
    return jnp.concatenate(blocks, axis=0).astype(BF16)


def _ssm_kernel(u_ref, m_ref, w_ref, v_ref, p_ref, y_ref, gx_ref, gz_ref, h_ref, *, groups, nchunk, bsz):
    rows = nchunk * bsz
    row = lax.broadcasted_iota(jnp.int32, (rows, LANES), 0)
    first_chunk = (row % nchunk) == 0

    def prev_chunk(a):
        return jnp.where(first_chunk, 0.0, pltpu.roll(a, 1, axis=0))

    for g in range(groups):
        e = jnp.dot(u_ref[g], w_ref[g], preferred_element_type=F32)
        ex, ez = e[:, :LANES], e[:, LANES:]
        exs, ezs = prev_chunk(ex), prev_chunk(ez)
        p1, p2, p3 = p_ref[g, 0:1, :], p_ref[g, 1:2, :], p_ref[g, 2:3, :]
        gx = ex + p1 * exs + p2 * ezs
        gz = ez + p1 * ezs + p3 * exs
        for b in range(bsz):
            gx_ref[g, pl.ds(b, nchunk, stride=bsz), :] = gx[b * nchunk:(b + 1) * nchunk]
            gz_ref[g, pl.ds(b, nchunk, stride=bsz), :] = gz[b * nchunk:(b + 1) * nchunk]

    def step(j, carry):
        r0 = pl.multiple_of(j * SUBLANES, SUBLANES)
        new = []
        for g in range(groups):
            x, z = carry[g]
            q1, q2, q3 = p_ref[g, 3:4, :], p_ref[g, 4:5, :], p_ref[g, 5:6, :]
            xn = q1 * x + q2 * z + gx_ref[g, pl.ds(r0, SUBLANES), :]
            zn = q1 * z + q3 * x + gz_ref[g, pl.ds(r0, SUBLANES), :]
            h_ref[g, pl.ds(r0, SUBLANES), :] = xn
            new.append((xn, zn))
        return tuple(new)

    zero = jnp.zeros((SUBLANES, LANES), F32)
    lax.fori_loop(0, rows // SUBLANES, step, tuple((zero, zero) for _ in range(groups)))

    for g in range(groups):
        h_end = jnp.concatenate([h_ref[g, pl.ds(b, nchunk, stride=bsz), :] for b in range(bsz)], axis=0)
        h_in = prev_chunk(h_end).astype(BF16)
        y = jnp.dot(u_ref[g], _toeplitz(m_ref[g]), preferred_element_type=F32)
        y = y + jnp.dot(h_in, v_ref[g], preferred_element_type=F32)
        y_ref[g] = y.astype(y_ref.dtype)


def _ssm(uc, tables, bsz, seq):
    m_row, w_mat, v_mat, ptab = tables
    g, rows, _ = uc.shape
    nchunk = seq // SSM_CHUNK
    assert rows == nchunk * bsz and SUBLANES % bsz == 0 and SUBLANES // bsz == 2
    gs = 4
    blk = lambda a, b: pl.BlockSpec((gs, a, b), lambda i: (i, 0, 0))
    return pl.pallas_call(
        functools.partial(_ssm_kernel, groups=gs, nchunk=nchunk, bsz=bsz),
        grid=(g // gs,),
        in_specs=[blk(rows, CHUNK_W), blk(SSM_P, CHUNK_W), blk(CHUNK_W, CHUNK_W),
                  blk(LANES, CHUNK_W), blk(SUBLANES, LANES)],
        out_specs=blk(rows, CHUNK_W),
        out_shape=jax.ShapeDtypeStruct((g, rows, CHUNK_W), BF16),
        scratch_shapes=[pltpu.VMEM((gs, rows, LANES), F32)] * 3,
        compiler_params=_cparams(("parallel",), 48),
        name="ssm",
    )(uc, m_row, w_mat, v_mat, ptab)


def _rms(x):
    return x * lax.rsqrt(jnp.mean(x * x, axis=-1, keepdims=True) + EPS)


def _mix_kernel(o1_ref, l1_ref, o4_ref, l4_ref, o16_ref, l16_ref, y_ref, x_ref, mod_ref,
                ga_ref, gs_ref, gp_ref, ex_ref, wg_ref, bg_ref, wo_ref, out_ref,
                so_ref, sl_ref, sy_ref, *, tm):
    for gi, (dil, o_ref, l_ref) in enumerate(((4, o4_ref, l4_ref), (16, o16_ref, l16_ref))):
        for r in range(dil):
            rows = pl.ds(r, tm // dil, stride=dil)
            for jj in range(GROUP_SLABS):
                so_ref[gi * GROUP_SLABS + jj, rows, :] = o_ref[r, :, _slab(jj)].astype(F32)
            sl_ref[gi, rows, :] = l_ref[r]

    ngb, nc = y_ref.shape[0] // SLOTS, tm // SSM_CHUNK
    vs = [jnp.concatenate([y_ref[gb * SLOTS + g, :, _slab(half)].astype(F32)
                           for gb in range(ngb) for half in range(2)], axis=0)
          for g in range(SLOTS)]
    for s, o in enumerate(_slot_transpose(vs)):
        for gb in range(ngb):
            for half in range(2):
                for cb in range(nc // SUBLANES):
                    r0 = (gb * 2 + half) * nc + cb * SUBLANES
                    sy_ref[gb, pl.ds(cb * SSM_CHUNK * SUBLANES + half * SLOTS + s, SUBLANES,
                                     stride=SSM_CHUNK), :] = o[r0:r0 + SUBLANES]

    lane = lax.broadcasted_iota(jnp.int32, (tm, LANES), 1)
    even = (lane & 1) == 0
    ms, ls = [], []
    for st in (l1_ref[...], sl_ref[0], sl_ref[1]):
        sw = pltpu.roll(st, HEAD_DIM, axis=1)
        ms.append(jnp.where(even, st, sw))
        ls.append(jnp.where(even, sw, st))
    mx = jnp.maximum(jnp.maximum(ms[0], ms[1]), ms[2])
    es = [jnp.exp2(m - mx) for m in ms]
    inv = 1.0 / (es[0] * ls[0] + es[1] * ls[1] + es[2] * ls[2])
    head_lane = lane < HEADS_PER_GROUP
    wexp = [jnp.dot(jnp.where(head_lane, e * inv, 0.0).astype(BF16), ex_ref[...],
                    preferred_element_type=F32) for e in es]
    att_slabs = []
    for jj in range(GROUP_SLABS):
        a = wexp[0][:, _slab(jj)] * o1_ref[:, _slab(jj)].astype(F32)
        a = a + wexp[1][:, _slab(jj)] * so_ref[jj]
        a = a + wexp[2][:, _slab(jj)] * so_ref[GROUP_SLABS + jj]
        att_slabs.append(a)
    att = _rms(jnp.concatenate(att_slabs, axis=1)) * ga_ref[...]

    yv = jnp.concatenate([sy_ref[gb] for gb in range(sy_ref.shape[0])], axis=1)
    s = jax.nn.gelu(yv)
    z = jnp.dot(s.astype(BF16), wg_ref[...], preferred_element_type=F32) + bg_ref[...]
    s = s * jax.nn.sigmoid(z)
    ssm = _rms(s) * gs_ref[...]
    aw = att.shape[1]
    mix = jnp.dot(att.astype(BF16), wo_ref[:aw, :], preferred_element_type=F32)
    mix = mix + jnp.dot(ssm.astype(BF16), wo_ref[aw:, :], preferred_element_type=F32)
    out_ref[...] = x_ref[...] + mod_ref[2:3, :] * (_rms(mix) * gp_ref[...])


def _mix(o1, l1, o4, l4, o16, l16, yc, x2, mod, g_attn, g_ssm, g_post, w_glu_b, b_glu, w_out_b, bsz, seq):
    t_tok, d = x2.shape
    groups = yc.shape[0]
    sw = groups * SSM_P
    aw = GROUP_WIDTH
    tm = TOK_TILE
    per_b = seq // tm
    tok = lambda w: pl.BlockSpec((tm, w), lambda i: (i, 0))
    res = lambda dil, w: pl.BlockSpec((None, dil, tm // dil, w), lambda i: (i // per_b, 0, i % per_b, 0))
    head_of_lane = jnp.arange(aw)[None, :] // HEAD_DIM
    expand = (jnp.arange(LANES)[:, None] == head_of_lane).astype(BF16)
    return pl.pallas_call(
        functools.partial(_mix_kernel, tm=tm),
        grid=(t_tok // tm,),
        in_specs=[tok(aw), tok(LANES), res(4, aw), res(4, LANES), res(16, aw), res(16, LANES),
                  pl.BlockSpec((groups, tm // SSM_CHUNK, CHUNK_W), lambda i: (0, i, 0)),
                  tok(d),
                  pl.BlockSpec((None, N_MOD, d), lambda i: (i // per_b, 0, 0)),
                  _resident((1, aw)), _resident((1, sw)), _resident((1, d)),
                  _resident((LANES, aw)),
                  _resident((sw, sw)), _resident((1, sw)), _resident((aw + sw, d))],
        out_specs=tok(d),
        out_shape=jax.ShapeDtypeStruct((t_tok, d), F32),
        scratch_shapes=[pltpu.VMEM((2 * GROUP_SLABS, tm, LANES), F32),
                        pltpu.VMEM((2, tm, LANES), F32),
                        pltpu.VMEM((sw // LANES, tm, LANES), F32)],
        compiler_params=_cparams(("parallel",), 52),
        name="mix",
    )(o1, l1, o4, l4, o16, l16, yc, x2, mod, g_attn, g_ssm, g_post, expand, w_glu_b, b_glu, w_out_b)


def _mlp_kernel(x_ref, mod_ref, gpre_ref, gpost_ref, w1_ref, w2_ref, o_ref, h_ref):
    f = pl.program_id(1)
    last = pl.num_programs(1) - 1
    chunks = [slice(r, r + ROW_CHUNK) for r in range(0, x_ref.shape[0], ROW_CHUNK)]

    def partial():
        t = jnp.dot(h_ref[...], w1_ref[...], preferred_element_type=F32)
        t = jnp.square(jnp.maximum(t, 0.0)).astype(BF16)
        return jnp.dot(t, w2_ref[...], preferred_element_type=F32)

    @pl.when(f == 0)
    def _():
        scale = gpre_ref[...] * (1.0 + mod_ref[4:5, :])
        shift = mod_ref[3:4, :]
        for rows in chunks:
            h_ref[rows, :] = (_rms(x_ref[rows, :]) * scale + shift).astype(BF16)
        o_ref[...] = partial()

    @pl.when((f > 0) & (f < last))
    def _():
        o_ref[...] += partial()

    @pl.when(f == last)
    def _():
        gate = mod_ref[5:6, :] * gpost_ref[...]
        acc = o_ref[...] + partial()
        for rows in chunks:
            o_ref[rows, :] = x_ref[rows, :] + gate * _rms(acc[rows, :])


def _mlp(x2, mod, g_pre, g_post, w1_b, w2_b, seq):
    t_tok, d = x2.shape
    d_ff = w1_b.shape[1]
    tm, tf = 512, 2048
    assert d_ff // tf >= 2
    per_b = seq // tm
    return pl.pallas_call(
        _mlp_kernel,
        grid=(t_tok // tm, d_ff // tf),
        in_specs=[pl.BlockSpec((tm, d), lambda i, f: (i, 0)),
                  pl.BlockSpec((None, N_MOD, d), lambda i, f: (i // per_b, 0, 0)),
                  _resident((1, d)), _resident((1, d)),
                  pl.BlockSpec((d, tf), lambda i, f: (0, f)),
                  pl.BlockSpec((tf, d), lambda i, f: (f, 0))],
        out_specs=pl.BlockSpec((tm, d), lambda i, f: (i, 0)),
        out_shape=jax.ShapeDtypeStruct((t_tok, d), F32),
        scratch_shapes=[pltpu.VMEM((tm, d), BF16)],
        compiler_params=_cparams(("parallel", "arbitrary"), 60),
        name="mlp",
    )(x2, mod, g_pre, g_post, w1_b, w2_b)


def _rope_cos_sin(positions):
    freqs = ROPE_THETA ** (-jnp.arange(0, ROT_DIM, 2, dtype=F32) / ROT_DIM)
    ang = positions.reshape(-1).astype(F32)[:, None] * freqs
    return jnp.concatenate([jnp.cos(ang), jnp.sin(ang)], axis=1)


def kernel(x, c, positions, w_ada, b_ada, g_pre_mix, g_post_mix, w_in, ssm_a_re, ssm_a_im, ssm_log_dt,
           ssm_b_re, ssm_b_im, ssm_c_re, ssm_c_im, ssm_d, w_glu, b_glu, g_attn_out, g_ssm_out, w_out,
           g_pre_mlp, g_post_mlp, w_mlp_in, w_mlp_out):
    bsz, seq, d = x.shape
    depth = w_ada.shape[0]
    x2 = x.reshape(bsz * seq, d)
    cos_sin = _rope_cos_sin(positions)
    row = lambda a: a.reshape(1, -1).astype(F32)
    for l in range(depth):
        mod = _adaln(c, w_ada[l], b_ada[l])
        q1, q4, q16, k1, k4, k16, v1, v4, v16, uc = _inproj(
            x2, mod, row(g_pre_mix[l]), w_in[l].astype(BF16), cos_sin, bsz, seq)

        nat = lambda a: a.reshape(bsz, 1, seq, a.shape[-1])
        o1, l1 = _attn_group(nat(q1), nat(k1), nat(v1), 1, bsz, seq)
        o4, l4 = _attn_group(q4, k4, v4, 4, bsz, seq)
        o16, l16 = _attn_group(q16, k16, v16, 16, bsz, seq)

        tables = _ssm_tables(ssm_a_re[l], ssm_a_im[l], ssm_log_dt[l], ssm_b_re[l], ssm_b_im[l],
                             ssm_c_re[l], ssm_c_im[l], ssm_d[l])
        yc = _ssm(uc, tables, bsz, seq)

        x2 = _mix(o1.reshape(bsz * seq, -1), l1.reshape(bsz * seq, -1), o4, l4, o16, l16, yc, x2, mod,
                  row(g_attn_out[l]), row(g_ssm_out[l]), row(g_post_mix[l]),
                  w_glu[l].astype(BF16), row(b_glu[l]), w_out[l].astype(BF16), bsz, seq)
        x2 = _mlp(x2, mod, row(g_pre_mlp[l]), row(g_post_mlp[l]),
                  w_mlp_in[l].astype(BF16), w_mlp_out[l].astype(BF16), seq)
    return x2.reshape(bsz, seq, d)
```

```python
import functools
import math

import jax
import jax.numpy as jnp
from jax import lax
from jax.experimental import pallas as pl
from jax.experimental.pallas import tpu as pltpu

F32 = jnp.float32
BF16 = jnp.bfloat16

LANES = 128
SUBLANES = 8

HEAD_DIM = 64
ROT_DIM = HEAD_DIM // 4
ROPE_THETA = 500000.0
DILATIONS = (1, 4, 16)
HEADS_PER_GROUP = 6
GROUP_WIDTH = HEADS_PER_GROUP * HEAD_DIM
GROUP_SLABS = GROUP_WIDTH // LANES
ATT_BLK = 128
SSM_P = 16
SSM_N = 64
SSM_CHUNK = 16
CHUNK_W = SSM_CHUNK * SSM_P
SLOTS = LANES // SSM_P
N_MOD = 6
EPS = 1e-6
NEG = -1e30
TOK_TILE = 512
ROW_CHUNK = 128


def _cparams(sem, vmem_mb):
    return pltpu.CompilerParams(dimension_semantics=sem, vmem_limit_bytes=vmem_mb * 1024 * 1024)


def _resident(shape):
    nd = len(shape)
    return pl.BlockSpec(shape, lambda *_: (0,) * nd, pipeline_mode=pl.Buffered(1))


def _slab(j):
    return slice(j * LANES, (j + 1) * LANES)


def _slot_transpose(vs):
    slot = lax.broadcasted_iota(jnp.int32, vs[0].shape, 1) // SSM_P
    vs = list(vs)
    for dist in (4, 2, 1):
        low = (slot & dist) == 0
        nxt = list(vs)
        for i in range(SLOTS):
            if i & dist == 0:
                a, b = vs[i], vs[i + dist]
                nxt[i] = jnp.where(low, a, pltpu.roll(b, dist * SSM_P, axis=1))
                nxt[i + dist] = jnp.where(low, pltpu.roll(a, LANES - dist * SSM_P, axis=1), b)
        vs = nxt
    return vs


def _adaln_kernel(s_ref, w_ref, b_ref, o_ref):
    o_ref[...] = jnp.dot(s_ref[...], w_ref[...].astype(BF16),
                         preferred_element_type=F32) + b_ref[...]


def _adaln(c, w_ada, b_ada):
    bsz, d = c.shape
    n = w_ada.shape[1]
    tn = 1024
    s = jnp.zeros((SUBLANES, d), BF16).at[:bsz].set(jax.nn.silu(c).astype(BF16))
    out = pl.pallas_call(
        _adaln_kernel,
        grid=(n // tn,),
        in_specs=[pl.BlockSpec((SUBLANES, d), lambda j: (0, 0)),
                  pl.BlockSpec((d, tn), lambda j: (0, j)),
                  pl.BlockSpec((1, tn), lambda j: (0, j))],
        out_specs=pl.BlockSpec((SUBLANES, tn), lambda j: (0, j)),
        out_shape=jax.ShapeDtypeStruct((SUBLANES, n), F32),
        compiler_params=_cparams(("arbitrary",), 40),
        name="adaln",
    )(s, w_ada, b_ada.reshape(1, n))
    return out[:bsz].reshape(bsz, N_MOD, d)


def _inproj_kernel(x_ref, mod_ref, g_ref, w_ref, cs_ref, rx_ref, r1_ref, wside_ref,
                   q1_ref, q4_ref, q16_ref, k1_ref, k4_ref, k16_ref, v1_ref, v4_ref, v16_ref,
                   uc_ref, wside_out_ref, sc_ref, *, tm):
    wside_out_ref[...] = wside_ref[...].astype(BF16)

    x = x_ref[...]
    ms = jnp.mean(x * x, axis=-1, keepdims=True)
    scale = g_ref[...] * (1.0 + mod_ref[1:2, :])
    h = (x * lax.rsqrt(ms + EPS)) * scale + mod_ref[0:1, :]
    hb = h.astype(BF16)

    qk_slabs = 4 * GROUP_SLABS

    def scatter(j, jj, refs):
        for dil, ref in refs:
            if dil == 1:
                ref[:, _slab(jj)] = sc_ref[j].astype(BF16)
            else:
                for r in range(dil):
                    ref[r, :, _slab(jj)] = sc_ref[j, pl.ds(r, tm // dil, stride=dil), :].astype(BF16)

    pvu = jnp.dot(hb, w_ref[:, qk_slabs * LANES:], preferred_element_type=F32)
    for j in range(pvu.shape[1] // LANES):
        sc_ref[qk_slabs + j] = pvu[:, _slab(j)]
    for jj in range(GROUP_SLABS):
        scatter(qk_slabs + jj, jj, ((1, v1_ref), (4, v4_ref), (16, v16_ref)))

    u0 = qk_slabs + GROUP_SLABS
    ngb, nc = uc_ref.shape[0] // SLOTS, tm // SSM_CHUNK
    vs = [jnp.concatenate([sc_ref[u0 + gb, pl.ds(cb * SSM_CHUNK * SUBLANES + half * SLOTS + s, SUBLANES,
                                                 stride=SSM_CHUNK), :]
                           for gb in range(ngb) for half in range(2) for cb in range(nc // SUBLANES)],
                          axis=0) for s in range(SLOTS)]
    for g, o in enumerate(_slot_transpose(vs)):
        for gb in range(ngb):
            r0 = gb * 2 * nc
            full = jnp.concatenate([o[r0:r0 + nc], o[r0 + nc:r0 + 2 * nc]], axis=1)
            uc_ref[gb * SLOTS + g] = full.astype(BF16)

    cs = cs_ref[...]
    cs_hi = cs.astype(BF16)
    cs_lo = (cs - cs_hi.astype(F32)).astype(BF16)
    tab = jnp.dot(cs_hi, rx_ref[...], preferred_element_type=F32) + \
        jnp.dot(cs_lo, rx_ref[...], preferred_element_type=F32)
    ct = tab[:, :LANES] + r1_ref[...]
    st = tab[:, LANES:]
    lane = lax.broadcasted_iota(jnp.int32, ct.shape, 1) % HEAD_DIM
    second_half = (lane >= ROT_DIM // 2) & (lane < ROT_DIM)
    q_refs = ((1, q1_ref),), ((4, q4_ref),), ((16, q16_ref),)
    chunk_slabs = 4
    for c0 in reversed(range(0, qk_slabs, chunk_slabs)):
        pc = jnp.dot(hb, w_ref[:, c0 * LANES:(c0 + chunk_slabs) * LANES], preferred_element_type=F32)
        for s in range(chunk_slabs):
            j = c0 + s
            grp, jj = divmod(j, GROUP_SLABS)
            t = pc[:, _slab(s)]
            partner = jnp.where(second_half,
                                pltpu.roll(t, ROT_DIM // 2, axis=1),
                                pltpu.roll(t, LANES - ROT_DIM // 2, axis=1))
            r = t * ct + partner * st
            if grp < 3:
                r = r * (math.log2(math.e) / math.sqrt(HEAD_DIM))
            sc_ref[j] = r
            scatter(j, jj, q_refs[grp] if grp < 3 else ((1, k1_ref), (4, k4_ref), (16, k16_ref)))


def _rope_spread():
    half = ROT_DIM // 2
    dim = jnp.arange(LANES)[None, :] % HEAD_DIM
    i = jnp.arange(half)[:, None]
    first, second = dim == i, dim == i + half
    ct_rows = (first | second).astype(F32)
    st_rows = second.astype(F32) - first.astype(F32)
    zero = jnp.zeros_like(ct_rows)
    spread = jnp.concatenate([jnp.concatenate([ct_rows, zero], axis=1),
                              jnp.concatenate([zero, st_rows], axis=1)], axis=0)
    return spread.astype(BF16), (dim >= ROT_DIM).astype(F32)


def _side_cast_spec(w, steps):
    rows, cols = w.shape
    assert rows % (steps * 2 * SUBLANES) == 0
    return (pl.BlockSpec((rows // steps, cols), lambda i: (i, 0)),
            jax.ShapeDtypeStruct(w.shape, BF16))


def _inproj(x2, mod, g_pre, w_in_b, cos_sin, w_side, bsz, seq):
    t_tok, d = x2.shape
    in_w = w_in_b.shape[1]
    ssm_w = in_w - 5 * GROUP_WIDTH
    groups = ssm_w // SSM_P
    tm = TOK_TILE
    per_b = seq // tm
    tok = lambda w: pl.BlockSpec((tm, w), lambda i: (i, 0))
    side_spec, side_shape = _side_cast_spec(w_side, t_tok // tm)

    def res_spec(dil):
        return pl.BlockSpec((None, dil, tm // dil, GROUP_WIDTH), lambda i: (i // per_b, 0, i % per_b, 0))

    def res_shape(dil):
        return jax.ShapeDtypeStruct((bsz, dil, seq // dil, GROUP_WIDTH), BF16)

    nat = jax.ShapeDtypeStruct((t_tok, GROUP_WIDTH), BF16)
    out_shape = [nat, res_shape(4), res_shape(16)] * 3 + \
                [jax.ShapeDtypeStruct((groups, t_tok // SSM_CHUNK, CHUNK_W), BF16), side_shape]
    out_specs = [tok(GROUP_WIDTH), res_spec(4), res_spec(16)] * 3 + \
                [pl.BlockSpec((groups, tm // SSM_CHUNK, CHUNK_W), lambda i: (0, i, 0)), side_spec]
    return pl.pallas_call(
        functools.partial(_inproj_kernel, tm=tm),
        grid=(t_tok // tm,),
        in_specs=[tok(d),
                  pl.BlockSpec((None, N_MOD, d), lambda i: (i // per_b, 0, 0)),
                  _resident((1, d)),
                  _resident((d, in_w)),
                  tok(ROT_DIM), _resident((ROT_DIM, 2 * LANES)), _resident((1, LANES)),
                  side_spec],
        out_specs=out_specs,
        out_shape=out_shape,
        scratch_shapes=[pltpu.VMEM((in_w // LANES, tm, LANES), F32)],
        compiler_params=_cparams(("parallel",), 58),
        name="inproj",
    )(x2, mod, g_pre, w_in_b, cos_sin, *_rope_spread(), w_side)


def _stat_lanes(h):
    a, b = slice(h, h + 1), slice(HEAD_DIM + h, HEAD_DIM + h + 1)
    return (a, b) if h % 2 == 0 else (b, a)


def _attn_kernel(q_ref, k_ref, v_ref, o_ref, s_ref, sc_ref, pr_ref, mk_ref, *, nblk, units):
    pair = 2 * ATT_BLK
    qi = lax.broadcasted_iota(jnp.int32, (pair, 2 * ATT_BLK), 0) % ATT_BLK
    kj = lax.broadcasted_iota(jnp.int32, (pair, 2 * ATT_BLK), 1)
    rel = qi - kj
    mk_ref[0] = jnp.where((rel >= 0) & (rel <= ATT_BLK), 0.0, NEG)
    mk_ref[1] = jnp.where((rel + ATT_BLK >= 0) & (rel <= 0), 0.0, NEG)
    low = lax.broadcasted_iota(jnp.int32, (ATT_BLK, LANES), 1) < HEAD_DIM
    top, bot = slice(0, ATT_BLK), slice(ATT_BLK, pair)
    ones = jnp.ones((2 * ATT_BLK, LANES), BF16)

    def coords(u):
        u = jnp.minimum(u, units - 1)
        r, i = u // nblk, u % nblk
        rows = pl.ds(pl.multiple_of(i * ATT_BLK, ATT_BLK), ATT_BLK)
        krows = pl.ds(pl.multiple_of(jnp.maximum(i - 1, 0) * ATT_BLK, ATT_BLK), 2 * ATT_BLK)
        return r, i, rows, krows

    def scores(u, slot):
        r, i, rows, krows = coords(u)
        bias = mk_ref[jnp.minimum(i, 1)]
        for jj in range(GROUP_SLABS):
            q2 = q_ref[r, rows, _slab(jj)]
            zero = jnp.zeros_like(q2)
            qp = jnp.concatenate([jnp.where(low, q2, zero), jnp.where(low, zero, q2)], axis=0)
            s = lax.dot_general(qp, k_ref[r, krows, _slab(jj)], (((1,), (1,)), ((), ())),
                                preferred_element_type=F32)
            sc_ref[slot, jj] = s + bias

    def softmax(u, slot):
        r, _, rows, _ = coords(u)
        s_ref[r, rows, :] = jnp.zeros((ATT_BLK, LANES), F32)
        for jj in range(GROUP_SLABS):
            s = sc_ref[slot, jj]
            m = jnp.max(s, axis=1, keepdims=True)
            pr_ref[slot, jj] = jnp.exp2(s - m).astype(BF16)
            s_ref[r, rows, _stat_lanes(2 * jj)[0]] = m[top]
            s_ref[r, rows, _stat_lanes(2 * jj + 1)[0]] = m[bot]

    def values(u, slot):
        r, _, rows, krows = coords(u)
        for jj in range(GROUP_SLABS):
            v_ext = jnp.concatenate([v_ref[r, krows, _slab(jj)], ones], axis=1)
            oe = jnp.dot(pr_ref[slot, jj], v_ext, preferred_element_type=F32)
            l_even, l_odd = _stat_lanes(2 * jj)[1], _stat_lanes(2 * jj + 1)[1]
            s_ref[r, rows, l_even] = oe[top, LANES:][:, l_even]
            s_ref[r, rows, l_odd] = oe[bot, LANES:][:, l_odd]
            o_ref[r, rows, _slab(jj)] = jnp.where(low, oe[top, :LANES], oe[bot, :LANES]).astype(o_ref.dtype)

    scores(0, 0)
    softmax(0, 0)
    scores(1, 1)

    def body(t, carry):
        u = 2 * t
        scores(u + 2, 0)
        softmax(u + 1, 1)
        values(u, 0)
        scores(u + 3, 1)
        softmax(u + 2, 0)
        values(u + 1, 1)
        return carry

    assert units % 2 == 0
    lax.fori_loop(0, units // 2, body, 0)


def _attn_group(q, k, v, dilation, bsz, seq):
    n = seq // dilation
    nblk = n // ATT_BLK
    spec = lambda w: pl.BlockSpec((None, dilation, n, w), lambda b: (b, 0, 0, 0))
    win = (2 * ATT_BLK, 2 * ATT_BLK)
    return pl.pallas_call(
        functools.partial(_attn_kernel, nblk=nblk, units=dilation * nblk),
        grid=(bsz,),
        in_specs=[spec(GROUP_WIDTH)] * 3,
        out_specs=[spec(GROUP_WIDTH), spec(LANES)],
        out_shape=[jax.ShapeDtypeStruct((bsz, dilation, n, GROUP_WIDTH), BF16),
                   jax.ShapeDtypeStruct((bsz, dilation, n, LANES), F32)],
        scratch_shapes=[pltpu.VMEM((2, GROUP_SLABS) + win, F32),
                        pltpu.VMEM((2, GROUP_SLABS) + win, BF16),
                        pltpu.VMEM((2,) + win, F32)],
        compiler_params=_cparams(("parallel",), 48),
        name=f"attn_d{dilation}",
    )(q, k, v)


def _cmul(a, b):
    return a[0] * b[0] - a[1] * b[1], a[0] * b[1] + a[1] * b[0]


def _ssm_tables(a_re, a_im, log_dt, b_re, b_im, c_re, c_im, d_skip):
    hp = lax.Precision.HIGH
    f = lambda t: t.astype(F32)
    lam = (f(a_re), f(a_im))
    dt = jnp.exp(f(log_dt))[:, None]
    g = a_re.shape[0]
    taus = jnp.arange(2 * SSM_CHUNK + 1, dtype=F32)[None, :, None]
    mag = jnp.exp(lam[0][:, None, :] * dt[:, None, :] * taus)
    ang = lam[1][:, None, :] * dt[:, None, :] * taus
    apow = (mag * jnp.cos(ang), mag * jnp.sin(ang))
    a_bar = (apow[0][:, 1], apow[1][:, 1])
    den = lam[0] * lam[0] + lam[1] * lam[1]
    coef = _cmul((a_bar[0] - 1.0, a_bar[1]), (lam[0] / den, -lam[1] / den))
    b_bar = _cmul((coef[0][..., None], coef[1][..., None]), (f(b_re), f(b_im)))
    c_mat = (f(c_re), f(c_im))

    ca = _cmul((c_mat[0][:, None], c_mat[1][:, None]),
               (apow[0][:, :SSM_CHUNK, None, :], apow[1][:, :SSM_CHUNK, None, :]))
    kern = jnp.einsum('gtpn,gnq->gtpq', ca[0], b_bar[0], precision=hp) - \
        jnp.einsum('gtpn,gnq->gtpq', ca[1], b_bar[1], precision=hp)
    kern = kern.at[:, 0].add(f(d_skip)[:, :, None] * jnp.eye(SSM_P, dtype=F32))
    m_row = kern.transpose(0, 3, 1, 2).reshape(g, SSM_P, CHUNK_W)

    rev = (apow[0][:, SSM_CHUNK - 1::-1][:, :SSM_CHUNK], apow[1][:, SSM_CHUNK - 1::-1][:, :SSM_CHUNK])
    bt = (b_bar[0].transpose(0, 2, 1)[:, None], b_bar[1].transpose(0, 2, 1)[:, None])
    w4 = _cmul((rev[0][:, :, None, :], rev[1][:, :, None, :]), bt)
    w_re, w_im = (t.reshape(g, CHUNK_W, SSM_N) for t in w4)
    w_mat = jnp.concatenate([w_re, w_im, w_im, w_re], axis=-1)

    ct = (c_mat[0].transpose(0, 2, 1)[:, :, None, :], c_mat[1].transpose(0, 2, 1)[:, :, None, :])
    at = (apow[0][:, 1:SSM_CHUNK + 1].transpose(0, 2, 1)[..., None],
          apow[1][:, 1:SSM_CHUNK + 1].transpose(0, 2, 1)[..., None])
    v4 = _cmul(ct, at)
    v_re, v_im = (t.reshape(g, SSM_N, CHUNK_W) for t in v4)
    v_mat = jnp.concatenate([v_re, -v_im], axis=1)

    def mul_rows(tau):
        re, im = apow[0][:, tau], apow[1][:, tau]
        return [jnp.concatenate([re, re], -1), jnp.concatenate([-im, im], -1),
                jnp.concatenate([im, -im], -1)]
    ptab = jnp.stack(mul_rows(SSM_CHUNK) + mul_rows(2 * SSM_CHUNK) +
                     [jnp.zeros((g, LANES), F32)] * 2, axis=1)
    return m_row, w_mat.astype(BF16), v_mat.astype(BF16), ptab


def _toeplitz(m_row):
    lane = lax.broadcasted_iota(jnp.int32, m_row.shape, 1)
    blocks = [m_row]
    for s in range(1, SSM_CHUNK):
        blocks.append(jnp.where(lane >= s * SSM_P, pltpu.roll(m_row, s * SSM_P, axis=1), 0.0))
    return jnp.concatenate(blocks, axis=0).astype(BF16)


def _ssm_kernel(u_ref, m_ref, w_ref, v_ref, p_ref, y_ref, gx_ref, gz_ref, h_ref, *, groups, nchunk, bsz):
    rows = nchunk * bsz
    row = lax.broadcasted_iota(jnp.int32, (rows, LANES), 0)
    first_chunk = (row % nchunk) == 0

    def prev_chunk(a):
        return jnp.where(first_chunk, 0.0, pltpu.roll(a, 1, axis=0))

    for g in range(groups):
        e = jnp.dot(u_ref[g], w_ref[g], preferred_element_type=F32)
        ex, ez = e[:, :LANES], e[:, LANES:]
        exs, ezs = prev_chunk(ex), prev_chunk(ez)
        p1, p2, p3 = p_ref[g, 0:1, :], p_ref[g, 1:2, :], p_ref[g, 2:3, :]
        gx = ex + p1 * exs + p2 * ezs
        gz = ez + p1 * ezs + p3 * exs
        for b in range(bsz):
            gx_ref[g, pl.ds(b, nchunk, stride=bsz), :] = gx[b * nchunk:(b + 1) * nchunk]
            gz_ref[g, pl.ds(b, nchunk, stride=bsz), :] = gz[b * nchunk:(b + 1) * nchunk]

    def step(j, carry):
        r0 = pl.multiple_of(j * SUBLANES, SUBLANES)
        new = []
        for g in range(groups):
            x, z = carry[g]
            q1, q2, q3 = p_ref[g, 3:4, :], p_ref[g, 4:5, :], p_ref[g, 5:6, :]
            xn = q1 * x + q2 * z + gx_ref[g, pl.ds(r0, SUBLANES), :]
            zn = q1 * z + q3 * x + gz_ref[g, pl.ds(r0, SUBLANES), :]
            h_ref[g, pl.ds(r0, SUBLANES), :] = xn
            new.append((xn, zn))
        return tuple(new)

    zero = jnp.zeros((SUBLANES, LANES), F32)
    lax.fori_loop(0, rows // SUBLANES, step, tuple((zero, zero) for _ in range(groups)))

    for g in range(groups):
        h_end = jnp.concatenate([h_ref[g, pl.ds(b, nchunk, stride=bsz), :] for b in range(bsz)], axis=0)
        h_in = prev_chunk(h_end).astype(BF16)
        y = jnp.dot(u_ref[g], _toeplitz(m_ref[g]), preferred_element_type=F32)
        y = y + jnp.dot(h_in, v_ref[g], preferred_element_type=F32)
        y_ref[g] = y.astype(y_ref.dtype)


def _ssm(uc, tables, bsz, seq):
    m_row, w_mat, v_mat, ptab = tables
    g, rows, _ = uc.shape
    nchunk = seq // SSM_CHUNK
    assert rows == nchunk * bsz and SUBLANES % bsz == 0 and SUBLANES // bsz == 2
    gs = 4
    blk = lambda a, b: pl.BlockSpec((gs, a, b), lambda i: (i, 0, 0))
    return pl.pallas_call(
        functools.partial(_ssm_kernel, groups=gs, nchunk=nchunk, bsz=bsz),
        grid=(g // gs,),
        in_specs=[blk(rows, CHUNK_W), blk(SSM_P, CHUNK_W), blk(CHUNK_W, CHUNK_W),
                  blk(LANES, CHUNK_W), blk(SUBLANES, LANES)],
        out_specs=blk(rows, CHUNK_W),
        out_shape=jax.ShapeDtypeStruct((g, rows, CHUNK_W), BF16),
        scratch_shapes=[pltpu.VMEM((gs, rows, LANES), F32)] * 3,
        compiler_params=_cparams(("parallel",), 48),
        name="ssm",
    )(uc, m_row, w_mat, v_mat, ptab)


def _rms(x):
    return x * lax.rsqrt(jnp.mean(x * x, axis=-1, keepdims=True) + EPS)


def _mix_kernel(o1_ref, l1_ref, o4_ref, l4_ref, o16_ref, l16_ref, y_ref, x_ref, mod_ref,
                ga_ref, gs_ref, gp_ref, ex_ref, wg_ref, bg_ref, wo_ref, wside_ref, out_ref,
                wside_out_ref, so_ref, sl_ref, sy_ref, *, tm):
    wside_out_ref[...] = wside_ref[...].astype(BF16)

    for gi, (dil, o_ref, l_ref) in enumerate(((4, o4_ref, l4_ref), (16, o16_ref, l16_ref))):
        for r in range(dil):
            rows = pl.ds(r, tm // dil, stride=dil)
            for jj in range(GROUP_SLABS):
                so_ref[gi * GROUP_SLABS + jj, rows, :] = o_ref[r, :, _slab(jj)].astype(F32)
            sl_ref[gi, rows, :] = l_ref[r]

    ngb, nc = y_ref.shape[0] // SLOTS, tm // SSM_CHUNK
    vs = [jnp.concatenate([y_ref[gb * SLOTS + g, :, _slab(half)].astype(F32)
                           for gb in range(ngb) for half in range(2)], axis=0)
          for g in range(SLOTS)]
    for s, o in enumerate(_slot_transpose(vs)):
        for gb in range(ngb):
            for half in range(2):
                for cb in range(nc // SUBLANES):
                    r0 = (gb * 2 + half) * nc + cb * SUBLANES
                    sy_ref[gb, pl.ds(cb * SSM_CHUNK * SUBLANES + half * SLOTS + s, SUBLANES,
                                     stride=SSM_CHUNK), :] = o[r0:r0 + SUBLANES]

    lane = lax.broadcasted_iota(jnp.int32, (tm, LANES), 1)
    even = (lane & 1) == 0
    ms, ls = [], []
    for st in (l1_ref[...], sl_ref[0], sl_ref[1]):
        sw = pltpu.roll(st, HEAD_DIM, axis=1)
        ms.append(jnp.where(even, st, sw))
        ls.append(jnp.where(even, sw, st))
    mx = jnp.maximum(jnp.maximum(ms[0], ms[1]), ms[2])
    es = [jnp.exp2(m - mx) for m in ms]
    inv = 1.0 / (es[0] * ls[0] + es[1] * ls[1] + es[2] * ls[2])
    head_lane = lane < HEADS_PER_GROUP
    wexp = [jnp.dot(jnp.where(head_lane, e * inv, 0.0).astype(BF16), ex_ref[...],
                    preferred_element_type=F32) for e in es]
    att_slabs = []
    for jj in range(GROUP_SLABS):
        a = wexp[0][:, _slab(jj)] * o1_ref[:, _slab(jj)].astype(F32)
        a = a + wexp[1][:, _slab(jj)] * so_ref[jj]
        a = a + wexp[2][:, _slab(jj)] * so_ref[GROUP_SLABS + jj]
        att_slabs.append(a)
    att = _rms(jnp.concatenate(att_slabs, axis=1)) * ga_ref[...]

    yv = jnp.concatenate([sy_ref[gb] for gb in range(sy_ref.shape[0])], axis=1)
    s = jax.nn.gelu(yv)
    z = jnp.dot(s.astype(BF16), wg_ref[...], preferred_element_type=F32) + bg_ref[...]
    s = s * jax.nn.sigmoid(z)
    ssm = _rms(s) * gs_ref[...]
    aw = att.shape[1]
    mix = jnp.dot(att.astype(BF16), wo_ref[:aw, :], preferred_element_type=F32)
    mix = mix + jnp.dot(ssm.astype(BF16), wo_ref[aw:, :], preferred_element_type=F32)
    out_ref[...] = x_ref[...] + mod_ref[2:3, :] * (_rms(mix) * gp_ref[...])


def _mix(o1, l1, o4, l4, o16, l16, yc, x2, mod, g_attn, g_ssm, g_post, w_glu_b, b_glu, w_out_b, w_side,
         bsz, seq):
    t_tok, d = x2.shape
    groups = yc.shape[0]
    sw = groups * SSM_P
    aw = GROUP_WIDTH
    tm = TOK_TILE
    per_b = seq // tm
    tok = lambda w: pl.BlockSpec((tm, w), lambda i: (i, 0))
    res = lambda dil, w: pl.BlockSpec((None, dil, tm // dil, w), lambda i: (i // per_b, 0, i % per_b, 0))
    head_of_lane = jnp.arange(aw)[None, :] // HEAD_DIM
    expand = (jnp.arange(LANES)[:, None] == head_of_lane).astype(BF16)
    side_spec, side_shape = _side_cast_spec(w_side, t_tok // tm)
    return pl.pallas_call(
        functools.partial(_mix_kernel, tm=tm),
        grid=(t_tok // tm,),
        in_specs=[tok(aw), tok(LANES), res(4, aw), res(4, LANES), res(16, aw), res(16, LANES),
                  pl.BlockSpec((groups, tm // SSM_CHUNK, CHUNK_W), lambda i: (0, i, 0)),
                  tok(d),
                  pl.BlockSpec((None, N_MOD, d), lambda i: (i // per_b, 0, 0)),
                  _resident((1, aw)), _resident((1, sw)), _resident((1, d)),
                  _resident((LANES, aw)),
                  _resident((sw, sw)), _resident((1, sw)), _resident((aw + sw, d)),
                  side_spec],
        out_specs=[tok(d), side_spec],
        out_shape=[jax.ShapeDtypeStruct((t_tok, d), F32), side_shape],
        scratch_shapes=[pltpu.VMEM((2 * GROUP_SLABS, tm, LANES), F32),
                        pltpu.VMEM((2, tm, LANES), F32),
                        pltpu.VMEM((sw // LANES, tm, LANES), F32)],
        compiler_params=_cparams(("parallel",), 56),
        name="mix",
    )(o1, l1, o4, l4, o16, l16, yc, x2, mod, g_attn, g_ssm, g_post, expand, w_glu_b, b_glu, w_out_b, w_side)


def _mlp_kernel(x_ref, mod_ref, gpre_ref, gpost_ref, w1_ref, w2_ref, o_ref, h_ref):
    f = pl.program_id(1)
    last = pl.num_programs(1) - 1
    chunks = [slice(r, r + ROW_CHUNK) for r in range(0, x_ref.shape[0], ROW_CHUNK)]

    def partial():
        t = jnp.dot(h_ref[...], w1_ref[...], preferred_element_type=F32)
        t = jnp.square(jnp.maximum(t, 0.0)).astype(BF16)
        return jnp.dot(t, w2_ref[...], preferred_element_type=F32)

    @pl.when(f == 0)
    def _():
        scale = gpre_ref[...] * (1.0 + mod_ref[4:5, :])
        shift = mod_ref[3:4, :]
        for rows in chunks:
            h_ref[rows, :] = (_rms(x_ref[rows, :]) * scale + shift).astype(BF16)
        o_ref[...] = partial()

    @pl.when((f > 0) & (f < last))
    def _():
        o_ref[...] += partial()

    @pl.when(f == last)
    def _():
        gate = mod_ref[5:6, :] * gpost_ref[...]
        acc = o_ref[...] + partial()
        for rows in chunks:
            o_ref[rows, :] = x_ref[rows, :] + gate * _rms(acc[rows, :])


def _mlp(x2, mod, g_pre, g_post, w1_b, w2_b, seq):
    t_tok, d = x2.shape
    d_ff = w1_b.shape[1]
    tm, tf = 512, 2048
    assert d_ff // tf >= 2
    per_b = seq // tm
    return pl.pallas_call(
        _mlp_kernel,
        grid=(t_tok // tm, d_ff // tf),
        in_specs=[pl.BlockSpec((tm, d), lambda i, f: (i, 0)),
                  pl.BlockSpec((None, N_MOD, d), lambda i, f: (i // per_b, 0, 0)),
                  _resident((1, d)), _resident((1, d)),
                  pl.BlockSpec((d, tf), lambda i, f: (0, f)),
                  pl.BlockSpec((tf, d), lambda i, f: (f, 0))],
        out_specs=pl.BlockSpec((tm, d), lambda i, f: (i, 0)),
        out_shape=jax.ShapeDtypeStruct((t_tok, d), F32),
        scratch_shapes=[pltpu.VMEM((tm, d), BF16)],
        compiler_params=_cparams(("parallel", "arbitrary"), 60),
        name="mlp",
    )(x2, mod, g_pre, g_post, w1_b, w2_b)


def _rope_cos_sin(positions):
    freqs = ROPE_THETA ** (-jnp.arange(0, ROT_DIM, 2, dtype=F32) / ROT_DIM)
    ang = positions.reshape(-1).astype(F32)[:, None] * freqs
    return jnp.concatenate([jnp.cos(ang), jnp.sin(ang)], axis=1)


def kernel(x, c, positions, w_ada, b_ada, g_pre_mix, g_post_mix, w_in, ssm_a_re, ssm_a_im, ssm_log_dt,
           ssm_b_re, ssm_b_im, ssm_c_re, ssm_c_im, ssm_d, w_glu, b_glu, g_attn_out, g_ssm_out, w_out,
           g_pre_mlp, g_post_mlp, w_mlp_in, w_mlp_out):
    bsz, seq, d = x.shape
    depth = w_ada.shape[0]
    x2 = x.reshape(bsz * seq, d)
    cos_sin = _rope_cos_sin(positions)
    row = lambda a: a.reshape(1, -1).astype(F32)
    for l in range(depth):
        mod = _adaln(c, w_ada[l], b_ada[l])
        q1, q4, q16, k1, k4, k16, v1, v4, v16, uc, w_mlp_out_b = _inproj(
            x2, mod, row(g_pre_mix[l]), w_in[l].astype(BF16), cos_sin, w_mlp_out[l], bsz, seq)

        nat = lambda a: a.reshape(bsz, 1, seq, a.shape[-1])
        o1, l1 = _attn_group(nat(q1), nat(k1), nat(v1), 1, bsz, seq)
        o4, l4 = _attn_group(q4, k4, v4, 4, bsz, seq)
        o16, l16 = _attn_group(q16, k16, v16, 16, bsz, seq)

        tables = _ssm_tables(ssm_a_re[l], ssm_a_im[l], ssm_log_dt[l], ssm_b_re[l], ssm_b_im[l],
                             ssm_c_re[l], ssm_c_im[l], ssm_d[l])
        yc = _ssm(uc, tables, bsz, seq)

        x2, w_mlp_in_b = _mix(o1.reshape(bsz * seq, -1), l1.reshape(bsz * seq, -1), o4, l4, o16, l16, yc,
                              x2, mod, row(g_attn_out[l]), row(g_ssm_out[l]), row(g_post_mix[l]),
                              w_glu[l].astype(BF16), row(b_glu[l]), w_out[l].astype(BF16), w_mlp_in[l],
                              bsz, seq)
        x2 = _mlp(x2, mod, row(g_pre_mlp[l]), row(g_post_mlp[l]), w_mlp_in_b, w_mlp_out_b, seq)
    return x2.reshape(bsz, seq, d)
```

```python
import functools
import math

import jax
import jax.numpy as jnp
from jax import lax
from jax.experimental import pallas as pl
from jax.experimental.pallas import tpu as pltpu

F32 = jnp.float32
BF16 = jnp.bfloat16

LANES = 128
SUBLANES = 8

HEAD_DIM = 64
ROT_DIM = HEAD_DIM // 4
ROPE_THETA = 500000.0
DILATIONS = (1, 4, 16)
HEADS_PER_GROUP = 6
GROUP_WIDTH = HEADS_PER_GROUP * HEAD_DIM
GROUP_SLABS = GROUP_WIDTH // LANES
ATT_BLK = 128
SSM_P = 16
SSM_N = 64
SSM_CHUNK = 16
CHUNK_W = SSM_CHUNK * SSM_P
SLOTS = LANES // SSM_P
N_MOD = 6
EPS = 1e-6
NEG = -1e30
TOK_TILE = 512
ROW_CHUNK = 128


def _cparams(sem, vmem_mb):
    return pltpu.CompilerParams(dimension_semantics=sem, vmem_limit_bytes=vmem_mb * 1024 * 1024)


def _resident(shape):
    nd = len(shape)
    return pl.BlockSpec(shape, lambda *_: (0,) * nd, pipeline_mode=pl.Buffered(1))


def _slab(j):
    return slice(j * LANES, (j + 1) * LANES)


def _slot_transpose(vs):
    slot = lax.broadcasted_iota(jnp.int32, vs[0].shape, 1) // SSM_P
    vs = list(vs)
    for dist in (4, 2, 1):
        low = (slot & dist) == 0
        nxt = list(vs)
        for i in range(SLOTS):
            if i & dist == 0:
                a, b = vs[i], vs[i + dist]
                nxt[i] = jnp.where(low, a, pltpu.roll(b, dist * SSM_P, axis=1))
                nxt[i + dist] = jnp.where(low, pltpu.roll(a, LANES - dist * SSM_P, axis=1), b)
        vs = nxt
    return vs


def _adaln_kernel(s_ref, w_ref, b_ref, o_ref):
    o_ref[...] = jnp.dot(s_ref[...], w_ref[...].astype(BF16),
                         preferred_element_type=F32) + b_ref[...]


def _adaln(c, w_ada, b_ada):
    bsz, d = c.shape
    n = w_ada.shape[1]
    tn = 1024
    s = jnp.zeros((SUBLANES, d), BF16).at[:bsz].set(jax.nn.silu(c).astype(BF16))
    out = pl.pallas_call(
        _adaln_kernel,
        grid=(n // tn,),
        in_specs=[pl.BlockSpec((SUBLANES, d), lambda j: (0, 0)),
                  pl.BlockSpec((d, tn), lambda j: (0, j)),
                  pl.BlockSpec((1, tn), lambda j: (0, j))],
        out_specs=pl.BlockSpec((SUBLANES, tn), lambda j: (0, j)),
        out_shape=jax.ShapeDtypeStruct((SUBLANES, n), F32),
        compiler_params=_cparams(("arbitrary",), 40),
        name="adaln",
    )(s, w_ada, b_ada.reshape(1, n))
    return out[:bsz].reshape(bsz, N_MOD, d)


def _inproj_kernel(x_ref, mod_ref, g_ref, w_ref, cs_ref, rx_ref, r1_ref, wside_ref,
                   q1_ref, q4_ref, q16_ref, k1_ref, k4_ref, k16_ref, v1_ref, v4_ref, v16_ref,
                   uc_ref, wside_out_ref, sc_ref, *, tm):
    wside_out_ref[...] = wside_ref[...].astype(BF16)

    x = x_ref[...]
    ms = jnp.mean(x * x, axis=-1, keepdims=True)
    scale = g_ref[...] * (1.0 + mod_ref[1:2, :])
    h = (x * lax.rsqrt(ms + EPS)) * scale + mod_ref[0:1, :]
    hb = h.astype(BF16)

    qk_slabs = 4 * GROUP_SLABS

    def scatter(j, jj, refs):
        for dil, ref in refs:
            if dil == 1:
                ref[:, _slab(jj)] = sc_ref[j].astype(BF16)
            else:
                for r in range(dil):
                    ref[r, :, _slab(jj)] = sc_ref[j, pl.ds(r, tm // dil, stride=dil), :].astype(BF16)

    pvu = jnp.dot(hb, w_ref[:, qk_slabs * LANES:], preferred_element_type=F32)
    for j in range(pvu.shape[1] // LANES):
        sc_ref[qk_slabs + j] = pvu[:, _slab(j)]
    for jj in range(GROUP_SLABS):
        scatter(qk_slabs + jj, jj, ((1, v1_ref), (4, v4_ref), (16, v16_ref)))

    u0 = qk_slabs + GROUP_SLABS
    ngb, nc = uc_ref.shape[0] // SLOTS, tm // SSM_CHUNK
    vs = [jnp.concatenate([sc_ref[u0 + gb, pl.ds(cb * SSM_CHUNK * SUBLANES + half * SLOTS + s, SUBLANES,
                                                 stride=SSM_CHUNK), :]
                           for gb in range(ngb) for half in range(2) for cb in range(nc // SUBLANES)],
                          axis=0) for s in range(SLOTS)]
    for g, o in enumerate(_slot_transpose(vs)):
        for gb in range(ngb):
            r0 = gb * 2 * nc
            full = jnp.concatenate([o[r0:r0 + nc], o[r0 + nc:r0 + 2 * nc]], axis=1)
            uc_ref[gb * SLOTS + g] = full.astype(BF16)

    cs = cs_ref[...]
    cs_hi = cs.astype(BF16)
    cs_lo = (cs - cs_hi.astype(F32)).astype(BF16)
    rows_contract = (((0,), (0,)), ((), ()))
    tab = lax.dot_general(cs_hi, rx_ref[...], rows_contract, preferred_element_type=F32) + \
        lax.dot_general(cs_lo, rx_ref[...], rows_contract, preferred_element_type=F32)
    ct = tab[:, :LANES] + r1_ref[...]
    st = tab[:, LANES:]
    lane = lax.broadcasted_iota(jnp.int32, ct.shape, 1) % HEAD_DIM
    second_half = (lane >= ROT_DIM // 2) & (lane < ROT_DIM)
    q_refs = ((1, q1_ref),), ((4, q4_ref),), ((16, q16_ref),)
    chunk_slabs = 4
    for c0 in reversed(range(0, qk_slabs, chunk_slabs)):
        pc = jnp.dot(hb, w_ref[:, c0 * LANES:(c0 + chunk_slabs) * LANES], preferred_element_type=F32)
        for s in range(chunk_slabs):
            j = c0 + s
            grp, jj = divmod(j, GROUP_SLABS)
            t = pc[:, _slab(s)]
            partner = jnp.where(second_half,
                                pltpu.roll(t, ROT_DIM // 2, axis=1),
                                pltpu.roll(t, LANES - ROT_DIM // 2, axis=1))
            r = t * ct + partner * st
            if grp < 3:
                r = r * (math.log2(math.e) / math.sqrt(HEAD_DIM))
            sc_ref[j] = r
            scatter(j, jj, q_refs[grp] if grp < 3 else ((1, k1_ref), (4, k4_ref), (16, k16_ref)))


def _rope_spread():
    half = ROT_DIM // 2
    dim = jnp.arange(LANES)[None, :] % HEAD_DIM
    i = jnp.arange(half)[:, None]
    first, second = dim == i, dim == i + half
    ct_rows = (first | second).astype(F32)
    st_rows = second.astype(F32) - first.astype(F32)
    zero = jnp.zeros_like(ct_rows)
    spread = jnp.concatenate([jnp.concatenate([ct_rows, zero], axis=1),
                              jnp.concatenate([zero, st_rows], axis=1)], axis=0)
    return spread.astype(BF16), (dim >= ROT_DIM).astype(F32)


def _side_cast_spec(w, steps):
    rows, cols = w.shape
    assert rows % (steps * 2 * SUBLANES) == 0
    return (pl.BlockSpec((rows // steps, cols), lambda i: (i, 0)),
            jax.ShapeDtypeStruct(w.shape, BF16))


def _inproj(x2, mod, g_pre, w_in_b, cos_sin, w_side, bsz, seq):
    t_tok, d = x2.shape
    in_w = w_in_b.shape[1]
    ssm_w = in_w - 5 * GROUP_WIDTH
    groups = ssm_w // SSM_P
    tm = TOK_TILE
    per_b = seq // tm
    tok = lambda w: pl.BlockSpec((tm, w), lambda i: (i, 0))
    side_spec, side_shape = _side_cast_spec(w_side, t_tok // tm)

    def res_spec(dil):
        return pl.BlockSpec((None, dil, tm // dil, GROUP_WIDTH), lambda i: (i // per_b, 0, i % per_b, 0))

    def res_shape(dil):
        return jax.ShapeDtypeStruct((bsz, dil, seq // dil, GROUP_WIDTH), BF16)

    nat = jax.ShapeDtypeStruct((t_tok, GROUP_WIDTH), BF16)
    out_shape = [nat, res_shape(4), res_shape(16)] * 3 + \
                [jax.ShapeDtypeStruct((groups, t_tok // SSM_CHUNK, CHUNK_W), BF16), side_shape]
    out_specs = [tok(GROUP_WIDTH), res_spec(4), res_spec(16)] * 3 + \
                [pl.BlockSpec((groups, tm // SSM_CHUNK, CHUNK_W), lambda i: (0, i, 0)), side_spec]
    return pl.pallas_call(
        functools.partial(_inproj_kernel, tm=tm),
        grid=(t_tok // tm,),
        in_specs=[tok(d),
                  pl.BlockSpec((None, N_MOD, d), lambda i: (i // per_b, 0, 0)),
                  _resident((1, d)),
                  _resident((d, in_w)),
                  pl.BlockSpec((ROT_DIM, tm), lambda i: (0, i)),
                  _resident((ROT_DIM, 2 * LANES)), _resident((1, LANES)),
                  side_spec],
        out_specs=out_specs,
        out_shape=out_shape,
        scratch_shapes=[pltpu.VMEM((in_w // LANES, tm, LANES), F32)],
        compiler_params=_cparams(("parallel",), 58),
        name="inproj",
    )(x2, mod, g_pre, w_in_b, cos_sin, *_rope_spread(), w_side)


def _stat_lanes(h):
    a, b = slice(h, h + 1), slice(HEAD_DIM + h, HEAD_DIM + h + 1)
    return (a, b) if h % 2 == 0 else (b, a)


def _attn_kernel(*refs, nblk, units, side):
    if side:
        q_ref, k_ref, v_ref, wside_ref, o_ref, s_ref, wside_out_ref, sc_ref, pr_ref, mk_ref = refs
        wside_out_ref[...] = wside_ref[...].astype(BF16)
    else:
        q_ref, k_ref, v_ref, o_ref, s_ref, sc_ref, pr_ref, mk_ref = refs

    pair = 2 * ATT_BLK
    qi = lax.broadcasted_iota(jnp.int32, (pair, 2 * ATT_BLK), 0) % ATT_BLK
    kj = lax.broadcasted_iota(jnp.int32, (pair, 2 * ATT_BLK), 1)
    rel = qi - kj
    mk_ref[0] = jnp.where((rel >= 0) & (rel <= ATT_BLK), 0.0, NEG)
    mk_ref[1] = jnp.where((rel + ATT_BLK >= 0) & (rel <= 0), 0.0, NEG)
    low = lax.broadcasted_iota(jnp.int32, (ATT_BLK, LANES), 1) < HEAD_DIM
    top, bot = slice(0, ATT_BLK), slice(ATT_BLK, pair)
    ones = jnp.ones((2 * ATT_BLK, LANES), BF16)

    def coords(u):
        u = jnp.minimum(u, units - 1)
        r, i = u // nblk, u % nblk
        rows = pl.ds(pl.multiple_of(i * ATT_BLK, ATT_BLK), ATT_BLK)
        krows = pl.ds(pl.multiple_of(jnp.maximum(i - 1, 0) * ATT_BLK, ATT_BLK), 2 * ATT_BLK)
        return r, i, rows, krows

    def scores(u, slot):
        r, i, rows, krows = coords(u)
        bias = mk_ref[jnp.minimum(i, 1)]
        for jj in range(GROUP_SLABS):
            q2 = q_ref[r, rows, _slab(jj)]
            zero = jnp.zeros_like(q2)
            qp = jnp.concatenate([jnp.where(low, q2, zero), jnp.where(low, zero, q2)], axis=0)
            s = lax.dot_general(qp, k_ref[r, krows, _slab(jj)], (((1,), (1,)), ((), ())),
                                preferred_element_type=F32)
            sc_ref[slot, jj] = s + bias

    def softmax(u, slot):
        r, _, rows, _ = coords(u)
        s_ref[r, rows, :] = jnp.zeros((ATT_BLK, LANES), F32)
        for jj in range(GROUP_SLABS):
            s = sc_ref[slot, jj]
            m = jnp.max(s, axis=1, keepdims=True)
            pr_ref[slot, jj] = jnp.exp2(s - m).astype(BF16)
            s_ref[r, rows, _stat_lanes(2 * jj)[0]] = m[top]
            s_ref[r, rows, _stat_lanes(2 * jj + 1)[0]] = m[bot]

    def values(u, slot):
        r, _, rows, krows = coords(u)
        for jj in range(GROUP_SLABS):
            v_ext = jnp.concatenate([v_ref[r, krows, _slab(jj)], ones], axis=1)
            oe = jnp.dot(pr_ref[slot, jj], v_ext, preferred_element_type=F32)
            l_even, l_odd = _stat_lanes(2 * jj)[1], _stat_lanes(2 * jj + 1)[1]
            s_ref[r, rows, l_even] = oe[top, LANES:][:, l_even]
            s_ref[r, rows, l_odd] = oe[bot, LANES:][:, l_odd]
            o_ref[r, rows, _slab(jj)] = jnp.where(low, oe[top, :LANES], oe[bot, :LANES]).astype(o_ref.dtype)

    scores(0, 0)
    softmax(0, 0)
    scores(1, 1)

    def body(t, carry):
        u = 2 * t
        scores(u + 2, 0)
        softmax(u + 1, 1)
        values(u, 0)
        scores(u + 3, 1)
        softmax(u + 2, 0)
        values(u + 1, 1)
        return carry

    assert units % 2 == 0
    lax.fori_loop(0, units // 2, body, 0)


def _attn_group(q, k, v, dilation, bsz, seq, w_side=None):
    n = seq // dilation
    nblk = n // ATT_BLK
    spec = lambda w: pl.BlockSpec((None, dilation, n, w), lambda b: (b, 0, 0, 0))
    win = (2 * ATT_BLK, 2 * ATT_BLK)
    ins, in_specs = [q, k, v], [spec(GROUP_WIDTH)] * 3
    out_specs = [spec(GROUP_WIDTH), spec(LANES)]
    out_shape = [jax.ShapeDtypeStruct((bsz, dilation, n, GROUP_WIDTH), BF16),
                 jax.ShapeDtypeStruct((bsz, dilation, n, LANES), F32)]
    if w_side is not None:
        side_spec, side_shape = _side_cast_spec(w_side, bsz)
        ins, in_specs = ins + [w_side], in_specs + [side_spec]
        out_specs, out_shape = out_specs + [side_spec], out_shape + [side_shape]
    return pl.pallas_call(
        functools.partial(_attn_kernel, nblk=nblk, units=dilation * nblk, side=w_side is not None),
        grid=(bsz,),
        in_specs=in_specs,
        out_specs=out_specs,
        out_shape=out_shape,
        scratch_shapes=[pltpu.VMEM((2, GROUP_SLABS) + win, F32),
                        pltpu.VMEM((2, GROUP_SLABS) + win, BF16),
                        pltpu.VMEM((2,) + win, F32)],
        compiler_params=_cparams(("parallel",), 48),
        name=f"attn_d{dilation}",
    )(*ins)


def _cmul(a, b):
    return a[0] * b[0] - a[1] * b[1], a[0] * b[1] + a[1] * b[0]


def _ssm_tables(a_re, a_im, log_dt, b_re, b_im, c_re, c_im, d_skip):
    hp = lax.Precision.HIGH
    f = lambda t: t.astype(F32)
    lam = (f(a_re), f(a_im))
    dt = jnp.exp(f(log_dt))[:, None]
    g = a_re.shape[0]
    taus = jnp.arange(2 * SSM_CHUNK + 1, dtype=F32)[None, :, None]
    mag = jnp.exp(lam[0][:, None, :] * dt[:, None, :] * taus)
    ang = lam[1][:, None, :] * dt[:, None, :] * taus
    apow = (mag * jnp.cos(ang), mag * jnp.sin(ang))
    a_bar = (apow[0][:, 1], apow[1][:, 1])
    den = lam[0] * lam[0] + lam[1] * lam[1]
    coef = _cmul((a_bar[0] - 1.0, a_bar[1]), (lam[0] / den, -lam[1] / den))
    b_bar = _cmul((coef[0][..., None], coef[1][..., None]), (f(b_re), f(b_im)))
    c_mat = (f(c_re), f(c_im))

    ca = _cmul((c_mat[0][:, None], c_mat[1][:, None]),
               (apow[0][:, :SSM_CHUNK, None, :], apow[1][:, :SSM_CHUNK, None, :]))
    kern = jnp.einsum('gtpn,gnq->gtpq', ca[0], b_bar[0], precision=hp) - \
        jnp.einsum('gtpn,gnq->gtpq', ca[1], b_bar[1], precision=hp)
    kern = kern.at[:, 0].add(f(d_skip)[:, :, None] * jnp.eye(SSM_P, dtype=F32))
    m_row = kern.transpose(0, 3, 1, 2).reshape(g, SSM_P, CHUNK_W)

    rev = (apow[0][:, SSM_CHUNK - 1::-1][:, :SSM_CHUNK], apow[1][:, SSM_CHUNK - 1::-1][:, :SSM_CHUNK])
    bt = (b_bar[0].transpose(0, 2, 1)[:, None], b_bar[1].transpose(0, 2, 1)[:, None])
    w4 = _cmul((rev[0][:, :, None, :], rev[1][:, :, None, :]), bt)
    w_re, w_im = (t.reshape(g, CHUNK_W, SSM_N) for t in w4)
    w_mat = jnp.concatenate([w_re, w_im, w_im, w_re], axis=-1)

    ct = (c_mat[0].transpose(0, 2, 1)[:, :, None, :], c_mat[1].transpose(0, 2, 1)[:, :, None, :])
    at = (apow[0][:, 1:SSM_CHUNK + 1].transpose(0, 2, 1)[..., None],
          apow[1][:, 1:SSM_CHUNK + 1].transpose(0, 2, 1)[..., None])
    v4 = _cmul(ct, at)
    v_re, v_im = (t.reshape(g, SSM_N, CHUNK_W) for t in v4)
    v_mat = jnp.concatenate([v_re, -v_im], axis=1)

    def mul_rows(tau):
        re, im = apow[0][:, tau], apow[1][:, tau]
        return [jnp.concatenate([re, re], -1), jnp.concatenate([-im, im], -1),
                jnp.concatenate([im, -im], -1)]
    ptab = jnp.stack(mul_rows(SSM_CHUNK) + mul_rows(2 * SSM_CHUNK) +
                     [jnp.zeros((g, LANES), F32)] * 2, axis=1)
    return m_row, w_mat.astype(BF16), v_mat.astype(BF16), ptab


def _toeplitz(m_row):
    lane = lax.broadcasted_iota(jnp.int32, m_row.shape, 1)
    blocks = [m_row]
    for s in range(1, SSM_CHUNK):
        blocks.append(jnp.where(lane >= s * SSM_P, pltpu.roll(m_row, s * SSM_P, axis=1), 0.0))
    return jnp.concatenate(blocks, axis=0).astype(BF16)


def _ssm_kernel(u_ref, m_ref, w_ref, v_ref, p_ref, y_ref, gx_ref, gz_ref, h_ref, *, groups, nchunk, bsz):
    rows = nchunk * bsz
    row = lax.broadcasted_iota(jnp.int32, (rows, LANES), 0)
    first_chunk = (row % nchunk) == 0

    def prev_chunk(a):
        return jnp.where(first_chunk, 0.0, pltpu.roll(a, 1, axis=0))

    for g in range(groups):
        e = jnp.dot(u_ref[g], w_ref[g], preferred_element_type=F32)
        ex, ez = e[:, :LANES], e[:, LANES:]
        exs, ezs = prev_chunk(ex), prev_chunk(ez)
        p1, p2, p3 = p_ref[g, 0:1, :], p_ref[g, 1:2, :], p_ref[g, 2:3, :]
        gx = ex + p1 * exs + p2 * ezs
        gz = ez + p1 * ezs + p3 * exs
        for b in range(bsz):
            gx_ref[g, pl.ds(b, nchunk, stride=bsz), :] = gx[b * nchunk:(b + 1) * nchunk]
            gz_ref[g, pl.ds(b, nchunk, stride=bsz), :] = gz[b * nchunk:(b + 1) * nchunk]

    def step(j, carry):
        r0 = pl.multiple_of(j * SUBLANES, SUBLANES)
        new = []
        for g in range(groups):
            x, z = carry[g]
            q1, q2, q3 = p_ref[g, 3:4, :], p_ref[g, 4:5, :], p_ref[g, 5:6, :]
            xn = q1 * x + q2 * z + gx_ref[g, pl.ds(r0, SUBLANES), :]
            zn = q1 * z + q3 * x + gz_ref[g, pl.ds(r0, SUBLANES), :]
            h_ref[g, pl.ds(r0, SUBLANES), :] = xn
            new.append((xn, zn))
        return tuple(new)

    zero = jnp.zeros((SUBLANES, LANES), F32)
    lax.fori_loop(0, rows // SUBLANES, step, tuple((zero, zero) for _ in range(groups)))

    for g in range(groups):
        h_end = jnp.concatenate([h_ref[g, pl.ds(b, nchunk, stride=bsz), :] for b in range(bsz)], axis=0)
        h_in = prev_chunk(h_end).astype(BF16)
        y = jnp.dot(u_ref[g], _toeplitz(m_ref[g]), preferred_element_type=F32)
        y = y + jnp.dot(h_in, v_ref[g], preferred_element_type=F32)
        y_ref[g] = y.astype(y_ref.dtype)


def _ssm(uc, tables, bsz, seq):
    m_row, w_mat, v_mat, ptab = tables
    g, rows, _ = uc.shape
    nchunk = seq // SSM_CHUNK
    assert rows == nchunk * bsz and SUBLANES % bsz == 0 and SUBLANES // bsz == 2
    gs = 4
    blk = lambda a, b: pl.BlockSpec((gs, a, b), lambda i: (i, 0, 0))
    return pl.pallas_call(
        functools.partial(_ssm_kernel, groups=gs, nchunk=nchunk, bsz=bsz),
        grid=(g // gs,),
        in_specs=[blk(rows, CHUNK_W), blk(SSM_P, CHUNK_W), blk(CHUNK_W, CHUNK_W),
                  blk(LANES, CHUNK_W), blk(SUBLANES, LANES)],
        out_specs=blk(rows, CHUNK_W),
        out_shape=jax.ShapeDtypeStruct((g, rows, CHUNK_W), BF16),
        scratch_shapes=[pltpu.VMEM((gs, rows, LANES), F32)] * 3,
        compiler_params=_cparams(("parallel",), 48),
        name="ssm",
    )(uc, m_row, w_mat, v_mat, ptab)


def _rms(x):
    return x * lax.rsqrt(jnp.mean(x * x, axis=-1, keepdims=True) + EPS)


def _mix_kernel(o1_ref, l1_ref, o4_ref, l4_ref, o16_ref, l16_ref, y_ref, x_ref, mod_ref,
                ga_ref, gs_ref, gp_ref, ex_ref, wg_ref, bg_ref, wo_ref, wside_ref, out_ref,
                wside_out_ref, so_ref, sl_ref, sy_ref, *, tm):
    wside_out_ref[...] = wside_ref[...].astype(BF16)

    for gi, (dil, o_ref, l_ref) in enumerate(((4, o4_ref, l4_ref), (16, o16_ref, l16_ref))):
        for r in range(dil):
            rows = pl.ds(r, tm // dil, stride=dil)
            for jj in range(GROUP_SLABS):
                so_ref[gi * GROUP_SLABS + jj, rows, :] = o_ref[r, :, _slab(jj)].astype(F32)
            sl_ref[gi, rows, :] = l_ref[r]

    ngb, nc = y_ref.shape[0] // SLOTS, tm // SSM_CHUNK
    vs = [jnp.concatenate([y_ref[gb * SLOTS + g, :, _slab(half)].astype(F32)
                           for gb in range(ngb) for half in range(2)], axis=0)
          for g in range(SLOTS)]
    for s, o in enumerate(_slot_transpose(vs)):
        for gb in range(ngb):
            for half in range(2):
                for cb in range(nc // SUBLANES):
                    r0 = (gb * 2 + half) * nc + cb * SUBLANES
                    sy_ref[gb, pl.ds(cb * SSM_CHUNK * SUBLANES + half * SLOTS + s, SUBLANES,
                                     stride=SSM_CHUNK), :] = o[r0:r0 + SUBLANES]

    lane = lax.broadcasted_iota(jnp.int32, (tm, LANES), 1)
    even = (lane & 1) == 0
    ms, ls = [], []
    for st in (l1_ref[...], sl_ref[0], sl_ref[1]):
        sw = pltpu.roll(st, HEAD_DIM, axis=1)
        ms.append(jnp.where(even, st, sw))
        ls.append(jnp.where(even, sw, st))
    mx = jnp.maximum(jnp.maximum(ms[0], ms[1]), ms[2])
    es = [jnp.exp2(m - mx) for m in ms]
    inv = 1.0 / (es[0] * ls[0] + es[1] * ls[1] + es[2] * ls[2])
    head_lane = lane < HEADS_PER_GROUP
    wexp = [jnp.dot(jnp.where(head_lane, e * inv, 0.0).astype(BF16), ex_ref[...],
                    preferred_element_type=F32) for e in es]
    att_slabs = []
    for jj in range(GROUP_SLABS):
        a = wexp[0][:, _slab(jj)] * o1_ref[:, _slab(jj)].astype(F32)
        a = a + wexp[1][:, _slab(jj)] * so_ref[jj]
        a = a + wexp[2][:, _slab(jj)] * so_ref[GROUP_SLABS + jj]
        att_slabs.append(a)
    att = _rms(jnp.concatenate(att_slabs, axis=1)) * ga_ref[...]

    yv = jnp.concatenate([sy_ref[gb] for gb in range(sy_ref.shape[0])], axis=1)
    s = jax.nn.gelu(yv)
    z = jnp.dot(s.astype(BF16), wg_ref[...], preferred_element_type=F32) + bg_ref[...]
    s = s * jax.nn.sigmoid(z)
    ssm = _rms(s) * gs_ref[...]
    aw = att.shape[1]
    mix = jnp.dot(att.astype(BF16), wo_ref[:aw, :], preferred_element_type=F32)
    mix = mix + jnp.dot(ssm.astype(BF16), wo_ref[aw:, :], preferred_element_type=F32)
    out_ref[...] = x_ref[...] + mod_ref[2:3, :] * (_rms(mix) * gp_ref[...])


def _mix(o1, l1, o4, l4, o16, l16, yc, x2, mod, g_attn, g_ssm, g_post, w_glu_b, b_glu, w_out_b, w_side,
         bsz, seq):
    t_tok, d = x2.shape
    groups = yc.shape[0]
    sw = groups * SSM_P
    aw = GROUP_WIDTH
    tm = TOK_TILE
    per_b = seq // tm
    tok = lambda w: pl.BlockSpec((tm, w), lambda i: (i, 0))
    res = lambda dil, w: pl.BlockSpec((None, dil, tm // dil, w), lambda i: (i // per_b, 0, i % per_b, 0))
    head_of_lane = jnp.arange(aw)[None, :] // HEAD_DIM
    expand = (jnp.arange(LANES)[:, None] == head_of_lane).astype(BF16)
    side_spec, side_shape = _side_cast_spec(w_side, t_tok // tm)
    return pl.pallas_call(
        functools.partial(_mix_kernel, tm=tm),
        grid=(t_tok // tm,),
        in_specs=[tok(aw), tok(LANES), res(4, aw), res(4, LANES), res(16, aw), res(16, LANES),
                  pl.BlockSpec((groups, tm // SSM_CHUNK, CHUNK_W), lambda i: (0, i, 0)),
                  tok(d),
                  pl.BlockSpec((None, N_MOD, d), lambda i: (i // per_b, 0, 0)),
                  _resident((1, aw)), _resident((1, sw)), _resident((1, d)),
                  _resident((LANES, aw)),
                  _resident((sw, sw)), _resident((1, sw)), _resident((aw + sw, d)),
                  side_spec],
        out_specs=[tok(d), side_spec],
        out_shape=[jax.ShapeDtypeStruct((t_tok, d), F32), side_shape],
        scratch_shapes=[pltpu.VMEM((2 * GROUP_SLABS, tm, LANES), F32),
                        pltpu.VMEM((2, tm, LANES), F32),
                        pltpu.VMEM((sw // LANES, tm, LANES), F32)],
        compiler_params=_cparams(("parallel",), 56),
        name="mix",
    )(o1, l1, o4, l4, o16, l16, yc, x2, mod, g_attn, g_ssm, g_post, expand, w_glu_b, b_glu, w_out_b, w_side)


def _mlp_kernel(x_ref, mod_ref, gpre_ref, gpost_ref, w1_ref, w2_ref, o_ref, h_ref):
    f = pl.program_id(1)
    last = pl.num_programs(1) - 1
    chunks = [slice(r, r + ROW_CHUNK) for r in range(0, x_ref.shape[0], ROW_CHUNK)]

    def partial():
        t = jnp.dot(h_ref[...], w1_ref[...], preferred_element_type=F32)
        t = jnp.square(jnp.maximum(t, 0.0)).astype(BF16)
        return jnp.dot(t, w2_ref[...], preferred_element_type=F32)

    @pl.when(f == 0)
    def _():
        scale = gpre_ref[...] * (1.0 + mod_ref[4:5, :])
        shift = mod_ref[3:4, :]
        for rows in chunks:
            h_ref[rows, :] = (_rms(x_ref[rows, :]) * scale + shift).astype(BF16)
        o_ref[...] = partial()

    @pl.when((f > 0) & (f < last))
    def _():
        o_ref[...] += partial()

    @pl.when(f == last)
    def _():
        gate = mod_ref[5:6, :] * gpost_ref[...]
        acc = o_ref[...] + partial()
        for rows in chunks:
            o_ref[rows, :] = x_ref[rows, :] + gate * _rms(acc[rows, :])


def _mlp(x2, mod, g_pre, g_post, w1_b, w2_b, seq):
    t_tok, d = x2.shape
    d_ff = w1_b.shape[1]
    tm, tf = 512, 2048
    assert d_ff // tf >= 2
    per_b = seq // tm
    return pl.pallas_call(
        _mlp_kernel,
        grid=(t_tok // tm, d_ff // tf),
        in_specs=[pl.BlockSpec((tm, d), lambda i, f: (i, 0)),
                  pl.BlockSpec((None, N_MOD, d), lambda i, f: (i // per_b, 0, 0)),
                  _resident((1, d)), _resident((1, d)),
                  pl.BlockSpec((d, tf), lambda i, f: (0, f)),
                  pl.BlockSpec((tf, d), lambda i, f: (f, 0))],
        out_specs=pl.BlockSpec((tm, d), lambda i, f: (i, 0)),
        out_shape=jax.ShapeDtypeStruct((t_tok, d), F32),
        scratch_shapes=[pltpu.VMEM((tm, d), BF16)],
        compiler_params=_cparams(("parallel", "arbitrary"), 60),
        name="mlp",
    )(x2, mod, g_pre, g_post, w1_b, w2_b)


def _rope_cos_sin(positions):
    freqs = ROPE_THETA ** (-jnp.arange(0, ROT_DIM, 2, dtype=F32) / ROT_DIM)
    ang = positions.reshape(-1).astype(F32)[None, :] * freqs[:, None]
    return jnp.concatenate([jnp.cos(ang), jnp.sin(ang)], axis=0)


def kernel(x, c, positions, w_ada, b_ada, g_pre_mix, g_post_mix, w_in, ssm_a_re, ssm_a_im, ssm_log_dt,
           ssm_b_re, ssm_b_im, ssm_c_re, ssm_c_im, ssm_d, w_glu, b_glu, g_attn_out, g_ssm_out, w_out,
           g_pre_mlp, g_post_mlp, w_mlp_in, w_mlp_out):
    bsz, seq, d = x.shape
    depth = w_ada.shape[0]
    x2 = x.reshape(bsz * seq, d)
    cos_sin = _rope_cos_sin(positions)
    row = lambda a: a.reshape(1, -1).astype(F32)
    for l in range(depth):
        mod = _adaln(c, w_ada[l], b_ada[l])
        q1, q4, q16, k1, k4, k16, v1, v4, v16, uc, w_mlp_out_b = _inproj(
            x2, mod, row(g_pre_mix[l]), w_in[l].astype(BF16), cos_sin, w_mlp_out[l], bsz, seq)

        nat = lambda a: a.reshape(bsz, 1, seq, a.shape[-1])
        o1, l1, w_out_b = _attn_group(nat(q1), nat(k1), nat(v1), 1, bsz, seq, w_out[l])
        o4, l4, w_glu_b = _attn_group(q4, k4, v4, 4, bsz, seq, w_glu[l])
        o16, l16 = _attn_group(q16, k16, v16, 16, bsz, seq)

        tables = _ssm_tables(ssm_a_re[l], ssm_a_im[l], ssm_log_dt[l], ssm_b_re[l], ssm_b_im[l],
                             ssm_c_re[l], ssm_c_im[l], ssm_d[l])
        yc = _ssm(uc, tables, bsz, seq)

        x2, w_mlp_in_b = _mix(o1.reshape(bsz * seq, -1), l1.reshape(bsz * seq, -1), o4, l4, o16, l16, yc,
                              x2, mod, row(g_attn_out[l]), row(g_ssm_out[l]), row(g_post_mix[l]),
                              w_glu_b, row(b_glu[l]), w_out_b, w_mlp_in[l],
                              bsz, seq)
        x2 = _mlp(x2, mod, row(g_pre_mlp[l]), row(g_post_mlp[l]), w_mlp_in_b, w_mlp_out_b, seq)
    return x2.reshape(bsz, seq, d)
```

```python
import functools
import math

import jax
import jax.numpy as jnp
from jax import lax
from jax.experimental import pallas as pl
from jax.experimental.pallas import tpu as pltpu

F32 = jnp.float32
BF16 = jnp.bfloat16

LANES = 128
SUBLANES = 8

HEAD_DIM = 64
ROT_DIM = HEAD_DIM // 4
ROPE_THETA = 500000.0
DILATIONS = (1, 4, 16)
HEADS_PER_GROUP = 6
GROUP_WIDTH = HEADS_PER_GROUP * HEAD_DIM
GROUP_SLABS = GROUP_WIDTH // LANES
ATT_BLK = 128
UNITS_PER_TRIP = 16
SSM_P = 16
SSM_N = 64
SSM_CHUNK = 16
CHUNK_W = SSM_CHUNK * SSM_P
SLOTS = LANES // SSM_P
N_MOD = 6
EPS = 1e-6
NEG = -1e30
TOK_TILE = 512
ROW_CHUNK = 128


def _cparams(sem, vmem_mb):
    return pltpu.CompilerParams(dimension_semantics=sem, vmem_limit_bytes=vmem_mb * 1024 * 1024)


def _resident(shape):
    nd = len(shape)
    return pl.BlockSpec(shape, lambda *_: (0,) * nd, pipeline_mode=pl.Buffered(1))


def _slab(j):
    return slice(j * LANES, (j + 1) * LANES)


def _slot_transpose(vs):
    slot = lax.broadcasted_iota(jnp.int32, vs[0].shape, 1) // SSM_P
    vs = list(vs)
    for dist in (4, 2, 1):
        low = (slot & dist) == 0
        nxt = list(vs)
        for i in range(SLOTS):
            if i & dist == 0:
                a, b = vs[i], vs[i + dist]
                nxt[i] = jnp.where(low, a, pltpu.roll(b, dist * SSM_P, axis=1))
                nxt[i + dist] = jnp.where(low, pltpu.roll(a, LANES - dist * SSM_P, axis=1), b)
        vs = nxt
    return vs


def _adaln_kernel(s_ref, w_ref, b_ref, o_ref):
    o_ref[...] = jnp.dot(s_ref[...], w_ref[...].astype(BF16),
                         preferred_element_type=F32) + b_ref[...]


def _adaln(c, w_ada, b_ada):
    bsz, d = c.shape
    n = w_ada.shape[1]
    tn = 1024
    s = jnp.zeros((SUBLANES, d), BF16).at[:bsz].set(jax.nn.silu(c).astype(BF16))
    out = pl.pallas_call(
        _adaln_kernel,
        grid=(n // tn,),
        in_specs=[pl.BlockSpec((SUBLANES, d), lambda j: (0, 0)),
                  pl.BlockSpec((d, tn), lambda j: (0, j)),
                  pl.BlockSpec((1, tn), lambda j: (0, j))],
        out_specs=pl.BlockSpec((SUBLANES, tn), lambda j: (0, j)),
        out_shape=jax.ShapeDtypeStruct((SUBLANES, n), F32),
        compiler_params=_cparams(("arbitrary",), 40),
        name="adaln",
    )(s, w_ada, b_ada.reshape(1, n))
    return out[:bsz].reshape(bsz, N_MOD, d)


def _inproj_kernel(x_ref, mod_ref, g_ref, w_ref, cs_ref, rx_ref, r1_ref, wside_ref,
                   q1_ref, q4_ref, q16_ref, k1_ref, k4_ref, k16_ref, v1_ref, v4_ref, v16_ref,
                   uc_ref, wside_out_ref, sc_ref, *, tm):
    wside_out_ref[...] = wside_ref[...].astype(BF16)

    x = x_ref[...]
    ms = jnp.mean(x * x, axis=-1, keepdims=True)
    scale = g_ref[...] * (1.0 + mod_ref[1:2, :])
    h = (x * lax.rsqrt(ms + EPS)) * scale + mod_ref[0:1, :]
    hb = h.astype(BF16)

    qk_slabs = 4 * GROUP_SLABS

    def scatter(j, jj, refs):
        for dil, ref in refs:
            if dil == 1:
                ref[:, _slab(jj)] = sc_ref[j].astype(BF16)
            else:
                for r in range(dil):
                    ref[r, :, _slab(jj)] = sc_ref[j, pl.ds(r, tm // dil, stride=dil), :].astype(BF16)

    pvu = jnp.dot(hb, w_ref[:, qk_slabs * LANES:], preferred_element_type=F32)
    for j in range(pvu.shape[1] // LANES):
        sc_ref[qk_slabs + j] = pvu[:, _slab(j)]
    for jj in range(GROUP_SLABS):
        scatter(qk_slabs + jj, jj, ((1, v1_ref), (4, v4_ref), (16, v16_ref)))

    u0 = qk_slabs + GROUP_SLABS
    ngb, nc = uc_ref.shape[0] // SLOTS, tm // SSM_CHUNK
    vs = [jnp.concatenate([sc_ref[u0 + gb, pl.ds(cb * SSM_CHUNK * SUBLANES + half * SLOTS + s, SUBLANES,
                                                 stride=SSM_CHUNK), :]
                           for gb in range(ngb) for half in range(2) for cb in range(nc // SUBLANES)],
                          axis=0) for s in range(SLOTS)]
    for g, o in enumerate(_slot_transpose(vs)):
        for gb in range(ngb):
            r0 = gb * 2 * nc
            full = jnp.concatenate([o[r0:r0 + nc], o[r0 + nc:r0 + 2 * nc]], axis=1)
            uc_ref[gb * SLOTS + g] = full.astype(BF16)

    cs = cs_ref[...]
    cs_hi = cs.astype(BF16)
    cs_lo = (cs - cs_hi.astype(F32)).astype(BF16)
    rows_contract = (((0,), (0,)), ((), ()))
    tab = lax.dot_general(cs_hi, rx_ref[...], rows_contract, preferred_element_type=F32) + \
        lax.dot_general(cs_lo, rx_ref[...], rows_contract, preferred_element_type=F32)
    ct = tab[:, :LANES] + r1_ref[...]
    st = tab[:, LANES:]
    lane = lax.broadcasted_iota(jnp.int32, ct.shape, 1) % HEAD_DIM
    second_half = (lane >= ROT_DIM // 2) & (lane < ROT_DIM)
    q_refs = ((1, q1_ref),), ((4, q4_ref),), ((16, q16_ref),)
    chunk_slabs = 4
    for c0 in reversed(range(0, qk_slabs, chunk_slabs)):
        pc = jnp.dot(hb, w_ref[:, c0 * LANES:(c0 + chunk_slabs) * LANES], preferred_element_type=F32)
        for s in range(chunk_slabs):
            j = c0 + s
            grp, jj = divmod(j, GROUP_SLABS)
            t = pc[:, _slab(s)]
            partner = jnp.where(second_half,
                                pltpu.roll(t, ROT_DIM // 2, axis=1),
                                pltpu.roll(t, LANES - ROT_DIM // 2, axis=1))
            r = t * ct + partner * st
            if grp < 3:
                r = r * (math.log2(math.e) / math.sqrt(HEAD_DIM))
            sc_ref[j] = r
            scatter(j, jj, q_refs[grp] if grp < 3 else ((1, k1_ref), (4, k4_ref), (16, k16_ref)))


def _rope_spread():
    half = ROT_DIM // 2
    dim = jnp.arange(LANES)[None, :] % HEAD_DIM
    i = jnp.arange(half)[:, None]
    first, second = dim == i, dim == i + half
    ct_rows = (first | second).astype(F32)
    st_rows = second.astype(F32) - first.astype(F32)
    zero = jnp.zeros_like(ct_rows)
    spread = jnp.concatenate([jnp.concatenate([ct_rows, zero], axis=1),
                              jnp.concatenate([zero, st_rows], axis=1)], axis=0)
    return spread.astype(BF16), (dim >= ROT_DIM).astype(F32)


def _side_cast_spec(w, steps):
    rows, cols = w.shape
    assert rows % (steps * 2 * SUBLANES) == 0
    return (pl.BlockSpec((rows // steps, cols), lambda i: (i, 0)),
            jax.ShapeDtypeStruct(w.shape, BF16))


def _inproj(x2, mod, g_pre, w_in_b, cos_sin, w_side, bsz, seq):
    t_tok, d = x2.shape
    in_w = w_in_b.shape[1]
    ssm_w = in_w - 5 * GROUP_WIDTH
    groups = ssm_w // SSM_P
    tm = TOK_TILE
    per_b = seq // tm
    tok = lambda w: pl.BlockSpec((tm, w), lambda i: (i, 0))
    side_spec, side_shape = _side_cast_spec(w_side, t_tok // tm)

    def res_spec(dil):
        return pl.BlockSpec((None, dil, tm // dil, GROUP_WIDTH), lambda i: (i // per_b, 0, i % per_b, 0))

    def res_shape(dil):
        return jax.ShapeDtypeStruct((bsz, dil, seq // dil, GROUP_WIDTH), BF16)

    nat = jax.ShapeDtypeStruct((t_tok, GROUP_WIDTH), BF16)
    out_shape = [nat, res_shape(4), res_shape(16)] * 3 + \
                [jax.ShapeDtypeStruct((groups, t_tok // SSM_CHUNK, CHUNK_W), BF16), side_shape]
    out_specs = [tok(GROUP_WIDTH), res_spec(4), res_spec(16)] * 3 + \
                [pl.BlockSpec((groups, tm // SSM_CHUNK, CHUNK_W), lambda i: (0, i, 0)), side_spec]
    return pl.pallas_call(
        functools.partial(_inproj_kernel, tm=tm),
        grid=(t_tok // tm,),
        in_specs=[tok(d),
                  pl.BlockSpec((None, N_MOD, d), lambda i: (i // per_b, 0, 0)),
                  _resident((1, d)),
                  _resident((d, in_w)),
                  pl.BlockSpec((ROT_DIM, tm), lambda i: (0, i)),
                  _resident((ROT_DIM, 2 * LANES)), _resident((1, LANES)),
                  side_spec],
        out_specs=out_specs,
        out_shape=out_shape,
        scratch_shapes=[pltpu.VMEM((in_w // LANES, tm, LANES), F32)],
        compiler_params=_cparams(("parallel",), 58),
        name="inproj",
    )(x2, mod, g_pre, w_in_b, cos_sin, *_rope_spread(), w_side)


def _stat_lanes(h):
    a, b = slice(h, h + 1), slice(HEAD_DIM + h, HEAD_DIM + h + 1)
    return (a, b) if h % 2 == 0 else (b, a)


def _attn_kernel(*refs, nblk, units, side):
    if side:
        q_ref, k_ref, v_ref, wside_ref, o_ref, s_ref, wside_out_ref, sc_ref, pr_ref, mk_ref = refs
        wside_out_ref[...] = wside_ref[...].astype(BF16)
    else:
        q_ref, k_ref, v_ref, o_ref, s_ref, sc_ref, pr_ref, mk_ref = refs

    pair = 2 * ATT_BLK
    qi = lax.broadcasted_iota(jnp.int32, (pair, 2 * ATT_BLK), 0) % ATT_BLK
    kj = lax.broadcasted_iota(jnp.int32, (pair, 2 * ATT_BLK), 1)
    rel = qi - kj
    mk_ref[0] = jnp.where((rel >= 0) & (rel <= ATT_BLK), 0.0, NEG)
    mk_ref[1] = jnp.where((rel + ATT_BLK >= 0) & (rel <= 0), 0.0, NEG)
    low = lax.broadcasted_iota(jnp.int32, (ATT_BLK, LANES), 1) < HEAD_DIM
    top, bot = slice(0, ATT_BLK), slice(ATT_BLK, pair)
    ones = jnp.ones((2 * ATT_BLK, LANES), BF16)

    def coords(u):
        u = jnp.minimum(u, units - 1)
        r, i = u // nblk, u % nblk
        rows = pl.ds(pl.multiple_of(i * ATT_BLK, ATT_BLK), ATT_BLK)
        krows = pl.ds(pl.multiple_of(jnp.maximum(i - 1, 0) * ATT_BLK, ATT_BLK), 2 * ATT_BLK)
        return r, i, rows, krows

    def scores(u, slot):
        r, i, rows, krows = coords(u)
        bias = mk_ref[jnp.minimum(i, 1)]
        for jj in range(GROUP_SLABS):
            q2 = q_ref[r, rows, _slab(jj)]
            zero = jnp.zeros_like(q2)
            qp = jnp.concatenate([jnp.where(low, q2, zero), jnp.where(low, zero, q2)], axis=0)
            s = lax.dot_general(qp, k_ref[r, krows, _slab(jj)], (((1,), (1,)), ((), ())),
                                preferred_element_type=F32)
            sc_ref[slot, jj] = s + bias

    def softmax(u, slot):
        r, _, rows, _ = coords(u)
        s_ref[r, rows, :] = jnp.zeros((ATT_BLK, LANES), F32)
        for jj in range(GROUP_SLABS):
            s = sc_ref[slot, jj]
            m = jnp.max(s, axis=1, keepdims=True)
            pr_ref[slot, jj] = jnp.exp2(s - m).astype(BF16)
            s_ref[r, rows, _stat_lanes(2 * jj)[0]] = m[top]
            s_ref[r, rows, _stat_lanes(2 * jj + 1)[0]] = m[bot]

    def values(u, slot):
        r, _, rows, krows = coords(u)
        for jj in range(GROUP_SLABS):
            v_ext = jnp.concatenate([v_ref[r, krows, _slab(jj)], ones], axis=1)
            oe = jnp.dot(pr_ref[slot, jj], v_ext, preferred_element_type=F32)
            l_even, l_odd = _stat_lanes(2 * jj)[1], _stat_lanes(2 * jj + 1)[1]
            s_ref[r, rows, l_even] = oe[top, LANES:][:, l_even]
            s_ref[r, rows, l_odd] = oe[bot, LANES:][:, l_odd]
            o_ref[r, rows, _slab(jj)] = jnp.where(low, oe[top, :LANES], oe[bot, :LANES]).astype(o_ref.dtype)

    scores(0, 0)
    softmax(0, 0)
    scores(1, 1)

    def body(t, carry):
        u = UNITS_PER_TRIP * t
        for k in range(UNITS_PER_TRIP):
            scores(u + k + 2, k % 2)
            softmax(u + k + 1, (k + 1) % 2)
            values(u + k, k % 2)
        return carry

    assert UNITS_PER_TRIP % 2 == 0 and units % UNITS_PER_TRIP == 0
    lax.fori_loop(0, units // UNITS_PER_TRIP, body, 0)


def _attn_group(q, k, v, dilation, bsz, seq, w_side=None):
    n = seq // dilation
    nblk = n // ATT_BLK
    spec = lambda w: pl.BlockSpec((None, dilation, n, w), lambda b: (b, 0, 0, 0))
    win = (2 * ATT_BLK, 2 * ATT_BLK)
    ins, in_specs = [q, k, v], [spec(GROUP_WIDTH)] * 3
    out_specs = [spec(GROUP_WIDTH), spec(LANES)]
    out_shape = [jax.ShapeDtypeStruct((bsz, dilation, n, GROUP_WIDTH), BF16),
                 jax.ShapeDtypeStruct((bsz, dilation, n, LANES), F32)]
    if w_side is not None:
        side_spec, side_shape = _side_cast_spec(w_side, bsz)
        ins, in_specs = ins + [w_side], in_specs + [side_spec]
        out_specs, out_shape = out_specs + [side_spec], out_shape + [side_shape]
    return pl.pallas_call(
        functools.partial(_attn_kernel, nblk=nblk, units=dilation * nblk, side=w_side is not None),
        grid=(bsz,),
        in_specs=in_specs,
        out_specs=out_specs,
        out_shape=out_shape,
        scratch_shapes=[pltpu.VMEM((2, GROUP_SLABS) + win, F32),
                        pltpu.VMEM((2, GROUP_SLABS) + win, BF16),
                        pltpu.VMEM((2,) + win, F32)],
        compiler_params=_cparams(("parallel",), 48),
        name=f"attn_d{dilation}",
    )(*ins)


def _cmul(a, b):
    return a[0] * b[0] - a[1] * b[1], a[0] * b[1] + a[1] * b[0]


def _ssm_tables(a_re, a_im, log_dt, b_re, b_im, c_re, c_im, d_skip):
    hp = lax.Precision.HIGH
    f = lambda t: t.astype(F32)
    lam = (f(a_re), f(a_im))
    dt = jnp.exp(f(log_dt))[:, None]
    g = a_re.shape[0]
    taus = jnp.arange(2 * SSM_CHUNK + 1, dtype=F32)[None, :, None]
    mag = jnp.exp(lam[0][:, None, :] * dt[:, None, :] * taus)
    ang = lam[1][:, None, :] * dt[:, None, :] * taus
    apow = (mag * jnp.cos(ang), mag * jnp.sin(ang))
    a_bar = (apow[0][:, 1], apow[1][:, 1])
    den = lam[0] * lam[0] + lam[1] * lam[1]
    coef = _cmul((a_bar[0] - 1.0, a_bar[1]), (lam[0] / den, -lam[1] / den))
    b_bar = _cmul((coef[0][..., None], coef[1][..., None]), (f(b_re), f(b_im)))
    c_mat = (f(c_re), f(c_im))

    ca = _cmul((c_mat[0][:, None], c_mat[1][:, None]),
               (apow[0][:, :SSM_CHUNK, None, :], apow[1][:, :SSM_CHUNK, None, :]))
    kern = jnp.einsum('gtpn,gnq->gtpq', ca[0], b_bar[0], precision=hp) - \
        jnp.einsum('gtpn,gnq->gtpq', ca[1], b_bar[1], precision=hp)
    kern = kern.at[:, 0].add(f(d_skip)[:, :, None] * jnp.eye(SSM_P, dtype=F32))
    m_row = kern.transpose(0, 3, 1, 2).reshape(g, SSM_P, CHUNK_W)

    rev = (apow[0][:, SSM_CHUNK - 1::-1][:, :SSM_CHUNK], apow[1][:, SSM_CHUNK - 1::-1][:, :SSM_CHUNK])
    bt = (b_bar[0].transpose(0, 2, 1)[:, None], b_bar[1].transpose(0, 2, 1)[:, None])
    w4 = _cmul((rev[0][:, :, None, :], rev[1][:, :, None, :]), bt)
    w_re, w_im = (t.reshape(g, CHUNK_W, SSM_N) for t in w4)
    w_mat = jnp.concatenate([w_re, w_im, w_im, w_re], axis=-1)

    ct = (c_mat[0].transpose(0, 2, 1)[:, :, None, :], c_mat[1].transpose(0, 2, 1)[:, :, None, :])
    at = (apow[0][:, 1:SSM_CHUNK + 1].transpose(0, 2, 1)[..., None],
          apow[1][:, 1:SSM_CHUNK + 1].transpose(0, 2, 1)[..., None])
    v4 = _cmul(ct, at)
    v_re, v_im = (t.reshape(g, SSM_N, CHUNK_W) for t in v4)
    v_mat = jnp.concatenate([v_re, -v_im], axis=1)

    def mul_rows(tau):
        re, im = apow[0][:, tau], apow[1][:, tau]
        return [jnp.concatenate([re, re], -1), jnp.concatenate([-im, im], -1),
                jnp.concatenate([im, -im], -1)]
    ptab = jnp.stack(mul_rows(SSM_CHUNK) + mul_rows(2 * SSM_CHUNK) +
                     [jnp.zeros((g, LANES), F32)] * 2, axis=1)
    return m_row, w_mat.astype(BF16), v_mat.astype(BF16), ptab


def _toeplitz(m_row):
    lane = lax.broadcasted_iota(jnp.int32, m_row.shape, 1)
    blocks = [m_row]
    for s in range(1, SSM_CHUNK):
        blocks.append(jnp.where(lane >= s * SSM_P, pltpu.roll(m_row, s * SSM_P, axis=1), 0.0))
    return jnp.concatenate(blocks, axis=0).astype(BF16)


def _ssm_kernel(u_ref, m_ref, w_ref, v_ref, p_ref, y_ref, gx_ref, gz_ref, h_ref, *, groups, nchunk, bsz):
    rows = nchunk * bsz
    row = lax.broadcasted_iota(jnp.int32, (rows, LANES), 0)
    first_chunk = (row % nchunk) == 0

    def prev_chunk(a):
        return jnp.where(first_chunk, 0.0, pltpu.roll(a, 1, axis=0))

    for g in range(groups):
        e = jnp.dot(u_ref[g], w_ref[g], preferred_element_type=F32)
        ex, ez = e[:, :LANES], e[:, LANES:]
        exs, ezs = prev_chunk(ex), prev_chunk(ez)
        p1, p2, p3 = p_ref[g, 0:1, :], p_ref[g, 1:2, :], p_ref[g, 2:3, :]
        gx = ex + p1 * exs + p2 * ezs
        gz = ez + p1 * ezs + p3 * exs
        for b in range(bsz):
            gx_ref[g, pl.ds(b, nchunk, stride=bsz), :] = gx[b * nchunk:(b + 1) * nchunk]
            gz_ref[g, pl.ds(b, nchunk, stride=bsz), :] = gz[b * nchunk:(b + 1) * nchunk]

    def step(j, carry):
        r0 = pl.multiple_of(j * SUBLANES, SUBLANES)
        new = []
        for g in range(groups):
            x, z = carry[g]
            q1, q2, q3 = p_ref[g, 3:4, :], p_ref[g, 4:5, :], p_ref[g, 5:6, :]
            xn = q1 * x + q2 * z + gx_ref[g, pl.ds(r0, SUBLANES), :]
            zn = q1 * z + q3 * x + gz_ref[g, pl.ds(r0, SUBLANES), :]
            h_ref[g, pl.ds(r0, SUBLANES), :] = xn
            new.append((xn, zn))
        return tuple(new)

    zero = jnp.zeros((SUBLANES, LANES), F32)
    lax.fori_loop(0, rows // SUBLANES, step, tuple((zero, zero) for _ in range(groups)))

    for g in range(groups):
        h_end = jnp.concatenate([h_ref[g, pl.ds(b, nchunk, stride=bsz), :] for b in range(bsz)], axis=0)
        h_in = prev_chunk(h_end).astype(BF16)
        y = jnp.dot(u_ref[g], _toeplitz(m_ref[g]), preferred_element_type=F32)
        y = y + jnp.dot(h_in, v_ref[g], preferred_element_type=F32)
        y_ref[g] = y.astype(y_ref.dtype)


def _ssm(uc, tables, bsz, seq):
    m_row, w_mat, v_mat, ptab = tables
    g, rows, _ = uc.shape
    nchunk = seq // SSM_CHUNK
    assert rows == nchunk * bsz and SUBLANES % bsz == 0 and SUBLANES // bsz == 2
    gs = 4
    blk = lambda a, b: pl.BlockSpec((gs, a, b), lambda i: (i, 0, 0))
    return pl.pallas_call(
        functools.partial(_ssm_kernel, groups=gs, nchunk=nchunk, bsz=bsz),
        grid=(g // gs,),
        in_specs=[blk(rows, CHUNK_W), blk(SSM_P, CHUNK_W), blk(CHUNK_W, CHUNK_W),
                  blk(LANES, CHUNK_W), blk(SUBLANES, LANES)],
        out_specs=blk(rows, CHUNK_W),
        out_shape=jax.ShapeDtypeStruct((g, rows, CHUNK_W), BF16),
        scratch_shapes=[pltpu.VMEM((gs, rows, LANES), F32)] * 3,
        compiler_params=_cparams(("parallel",), 48),
        name="ssm",
    )(uc, m_row, w_mat, v_mat, ptab)


def _rms(x):
    return x * lax.rsqrt(jnp.mean(x * x, axis=-1, keepdims=True) + EPS)


def _mix_kernel(o1_ref, l1_ref, o4_ref, l4_ref, o16_ref, l16_ref, y_ref, x_ref, mod_ref,
                ga_ref, gs_ref, gp_ref, ex_ref, wg_ref, bg_ref, wo_ref, wside_ref, out_ref,
                wside_out_ref, so_ref, sl_ref, sy_ref, *, tm):
    wside_out_ref[...] = wside_ref[...].astype(BF16)

    for gi, (dil, o_ref, l_ref) in enumerate(((4, o4_ref, l4_ref), (16, o16_ref, l16_ref))):
        for r in range(dil):
            rows = pl.ds(r, tm // dil, stride=dil)
            for jj in range(GROUP_SLABS):
                so_ref[gi * GROUP_SLABS + jj, rows, :] = o_ref[r, :, _slab(jj)].astype(F32)
            sl_ref[gi, rows, :] = l_ref[r]

    ngb, nc = y_ref.shape[0] // SLOTS, tm // SSM_CHUNK
    vs = [jnp.concatenate([y_ref[gb * SLOTS + g, :, _slab(half)].astype(F32)
                           for gb in range(ngb) for half in range(2)], axis=0)
          for g in range(SLOTS)]
    for s, o in enumerate(_slot_transpose(vs)):
        for gb in range(ngb):
            for half in range(2):
                for cb in range(nc // SUBLANES):
                    r0 = (gb * 2 + half) * nc + cb * SUBLANES
                    sy_ref[gb, pl.ds(cb * SSM_CHUNK * SUBLANES + half * SLOTS + s, SUBLANES,
                                     stride=SSM_CHUNK), :] = o[r0:r0 + SUBLANES]

    lane = lax.broadcasted_iota(jnp.int32, (tm, LANES), 1)
    even = (lane & 1) == 0
    ms, ls = [], []
    for st in (l1_ref[...], sl_ref[0], sl_ref[1]):
        sw = pltpu.roll(st, HEAD_DIM, axis=1)
        ms.append(jnp.where(even, st, sw))
        ls.append(jnp.where(even, sw, st))
    mx = jnp.maximum(jnp.maximum(ms[0], ms[1]), ms[2])
    es = [jnp.exp2(m - mx) for m in ms]
    inv = 1.0 / (es[0] * ls[0] + es[1] * ls[1] + es[2] * ls[2])
    head_lane = lane < HEADS_PER_GROUP
    wexp = [jnp.dot(jnp.where(head_lane, e * inv, 0.0).astype(BF16), ex_ref[...],
                    preferred_element_type=F32) for e in es]
    att_slabs = []
    for jj in range(GROUP_SLABS):
        a = wexp[0][:, _slab(jj)] * o1_ref[:, _slab(jj)].astype(F32)
        a = a + wexp[1][:, _slab(jj)] * so_ref[jj]
        a = a + wexp[2][:, _slab(jj)] * so_ref[GROUP_SLABS + jj]
        att_slabs.append(a)
    att = _rms(jnp.concatenate(att_slabs, axis=1)) * ga_ref[...]

    yv = jnp.concatenate([sy_ref[gb] for gb in range(sy_ref.shape[0])], axis=1)
    s = jax.nn.gelu(yv)
    z = jnp.dot(s.astype(BF16), wg_ref[...], preferred_element_type=F32) + bg_ref[...]
    s = s * jax.nn.sigmoid(z)
    ssm = _rms(s) * gs_ref[...]
    aw = att.shape[1]
    mix = jnp.dot(att.astype(BF16), wo_ref[:aw, :], preferred_element_type=F32)
    mix = mix + jnp.dot(ssm.astype(BF16), wo_ref[aw:, :], preferred_element_type=F32)
    out_ref[...] = x_ref[...] + mod_ref[2:3, :] * (_rms(mix) * gp_ref[...])


def _mix(o1, l1, o4, l4, o16, l16, yc, x2, mod, g_attn, g_ssm, g_post, w_glu_b, b_glu, w_out_b, w_side,
         bsz, seq):
    t_tok, d = x2.shape
    groups = yc.shape[0]
    sw = groups * SSM_P
    aw = GROUP_WIDTH
    tm = TOK_TILE
    per_b = seq // tm
    tok = lambda w: pl.BlockSpec((tm, w), lambda i: (i, 0))
    res = lambda dil, w: pl.BlockSpec((None, dil, tm // dil, w), lambda i: (i // per_b, 0, i % per_b, 0))
    head_of_lane = jnp.arange(aw)[None, :] // HEAD_DIM
    expand = (jnp.arange(LANES)[:, None] == head_of_lane).astype(BF16)
    side_spec, side_shape = _side_cast_spec(w_side, t_tok // tm)
    return pl.pallas_call(
        functools.partial(_mix_kernel, tm=tm),
        grid=(t_tok // tm,),
        in_specs=[tok(aw), tok(LANES), res(4, aw), res(4, LANES), res(16, aw), res(16, LANES),
                  pl.BlockSpec((groups, tm // SSM_CHUNK, CHUNK_W), lambda i: (0, i, 0)),
                  tok(d),
                  pl.BlockSpec((None, N_MOD, d), lambda i: (i // per_b, 0, 0)),
                  _resident((1, aw)), _resident((1, sw)), _resident((1, d)),
                  _resident((LANES, aw)),
                  _resident((sw, sw)), _resident((1, sw)), _resident((aw + sw, d)),
                  side_spec],
        out_specs=[tok(d), side_spec],
        out_shape=[jax.ShapeDtypeStruct((t_tok, d), F32), side_shape],
        scratch_shapes=[pltpu.VMEM((2 * GROUP_SLABS, tm, LANES), F32),
                        pltpu.VMEM((2, tm, LANES), F32),
                        pltpu.VMEM((sw // LANES, tm, LANES), F32)],
        compiler_params=_cparams(("parallel",), 56),
        name="mix",
    )(o1, l1, o4, l4, o16, l16, yc, x2, mod, g_attn, g_ssm, g_post, expand, w_glu_b, b_glu, w_out_b, w_side)


def _mlp_kernel(x_ref, mod_ref, gpre_ref, gpost_ref, w1_ref, w2_ref, o_ref, h_ref):
    f = pl.program_id(1)
    last = pl.num_programs(1) - 1
    chunks = [slice(r, r + ROW_CHUNK) for r in range(0, x_ref.shape[0], ROW_CHUNK)]

    def partial():
        t = jnp.dot(h_ref[...], w1_ref[...], preferred_element_type=F32)
        t = jnp.square(jnp.maximum(t, 0.0)).astype(BF16)
        return jnp.dot(t, w2_ref[...], preferred_element_type=F32)

    @pl.when(f == 0)
    def _():
        scale = gpre_ref[...] * (1.0 + mod_ref[4:5, :])
        shift = mod_ref[3:4, :]
        for rows in chunks:
            h_ref[rows, :] = (_rms(x_ref[rows, :]) * scale + shift).astype(BF16)
        o_ref[...] = partial()

    @pl.when((f > 0) & (f < last))
    def _():
        o_ref[...] += partial()

    @pl.when(f == last)
    def _():
        gate = mod_ref[5:6, :] * gpost_ref[...]
        acc = o_ref[...] + partial()
        for rows in chunks:
            o_ref[rows, :] = x_ref[rows, :] + gate * _rms(acc[rows, :])


def _mlp(x2, mod, g_pre, g_post, w1_b, w2_b, seq):
    t_tok, d = x2.shape
    d_ff = w1_b.shape[1]
    tm, tf = 512, 2048
    assert d_ff // tf >= 2
    per_b = seq // tm
    return pl.pallas_call(
        _mlp_kernel,
        grid=(t_tok // tm, d_ff // tf),
        in_specs=[pl.BlockSpec((tm, d), lambda i, f: (i, 0)),
                  pl.BlockSpec((None, N_MOD, d), lambda i, f: (i // per_b, 0, 0)),
                  _resident((1, d)), _resident((1, d)),
                  pl.BlockSpec((d, tf), lambda i, f: (0, f)),
                  pl.BlockSpec((tf, d), lambda i, f: (f, 0))],
        out_specs=pl.BlockSpec((tm, d), lambda i, f: (i, 0)),
        out_shape=jax.ShapeDtypeStruct((t_tok, d), F32),
        scratch_shapes=[pltpu.VMEM((tm, d), BF16)],
        compiler_params=_cparams(("parallel", "arbitrary"), 60),
        name="mlp",
    )(x2, mod, g_pre, g_post, w1_b, w2_b)


def _rope_cos_sin(positions):
    freqs = ROPE_THETA ** (-jnp.arange(0, ROT_DIM, 2, dtype=F32) / ROT_DIM)
    ang = positions.reshape(-1).astype(F32)[None, :] * freqs[:, None]
    return jnp.concatenate([jnp.cos(ang), jnp.sin(ang)], axis=0)


def kernel(x, c, positions, w_ada, b_ada, g_pre_mix, g_post_mix, w_in, ssm_a_re, ssm_a_im, ssm_log_dt,
           ssm_b_re, ssm_b_im, ssm_c_re, ssm_c_im, ssm_d, w_glu, b_glu, g_attn_out, g_ssm_out, w_out,
           g_pre_mlp, g_post_mlp, w_mlp_in, w_mlp_out):
    bsz, seq, d = x.shape
    depth = w_ada.shape[0]
    x2 = x.reshape(bsz * seq, d)
    cos_sin = _rope_cos_sin(positions)
    row = lambda a: a.reshape(1, -1).astype(F32)
    for l in range(depth):
        mod = _adaln(c, w_ada[l], b_ada[l])
        q1, q4, q16, k1, k4, k16, v1, v4, v16, uc, w_mlp_out_b = _inproj(
            x2, mod, row(g_pre_mix[l]), w_in[l].astype(BF16), cos_sin, w_mlp_out[l], bsz, seq)

        nat = lambda a: a.reshape(bsz, 1, seq, a.shape[-1])
        o1, l1, w_out_b = _attn_group(nat(q1), nat(k1), nat(v1), 1, bsz, seq, w_out[l])
        o4, l4, w_glu_b = _attn_group(q4, k4, v4, 4, bsz, seq, w_glu[l])
        o16, l16 = _attn_group(q16, k16, v16, 16, bsz, seq)

        tables = _ssm_tables(ssm_a_re[l], ssm_a_im[l], ssm_log_dt[l], ssm_b_re[l], ssm_b_im[l],
                             ssm_c_re[l], ssm_c_im[l], ssm_d[l])
        yc = _ssm(uc, tables, bsz, seq)

        x2, w_mlp_in_b = _mix(o1.reshape(bsz * seq, -1), l1.reshape(bsz * seq, -1), o4, l4, o16, l16, yc,
                              x2, mod, row(g_attn_out[l]), row(g_ssm_out[l]), row(g_post_mix[l]),
                              w_glu_b, row(b_glu[l]), w_out_b, w_mlp_in[l],
                              bsz, seq)
        x2 = _mlp(x2, mod, row(g_pre_mlp[l]), row(g_post_mlp[l]), w_mlp_in_b, w_mlp_out_b, seq)
    return x2.reshape(bsz, seq, d)
```

```python
import functools
import math

import jax
import jax.numpy as jnp
from jax import lax
from jax.experimental import pallas as pl
from jax.experimental.pallas import tpu as pltpu

F32 = jnp.float32
BF16 = jnp.bfloat16

LANES = 128
SUBLANES = 8

HEAD_DIM = 64
ROT_DIM = HEAD_DIM // 4
ROPE_THETA = 500000.0
DILATIONS = (1, 4, 16)
HEADS_PER_GROUP = 6
GROUP_WIDTH = HEADS_PER_GROUP * HEAD_DIM
GROUP_SLABS = GROUP_WIDTH // LANES
ATT_BLK = 128
UNITS_PER_TRIP = 16
SSM_P = 16
SSM_N = 64
SSM_CHUNK = 16
CHUNK_W = SSM_CHUNK * SSM_P
SLOTS = LANES // SSM_P
N_MOD = 6
EPS = 1e-6
NEG = -1e30
TOK_TILE = 512
ROW_CHUNK = 128


MIB = 1024 * 1024
VMEM_MIB_V7X = 64
VMEM_MIB = dict(adaln=40, inproj=58, attn=48, ssm=48, mix=56, mlp=60)


def _cparams(sem, kernel_name):
    assert VMEM_MIB[kernel_name] < VMEM_MIB_V7X
    return pltpu.CompilerParams(dimension_semantics=sem, vmem_limit_bytes=VMEM_MIB[kernel_name] * MIB)


def _resident(shape):
    nd = len(shape)
    return pl.BlockSpec(shape, lambda *_: (0,) * nd, pipeline_mode=pl.Buffered(1))


def _slab(j):
    return slice(j * LANES, (j + 1) * LANES)


def _slot_transpose(vs):
    slot = lax.broadcasted_iota(jnp.int32, vs[0].shape, 1) // SSM_P
    vs = list(vs)
    for dist in (4, 2, 1):
        low = (slot & dist) == 0
        nxt = list(vs)
        for i in range(SLOTS):
            if i & dist == 0:
                a, b = vs[i], vs[i + dist]
                nxt[i] = jnp.where(low, a, pltpu.roll(b, dist * SSM_P, axis=1))
                nxt[i + dist] = jnp.where(low, pltpu.roll(a, LANES - dist * SSM_P, axis=1), b)
        vs = nxt
    return vs


def _adaln_kernel(s_ref, w_ref, b_ref, o_ref):
    o_ref[...] = jnp.dot(s_ref[...], w_ref[...].astype(BF16),
                         preferred_element_type=F32) + b_ref[...]


def _adaln(c, w_ada, b_ada):
    bsz, d = c.shape
    n = w_ada.shape[1]
    tn = 1024
    s = jnp.zeros((SUBLANES, d), BF16).at[:bsz].set(jax.nn.silu(c).astype(BF16))
    out = pl.pallas_call(
        _adaln_kernel,
        grid=(n // tn,),
        in_specs=[pl.BlockSpec((SUBLANES, d), lambda j: (0, 0)),
                  pl.BlockSpec((d, tn), lambda j: (0, j)),
                  pl.BlockSpec((1, tn), lambda j: (0, j))],
        out_specs=pl.BlockSpec((SUBLANES, tn), lambda j: (0, j)),
        out_shape=jax.ShapeDtypeStruct((SUBLANES, n), F32),
        compiler_params=_cparams(("arbitrary",), "adaln"),
        name="adaln",
    )(s, w_ada, b_ada.reshape(1, n))
    return out[:bsz].reshape(bsz, N_MOD, d)


def _inproj_kernel(x_ref, mod_ref, g_ref, w_ref, cs_ref, rx_ref, r1_ref, wside_ref,
                   q1_ref, q4_ref, q16_ref, k1_ref, k4_ref, k16_ref, v1_ref, v4_ref, v16_ref,
                   uc_ref, wside_out_ref, sc_ref, *, tm):
    wside_out_ref[...] = wside_ref[...].astype(BF16)

    x = x_ref[...]
    ms = jnp.mean(x * x, axis=-1, keepdims=True)
    scale = g_ref[...] * (1.0 + mod_ref[1:2, :])
    h = (x * lax.rsqrt(ms + EPS)) * scale + mod_ref[0:1, :]
    hb = h.astype(BF16)

    qk_slabs = 4 * GROUP_SLABS

    def scatter(j, jj, refs):
        for dil, ref in refs:
            if dil == 1:
                ref[:, _slab(jj)] = sc_ref[j].astype(BF16)
            else:
                for r in range(dil):
                    ref[r, :, _slab(jj)] = sc_ref[j, pl.ds(r, tm // dil, stride=dil), :].astype(BF16)

    pvu = jnp.dot(hb, w_ref[:, qk_slabs * LANES:], preferred_element_type=F32)
    for j in range(pvu.shape[1] // LANES):
        sc_ref[qk_slabs + j] = pvu[:, _slab(j)]
    for jj in range(GROUP_SLABS):
        scatter(qk_slabs + jj, jj, ((1, v1_ref), (4, v4_ref), (16, v16_ref)))

    u0 = qk_slabs + GROUP_SLABS
    ngb, nc = uc_ref.shape[0] // SLOTS, tm // SSM_CHUNK
    vs = [jnp.concatenate([sc_ref[u0 + gb, pl.ds(cb * SSM_CHUNK * SUBLANES + half * SLOTS + s, SUBLANES,
                                                 stride=SSM_CHUNK), :]
                           for gb in range(ngb) for half in range(2) for cb in range(nc // SUBLANES)],
                          axis=0) for s in range(SLOTS)]
    for g, o in enumerate(_slot_transpose(vs)):
        for gb in range(ngb):
            r0 = gb * 2 * nc
            full = jnp.concatenate([o[r0:r0 + nc], o[r0 + nc:r0 + 2 * nc]], axis=1)
            uc_ref[gb * SLOTS + g] = full.astype(BF16)

    cs = cs_ref[...]
    cs_hi = cs.astype(BF16)
    cs_lo = (cs - cs_hi.astype(F32)).astype(BF16)
    rows_contract = (((0,), (0,)), ((), ()))
    tab = lax.dot_general(cs_hi, rx_ref[...], rows_contract, preferred_element_type=F32) + \
        lax.dot_general(cs_lo, rx_ref[...], rows_contract, preferred_element_type=F32)
    ct = tab[:, :LANES] + r1_ref[...]
    st = tab[:, LANES:]
    q_scale = math.log2(math.e) / math.sqrt(HEAD_DIM)
    ct_q, st_q = ct * q_scale, st * q_scale
    lane = lax.broadcasted_iota(jnp.int32, ct.shape, 1) % HEAD_DIM
    second_half = (lane >= ROT_DIM // 2) & (lane < ROT_DIM)
    q_refs = ((1, q1_ref),), ((4, q4_ref),), ((16, q16_ref),)
    chunk_slabs = 4
    for c0 in reversed(range(0, qk_slabs, chunk_slabs)):
        pc = jnp.dot(hb, w_ref[:, c0 * LANES:(c0 + chunk_slabs) * LANES], preferred_element_type=F32)
        for s in range(chunk_slabs):
            j = c0 + s
            grp, jj = divmod(j, GROUP_SLABS)
            t = pc[:, _slab(s)]
            partner = jnp.where(second_half,
                                pltpu.roll(t, ROT_DIM // 2, axis=1),
                                pltpu.roll(t, LANES - ROT_DIM // 2, axis=1))
            r = t * ct_q + partner * st_q if grp < 3 else t * ct + partner * st
            sc_ref[j] = r
            scatter(j, jj, q_refs[grp] if grp < 3 else ((1, k1_ref), (4, k4_ref), (16, k16_ref)))


def _rope_spread():
    half = ROT_DIM // 2
    dim = jnp.arange(LANES)[None, :] % HEAD_DIM
    i = jnp.arange(half)[:, None]
    first, second = dim == i, dim == i + half
    ct_rows = (first | second).astype(F32)
    st_rows = second.astype(F32) - first.astype(F32)
    zero = jnp.zeros_like(ct_rows)
    spread = jnp.concatenate([jnp.concatenate([ct_rows, zero], axis=1),
                              jnp.concatenate([zero, st_rows], axis=1)], axis=0)
    return spread.astype(BF16), (dim >= ROT_DIM).astype(F32)


def _side_cast_spec(w, steps):
    rows, cols = w.shape
    assert rows % (steps * 2 * SUBLANES) == 0
    return (pl.BlockSpec((rows // steps, cols), lambda i: (i, 0)),
            jax.ShapeDtypeStruct(w.shape, BF16))


def _inproj(x2, mod, g_pre, w_in_b, cos_sin, w_side, bsz, seq):
    t_tok, d = x2.shape
    in_w = w_in_b.shape[1]
    ssm_w = in_w - 5 * GROUP_WIDTH
    groups = ssm_w // SSM_P
    tm = TOK_TILE
    per_b = seq // tm
    tok = lambda w: pl.BlockSpec((tm, w), lambda i: (i, 0))
    side_spec, side_shape = _side_cast_spec(w_side, t_tok // tm)

    def res_spec(dil):
        return pl.BlockSpec((None, dil, tm // dil, GROUP_WIDTH), lambda i: (i // per_b, 0, i % per_b, 0))

    def res_shape(dil):
        return jax.ShapeDtypeStruct((bsz, dil, seq // dil, GROUP_WIDTH), BF16)

    nat = jax.ShapeDtypeStruct((t_tok, GROUP_WIDTH), BF16)
    out_shape = [nat, res_shape(4), res_shape(16)] * 3 + \
                [jax.ShapeDtypeStruct((groups, t_tok // SSM_CHUNK, CHUNK_W), BF16), side_shape]
    out_specs = [tok(GROUP_WIDTH), res_spec(4), res_spec(16)] * 3 + \
                [pl.BlockSpec((groups, tm // SSM_CHUNK, CHUNK_W), lambda i: (0, i, 0)), side_spec]
    return pl.pallas_call(
        functools.partial(_inproj_kernel, tm=tm),
        grid=(t_tok // tm,),
        in_specs=[tok(d),
                  pl.BlockSpec((None, N_MOD, d), lambda i: (i // per_b, 0, 0)),
                  _resident((1, d)),
                  _resident((d, in_w)),
                  pl.BlockSpec((ROT_DIM, tm), lambda i: (0, i)),
                  _resident((ROT_DIM, 2 * LANES)), _resident((1, LANES)),
                  side_spec],
        out_specs=out_specs,
        out_shape=out_shape,
        scratch_shapes=[pltpu.VMEM((in_w // LANES, tm, LANES), F32)],
        compiler_params=_cparams(("parallel",), "inproj"),
        name="inproj",
    )(x2, mod, g_pre, w_in_b, cos_sin, *_rope_spread(), w_side)


def _stat_lanes(h):
    a, b = slice(h, h + 1), slice(HEAD_DIM + h, HEAD_DIM + h + 1)
    return (a, b) if h % 2 == 0 else (b, a)


def _attn_kernel(*refs, nblk, units, side):
    if side:
        q_ref, k_ref, v_ref, wside_ref, o_ref, s_ref, wside_out_ref, sc_ref, pr_ref, mk_ref = refs
        wside_out_ref[...] = wside_ref[...].astype(BF16)
    else:
        q_ref, k_ref, v_ref, o_ref, s_ref, sc_ref, pr_ref, mk_ref = refs

    pair = 2 * ATT_BLK
    qi = lax.broadcasted_iota(jnp.int32, (pair, 2 * ATT_BLK), 0) % ATT_BLK
    kj = lax.broadcasted_iota(jnp.int32, (pair, 2 * ATT_BLK), 1)
    rel = qi - kj
    mk_ref[0] = jnp.where((rel >= 0) & (rel <= ATT_BLK), 0.0, NEG)
    mk_ref[1] = jnp.where((rel + ATT_BLK >= 0) & (rel <= 0), 0.0, NEG)
    low = lax.broadcasted_iota(jnp.int32, (ATT_BLK, LANES), 1) < HEAD_DIM
    top, bot = slice(0, ATT_BLK), slice(ATT_BLK, pair)
    ones = jnp.ones((2 * ATT_BLK, LANES), BF16)

    def coords(u):
        u = jnp.minimum(u, units - 1)
        r, i = u // nblk, u % nblk
        rows = pl.ds(pl.multiple_of(i * ATT_BLK, ATT_BLK), ATT_BLK)
        krows = pl.ds(pl.multiple_of(jnp.maximum(i - 1, 0) * ATT_BLK, ATT_BLK), 2 * ATT_BLK)
        return r, i, rows, krows

    def scores(u, slot):
        r, i, rows, krows = coords(u)
        bias = mk_ref[jnp.minimum(i, 1)]
        for jj in range(GROUP_SLABS):
            q2 = q_ref[r, rows, _slab(jj)]
            zero = jnp.zeros_like(q2)
            qp = jnp.concatenate([jnp.where(low, q2, zero), jnp.where(low, zero, q2)], axis=0)
            s = lax.dot_general(qp, k_ref[r, krows, _slab(jj)], (((1,), (1,)), ((), ())),
                                preferred_element_type=F32)
            sc_ref[slot, jj] = s + bias

    def softmax(u, slot):
        r, _, rows, _ = coords(u)
        s_ref[r, rows, :] = jnp.zeros((ATT_BLK, LANES), F32)
        for jj in range(GROUP_SLABS):
            s = sc_ref[slot, jj]
            m = jnp.max(s, axis=1, keepdims=True)
            pr_ref[slot, jj] = jnp.exp2(s - m).astype(BF16)
            s_ref[r, rows, _stat_lanes(2 * jj)[0]] = m[top]
            s_ref[r, rows, _stat_lanes(2 * jj + 1)[0]] = m[bot]

    def values(u, slot):
        r, _, rows, krows = coords(u)
        for jj in range(GROUP_SLABS):
            v_ext = jnp.concatenate([v_ref[r, krows, _slab(jj)], ones], axis=1)
            oe = jnp.dot(pr_ref[slot, jj], v_ext, preferred_element_type=F32)
            l_even, l_odd = _stat_lanes(2 * jj)[1], _stat_lanes(2 * jj + 1)[1]
            s_ref[r, rows, l_even] = oe[top, LANES:][:, l_even]
            s_ref[r, rows, l_odd] = oe[bot, LANES:][:, l_odd]
            o_ref[r, rows, _slab(jj)] = jnp.where(low, oe[top, :LANES], oe[bot, :LANES]).astype(o_ref.dtype)

    scores(0, 0)
    softmax(0, 0)
    scores(1, 1)

    def body(t, carry):
        u = UNITS_PER_TRIP * t
        for k in range(UNITS_PER_TRIP):
            scores(u + k + 2, k % 2)
            softmax(u + k + 1, (k + 1) % 2)
            values(u + k, k % 2)
        return carry

    assert UNITS_PER_TRIP % 2 == 0 and units % UNITS_PER_TRIP == 0
    lax.fori_loop(0, units // UNITS_PER_TRIP, body, 0)


def _attn_group(q, k, v, dilation, bsz, seq, w_side=None):
    n = seq // dilation
    nblk = n // ATT_BLK
    spec = lambda w: pl.BlockSpec((None, dilation, n, w), lambda b: (b, 0, 0, 0))
    win = (2 * ATT_BLK, 2 * ATT_BLK)
    ins, in_specs = [q, k, v], [spec(GROUP_WIDTH)] * 3
    out_specs = [spec(GROUP_WIDTH), spec(LANES)]
    out_shape = [jax.ShapeDtypeStruct((bsz, dilation, n, GROUP_WIDTH), BF16),
                 jax.ShapeDtypeStruct((bsz, dilation, n, LANES), F32)]
    if w_side is not None:
        side_spec, side_shape = _side_cast_spec(w_side, bsz)
        ins, in_specs = ins + [w_side], in_specs + [side_spec]
        out_specs, out_shape = out_specs + [side_spec], out_shape + [side_shape]
    return pl.pallas_call(
        functools.partial(_attn_kernel, nblk=nblk, units=dilation * nblk, side=w_side is not None),
        grid=(bsz,),
        in_specs=in_specs,
        out_specs=out_specs,
        out_shape=out_shape,
        scratch_shapes=[pltpu.VMEM((2, GROUP_SLABS) + win, F32),
                        pltpu.VMEM((2, GROUP_SLABS) + win, BF16),
                        pltpu.VMEM((2,) + win, F32)],
        compiler_params=_cparams(("parallel",), "attn"),
        name=f"attn_d{dilation}",
    )(*ins)


def _cmul(a, b):
    return a[0] * b[0] - a[1] * b[1], a[0] * b[1] + a[1] * b[0]


def _ssm_tables(a_re, a_im, log_dt, b_re, b_im, c_re, c_im, d_skip):
    hp = lax.Precision.HIGH
    f = lambda t: t.astype(F32)
    lam = (f(a_re), f(a_im))
    dt = jnp.exp(f(log_dt))[:, None]
    g = a_re.shape[0]
    taus = jnp.arange(2 * SSM_CHUNK + 1, dtype=F32)[None, :, None]
    mag = jnp.exp(lam[0][:, None, :] * dt[:, None, :] * taus)
    ang = lam[1][:, None, :] * dt[:, None, :] * taus
    apow = (mag * jnp.cos(ang), mag * jnp.sin(ang))
    a_bar = (apow[0][:, 1], apow[1][:, 1])
    den = lam[0] * lam[0] + lam[1] * lam[1]
    coef = _cmul((a_bar[0] - 1.0, a_bar[1]), (lam[0] / den, -lam[1] / den))
    b_bar = _cmul((coef[0][..., None], coef[1][..., None]), (f(b_re), f(b_im)))
    c_mat = (f(c_re), f(c_im))

    ca = _cmul((c_mat[0][:, None], c_mat[1][:, None]),
               (apow[0][:, :SSM_CHUNK, None, :], apow[1][:, :SSM_CHUNK, None, :]))
    kern = jnp.einsum('gtpn,gnq->gtpq', ca[0], b_bar[0], precision=hp) - \
        jnp.einsum('gtpn,gnq->gtpq', ca[1], b_bar[1], precision=hp)
    kern = kern.at[:, 0].add(f(d_skip)[:, :, None] * jnp.eye(SSM_P, dtype=F32))
    m_row = kern.transpose(0, 3, 1, 2).reshape(g, SSM_P, CHUNK_W)

    rev = (apow[0][:, SSM_CHUNK - 1::-1][:, :SSM_CHUNK], apow[1][:, SSM_CHUNK - 1::-1][:, :SSM_CHUNK])
    bt = (b_bar[0].transpose(0, 2, 1)[:, None], b_bar[1].transpose(0, 2, 1)[:, None])
    w4 = _cmul((rev[0][:, :, None, :], rev[1][:, :, None, :]), bt)
    w_re, w_im = (t.reshape(g, CHUNK_W, SSM_N) for t in w4)
    w_mat = jnp.concatenate([w_re, w_im, w_im, w_re], axis=-1)

    ct = (c_mat[0].transpose(0, 2, 1)[:, :, None, :], c_mat[1].transpose(0, 2, 1)[:, :, None, :])
    at = (apow[0][:, 1:SSM_CHUNK + 1].transpose(0, 2, 1)[..., None],
          apow[1][:, 1:SSM_CHUNK + 1].transpose(0, 2, 1)[..., None])
    v4 = _cmul(ct, at)
    v_re, v_im = (t.reshape(g, SSM_N, CHUNK_W) for t in v4)
    v_mat = jnp.concatenate([v_re, -v_im], axis=1)

    def mul_rows(tau):
        re, im = apow[0][:, tau], apow[1][:, tau]
        return [jnp.concatenate([re, re], -1), jnp.concatenate([-im, im], -1),
                jnp.concatenate([im, -im], -1)]
    ptab = jnp.stack(mul_rows(SSM_CHUNK) + mul_rows(2 * SSM_CHUNK) +
                     [jnp.zeros((g, LANES), F32)] * 2, axis=1)
    return m_row, w_mat.astype(BF16), v_mat.astype(BF16), ptab


def _toeplitz(m_row):
    lane = lax.broadcasted_iota(jnp.int32, m_row.shape, 1)
    blocks = [m_row]
    for s in range(1, SSM_CHUNK):
        blocks.append(jnp.where(lane >= s * SSM_P, pltpu.roll(m_row, s * SSM_P, axis=1), 0.0))
    return jnp.concatenate(blocks, axis=0).astype(BF16)


def _ssm_kernel(u_ref, m_ref, w_ref, v_ref, p_ref, y_ref, gx_ref, gz_ref, h_ref, *, groups, nchunk, bsz):
    rows = nchunk * bsz
    row = lax.broadcasted_iota(jnp.int32, (rows, LANES), 0)
    first_chunk = (row % nchunk) == 0

    def prev_chunk(a):
        return jnp.where(first_chunk, 0.0, pltpu.roll(a, 1, axis=0))

    for g in range(groups):
        e = jnp.dot(u_ref[g], w_ref[g], preferred_element_type=F32)
        ex, ez = e[:, :LANES], e[:, LANES:]
        exs, ezs = prev_chunk(ex), prev_chunk(ez)
        p1, p2, p3 = p_ref[g, 0:1, :], p_ref[g, 1:2, :], p_ref[g, 2:3, :]
        gx = ex + p1 * exs + p2 * ezs
        gz = ez + p1 * ezs + p3 * exs
        for b in range(bsz):
            gx_ref[g, pl.ds(b, nchunk, stride=bsz), :] = gx[b * nchunk:(b + 1) * nchunk]
            gz_ref[g, pl.ds(b, nchunk, stride=bsz), :] = gz[b * nchunk:(b + 1) * nchunk]

    tile = (SUBLANES, LANES)
    mults = [[jnp.broadcast_to(p_ref[g, r:r + 1, :], tile) for r in (3, 4, 5)] for g in range(groups)]

    def step(j, carry):
        r0 = pl.multiple_of(j * SUBLANES, SUBLANES)
        new = []
        for g in range(groups):
            x, z = carry[g]
            q1, q2, q3 = mults[g]
            xn = q1 * x + q2 * z + gx_ref[g, pl.ds(r0, SUBLANES), :]
            zn = q1 * z + q3 * x + gz_ref[g, pl.ds(r0, SUBLANES), :]
            h_ref[g, pl.ds(r0, SUBLANES), :] = xn
            new.append((xn, zn))
        return tuple(new)

    zero = jnp.zeros((SUBLANES, LANES), F32)
    lax.fori_loop(0, rows // SUBLANES, step, tuple((zero, zero) for _ in range(groups)))

    for g in range(groups):
        h_end = jnp.concatenate([h_ref[g, pl.ds(b, nchunk, stride=bsz), :] for b in range(bsz)], axis=0)
        h_in = prev_chunk(h_end).astype(BF16)
        y = jnp.dot(u_ref[g], _toeplitz(m_ref[g]), preferred_element_type=F32)
        y = y + jnp.dot(h_in, v_ref[g], preferred_element_type=F32)
        y_ref[g] = y.astype(y_ref.dtype)


def _ssm(uc, tables, bsz, seq):
    m_row, w_mat, v_mat, ptab = tables
    g, rows, _ = uc.shape
    nchunk = seq // SSM_CHUNK
    assert rows == nchunk * bsz and SUBLANES % bsz == 0 and SUBLANES // bsz == 2
    gs = 4
    blk = lambda a, b: pl.BlockSpec((gs, a, b), lambda i: (i, 0, 0))
    return pl.pallas_call(
        functools.partial(_ssm_kernel, groups=gs, nchunk=nchunk, bsz=bsz),
        grid=(g // gs,),
        in_specs=[blk(rows, CHUNK_W), blk(SSM_P, CHUNK_W), blk(CHUNK_W, CHUNK_W),
                  blk(LANES, CHUNK_W), blk(SUBLANES, LANES)],
        out_specs=blk(rows, CHUNK_W),
        out_shape=jax.ShapeDtypeStruct((g, rows, CHUNK_W), BF16),
        scratch_shapes=[pltpu.VMEM((gs, rows, LANES), F32)] * 3,
        compiler_params=_cparams(("parallel",), "ssm"),
        name="ssm",
    )(uc, m_row, w_mat, v_mat, ptab)


def _rms(x):
    return x * lax.rsqrt(jnp.mean(x * x, axis=-1, keepdims=True) + EPS)


def _gelu_tanh(x):
    k = -2.0 * math.sqrt(2.0 / math.pi) * math.log2(math.e)
    e = jnp.exp2(x * (k + (k * 0.044715) * (x * x)))
    return x * (1.0 / (1.0 + e))


def _mix_kernel(o1_ref, l1_ref, o4_ref, l4_ref, o16_ref, l16_ref, y_ref, x_ref, mod_ref,
                ga_ref, gs_ref, gp_ref, ex_ref, wg_ref, bg_ref, wo_ref, wside_ref, out_ref,
                wside_out_ref, so_ref, sl_ref, sy_ref, *, tm):
    wside_out_ref[...] = wside_ref[...].astype(BF16)

    for gi, (dil, o_ref, l_ref) in enumerate(((4, o4_ref, l4_ref), (16, o16_ref, l16_ref))):
        for r in range(dil):
            rows = pl.ds(r, tm // dil, stride=dil)
            for jj in range(GROUP_SLABS):
                so_ref[gi * GROUP_SLABS + jj, rows, :] = o_ref[r, :, _slab(jj)].astype(F32)
            sl_ref[gi, rows, :] = l_ref[r]

    ngb, nc = y_ref.shape[0] // SLOTS, tm // SSM_CHUNK
    vs = [jnp.concatenate([y_ref[gb * SLOTS + g, :, _slab(half)].astype(F32)
                           for gb in range(ngb) for half in range(2)], axis=0)
          for g in range(SLOTS)]
    for s, o in enumerate(_slot_transpose(vs)):
        for gb in range(ngb):
            for half in range(2):
                for cb in range(nc // SUBLANES):
                    r0 = (gb * 2 + half) * nc + cb * SUBLANES
                    sy_ref[gb, pl.ds(cb * SSM_CHUNK * SUBLANES + half * SLOTS + s, SUBLANES,
                                     stride=SSM_CHUNK), :] = o[r0:r0 + SUBLANES]

    lane = lax.broadcasted_iota(jnp.int32, (tm, LANES), 1)
    even = (lane & 1) == 0
    ms, ls = [], []
    for st in (l1_ref[...], sl_ref[0], sl_ref[1]):
        sw = pltpu.roll(st, HEAD_DIM, axis=1)
        ms.append(jnp.where(even, st, sw))
        ls.append(jnp.where(even, sw, st))
    mx = jnp.maximum(jnp.maximum(ms[0], ms[1]), ms[2])
    es = [jnp.exp2(m - mx) for m in ms]
    inv = 1.0 / (es[0] * ls[0] + es[1] * ls[1] + es[2] * ls[2])
    head_lane = lane < HEADS_PER_GROUP
    wexp = [jnp.dot(jnp.where(head_lane, e * inv, 0.0).astype(BF16), ex_ref[...],
                    preferred_element_type=F32) for e in es]
    att_slabs = []
    for jj in range(GROUP_SLABS):
        a = wexp[0][:, _slab(jj)] * o1_ref[:, _slab(jj)].astype(F32)
        a = a + wexp[1][:, _slab(jj)] * so_ref[jj]
        a = a + wexp[2][:, _slab(jj)] * so_ref[GROUP_SLABS + jj]
        att_slabs.append(a)
    att = _rms(jnp.concatenate(att_slabs, axis=1)) * ga_ref[...]

    yv = jnp.concatenate([sy_ref[gb] for gb in range(sy_ref.shape[0])], axis=1)
    s = _gelu_tanh(yv)
    z = jnp.dot(s.astype(BF16), wg_ref[...], preferred_element_type=F32) + bg_ref[...]
    s = s * jax.nn.sigmoid(z)
    ssm = _rms(s) * gs_ref[...]
    aw = att.shape[1]
    mix = jnp.dot(att.astype(BF16), wo_ref[:aw, :], preferred_element_type=F32)
    mix = mix + jnp.dot(ssm.astype(BF16), wo_ref[aw:, :], preferred_element_type=F32)
    out_ref[...] = x_ref[...] + (mod_ref[2:3, :] * gp_ref[...]) * _rms(mix)


def _mix(o1, l1, o4, l4, o16, l16, yc, x2, mod, g_attn, g_ssm, g_post, w_glu_b, b_glu, w_out_b, w_side,
         bsz, seq):
    t_tok, d = x2.shape
    groups = yc.shape[0]
    sw = groups * SSM_P
    aw = GROUP_WIDTH
    tm = TOK_TILE
    per_b = seq // tm
    tok = lambda w: pl.BlockSpec((tm, w), lambda i: (i, 0))
    res = lambda dil, w: pl.BlockSpec((None, dil, tm // dil, w), lambda i: (i // per_b, 0, i % per_b, 0))
    head_of_lane = jnp.arange(aw)[None, :] // HEAD_DIM
    expand = (jnp.arange(LANES)[:, None] == head_of_lane).astype(BF16)
    side_spec, side_shape = _side_cast_spec(w_side, t_tok // tm)
    return pl.pallas_call(
        functools.partial(_mix_kernel, tm=tm),
        grid=(t_tok // tm,),
        in_specs=[tok(aw), tok(LANES), res(4, aw), res(4, LANES), res(16, aw), res(16, LANES),
                  pl.BlockSpec((groups, tm // SSM_CHUNK, CHUNK_W), lambda i: (0, i, 0)),
                  tok(d),
                  pl.BlockSpec((None, N_MOD, d), lambda i: (i // per_b, 0, 0)),
                  _resident((1, aw)), _resident((1, sw)), _resident((1, d)),
                  _resident((LANES, aw)),
                  _resident((sw, sw)), _resident((1, sw)), _resident((aw + sw, d)),
                  side_spec],
        out_specs=[tok(d), side_spec],
        out_shape=[jax.ShapeDtypeStruct((t_tok, d), F32), side_shape],
        scratch_shapes=[pltpu.VMEM((2 * GROUP_SLABS, tm, LANES), F32),
                        pltpu.VMEM((2, tm, LANES), F32),
                        pltpu.VMEM((sw // LANES, tm, LANES), F32)],
        compiler_params=_cparams(("parallel",), "mix"),
        name="mix",
    )(o1, l1, o4, l4, o16, l16, yc, x2, mod, g_attn, g_ssm, g_post, expand, w_glu_b, b_glu, w_out_b, w_side)


def _mlp_kernel(x_ref, mod_ref, gpre_ref, gpost_ref, w1_ref, w2_ref, o_ref, h_ref):
    f = pl.program_id(1)
    last = pl.num_programs(1) - 1
    chunks = [slice(r, r + ROW_CHUNK) for r in range(0, x_ref.shape[0], ROW_CHUNK)]

    def partial():
        t = jnp.dot(h_ref[...], w1_ref[...], preferred_element_type=F32)
        t = jnp.square(jnp.maximum(t, 0.0)).astype(BF16)
        return jnp.dot(t, w2_ref[...], preferred_element_type=F32)

    @pl.when(f == 0)
    def _():
        scale = gpre_ref[...] * (1.0 + mod_ref[4:5, :])
        shift = mod_ref[3:4, :]
        for rows in chunks:
            h_ref[rows, :] = (_rms(x_ref[rows, :]) * scale + shift).astype(BF16)
        o_ref[...] = partial()

    @pl.when((f > 0) & (f < last))
    def _():
        o_ref[...] += partial()

    @pl.when(f == last)
    def _():
        gate = mod_ref[5:6, :] * gpost_ref[...]
        acc = o_ref[...] + partial()
        for rows in chunks:
            o_ref[rows, :] = x_ref[rows, :] + gate * _rms(acc[rows, :])


def _mlp(x2, mod, g_pre, g_post, w1_b, w2_b, seq):
    t_tok, d = x2.shape
    d_ff = w1_b.shape[1]
    tm, tf = 512, 2048
    assert d_ff // tf >= 2
    per_b = seq // tm
    return pl.pallas_call(
        _mlp_kernel,
        grid=(t_tok // tm, d_ff // tf),
        in_specs=[pl.BlockSpec((tm, d), lambda i, f: (i, 0)),
                  pl.BlockSpec((None, N_MOD, d), lambda i, f: (i // per_b, 0, 0)),
                  _resident((1, d)), _resident((1, d)),
                  pl.BlockSpec((d, tf), lambda i, f: (0, f)),
                  pl.BlockSpec((tf, d), lambda i, f: (f, 0))],
        out_specs=pl.BlockSpec((tm, d), lambda i, f: (i, 0)),
        out_shape=jax.ShapeDtypeStruct((t_tok, d), F32),
        scratch_shapes=[pltpu.VMEM((tm, d), BF16)],
        compiler_params=_cparams(("parallel", "arbitrary"), "mlp"),
        name="mlp",
    )(x2, mod, g_pre, g_post, w1_b, w2_b)


def _rope_cos_sin(positions):
    freqs = ROPE_THETA ** (-jnp.arange(0, ROT_DIM, 2, dtype=F32) / ROT_DIM)
    ang = positions.reshape(-1).astype(F32)[None, :] * freqs[:, None]
    return jnp.concatenate([jnp.cos(ang), jnp.sin(ang)], axis=0)


def kernel(x, c, positions, w_ada, b_ada, g_pre_mix, g_post_mix, w_in, ssm_a_re, ssm_a_im, ssm_log_dt,
           ssm_b_re, ssm_b_im, ssm_c_re, ssm_c_im, ssm_d, w_glu, b_glu, g_attn_out, g_ssm_out, w_out,
           g_pre_mlp, g_post_mlp, w_mlp_in, w_mlp_out):
    bsz, seq, d = x.shape
    depth = w_ada.shape[0]
    x2 = x.reshape(bsz * seq, d)
    cos_sin = _rope_cos_sin(positions)
    row = lambda a: a.reshape(1, -1).astype(F32)
    for l in range(depth):
        mod = _adaln(c, w_ada[l], b_ada[l])
        q1, q4, q16, k1, k4, k16, v1, v4, v16, uc, w_mlp_out_b = _inproj(
            x2, mod, row(g_pre_mix[l]), w_in[l].astype(BF16), cos_sin, w_mlp_out[l], bsz, seq)

        nat = lambda a: a.reshape(bsz, 1, seq, a.shape[-1])
        o1, l1, w_out_b = _attn_group(nat(q1), nat(k1), nat(v1), 1, bsz, seq, w_out[l])
        o4, l4, w_glu_b = _attn_group(q4, k4, v4, 4, bsz, seq, w_glu[l])
        o16, l16 = _attn_group(q16, k16, v16, 16, bsz, seq)

        tables = _ssm_tables(ssm_a_re[l], ssm_a_im[l], ssm_log_dt[l], ssm_b_re[l], ssm_b_im[l],
                             ssm_c_re[l], ssm_c_im[l], ssm_d[l])
        yc = _ssm(uc, tables, bsz, seq)

        x2, w_mlp_in_b = _mix(o1.reshape(bsz * seq, -1), l1.reshape(bsz * seq, -1), o4, l4, o16, l16, yc,
                              x2, mod, row(g_attn_out[l]), row(g_ssm_out[l]), row(g_post_mix[l]),
                              w_glu_b, row(b_glu[l]), w_out_b, w_mlp_in[l],
                              bsz, seq)
        x2 = _mlp(x2, mod, row(g_pre_mlp[l]), row(g_post_mlp[l]), w_mlp_in_b, w_mlp_out_b, seq)
    return x2.reshape(bsz, seq, d)
```

```python
import functools
import math

import jax
import jax.numpy as jnp
from jax import lax
from jax.experimental import pallas as pl
from jax.experimental.pallas import tpu as pltpu

F32 = jnp.float32
BF16 = jnp.bfloat16

LANES = 128
SUBLANES = 8

HEAD_DIM = 64
ROT_DIM = HEAD_DIM // 4
ROPE_THETA = 500000.0
DILATIONS = (1, 4, 16)
HEADS_PER_GROUP = 6
GROUP_WIDTH = HEADS_PER_GROUP * HEAD_DIM
GROUP_SLABS = GROUP_WIDTH // LANES
ATT_BLK = 128
UNITS_PER_TRIP = 16
SSM_P = 16
SSM_N = 64
SSM_CHUNK = 16
CHUNK_W = SSM_CHUNK * SSM_P
SLOTS = LANES // SSM_P
N_MOD = 6
EPS = 1e-6
NEG = -1e30
TOK_TILE = 512
ROW_CHUNK = 128


MIB = 1024 * 1024
VMEM_MIB_V7X = 64
VMEM_MIB = dict(adaln=40, inproj=58, attn=48, ssm=48, mix=56, mlp=60)


def _cparams(sem, kernel_name):
    assert VMEM_MIB[kernel_name] < VMEM_MIB_V7X
    return pltpu.CompilerParams(dimension_semantics=sem, vmem_limit_bytes=VMEM_MIB[kernel_name] * MIB)


def _resident(shape):
    nd = len(shape)
    return pl.BlockSpec(shape, lambda *_: (0,) * nd, pipeline_mode=pl.Buffered(1))


def _slab(j):
    return slice(j * LANES, (j + 1) * LANES)


def _slot_transpose(vs):
    slot = lax.broadcasted_iota(jnp.int32, vs[0].shape, 1) // SSM_P
    vs = list(vs)
    for dist in (4, 2, 1):
        low = (slot & dist) == 0
        nxt = list(vs)
        for i in range(SLOTS):
            if i & dist == 0:
                a, b = vs[i], vs[i + dist]
                nxt[i] = jnp.where(low, a, pltpu.roll(b, dist * SSM_P, axis=1))
                nxt[i + dist] = jnp.where(low, pltpu.roll(a, LANES - dist * SSM_P, axis=1), b)
        vs = nxt
    return vs


def _adaln_kernel(s_ref, w_ref, b_ref, o_ref):
    o_ref[...] = jnp.dot(s_ref[...], w_ref[...].astype(BF16),
                         preferred_element_type=F32) + b_ref[...]


def _adaln(c, w_ada, b_ada):
    bsz, d = c.shape
    n = w_ada.shape[1]
    tn = 1024
    s = jnp.zeros((SUBLANES, d), BF16).at[:bsz].set(jax.nn.silu(c).astype(BF16))
    out = pl.pallas_call(
        _adaln_kernel,
        grid=(n // tn,),
        in_specs=[pl.BlockSpec((SUBLANES, d), lambda j: (0, 0)),
                  pl.BlockSpec((d, tn), lambda j: (0, j)),
                  pl.BlockSpec((1, tn), lambda j: (0, j))],
        out_specs=pl.BlockSpec((SUBLANES, tn), lambda j: (0, j)),
        out_shape=jax.ShapeDtypeStruct((SUBLANES, n), F32),
        compiler_params=_cparams(("arbitrary",), "adaln"),
        name="adaln",
    )(s, w_ada, b_ada.reshape(1, n))
    return out[:bsz].reshape(bsz, N_MOD, d)


def _inproj_kernel(x_ref, mod_ref, g_ref, w_ref, cs_ref, rx_ref, r1_ref, wside_ref,
                   q1_ref, q4_ref, q16_ref, k1_ref, k4_ref, k16_ref, v1_ref, v4_ref, v16_ref,
                   uc_ref, wside_out_ref, sc_ref, s4_ref, *, tm):
    wside_out_ref[...] = wside_ref[...].astype(BF16)

    x = x_ref[...]
    ms = jnp.mean(x * x, axis=-1, keepdims=True)
    scale = g_ref[...] * (1.0 + mod_ref[1:2, :])
    h = (x * lax.rsqrt(ms + EPS)) * scale + mod_ref[0:1, :]
    hb = h.astype(BF16)

    qk_slabs = 4 * GROUP_SLABS
    quarter = tm // 4

    def by_fours(j, t4):
        pieces = [sc_ref[j, pl.ds(r4, quarter, stride=4), :] for r4 in range(4)]
        for r4, piece in enumerate(pieces):
            s4_ref[t4, r4 * quarter:(r4 + 1) * quarter, :] = piece
        return pieces

    def rows16(t4, r16, start, size):
        r4, a = r16 % 4, r16 // 4
        return s4_ref[t4, pl.ds(r4 * quarter + 4 * start + a, size, stride=4), :]

    def scatter(j, jj, refs, t4=None):
        refs = dict(refs)
        if 1 in refs:
            refs[1][:, _slab(jj)] = sc_ref[j].astype(BF16)
        if 16 in refs:
            pieces = by_fours(j, t4)
            for r16 in range(16):
                refs[16][r16, :, _slab(jj)] = rows16(t4, r16, 0, tm // 16).astype(BF16)
        elif 4 in refs:
            pieces = [sc_ref[j, pl.ds(r4, quarter, stride=4), :] for r4 in range(4)]
        if 4 in refs:
            for r4, piece in enumerate(pieces):
                refs[4][r4, :, _slab(jj)] = piece.astype(BF16)

    pvu = jnp.dot(hb, w_ref[:, qk_slabs * LANES:], preferred_element_type=F32)
    for j in range(pvu.shape[1] // LANES):
        sc_ref[qk_slabs + j] = pvu[:, _slab(j)]
    for jj in range(GROUP_SLABS):
        scatter(qk_slabs + jj, jj, ((1, v1_ref), (4, v4_ref), (16, v16_ref)), t4=jj)

    u0 = qk_slabs + GROUP_SLABS
    ngb, nc = uc_ref.shape[0] // SLOTS, tm // SSM_CHUNK
    for gb in range(ngb):
        by_fours(u0 + gb, GROUP_SLABS + gb)
    vs = [jnp.concatenate([rows16(GROUP_SLABS + gb, half * SLOTS + s, cb * SUBLANES, SUBLANES)
                           for gb in range(ngb) for half in range(2) for cb in range(nc // SUBLANES)],
                          axis=0) for s in range(SLOTS)]
    for g, o in enumerate(_slot_transpose(vs)):
        for gb in range(ngb):
            r0 = gb * 2 * nc
            full = jnp.concatenate([o[r0:r0 + nc], o[r0 + nc:r0 + 2 * nc]], axis=1)
            uc_ref[gb * SLOTS + g] = full.astype(BF16)

    cs = cs_ref[...]
    cs_hi = cs.astype(BF16)
    cs_lo = (cs - cs_hi.astype(F32)).astype(BF16)
    rows_contract = (((0,), (0,)), ((), ()))
    tab = lax.dot_general(cs_hi, rx_ref[...], rows_contract, preferred_element_type=F32) + \
        lax.dot_general(cs_lo, rx_ref[...], rows_contract, preferred_element_type=F32)
    ct = tab[:, :LANES] + r1_ref[...]
    st = tab[:, LANES:]
    q_scale = math.log2(math.e) / math.sqrt(HEAD_DIM)
    ct_q, st_q = ct * q_scale, st * q_scale
    lane = lax.broadcasted_iota(jnp.int32, ct.shape, 1) % HEAD_DIM
    second_half = (lane >= ROT_DIM // 2) & (lane < ROT_DIM)
    q_refs = ((1, q1_ref),), ((4, q4_ref),), ((16, q16_ref),)
    chunk_slabs = 4
    for c0 in reversed(range(0, qk_slabs, chunk_slabs)):
        pc = jnp.dot(hb, w_ref[:, c0 * LANES:(c0 + chunk_slabs) * LANES], preferred_element_type=F32)
        for s in range(chunk_slabs):
            j = c0 + s
            grp, jj = divmod(j, GROUP_SLABS)
            t = pc[:, _slab(s)]
            partner = jnp.where(second_half,
                                pltpu.roll(t, ROT_DIM // 2, axis=1),
                                pltpu.roll(t, LANES - ROT_DIM // 2, axis=1))
            r = t * ct_q + partner * st_q if grp < 3 else t * ct + partner * st
            sc_ref[j] = r
            scatter(j, jj, q_refs[grp] if grp < 3 else ((1, k1_ref), (4, k4_ref), (16, k16_ref)),
                    t4=GROUP_SLABS + ngb + (grp - 2) * GROUP_SLABS + jj if grp >= 2 else None)


def _rope_spread():
    half = ROT_DIM // 2
    dim = jnp.arange(LANES)[None, :] % HEAD_DIM
    i = jnp.arange(half)[:, None]
    first, second = dim == i, dim == i + half
    ct_rows = (first | second).astype(F32)
    st_rows = second.astype(F32) - first.astype(F32)
    zero = jnp.zeros_like(ct_rows)
    spread = jnp.concatenate([jnp.concatenate([ct_rows, zero], axis=1),
                              jnp.concatenate([zero, st_rows], axis=1)], axis=0)
    return spread.astype(BF16), (dim >= ROT_DIM).astype(F32)


def _side_cast_spec(w, steps):
    rows, cols = w.shape
    assert rows % (steps * 2 * SUBLANES) == 0
    return (pl.BlockSpec((rows // steps, cols), lambda i: (i, 0)),
            jax.ShapeDtypeStruct(w.shape, BF16))


def _inproj(x2, mod, g_pre, w_in_b, cos_sin, w_side, bsz, seq):
    t_tok, d = x2.shape
    in_w = w_in_b.shape[1]
    ssm_w = in_w - 5 * GROUP_WIDTH
    groups = ssm_w // SSM_P
    tm = TOK_TILE
    per_b = seq // tm
    tok = lambda w: pl.BlockSpec((tm, w), lambda i: (i, 0))
    side_spec, side_shape = _side_cast_spec(w_side, t_tok // tm)

    def res_spec(dil):
        return pl.BlockSpec((None, dil, tm // dil, GROUP_WIDTH), lambda i: (i // per_b, 0, i % per_b, 0))

    def res_shape(dil):
        return jax.ShapeDtypeStruct((bsz, dil, seq // dil, GROUP_WIDTH), BF16)

    nat = jax.ShapeDtypeStruct((t_tok, GROUP_WIDTH), BF16)
    out_shape = [nat, res_shape(4), res_shape(16)] * 3 + \
                [jax.ShapeDtypeStruct((groups, t_tok // SSM_CHUNK, CHUNK_W), BF16), side_shape]
    out_specs = [tok(GROUP_WIDTH), res_spec(4), res_spec(16)] * 3 + \
                [pl.BlockSpec((groups, tm // SSM_CHUNK, CHUNK_W), lambda i: (0, i, 0)), side_spec]
    return pl.pallas_call(
        functools.partial(_inproj_kernel, tm=tm),
        grid=(t_tok // tm,),
        in_specs=[tok(d),
                  pl.BlockSpec((None, N_MOD, d), lambda i: (i // per_b, 0, 0)),
                  _resident((1, d)),
                  _resident((d, in_w)),
                  pl.BlockSpec((ROT_DIM, tm), lambda i: (0, i)),
                  _resident((ROT_DIM, 2 * LANES)), _resident((1, LANES)),
                  side_spec],
        out_specs=out_specs,
        out_shape=out_shape,
        scratch_shapes=[pltpu.VMEM((in_w // LANES, tm, LANES), F32),
                        pltpu.VMEM((3 * GROUP_SLABS + ssm_w // LANES, tm, LANES), F32)],
        compiler_params=_cparams(("parallel",), "inproj"),
        name="inproj",
    )(x2, mod, g_pre, w_in_b, cos_sin, *_rope_spread(), w_side)


def _stat_lanes(h):
    a, b = slice(h, h + 1), slice(HEAD_DIM + h, HEAD_DIM + h + 1)
    return (a, b) if h % 2 == 0 else (b, a)


def _attn_kernel(*refs, nblk, units, side):
    if side:
        q_ref, k_ref, v_ref, wside_ref, o_ref, s_ref, wside_out_ref, sc_ref, pr_ref, mk_ref = refs
        wside_out_ref[...] = wside_ref[...].astype(BF16)
    else:
        q_ref, k_ref, v_ref, o_ref, s_ref, sc_ref, pr_ref, mk_ref = refs

    pair = 2 * ATT_BLK
    qi = lax.broadcasted_iota(jnp.int32, (pair, 2 * ATT_BLK), 0) % ATT_BLK
    kj = lax.broadcasted_iota(jnp.int32, (pair, 2 * ATT_BLK), 1)
    rel = qi - kj
    mk_ref[0] = jnp.where((rel >= 0) & (rel <= ATT_BLK), 0.0, NEG)
    mk_ref[1] = jnp.where((rel + ATT_BLK >= 0) & (rel <= 0), 0.0, NEG)
    low = lax.broadcasted_iota(jnp.int32, (ATT_BLK, LANES), 1) < HEAD_DIM
    top, bot = slice(0, ATT_BLK), slice(ATT_BLK, pair)
    ones = jnp.ones((2 * ATT_BLK, LANES), BF16)

    def coords(u):
        u = jnp.minimum(u, units - 1)
        r, i = u // nblk, u % nblk
        rows = pl.ds(pl.multiple_of(i * ATT_BLK, ATT_BLK), ATT_BLK)
        krows = pl.ds(pl.multiple_of(jnp.maximum(i - 1, 0) * ATT_BLK, ATT_BLK), 2 * ATT_BLK)
        return r, i, rows, krows

    def scores(u, slot):
        r, i, rows, krows = coords(u)
        bias = mk_ref[jnp.minimum(i, 1)]
        for jj in range(GROUP_SLABS):
            q2 = q_ref[r, rows, _slab(jj)]
            zero = jnp.zeros_like(q2)
            qp = jnp.concatenate([jnp.where(low, q2, zero), jnp.where(low, zero, q2)], axis=0)
            s = lax.dot_general(qp, k_ref[r, krows, _slab(jj)], (((1,), (1,)), ((), ())),
                                preferred_element_type=F32)
            sc_ref[slot, jj] = s + bias

    def softmax(u, slot):
        r, _, rows, _ = coords(u)
        s_ref[r, rows, :] = jnp.zeros((ATT_BLK, LANES), F32)
        for jj in range(GROUP_SLABS):
            s = sc_ref[slot, jj]
            m = jnp.max(s, axis=1, keepdims=True)
            pr_ref[slot, jj] = jnp.exp2(s - m).astype(BF16)
            s_ref[r, rows, _stat_lanes(2 * jj)[0]] = m[top]
            s_ref[r, rows, _stat_lanes(2 * jj + 1)[0]] = m[bot]

    def values(u, slot):
        r, _, rows, krows = coords(u)
        for jj in range(GROUP_SLABS):
            v_ext = jnp.concatenate([v_ref[r, krows, _slab(jj)], ones], axis=1)
            oe = jnp.dot(pr_ref[slot, jj], v_ext, preferred_element_type=F32)
            l_even, l_odd = _stat_lanes(2 * jj)[1], _stat_lanes(2 * jj + 1)[1]
            s_ref[r, rows, l_even] = oe[top, LANES:][:, l_even]
            s_ref[r, rows, l_odd] = oe[bot, LANES:][:, l_odd]
            o_ref[r, rows, _slab(jj)] = jnp.where(low, oe[top, :LANES], oe[bot, :LANES]).astype(o_ref.dtype)

    scores(0, 0)
    softmax(0, 0)
    scores(1, 1)

    def body(t, carry):
        u = UNITS_PER_TRIP * t
        for k in range(UNITS_PER_TRIP):
            scores(u + k + 2, k % 2)
            softmax(u + k + 1, (k + 1) % 2)
            values(u + k, k % 2)
        return carry

    assert UNITS_PER_TRIP % 2 == 0 and units % UNITS_PER_TRIP == 0
    lax.fori_loop(0, units // UNITS_PER_TRIP, body, 0)


def _attn_group(q, k, v, dilation, bsz, seq, w_side=None):
    n = seq // dilation
    nblk = n // ATT_BLK
    spec = lambda w: pl.BlockSpec((None, dilation, n, w), lambda b: (b, 0, 0, 0))
    win = (2 * ATT_BLK, 2 * ATT_BLK)
    ins, in_specs = [q, k, v], [spec(GROUP_WIDTH)] * 3
    out_specs = [spec(GROUP_WIDTH), spec(LANES)]
    out_shape = [jax.ShapeDtypeStruct((bsz, dilation, n, GROUP_WIDTH), BF16),
                 jax.ShapeDtypeStruct((bsz, dilation, n, LANES), F32)]
    if w_side is not None:
        side_spec, side_shape = _side_cast_spec(w_side, bsz)
        ins, in_specs = ins + [w_side], in_specs + [side_spec]
        out_specs, out_shape = out_specs + [side_spec], out_shape + [side_shape]
    return pl.pallas_call(
        functools.partial(_attn_kernel, nblk=nblk, units=dilation * nblk, side=w_side is not None),
        grid=(bsz,),
        in_specs=in_specs,
        out_specs=out_specs,
        out_shape=out_shape,
        scratch_shapes=[pltpu.VMEM((2, GROUP_SLABS) + win, F32),
                        pltpu.VMEM((2, GROUP_SLABS) + win, BF16),
                        pltpu.VMEM((2,) + win, F32)],
        compiler_params=_cparams(("parallel",), "attn"),
        name=f"attn_d{dilation}",
    )(*ins)


def _cmul(a, b):
    return a[0] * b[0] - a[1] * b[1], a[0] * b[1] + a[1] * b[0]


def _ssm_tables(a_re, a_im, log_dt, b_re, b_im, c_re, c_im, d_skip):
    hp = lax.Precision.HIGH
    f = lambda t: t.astype(F32)
    lam = (f(a_re), f(a_im))
    dt = jnp.exp(f(log_dt))[:, None]
    g = a_re.shape[0]
    taus = jnp.arange(2 * SSM_CHUNK + 1, dtype=F32)[None, :, None]
    mag = jnp.exp(lam[0][:, None, :] * dt[:, None, :] * taus)
    ang = lam[1][:, None, :] * dt[:, None, :] * taus
    apow = (mag * jnp.cos(ang), mag * jnp.sin(ang))
    a_bar = (apow[0][:, 1], apow[1][:, 1])
    den = lam[0] * lam[0] + lam[1] * lam[1]
    coef = _cmul((a_bar[0] - 1.0, a_bar[1]), (lam[0] / den, -lam[1] / den))
    b_bar = _cmul((coef[0][..., None], coef[1][..., None]), (f(b_re), f(b_im)))
    c_mat = (f(c_re), f(c_im))

    ca = _cmul((c_mat[0][:, None], c_mat[1][:, None]),
               (apow[0][:, :SSM_CHUNK, None, :], apow[1][:, :SSM_CHUNK, None, :]))
    kern = jnp.einsum('gtpn,gnq->gtpq', ca[0], b_bar[0], precision=hp) - \
        jnp.einsum('gtpn,gnq->gtpq', ca[1], b_bar[1], precision=hp)
    kern = kern.at[:, 0].add(f(d_skip)[:, :, None] * jnp.eye(SSM_P, dtype=F32))
    m_row = kern.transpose(0, 3, 1, 2).reshape(g, SSM_P, CHUNK_W)

    rev = (apow[0][:, SSM_CHUNK - 1::-1][:, :SSM_CHUNK], apow[1][:, SSM_CHUNK - 1::-1][:, :SSM_CHUNK])
    bt = (b_bar[0].transpose(0, 2, 1)[:, None], b_bar[1].transpose(0, 2, 1)[:, None])
    w4 = _cmul((rev[0][:, :, None, :], rev[1][:, :, None, :]), bt)
    w_re, w_im = (t.reshape(g, CHUNK_W, SSM_N) for t in w4)
    w_mat = jnp.concatenate([w_re, w_im, w_im, w_re], axis=-1)

    ct = (c_mat[0].transpose(0, 2, 1)[:, :, None, :], c_mat[1].transpose(0, 2, 1)[:, :, None, :])
    at = (apow[0][:, 1:SSM_CHUNK + 1].transpose(0, 2, 1)[..., None],
          apow[1][:, 1:SSM_CHUNK + 1].transpose(0, 2, 1)[..., None])
    v4 = _cmul(ct, at)
    v_re, v_im = (t.reshape(g, SSM_N, CHUNK_W) for t in v4)
    v_mat = jnp.concatenate([v_re, -v_im], axis=1)

    def mul_rows(tau):
        re, im = apow[0][:, tau], apow[1][:, tau]
        return [jnp.concatenate([re, re], -1), jnp.concatenate([-im, im], -1),
                jnp.concatenate([im, -im], -1)]
    ptab = jnp.stack(mul_rows(SSM_CHUNK) + mul_rows(2 * SSM_CHUNK) +
                     [jnp.zeros((g, LANES), F32)] * 2, axis=1)
    return m_row, w_mat.astype(BF16), v_mat.astype(BF16), ptab


def _toeplitz(m_row):
    lane = lax.broadcasted_iota(jnp.int32, m_row.shape, 1)
    blocks = [m_row]
    for s in range(1, SSM_CHUNK):
        blocks.append(jnp.where(lane >= s * SSM_P, pltpu.roll(m_row, s * SSM_P, axis=1), 0.0))
    return jnp.concatenate(blocks, axis=0).astype(BF16)


def _ssm_kernel(u_ref, m_ref, w_ref, v_ref, p_ref, y_ref, gx_ref, gz_ref, h_ref, *, groups, nchunk, bsz):
    rows = nchunk * bsz
    row = lax.broadcasted_iota(jnp.int32, (rows, LANES), 0)
    first_chunk = (row % nchunk) == 0

    def prev_chunk(a):
        return jnp.where(first_chunk, 0.0, pltpu.roll(a, 1, axis=0))

    for g in range(groups):
        e = jnp.dot(u_ref[g], w_ref[g], preferred_element_type=F32)
        ex, ez = e[:, :LANES], e[:, LANES:]
        exs, ezs = prev_chunk(ex), prev_chunk(ez)
        p1, p2, p3 = p_ref[g, 0:1, :], p_ref[g, 1:2, :], p_ref[g, 2:3, :]
        gx = ex + p1 * exs + p2 * ezs
        gz = ez + p1 * ezs + p3 * exs
        for b in range(bsz):
            gx_ref[g, pl.ds(b, nchunk, stride=bsz), :] = gx[b * nchunk:(b + 1) * nchunk]
            gz_ref[g, pl.ds(b, nchunk, stride=bsz), :] = gz[b * nchunk:(b + 1) * nchunk]

    tile = (SUBLANES, LANES)
    mults = [[jnp.broadcast_to(p_ref[g, r:r + 1, :], tile) for r in (3, 4, 5)] for g in range(groups)]

    def step(j, carry):
        r0 = pl.multiple_of(j * SUBLANES, SUBLANES)
        new = []
        for g in range(groups):
            x, z = carry[g]
            q1, q2, q3 = mults[g]
            xn = q1 * x + q2 * z + gx_ref[g, pl.ds(r0, SUBLANES), :]
            zn = q1 * z + q3 * x + gz_ref[g, pl.ds(r0, SUBLANES), :]
            h_ref[g, pl.ds(r0, SUBLANES), :] = xn
            new.append((xn, zn))
        return tuple(new)

    zero = jnp.zeros((SUBLANES, LANES), F32)
    lax.fori_loop(0, rows // SUBLANES, step, tuple((zero, zero) for _ in range(groups)))

    for g in range(groups):
        h_end = jnp.concatenate([h_ref[g, pl.ds(b, nchunk, stride=bsz), :] for b in range(bsz)], axis=0)
        h_in = prev_chunk(h_end).astype(BF16)
        y = jnp.dot(u_ref[g], _toeplitz(m_ref[g]), preferred_element_type=F32)
        y = y + jnp.dot(h_in, v_ref[g], preferred_element_type=F32)
        y_ref[g] = y.astype(y_ref.dtype)


def _ssm(uc, tables, bsz, seq):
    m_row, w_mat, v_mat, ptab = tables
    g, rows, _ = uc.shape
    nchunk = seq // SSM_CHUNK
    assert rows == nchunk * bsz and SUBLANES % bsz == 0 and SUBLANES // bsz == 2
    gs = 4
    blk = lambda a, b: pl.BlockSpec((gs, a, b), lambda i: (i, 0, 0))
    return pl.pallas_call(
        functools.partial(_ssm_kernel, groups=gs, nchunk=nchunk, bsz=bsz),
        grid=(g // gs,),
        in_specs=[blk(rows, CHUNK_W), blk(SSM_P, CHUNK_W), blk(CHUNK_W, CHUNK_W),
                  blk(LANES, CHUNK_W), blk(SUBLANES, LANES)],
        out_specs=blk(rows, CHUNK_W),
        out_shape=jax.ShapeDtypeStruct((g, rows, CHUNK_W), BF16),
        scratch_shapes=[pltpu.VMEM((gs, rows, LANES), F32)] * 3,
        compiler_params=_cparams(("parallel",), "ssm"),
        name="ssm",
    )(uc, m_row, w_mat, v_mat, ptab)


def _rms(x):
    return x * lax.rsqrt(jnp.mean(x * x, axis=-1, keepdims=True) + EPS)


def _gelu_tanh(x):
    k = -2.0 * math.sqrt(2.0 / math.pi) * math.log2(math.e)
    e = jnp.exp2(x * (k + (k * 0.044715) * (x * x)))
    return x * (1.0 / (1.0 + e))


def _mix_kernel(o1_ref, l1_ref, o4_ref, l4_ref, o16_ref, l16_ref, y_ref, x_ref, mod_ref,
                ga_ref, gs_ref, gp_ref, ex_ref, wg_ref, bg_ref, wo_ref, wside_ref, out_ref,
                wside_out_ref, so_ref, sl_ref, sy_ref, s4_ref, *, tm):
    wside_out_ref[...] = wside_ref[...].astype(BF16)

    quarter = tm // 4

    def put16(t4, r16, start, value):
        r4, a = r16 % 4, r16 // 4
        s4_ref[t4, pl.ds(r4 * quarter + 4 * start + a, value.shape[0], stride=4), :] = value

    def from_fours(t4, dst_ref, j):
        for r4 in range(4):
            dst_ref[j, pl.ds(r4, quarter, stride=4), :] = s4_ref[t4, r4 * quarter:(r4 + 1) * quarter, :]

    for r in range(4):
        rows = pl.ds(r, quarter, stride=4)
        for jj in range(GROUP_SLABS):
            so_ref[jj, rows, :] = o4_ref[r, :, _slab(jj)].astype(F32)
        sl_ref[0, rows, :] = l4_ref[r]
    for r in range(16):
        for jj in range(GROUP_SLABS):
            put16(jj, r, 0, o16_ref[r, :, _slab(jj)].astype(F32))
        put16(GROUP_SLABS, r, 0, l16_ref[r])
    for jj in range(GROUP_SLABS):
        from_fours(jj, so_ref, GROUP_SLABS + jj)
    from_fours(GROUP_SLABS, sl_ref, 1)

    ngb, nc = y_ref.shape[0] // SLOTS, tm // SSM_CHUNK
    vs = [jnp.concatenate([y_ref[gb * SLOTS + g, :, _slab(half)].astype(F32)
                           for gb in range(ngb) for half in range(2)], axis=0)
          for g in range(SLOTS)]
    for s, o in enumerate(_slot_transpose(vs)):
        for gb in range(ngb):
            for half in range(2):
                r0 = (gb * 2 + half) * nc
                put16(GROUP_SLABS + 1 + gb, half * SLOTS + s, 0, o[r0:r0 + nc])
    for gb in range(ngb):
        from_fours(GROUP_SLABS + 1 + gb, sy_ref, gb)

    lane = lax.broadcasted_iota(jnp.int32, (tm, LANES), 1)
    even = (lane & 1) == 0
    ms, ls = [], []
    for st in (l1_ref[...], sl_ref[0], sl_ref[1]):
        sw = pltpu.roll(st, HEAD_DIM, axis=1)
        ms.append(jnp.where(even, st, sw))
        ls.append(jnp.where(even, sw, st))
    mx = jnp.maximum(jnp.maximum(ms[0], ms[1]), ms[2])
    es = [jnp.exp2(m - mx) for m in ms]
    inv = 1.0 / (es[0] * ls[0] + es[1] * ls[1] + es[2] * ls[2])
    head_lane = lane < HEADS_PER_GROUP
    wexp = [jnp.dot(jnp.where(head_lane, e * inv, 0.0).astype(BF16), ex_ref[...],
                    preferred_element_type=F32) for e in es]
    att_slabs = []
    for jj in range(GROUP_SLABS):
        a = wexp[0][:, _slab(jj)] * o1_ref[:, _slab(jj)].astype(F32)
        a = a + wexp[1][:, _slab(jj)] * so_ref[jj]
        a = a + wexp[2][:, _slab(jj)] * so_ref[GROUP_SLABS + jj]
        att_slabs.append(a)
    att = _rms(jnp.concatenate(att_slabs, axis=1)) * ga_ref[...]

    yv = jnp.concatenate([sy_ref[gb] for gb in range(sy_ref.shape[0])], axis=1)
    s = _gelu_tanh(yv)
    z = jnp.dot(s.astype(BF16), wg_ref[...], preferred_element_type=F32) + bg_ref[...]
    s = s * jax.nn.sigmoid(z)
    ssm = _rms(s) * gs_ref[...]
    aw = att.shape[1]
    mix = jnp.dot(att.astype(BF16), wo_ref[:aw, :], preferred_element_type=F32)
    mix = mix + jnp.dot(ssm.astype(BF16), wo_ref[aw:, :], preferred_element_type=F32)
    out_ref[...] = x_ref[...] + (mod_ref[2:3, :] * gp_ref[...]) * _rms(mix)


def _mix(o1, l1, o4, l4, o16, l16, yc, x2, mod, g_attn, g_ssm, g_post, w_glu_b, b_glu, w_out_b, w_side,
         bsz, seq):
    t_tok, d = x2.shape
    groups = yc.shape[0]
    sw = groups * SSM_P
    aw = GROUP_WIDTH
    tm = TOK_TILE
    per_b = seq // tm
    tok = lambda w: pl.BlockSpec((tm, w), lambda i: (i, 0))
    res = lambda dil, w: pl.BlockSpec((None, dil, tm // dil, w), lambda i: (i // per_b, 0, i % per_b, 0))
    head_of_lane = jnp.arange(aw)[None, :] // HEAD_DIM
    expand = (jnp.arange(LANES)[:, None] == head_of_lane).astype(BF16)
    side_spec, side_shape = _side_cast_spec(w_side, t_tok // tm)
    return pl.pallas_call(
        functools.partial(_mix_kernel, tm=tm),
        grid=(t_tok // tm,),
        in_specs=[tok(aw), tok(LANES), res(4, aw), res(4, LANES), res(16, aw), res(16, LANES),
                  pl.BlockSpec((groups, tm // SSM_CHUNK, CHUNK_W), lambda i: (0, i, 0)),
                  tok(d),
                  pl.BlockSpec((None, N_MOD, d), lambda i: (i // per_b, 0, 0)),
                  _resident((1, aw)), _resident((1, sw)), _resident((1, d)),
                  _resident((LANES, aw)),
                  _resident((sw, sw)), _resident((1, sw)), _resident((aw + sw, d)),
                  side_spec],
        out_specs=[tok(d), side_spec],
        out_shape=[jax.ShapeDtypeStruct((t_tok, d), F32), side_shape],
        scratch_shapes=[pltpu.VMEM((2 * GROUP_SLABS, tm, LANES), F32),
                        pltpu.VMEM((2, tm, LANES), F32),
                        pltpu.VMEM((sw // LANES, tm, LANES), F32),
                        pltpu.VMEM((GROUP_SLABS + 1 + sw // LANES, tm, LANES), F32)],
        compiler_params=_cparams(("parallel",), "mix"),
        name="mix",
    )(o1, l1, o4, l4, o16, l16, yc, x2, mod, g_attn, g_ssm, g_post, expand, w_glu_b, b_glu, w_out_b, w_side)


def _mlp_kernel(x_ref, mod_ref, gpre_ref, gpost_ref, w1_ref, w2_ref, o_ref, h_ref):
    f = pl.program_id(1)
    last = pl.num_programs(1) - 1
    chunks = [slice(r, r + ROW_CHUNK) for r in range(0, x_ref.shape[0], ROW_CHUNK)]

    def partial():
        t = jnp.dot(h_ref[...], w1_ref[...], preferred_element_type=F32)
        t = jnp.square(jnp.maximum(t, 0.0)).astype(BF16)
        return jnp.dot(t, w2_ref[...], preferred_element_type=F32)

    @pl.when(f == 0)
    def _():
        scale = gpre_ref[...] * (1.0 + mod_ref[4:5, :])
        shift = mod_ref[3:4, :]
        for rows in chunks:
            h_ref[rows, :] = (_rms(x_ref[rows, :]) * scale + shift).astype(BF16)
        o_ref[...] = partial()

    @pl.when((f > 0) & (f < last))
    def _():
        o_ref[...] += partial()

    @pl.when(f == last)
    def _():
        gate = mod_ref[5:6, :] * gpost_ref[...]
        acc = o_ref[...] + partial()
        for rows in chunks:
            o_ref[rows, :] = x_ref[rows, :] + gate * _rms(acc[rows, :])


def _mlp(x2, mod, g_pre, g_post, w1_b, w2_b, seq):
    t_tok, d = x2.shape
    d_ff = w1_b.shape[1]
    tm, tf = 512, 2048
    assert d_ff // tf >= 2
    per_b = seq // tm
    return pl.pallas_call(
        _mlp_kernel,
        grid=(t_tok // tm, d_ff // tf),
        in_specs=[pl.BlockSpec((tm, d), lambda i, f: (i, 0)),
                  pl.BlockSpec((None, N_MOD, d), lambda i, f: (i // per_b, 0, 0)),
                  _resident((1, d)), _resident((1, d)),
                  pl.BlockSpec((d, tf), lambda i, f: (0, f)),
                  pl.BlockSpec((tf, d), lambda i, f: (f, 0))],
        out_specs=pl.BlockSpec((tm, d), lambda i, f: (i, 0)),
        out_shape=jax.ShapeDtypeStruct((t_tok, d), F32),
        scratch_shapes=[pltpu.VMEM((tm, d), BF16)],
        compiler_params=_cparams(("parallel", "arbitrary"), "mlp"),
        name="mlp",
    )(x2, mod, g_pre, g_post, w1_b, w2_b)


def _rope_cos_sin(positions):
    freqs = ROPE_THETA ** (-jnp.arange(0, ROT_DIM, 2, dtype=F32) / ROT_DIM)
    ang = positions.reshape(-1).astype(F32)[None, :] * freqs[:, None]
    return jnp.concatenate([jnp.cos(ang), jnp.sin(ang)], axis=0)


def kernel(x, c, positions, w_ada, b_ada, g_pre_mix, g_post_mix, w_in, ssm_a_re, ssm_a_im, ssm_log_dt,
           ssm_b_re, ssm_b_im, ssm_c_re, ssm_c_im, ssm_d, w_glu, b_glu, g_attn_out, g_ssm_out, w_out,
           g_pre_mlp, g_post_mlp, w_mlp_in, w_mlp_out):
    bsz, seq, d = x.shape
    depth = w_ada.shape[0]
    x2 = x.reshape(bsz * seq, d)
    cos_sin = _rope_cos_sin(positions)
    row = lambda a: a.reshape(1, -1).astype(F32)
    for l in range(depth):
        mod = _adaln(c, w_ada[l], b_ada[l])
        q1, q4, q16, k1, k4, k16, v1, v4, v16, uc, w_mlp_out_b = _inproj(
            x2, mod, row(g_pre_mix[l]), w_in[l].astype(BF16), cos_sin, w_mlp_out[l], bsz, seq)

        nat = lambda a: a.reshape(bsz, 1, seq, a.shape[-1])
        o1, l1, w_out_b = _attn_group(nat(q1), nat(k1), nat(v1), 1, bsz, seq, w_out[l])
        o4, l4, w_glu_b = _attn_group(q4, k4, v4, 4, bsz, seq, w_glu[l])
        o16, l16 = _attn_group(q16, k16, v16, 16, bsz, seq)

        tables = _ssm_tables(ssm_a_re[l], ssm_a_im[l], ssm_log_dt[l], ssm_b_re[l], ssm_b_im[l],
                             ssm_c_re[l], ssm_c_im[l], ssm_d[l])
        yc = _ssm(uc, tables, bsz, seq)

        x2, w_mlp_in_b = _mix(o1.reshape(bsz * seq, -1), l1.reshape(bsz * seq, -1), o4, l4, o16, l16, yc,
                              x2, mod, row(g_attn_out[l]), row(g_ssm_out[l]), row(g_post_mix[l]),
                              w_glu_b, row(b_glu[l]), w_out_b, w_mlp_in[l],
                              bsz, seq)
        x2 = _mlp(x2, mod, row(g_pre_mlp[l]), row(g_post_mlp[l]), w_mlp_in_b, w_mlp_out_b, seq)
    return x2.reshape(bsz, seq, d)
```

```python
import functools
import math

import jax
import jax.numpy as jnp
from jax import lax
from jax.experimental import pallas as pl
from jax.experimental.pallas import tpu as pltpu

F32 = jnp.float32
BF16 = jnp.bfloat16

LANES = 128
SUBLANES = 8

HEAD_DIM = 64
ROT_DIM = HEAD_DIM // 4
ROPE_THETA = 500000.0
DILATIONS = (1, 4, 16)
HEADS_PER_GROUP = 6
GROUP_WIDTH = HEADS_PER_GROUP * HEAD_DIM
GROUP_SLABS = GROUP_WIDTH // LANES
ATT_BLK = 128
UNITS_PER_TRIP = 16
SSM_P = 16
SSM_N = 64
SSM_CHUNK = 16
CHUNK_W = SSM_CHUNK * SSM_P
SLOTS = LANES // SSM_P
N_MOD = 6
EPS = 1e-6
NEG = -1e30
TOK_TILE = 512
ROW_CHUNK = 128


MIB = 1024 * 1024
VMEM_MIB_V7X = 64
VMEM_MIB = dict(adaln=40, inproj=58, attn=48, ssm=48, mix=56, mlp=60)


def _cparams(sem, kernel_name):
    assert VMEM_MIB[kernel_name] < VMEM_MIB_V7X
    return pltpu.CompilerParams(dimension_semantics=sem, vmem_limit_bytes=VMEM_MIB[kernel_name] * MIB)


def _resident(shape):
    nd = len(shape)
    return pl.BlockSpec(shape, lambda *_: (0,) * nd, pipeline_mode=pl.Buffered(1))


def _slab(j):
    return slice(j * LANES, (j + 1) * LANES)


def _slot_transpose(vs):
    slot = lax.broadcasted_iota(jnp.int32, vs[0].shape, 1) // SSM_P
    vs = list(vs)
    for dist in (4, 2, 1):
        low = (slot & dist) == 0
        nxt = list(vs)
        for i in range(SLOTS):
            if i & dist == 0:
                a, b = vs[i], vs[i + dist]
                nxt[i] = jnp.where(low, a, pltpu.roll(b, dist * SSM_P, axis=1))
                nxt[i + dist] = jnp.where(low, pltpu.roll(a, LANES - dist * SSM_P, axis=1), b)
        vs = nxt
    return vs


def _adaln_kernel(s_ref, w_ref, b_ref, o_ref):
    o_ref[...] = jnp.dot(s_ref[...], w_ref[...].astype(BF16),
                         preferred_element_type=F32) + b_ref[...]


def _adaln(c, w_ada, b_ada):
    bsz, d = c.shape
    n = w_ada.shape[1]
    tn = 1024
    s = jnp.zeros((SUBLANES, d), BF16).at[:bsz].set(jax.nn.silu(c).astype(BF16))
    out = pl.pallas_call(
        _adaln_kernel,
        grid=(n // tn,),
        in_specs=[pl.BlockSpec((SUBLANES, d), lambda j: (0, 0)),
                  pl.BlockSpec((d, tn), lambda j: (0, j)),
                  pl.BlockSpec((1, tn), lambda j: (0, j))],
        out_specs=pl.BlockSpec((SUBLANES, tn), lambda j: (0, j)),
        out_shape=jax.ShapeDtypeStruct((SUBLANES, n), F32),
        compiler_params=_cparams(("arbitrary",), "adaln"),
        name="adaln",
    )(s, w_ada, b_ada.reshape(1, n))
    return out[:bsz].reshape(bsz, N_MOD, d)


def _inproj_kernel(x_ref, mod_ref, g_ref, w_ref, cs_ref, rx_ref, r1_ref, wside_ref,
                   q1_ref, q4_ref, q16_ref, k1_ref, k4_ref, k16_ref, v1_ref, v4_ref, v16_ref,
                   uc_ref, wside_out_ref, sc_ref, s4_ref, *, tm):
    wside_out_ref[...] = wside_ref[...].astype(BF16)

    x = x_ref[...]
    ms = jnp.mean(x * x, axis=-1, keepdims=True)
    scale = g_ref[...] * (1.0 + mod_ref[1:2, :])
    h = (x * lax.rsqrt(ms + EPS)) * scale + mod_ref[0:1, :]
    hb = h.astype(BF16)

    qk_slabs = 4 * GROUP_SLABS
    quarter = tm // 4

    def by_fours(j, t4):
        pieces = [sc_ref[j, pl.ds(r4, quarter, stride=4), :] for r4 in range(4)]
        for r4, piece in enumerate(pieces):
            s4_ref[t4, r4 * quarter:(r4 + 1) * quarter, :] = piece
        return pieces

    def rows16(t4, r16, start, size):
        r4, a = r16 % 4, r16 // 4
        return s4_ref[t4, pl.ds(r4 * quarter + 4 * start + a, size, stride=4), :]

    def scatter(j, jj, refs, val, t4=None):
        refs = dict(refs)
        if 1 in refs:
            refs[1][:, _slab(jj)] = val.astype(BF16)
        if 16 in refs:
            pieces = by_fours(j, t4)
            for r16 in range(16):
                refs[16][r16, :, _slab(jj)] = rows16(t4, r16, 0, tm // 16).astype(BF16)
        elif 4 in refs:
            pieces = [sc_ref[j, pl.ds(r4, quarter, stride=4), :] for r4 in range(4)]
        if 4 in refs:
            for r4, piece in enumerate(pieces):
                refs[4][r4, :, _slab(jj)] = piece.astype(BF16)

    pvu = jnp.dot(hb, w_ref[:, qk_slabs * LANES:], preferred_element_type=F32)
    for j in range(pvu.shape[1] // LANES):
        sc_ref[qk_slabs + j] = pvu[:, _slab(j)]
    for jj in range(GROUP_SLABS):
        scatter(qk_slabs + jj, jj, ((1, v1_ref), (4, v4_ref), (16, v16_ref)), pvu[:, _slab(jj)], t4=jj)

    u0 = qk_slabs + GROUP_SLABS
    ngb, nc = uc_ref.shape[0] // SLOTS, tm // SSM_CHUNK
    for gb in range(ngb):
        by_fours(u0 + gb, GROUP_SLABS + gb)
    vs = [jnp.concatenate([rows16(GROUP_SLABS + gb, half * SLOTS + s, cb * SUBLANES, SUBLANES)
                           for gb in range(ngb) for half in range(2) for cb in range(nc // SUBLANES)],
                          axis=0) for s in range(SLOTS)]
    for g, o in enumerate(_slot_transpose(vs)):
        for gb in range(ngb):
            r0 = gb * 2 * nc
            full = jnp.concatenate([o[r0:r0 + nc], o[r0 + nc:r0 + 2 * nc]], axis=1)
            uc_ref[gb * SLOTS + g] = full.astype(BF16)

    cs = cs_ref[...]
    cs_hi = cs.astype(BF16)
    cs_lo = (cs - cs_hi.astype(F32)).astype(BF16)
    rows_contract = (((0,), (0,)), ((), ()))
    tab = lax.dot_general(cs_hi, rx_ref[...], rows_contract, preferred_element_type=F32) + \
        lax.dot_general(cs_lo, rx_ref[...], rows_contract, preferred_element_type=F32)
    ct = tab[:, :LANES] + r1_ref[...]
    st = tab[:, LANES:]
    q_scale = math.log2(math.e) / math.sqrt(HEAD_DIM)
    ct_q, st_q = ct * q_scale, st * q_scale
    lane = lax.broadcasted_iota(jnp.int32, ct.shape, 1) % HEAD_DIM
    second_half = (lane >= ROT_DIM // 2) & (lane < ROT_DIM)
    q_refs = ((1, q1_ref),), ((4, q4_ref),), ((16, q16_ref),)
    chunk_slabs = 4
    for c0 in reversed(range(0, qk_slabs, chunk_slabs)):
        pc = jnp.dot(hb, w_ref[:, c0 * LANES:(c0 + chunk_slabs) * LANES], preferred_element_type=F32)
        for s in range(chunk_slabs):
            j = c0 + s
            grp, jj = divmod(j, GROUP_SLABS)
            t = pc[:, _slab(s)]
            partner = jnp.where(second_half,
                                pltpu.roll(t, ROT_DIM // 2, axis=1),
                                pltpu.roll(t, LANES - ROT_DIM // 2, axis=1))
            r = t * ct_q + partner * st_q if grp < 3 else t * ct + partner * st
            if grp > 0:
                sc_ref[j] = r
            scatter(j, jj, q_refs[grp] if grp < 3 else ((1, k1_ref), (4, k4_ref), (16, k16_ref)), r,
                    t4=GROUP_SLABS + ngb + (grp - 2) * GROUP_SLABS + jj if grp >= 2 else None)


def _rope_spread():
    half = ROT_DIM // 2
    dim = jnp.arange(LANES)[None, :] % HEAD_DIM
    i = jnp.arange(half)[:, None]
    first, second = dim == i, dim == i + half
    ct_rows = (first | second).astype(F32)
    st_rows = second.astype(F32) - first.astype(F32)
    zero = jnp.zeros_like(ct_rows)
    spread = jnp.concatenate([jnp.concatenate([ct_rows, zero], axis=1),
                              jnp.concatenate([zero, st_rows], axis=1)], axis=0)
    return spread.astype(BF16), (dim >= ROT_DIM).astype(F32)


def _side_cast_spec(w, steps):
    rows, cols = w.shape
    assert rows % (steps * 2 * SUBLANES) == 0
    return (pl.BlockSpec((rows // steps, cols), lambda i: (i, 0)),
            jax.ShapeDtypeStruct(w.shape, BF16))


def _inproj(x2, mod, g_pre, w_in_b, cos_sin, w_side, bsz, seq):
    t_tok, d = x2.shape
    in_w = w_in_b.shape[1]
    ssm_w = in_w - 5 * GROUP_WIDTH
    groups = ssm_w // SSM_P
    tm = TOK_TILE
    per_b = seq // tm
    tok = lambda w: pl.BlockSpec((tm, w), lambda i: (i, 0))
    side_spec, side_shape = _side_cast_spec(w_side, t_tok // tm)

    def res_spec(dil):
        return pl.BlockSpec((None, dil, tm // dil, GROUP_WIDTH), lambda i: (i // per_b, 0, i % per_b, 0))

    def res_shape(dil):
        return jax.ShapeDtypeStruct((bsz, dil, seq // dil, GROUP_WIDTH), BF16)

    nat = jax.ShapeDtypeStruct((t_tok, GROUP_WIDTH), BF16)
    out_shape = [nat, res_shape(4), res_shape(16)] * 3 + \
                [jax.ShapeDtypeStruct((groups, t_tok // SSM_CHUNK, CHUNK_W), BF16), side_shape]
    out_specs = [tok(GROUP_WIDTH), res_spec(4), res_spec(16)] * 3 + \
                [pl.BlockSpec((groups, tm // SSM_CHUNK, CHUNK_W), lambda i: (0, i, 0)), side_spec]
    return pl.pallas_call(
        functools.partial(_inproj_kernel, tm=tm),
        grid=(t_tok // tm,),
        in_specs=[tok(d),
                  pl.BlockSpec((None, N_MOD, d), lambda i: (i // per_b, 0, 0)),
                  _resident((1, d)),
                  _resident((d, in_w)),
                  pl.BlockSpec((ROT_DIM, tm), lambda i: (0, i)),
                  _resident((ROT_DIM, 2 * LANES)), _resident((1, LANES)),
                  side_spec],
        out_specs=out_specs,
        out_shape=out_shape,
        scratch_shapes=[pltpu.VMEM((in_w // LANES, tm, LANES), F32),
                        pltpu.VMEM((3 * GROUP_SLABS + ssm_w // LANES, tm, LANES), F32)],
        compiler_params=_cparams(("parallel",), "inproj"),
        name="inproj",
    )(x2, mod, g_pre, w_in_b, cos_sin, *_rope_spread(), w_side)


def _stat_lanes(h):
    a, b = slice(h, h + 1), slice(HEAD_DIM + h, HEAD_DIM + h + 1)
    return (a, b) if h % 2 == 0 else (b, a)


def _attn_kernel(*refs, nblk, units, side):
    if side:
        q_ref, k_ref, v_ref, wside_ref, o_ref, s_ref, wside_out_ref, sc_ref, pr_ref, mk_ref = refs
        wside_out_ref[...] = wside_ref[...].astype(BF16)
    else:
        q_ref, k_ref, v_ref, o_ref, s_ref, sc_ref, pr_ref, mk_ref = refs

    pair = 2 * ATT_BLK
    qi = lax.broadcasted_iota(jnp.int32, (pair, 2 * ATT_BLK), 0) % ATT_BLK
    kj = lax.broadcasted_iota(jnp.int32, (pair, 2 * ATT_BLK), 1)
    rel = qi - kj
    mk_ref[0] = jnp.where((rel >= 0) & (rel <= ATT_BLK), 0.0, NEG)
    mk_ref[1] = jnp.where((rel + ATT_BLK >= 0) & (rel <= 0), 0.0, NEG)
    low = lax.broadcasted_iota(jnp.int32, (ATT_BLK, LANES), 1) < HEAD_DIM
    top, bot = slice(0, ATT_BLK), slice(ATT_BLK, pair)
    ones = jnp.ones((2 * ATT_BLK, LANES), BF16)

    def coords(u):
        u = jnp.minimum(u, units - 1)
        r, i = u // nblk, u % nblk
        rows = pl.ds(pl.multiple_of(i * ATT_BLK, ATT_BLK), ATT_BLK)
        krows = pl.ds(pl.multiple_of(jnp.maximum(i - 1, 0) * ATT_BLK, ATT_BLK), 2 * ATT_BLK)
        return r, i, rows, krows

    def scores(u, slot):
        r, i, rows, krows = coords(u)
        bias = mk_ref[jnp.minimum(i, 1)]
        for jj in range(GROUP_SLABS):
            q2 = q_ref[r, rows, _slab(jj)]
            zero = jnp.zeros_like(q2)
            qp = jnp.concatenate([jnp.where(low, q2, zero), jnp.where(low, zero, q2)], axis=0)
            s = lax.dot_general(qp, k_ref[r, krows, _slab(jj)], (((1,), (1,)), ((), ())),
                                preferred_element_type=F32)
            sc_ref[slot, jj] = s + bias

    def softmax(u, slot):
        r, _, rows, _ = coords(u)
        s_ref[r, rows, :] = jnp.zeros((ATT_BLK, LANES), F32)
        for jj in range(GROUP_SLABS):
            s = sc_ref[slot, jj]
            m = jnp.max(s, axis=1, keepdims=True)
            pr_ref[slot, jj] = jnp.exp2(s - m).astype(BF16)
            s_ref[r, rows, _stat_lanes(2 * jj)[0]] = m[top]
            s_ref[r, rows, _stat_lanes(2 * jj + 1)[0]] = m[bot]

    def values(u, slot):
        r, _, rows, krows = coords(u)
        for jj in range(GROUP_SLABS):
            v_ext = jnp.concatenate([v_ref[r, krows, _slab(jj)], ones], axis=1)
            oe = jnp.dot(pr_ref[slot, jj], v_ext, preferred_element_type=F32)
            l_even, l_odd = _stat_lanes(2 * jj)[1], _stat_lanes(2 * jj + 1)[1]
            s_ref[r, rows, l_even] = oe[top, LANES:][:, l_even]
            s_ref[r, rows, l_odd] = oe[bot, LANES:][:, l_odd]
            o_ref[r, rows, _slab(jj)] = jnp.where(low, oe[top, :LANES], oe[bot, :LANES]).astype(o_ref.dtype)

    scores(0, 0)
    softmax(0, 0)
    scores(1, 1)

    def body(t, carry):
        u = UNITS_PER_TRIP * t
        for k in range(UNITS_PER_TRIP):
            scores(u + k + 2, k % 2)
            softmax(u + k + 1, (k + 1) % 2)
            values(u + k, k % 2)
        return carry

    assert UNITS_PER_TRIP % 2 == 0 and units % UNITS_PER_TRIP == 0
    lax.fori_loop(0, units // UNITS_PER_TRIP, body, 0)


def _attn_group(q, k, v, dilation, bsz, seq, w_side=None):
    n = seq // dilation
    nblk = n // ATT_BLK
    spec = lambda w: pl.BlockSpec((None, dilation, n, w), lambda b: (b, 0, 0, 0))
    win = (2 * ATT_BLK, 2 * ATT_BLK)
    ins, in_specs = [q, k, v], [spec(GROUP_WIDTH)] * 3
    out_specs = [spec(GROUP_WIDTH), spec(LANES)]
    out_shape = [jax.ShapeDtypeStruct((bsz, dilation, n, GROUP_WIDTH), BF16),
                 jax.ShapeDtypeStruct((bsz, dilation, n, LANES), F32)]
    if w_side is not None:
        side_spec, side_shape = _side_cast_spec(w_side, bsz)
        ins, in_specs = ins + [w_side], in_specs + [side_spec]
        out_specs, out_shape = out_specs + [side_spec], out_shape + [side_shape]
    return pl.pallas_call(
        functools.partial(_attn_kernel, nblk=nblk, units=dilation * nblk, side=w_side is not None),
        grid=(bsz,),
        in_specs=in_specs,
        out_specs=out_specs,
        out_shape=out_shape,
        scratch_shapes=[pltpu.VMEM((2, GROUP_SLABS) + win, F32),
                        pltpu.VMEM((2, GROUP_SLABS) + win, BF16),
                        pltpu.VMEM((2,) + win, F32)],
        compiler_params=_cparams(("parallel",), "attn"),
        name=f"attn_d{dilation}",
    )(*ins)


def _cmul(a, b):
    return a[0] * b[0] - a[1] * b[1], a[0] * b[1] + a[1] * b[0]


def _ssm_tables(a_re, a_im, log_dt, b_re, b_im, c_re, c_im, d_skip):
    hp = lax.Precision.HIGH
    f = lambda t: t.astype(F32)
    lam = (f(a_re), f(a_im))
    dt = jnp.exp(f(log_dt))[:, None]
    g = a_re.shape[0]
    taus = jnp.arange(2 * SSM_CHUNK + 1, dtype=F32)[None, :, None]
    mag = jnp.exp(lam[0][:, None, :] * dt[:, None, :] * taus)
    ang = lam[1][:, None, :] * dt[:, None, :] * taus
    apow = (mag * jnp.cos(ang), mag * jnp.sin(ang))
    a_bar = (apow[0][:, 1], apow[1][:, 1])
    den = lam[0] * lam[0] + lam[1] * lam[1]
    coef = _cmul((a_bar[0] - 1.0, a_bar[1]), (lam[0] / den, -lam[1] / den))
    b_bar = _cmul((coef[0][..., None], coef[1][..., None]), (f(b_re), f(b_im)))
    c_mat = (f(c_re), f(c_im))

    ca = _cmul((c_mat[0][:, None], c_mat[1][:, None]),
               (apow[0][:, :SSM_CHUNK, None, :], apow[1][:, :SSM_CHUNK, None, :]))
    kern = jnp.einsum('gtpn,gnq->gtpq', ca[0], b_bar[0], precision=hp) - \
        jnp.einsum('gtpn,gnq->gtpq', ca[1], b_bar[1], precision=hp)
    kern = kern.at[:, 0].add(f(d_skip)[:, :, None] * jnp.eye(SSM_P, dtype=F32))
    m_row = kern.transpose(0, 3, 1, 2).reshape(g, SSM_P, CHUNK_W)

    rev = (apow[0][:, SSM_CHUNK - 1::-1][:, :SSM_CHUNK], apow[1][:, SSM_CHUNK - 1::-1][:, :SSM_CHUNK])
    bt = (b_bar[0].transpose(0, 2, 1)[:, None], b_bar[1].transpose(0, 2, 1)[:, None])
    w4 = _cmul((rev[0][:, :, None, :], rev[1][:, :, None, :]), bt)
    w_re, w_im = (t.reshape(g, CHUNK_W, SSM_N) for t in w4)
    w_mat = jnp.concatenate([w_re, w_im], axis=-1)

    ct = (c_mat[0].transpose(0, 2, 1)[:, :, None, :], c_mat[1].transpose(0, 2, 1)[:, :, None, :])
    at = (apow[0][:, 1:SSM_CHUNK + 1].transpose(0, 2, 1)[..., None],
          apow[1][:, 1:SSM_CHUNK + 1].transpose(0, 2, 1)[..., None])
    v4 = _cmul(ct, at)
    v_re, v_im = (t.reshape(g, SSM_N, CHUNK_W) for t in v4)
    v_mat = jnp.concatenate([v_re, -v_im], axis=1)

    def mul_rows(tau):
        re, im = apow[0][:, tau], apow[1][:, tau]
        return [jnp.concatenate([re, re], -1), jnp.concatenate([-im, im], -1),
                jnp.concatenate([im, -im], -1)]
    ptab = jnp.stack(mul_rows(SSM_CHUNK) + mul_rows(2 * SSM_CHUNK) +
                     [jnp.zeros((g, LANES), F32)] * 2, axis=1)
    return m_row, w_mat.astype(BF16), v_mat.astype(BF16), ptab


def _toeplitz(m_row):
    lane = lax.broadcasted_iota(jnp.int32, m_row.shape, 1)
    blocks = [m_row]
    for s in range(1, SSM_CHUNK):
        blocks.append(jnp.where(lane >= s * SSM_P, pltpu.roll(m_row, s * SSM_P, axis=1), 0.0))
    return jnp.concatenate(blocks, axis=0).astype(BF16)


def _ssm_kernel(u_ref, m_ref, w_ref, v_ref, p_ref, y_ref, gx_ref, gz_ref, h_ref, *, groups, nchunk, bsz):
    rows = nchunk * bsz
    row = lax.broadcasted_iota(jnp.int32, (rows, LANES), 0)
    first_chunk = (row % nchunk) == 0

    def prev_chunk(a):
        return jnp.where(first_chunk, 0.0, pltpu.roll(a, 1, axis=0))

    for g in range(groups):
        ex = jnp.dot(u_ref[g], w_ref[g], preferred_element_type=F32)
        ez = pltpu.roll(ex, SSM_N, axis=1)
        exs, ezs = prev_chunk(ex), prev_chunk(ez)
        p1, p2, p3 = p_ref[g, 0:1, :], p_ref[g, 1:2, :], p_ref[g, 2:3, :]
        gx = ex + p1 * exs + p2 * ezs
        gz = ez + p1 * ezs + p3 * exs
        for b in range(bsz):
            gx_ref[g, pl.ds(b, nchunk, stride=bsz), :] = gx[b * nchunk:(b + 1) * nchunk]
            gz_ref[g, pl.ds(b, nchunk, stride=bsz), :] = gz[b * nchunk:(b + 1) * nchunk]

    y_intra = [jnp.dot(u_ref[g], _toeplitz(m_ref[g]), preferred_element_type=F32) for g in range(groups)]

    tile = (SUBLANES, LANES)
    mults = [[jnp.broadcast_to(p_ref[g, r:r + 1, :], tile) for r in (3, 4, 5)] for g in range(groups)]
    zero = jnp.zeros(tile, F32)
    state = [(zero, zero) for _ in range(groups)]
    for j in range(rows // SUBLANES):
        r0 = j * SUBLANES
        for g in range(groups):
            x, z = state[g]
            q1, q2, q3 = mults[g]
            xn = q1 * x + q2 * z + gx_ref[g, r0:r0 + SUBLANES, :]
            zn = q1 * z + q3 * x + gz_ref[g, r0:r0 + SUBLANES, :]
            h_ref[g, r0:r0 + SUBLANES, :] = xn
            state[g] = (xn, zn)

    for g in range(groups):
        h_end = jnp.concatenate([h_ref[g, pl.ds(b, nchunk, stride=bsz), :] for b in range(bsz)], axis=0)
        h_in = prev_chunk(h_end).astype(BF16)
        y = y_intra[g] + jnp.dot(h_in, v_ref[g], preferred_element_type=F32)
        y_ref[g] = y.astype(y_ref.dtype)


def _ssm(uc, tables, bsz, seq):
    m_row, w_mat, v_mat, ptab = tables
    g, rows, _ = uc.shape
    nchunk = seq // SSM_CHUNK
    assert rows == nchunk * bsz and SUBLANES % bsz == 0 and SUBLANES // bsz == 2
    gs = 4
    blk = lambda a, b: pl.BlockSpec((gs, a, b), lambda i: (i, 0, 0))
    return pl.pallas_call(
        functools.partial(_ssm_kernel, groups=gs, nchunk=nchunk, bsz=bsz),
        grid=(g // gs,),
        in_specs=[blk(rows, CHUNK_W), blk(SSM_P, CHUNK_W), blk(CHUNK_W, LANES),
                  blk(LANES, CHUNK_W), blk(SUBLANES, LANES)],
        out_specs=blk(rows, CHUNK_W),
        out_shape=jax.ShapeDtypeStruct((g, rows, CHUNK_W), BF16),
        scratch_shapes=[pltpu.VMEM((gs, rows, LANES), F32)] * 3,
        compiler_params=_cparams(("parallel",), "ssm"),
        name="ssm",
    )(uc, m_row, w_mat, v_mat, ptab)


def _rms(x):
    return x * lax.rsqrt(jnp.mean(x * x, axis=-1, keepdims=True) + EPS)


def _gelu_tanh(x):
    k = -2.0 * math.sqrt(2.0 / math.pi) * math.log2(math.e)
    e = jnp.exp2(x * (k + (k * 0.044715) * (x * x)))
    return x * (1.0 / (1.0 + e))


def _mix_kernel(o1_ref, l1_ref, o4_ref, l4_ref, o16_ref, l16_ref, y_ref, x_ref, mod_ref,
                ga_ref, gs_ref, gp_ref, ex_ref, wg_ref, bg_ref, wo_ref, wside_ref, out_ref,
                wside_out_ref, so_ref, sl_ref, sy_ref, s4_ref, *, tm):
    wside_out_ref[...] = wside_ref[...].astype(BF16)

    quarter = tm // 4

    def put16(t4, r16, start, value):
        r4, a = r16 % 4, r16 // 4
        s4_ref[t4, pl.ds(r4 * quarter + 4 * start + a, value.shape[0], stride=4), :] = value

    def from_fours(t4, dst_ref, j):
        for r4 in range(4):
            dst_ref[j, pl.ds(r4, quarter, stride=4), :] = s4_ref[t4, r4 * quarter:(r4 + 1) * quarter, :]

    for r in range(4):
        rows = pl.ds(r, quarter, stride=4)
        for jj in range(GROUP_SLABS):
            so_ref[jj, rows, :] = o4_ref[r, :, _slab(jj)].astype(F32)
        sl_ref[0, rows, :] = l4_ref[r]
    for r in range(16):
        for jj in range(GROUP_SLABS):
            put16(jj, r, 0, o16_ref[r, :, _slab(jj)].astype(F32))
        put16(GROUP_SLABS, r, 0, l16_ref[r])
    for jj in range(GROUP_SLABS):
        from_fours(jj, so_ref, GROUP_SLABS + jj)
    from_fours(GROUP_SLABS, sl_ref, 1)

    ngb, nc = y_ref.shape[0] // SLOTS, tm // SSM_CHUNK
    vs = [jnp.concatenate([y_ref[gb * SLOTS + g, :, _slab(half)].astype(F32)
                           for gb in range(ngb) for half in range(2)], axis=0)
          for g in range(SLOTS)]
    for s, o in enumerate(_slot_transpose(vs)):
        for gb in range(ngb):
            for half in range(2):
                r0 = (gb * 2 + half) * nc
                put16(GROUP_SLABS + 1 + gb, half * SLOTS + s, 0, o[r0:r0 + nc])
    for gb in range(ngb):
        from_fours(GROUP_SLABS + 1 + gb, sy_ref, gb)

    lane = lax.broadcasted_iota(jnp.int32, (tm, LANES), 1)
    even = (lane & 1) == 0
    ms, ls = [], []
    for st in (l1_ref[...], sl_ref[0], sl_ref[1]):
        sw = pltpu.roll(st, HEAD_DIM, axis=1)
        ms.append(jnp.where(even, st, sw))
        ls.append(jnp.where(even, sw, st))
    mx = jnp.maximum(jnp.maximum(ms[0], ms[1]), ms[2])
    es = [jnp.exp2(m - mx) for m in ms]
    inv = 1.0 / (es[0] * ls[0] + es[1] * ls[1] + es[2] * ls[2])
    head_lane = lane < HEADS_PER_GROUP
    wexp = [jnp.dot(jnp.where(head_lane, e * inv, 0.0).astype(BF16), ex_ref[...],
                    preferred_element_type=F32) for e in es]
    att_slabs = []
    for jj in range(GROUP_SLABS):
        a = wexp[0][:, _slab(jj)] * o1_ref[:, _slab(jj)].astype(F32)
        a = a + wexp[1][:, _slab(jj)] * so_ref[jj]
        a = a + wexp[2][:, _slab(jj)] * so_ref[GROUP_SLABS + jj]
        att_slabs.append(a)
    att = _rms(jnp.concatenate(att_slabs, axis=1)) * ga_ref[...]

    yv = jnp.concatenate([sy_ref[gb] for gb in range(sy_ref.shape[0])], axis=1)
    s = _gelu_tanh(yv)
    z = jnp.dot(s.astype(BF16), wg_ref[...], preferred_element_type=F32) + bg_ref[...]
    s = s * jax.nn.sigmoid(z)
    ssm = _rms(s) * gs_ref[...]
    aw = att.shape[1]
    mix = jnp.dot(att.astype(BF16), wo_ref[:aw, :], preferred_element_type=F32)
    mix = mix + jnp.dot(ssm.astype(BF16), wo_ref[aw:, :], preferred_element_type=F32)
    out_ref[...] = x_ref[...] + (mod_ref[2:3, :] * gp_ref[...]) * _rms(mix)


def _mix(o1, l1, o4, l4, o16, l16, yc, x2, mod, g_attn, g_ssm, g_post, w_glu_b, b_glu, w_out_b, w_side,
         bsz, seq):
    t_tok, d = x2.shape
    groups = yc.shape[0]
    sw = groups * SSM_P
    aw = GROUP_WIDTH
    tm = TOK_TILE
    per_b = seq // tm
    tok = lambda w: pl.BlockSpec((tm, w), lambda i: (i, 0))
    res = lambda dil, w: pl.BlockSpec((None, dil, tm // dil, w), lambda i: (i // per_b, 0, i % per_b, 0))
    head_of_lane = jnp.arange(aw)[None, :] // HEAD_DIM
    expand = (jnp.arange(LANES)[:, None] == head_of_lane).astype(BF16)
    side_spec, side_shape = _side_cast_spec(w_side, t_tok // tm)
    return pl.pallas_call(
        functools.partial(_mix_kernel, tm=tm),
        grid=(t_tok // tm,),
        in_specs=[tok(aw), tok(LANES), res(4, aw), res(4, LANES), res(16, aw), res(16, LANES),
                  pl.BlockSpec((groups, tm // SSM_CHUNK, CHUNK_W), lambda i: (0, i, 0)),
                  tok(d),
                  pl.BlockSpec((None, N_MOD, d), lambda i: (i // per_b, 0, 0)),
                  _resident((1, aw)), _resident((1, sw)), _resident((1, d)),
                  _resident((LANES, aw)),
                  _resident((sw, sw)), _resident((1, sw)), _resident((aw + sw, d)),
                  side_spec],
        out_specs=[tok(d), side_spec],
        out_shape=[jax.ShapeDtypeStruct((t_tok, d), F32), side_shape],
        scratch_shapes=[pltpu.VMEM((2 * GROUP_SLABS, tm, LANES), F32),
                        pltpu.VMEM((2, tm, LANES), F32),
                        pltpu.VMEM((sw // LANES, tm, LANES), F32),
                        pltpu.VMEM((GROUP_SLABS + 1 + sw // LANES, tm, LANES), F32)],
        compiler_params=_cparams(("parallel",), "mix"),
        name="mix",
    )(o1, l1, o4, l4, o16, l16, yc, x2, mod, g_attn, g_ssm, g_post, expand, w_glu_b, b_glu, w_out_b, w_side)


def _mlp_kernel(x_ref, mod_ref, gpre_ref, gpost_ref, w1_ref, w2_ref, o_ref, h_ref):
    f = pl.program_id(1)
    last = pl.num_programs(1) - 1
    chunks = [slice(r, r + ROW_CHUNK) for r in range(0, x_ref.shape[0], ROW_CHUNK)]

    def partial():
        t = jnp.dot(h_ref[...], w1_ref[...], preferred_element_type=F32)
        t = jnp.square(jnp.maximum(t, 0.0)).astype(BF16)
        return jnp.dot(t, w2_ref[...], preferred_element_type=F32)

    @pl.when(f == 0)
    def _():
        scale = gpre_ref[...] * (1.0 + mod_ref[4:5, :])
        shift = mod_ref[3:4, :]
        for rows in chunks:
            h_ref[rows, :] = (_rms(x_ref[rows, :]) * scale + shift).astype(BF16)
        o_ref[...] = partial()

    @pl.when((f > 0) & (f < last))
    def _():
        o_ref[...] += partial()

    @pl.when(f == last)
    def _():
        gate = mod_ref[5:6, :] * gpost_ref[...]
        acc = o_ref[...] + partial()
        for rows in chunks:
            o_ref[rows, :] = x_ref[rows, :] + gate * _rms(acc[rows, :])


def _mlp(x2, mod, g_pre, g_post, w1_b, w2_b, seq):
    t_tok, d = x2.shape
    d_ff = w1_b.shape[1]
    tm, tf = 512, 2048
    assert d_ff // tf >= 2
    per_b = seq // tm
    return pl.pallas_call(
        _mlp_kernel,
        grid=(t_tok // tm, d_ff // tf),
        in_specs=[pl.BlockSpec((tm, d), lambda i, f: (i, 0)),
                  pl.BlockSpec((None, N_MOD, d), lambda i, f: (i // per_b, 0, 0)),
                  _resident((1, d)), _resident((1, d)),
                  pl.BlockSpec((d, tf), lambda i, f: (0, f)),
                  pl.BlockSpec((tf, d), lambda i, f: (f, 0))],
        out_specs=pl.BlockSpec((tm, d), lambda i, f: (i, 0)),
        out_shape=jax.ShapeDtypeStruct((t_tok, d), F32),
        scratch_shapes=[pltpu.VMEM((tm, d), BF16)],
        compiler_params=_cparams(("parallel", "arbitrary"), "mlp"),
        name="mlp",
    )(x2, mod, g_pre, g_post, w1_b, w2_b)


def _rope_cos_sin(positions):
    freqs = ROPE_THETA ** (-jnp.arange(0, ROT_DIM, 2, dtype=F32) / ROT_DIM)
    ang = positions.reshape(-1).astype(F32)[None, :] * freqs[:, None]
    return jnp.concatenate([jnp.cos(ang), jnp.sin(ang)], axis=0)


def kernel(x, c, positions, w_ada, b_ada, g_pre_mix, g_post_mix, w_in, ssm_a_re, ssm_a_im, ssm_log_dt,
           ssm_b_re, ssm_b_im, ssm_c_re, ssm_c_im, ssm_d, w_glu, b_glu, g_attn_out, g_ssm_out, w_out,
           g_pre_mlp, g_post_mlp, w_mlp_in, w_mlp_out):
    bsz, seq, d = x.shape
    depth = w_ada.shape[0]
    x2 = x.reshape(bsz * seq, d)
    cos_sin = _rope_cos_sin(positions)
    row = lambda a: a.reshape(1, -1).astype(F32)
    for l in range(depth):
        mod = _adaln(c, w_ada[l], b_ada[l])
        q1, q4, q16, k1, k4, k16, v1, v4, v16, uc, w_mlp_out_b = _inproj(
            x2, mod, row(g_pre_mix[l]), w_in[l].astype(BF16), cos_sin, w_mlp_out[l], bsz, seq)

        nat = lambda a: a.reshape(bsz, 1, seq, a.shape[-1])
        o1, l1, w_out_b = _attn_group(nat(q1), nat(k1), nat(v1), 1, bsz, seq, w_out[l])
        o4, l4, w_glu_b = _attn_group(q4, k4, v4, 4, bsz, seq, w_glu[l])
        o16, l16 = _attn_group(q16, k16, v16, 16, bsz, seq)

        tables = _ssm_tables(ssm_a_re[l], ssm_a_im[l], ssm_log_dt[l], ssm_b_re[l], ssm_b_im[l],
                             ssm_c_re[l], ssm_c_im[l], ssm_d[l])
        yc = _ssm(uc, tables, bsz, seq)

        x2, w_mlp_in_b = _mix(o1.reshape(bsz * seq, -1), l1.reshape(bsz * seq, -1), o4, l4, o16, l16, yc,
                              x2, mod, row(g_attn_out[l]), row(g_ssm_out[l]), row(g_post_mix[l]),
                              w_glu_b, row(b_glu[l]), w_out_b, w_mlp_in[l],
                              bsz, seq)
        x2 = _mlp(x2, mod, row(g_pre_mlp[l]), row(g_post_mlp[l]), w_mlp_in_b, w_mlp_out_b, seq)
    return x2.reshape(bsz, seq, d)
```

```python
import functools
import math

import jax
import jax.numpy as jnp
from jax import lax
from jax.experimental import pallas as pl
from jax.experimental.pallas import tpu as pltpu

F32 = jnp.float32
BF16 = jnp.bfloat16

LANES = 128
SUBLANES = 8

HEAD_DIM = 64
ROT_DIM = HEAD_DIM // 4
ROPE_THETA = 500000.0
DILATIONS = (1, 4, 16)
HEADS_PER_GROUP = 6
GROUP_WIDTH = HEADS_PER_GROUP * HEAD_DIM
GROUP_SLABS = GROUP_WIDTH // LANES
ATT_BLK = 128
UNITS_PER_TRIP = 16
SSM_P = 16
SSM_N = 64
SSM_CHUNK = 16
CHUNK_W = SSM_CHUNK * SSM_P
SLOTS = LANES // SSM_P
N_MOD = 6
EPS = 1e-6
NEG = -1e30
TOK_TILE = 512
ROW_CHUNK = 128


MIB = 1024 * 1024
VMEM_MIB_V7X = 64
VMEM_MIB = dict(adaln=40, inproj=58, attn=48, ssm=48, mix=56, mlp=60)


def _cparams(sem, kernel_name):
    assert VMEM_MIB[kernel_name] < VMEM_MIB_V7X
    return pltpu.CompilerParams(dimension_semantics=sem, vmem_limit_bytes=VMEM_MIB[kernel_name] * MIB)


def _resident(shape):
    nd = len(shape)
    return pl.BlockSpec(shape, lambda *_: (0,) * nd, pipeline_mode=pl.Buffered(1))


def _slab(j):
    return slice(j * LANES, (j + 1) * LANES)


def _slot_transpose(vs):
    slot = lax.broadcasted_iota(jnp.int32, vs[0].shape, 1) // SSM_P
    vs = list(vs)
    for dist in (4, 2, 1):
        low = (slot & dist) == 0
        nxt = list(vs)
        for i in range(SLOTS):
            if i & dist == 0:
                a, b = vs[i], vs[i + dist]
                nxt[i] = jnp.where(low, a, pltpu.roll(b, dist * SSM_P, axis=1))
                nxt[i + dist] = jnp.where(low, pltpu.roll(a, LANES - dist * SSM_P, axis=1), b)
        vs = nxt
    return vs


def _adaln_kernel(s_ref, w_ref, b_ref, o_ref):
    o_ref[...] = jnp.dot(s_ref[...], w_ref[...].astype(BF16),
                         preferred_element_type=F32) + b_ref[...]


N_MOD_EARLY = 2


def _adaln_early(s, w_ada, b_ada, bsz):
    d = s.shape[1]
    n = N_MOD_EARLY * d
    tn = 1024
    out = pl.pallas_call(
        _adaln_kernel,
        grid=(n // tn,),
        in_specs=[pl.BlockSpec((SUBLANES, d), lambda j: (0, 0)),
                  pl.BlockSpec((d, tn), lambda j: (0, j)),
                  pl.BlockSpec((1, tn), lambda j: (0, j))],
        out_specs=pl.BlockSpec((SUBLANES, tn), lambda j: (0, j)),
        out_shape=jax.ShapeDtypeStruct((SUBLANES, n), F32),
        compiler_params=_cparams(("arbitrary",), "adaln"),
        name="adaln",
    )(s, w_ada, b_ada)
    return out[:bsz].reshape(bsz, N_MOD_EARLY, d)


def _inproj_kernel(x_ref, mod_ref, g_ref, w_ref, cs_ref, rx_ref, r1_ref, wside_ref,
                   sblk_ref, wa_lo_ref, wa_hi_ref, ba_ref,
                   q1_ref, q4_ref, q16_ref, k1_ref, k4_ref, k16_ref, v1_ref, v4_ref, v16_ref,
                   uc_ref, wside_out_ref, modr_ref, sc_ref, s4_ref, *, tm):
    wside_out_ref[...] = wside_ref[...].astype(BF16)

    @pl.when(pl.program_id(0) == 0)
    def _():
        modr_ref[...] = jnp.broadcast_to(ba_ref[...], modr_ref.shape)
    sblk = sblk_ref[...]
    half_w = wa_lo_ref.shape[1]
    modr_ref[:, :half_w] += jnp.dot(sblk, wa_lo_ref[...].astype(BF16), preferred_element_type=F32)
    modr_ref[:, half_w:] += jnp.dot(sblk, wa_hi_ref[...].astype(BF16), preferred_element_type=F32)

    x = x_ref[...]
    ms = jnp.mean(x * x, axis=-1, keepdims=True)
    scale = g_ref[...] * (1.0 + mod_ref[1:2, :])
    h = (x * lax.rsqrt(ms + EPS)) * scale + mod_ref[0:1, :]
    hb = h.astype(BF16)

    qk_slabs = 4 * GROUP_SLABS
    quarter = tm // 4

    def by_fours(j, t4):
        pieces = [sc_ref[j, pl.ds(r4, quarter, stride=4), :] for r4 in range(4)]
        for r4, piece in enumerate(pieces):
            s4_ref[t4, r4 * quarter:(r4 + 1) * quarter, :] = piece
        return pieces

    def rows16(t4, r16, start, size):
        r4, a = r16 % 4, r16 // 4
        return s4_ref[t4, pl.ds(r4 * quarter + 4 * start + a, size, stride=4), :]

    def scatter(j, jj, refs, val, t4=None):
        refs = dict(refs)
        if 1 in refs:
            refs[1][:, _slab(jj)] = val.astype(BF16)
        if 16 in refs:
            pieces = by_fours(j, t4)
            for r16 in range(16):
                refs[16][r16, :, _slab(jj)] = rows16(t4, r16, 0, tm // 16).astype(BF16)
        elif 4 in refs:
            pieces = [sc_ref[j, pl.ds(r4, quarter, stride=4), :] for r4 in range(4)]
        if 4 in refs:
            for r4, piece in enumerate(pieces):
                refs[4][r4, :, _slab(jj)] = piece.astype(BF16)

    pvu = jnp.dot(hb, w_ref[:, qk_slabs * LANES:], preferred_element_type=F32)
    for j in range(pvu.shape[1] // LANES):
        sc_ref[qk_slabs + j] = pvu[:, _slab(j)]
    for jj in range(GROUP_SLABS):
        scatter(qk_slabs + jj, jj, ((1, v1_ref), (4, v4_ref), (16, v16_ref)), pvu[:, _slab(jj)], t4=jj)

    u0 = qk_slabs + GROUP_SLABS
    ngb, nc = uc_ref.shape[0] // SLOTS, tm // SSM_CHUNK
    for gb in range(ngb):
        by_fours(u0 + gb, GROUP_SLABS + gb)
    vs = [jnp.concatenate([rows16(GROUP_SLABS + gb, half * SLOTS + s, cb * SUBLANES, SUBLANES)
                           for gb in range(ngb) for half in range(2) for cb in range(nc // SUBLANES)],
                          axis=0) for s in range(SLOTS)]
    for g, o in enumerate(_slot_transpose(vs)):
        for gb in range(ngb):
            r0 = gb * 2 * nc
            full = jnp.concatenate([o[r0:r0 + nc], o[r0 + nc:r0 + 2 * nc]], axis=1)
            uc_ref[gb * SLOTS + g] = full.astype(BF16)

    cs = cs_ref[...]
    cs_hi = cs.astype(BF16)
    cs_lo = (cs - cs_hi.astype(F32)).astype(BF16)
    rows_contract = (((0,), (0,)), ((), ()))
    tab = lax.dot_general(cs_hi, rx_ref[...], rows_contract, preferred_element_type=F32) + \
        lax.dot_general(cs_lo, rx_ref[...], rows_contract, preferred_element_type=F32)
    ct = tab[:, :LANES] + r1_ref[...]
    st = tab[:, LANES:]
    q_scale = math.log2(math.e) / math.sqrt(HEAD_DIM)
    ct_q, st_q = ct * q_scale, st * q_scale
    lane = lax.broadcasted_iota(jnp.int32, ct.shape, 1) % HEAD_DIM
    second_half = (lane >= ROT_DIM // 2) & (lane < ROT_DIM)
    q_refs = ((1, q1_ref),), ((4, q4_ref),), ((16, q16_ref),)
    chunk_slabs = 4
    for c0 in reversed(range(0, qk_slabs, chunk_slabs)):
        pc = jnp.dot(hb, w_ref[:, c0 * LANES:(c0 + chunk_slabs) * LANES], preferred_element_type=F32)
        for s in range(chunk_slabs):
            j = c0 + s
            grp, jj = divmod(j, GROUP_SLABS)
            t = pc[:, _slab(s)]
            partner = jnp.where(second_half,
                                pltpu.roll(t, ROT_DIM // 2, axis=1),
                                pltpu.roll(t, LANES - ROT_DIM // 2, axis=1))
            r = t * ct_q + partner * st_q if grp < 3 else t * ct + partner * st
            if grp > 0:
                sc_ref[j] = r
            scatter(j, jj, q_refs[grp] if grp < 3 else ((1, k1_ref), (4, k4_ref), (16, k16_ref)), r,
                    t4=GROUP_SLABS + ngb + (grp - 2) * GROUP_SLABS + jj if grp >= 2 else None)


def _rope_spread():
    half = ROT_DIM // 2
    dim = jnp.arange(LANES)[None, :] % HEAD_DIM
    i = jnp.arange(half)[:, None]
    first, second = dim == i, dim == i + half
    ct_rows = (first | second).astype(F32)
    st_rows = second.astype(F32) - first.astype(F32)
    zero = jnp.zeros_like(ct_rows)
    spread = jnp.concatenate([jnp.concatenate([ct_rows, zero], axis=1),
                              jnp.concatenate([zero, st_rows], axis=1)], axis=0)
    return spread.astype(BF16), (dim >= ROT_DIM).astype(F32)


def _side_cast_spec(w, steps):
    rows, cols = w.shape
    assert rows % (steps * 2 * SUBLANES) == 0
    return (pl.BlockSpec((rows // steps, cols), lambda i: (i, 0)),
            jax.ShapeDtypeStruct(w.shape, BF16))


def _inproj(x2, mod, g_pre, w_in_b, cos_sin, w_side, s, w_ada, b_ada, bsz, seq):
    t_tok, d = x2.shape
    in_w = w_in_b.shape[1]
    ssm_w = in_w - 5 * GROUP_WIDTH
    groups = ssm_w // SSM_P
    tm = TOK_TILE
    per_b = seq // tm
    steps = t_tok // tm
    tok = lambda w: pl.BlockSpec((tm, w), lambda i: (i, 0))
    side_spec, side_shape = _side_cast_spec(w_side, steps)
    kblk = d // steps
    late_w = (N_MOD - N_MOD_EARLY) * d
    assert d % steps == 0 and N_MOD_EARLY * d == late_w // 2
    s_blocks = s.reshape(SUBLANES, steps, kblk).transpose(1, 0, 2)
    wa_spec = lambda half: pl.BlockSpec((kblk, late_w // 2), lambda i: (i, 1 + half))

    def res_spec(dil):
        return pl.BlockSpec((None, dil, tm // dil, GROUP_WIDTH), lambda i: (i // per_b, 0, i % per_b, 0))

    def res_shape(dil):
        return jax.ShapeDtypeStruct((bsz, dil, seq // dil, GROUP_WIDTH), BF16)

    nat = jax.ShapeDtypeStruct((t_tok, GROUP_WIDTH), BF16)
    out_shape = [nat, res_shape(4), res_shape(16)] * 3 + \
                [jax.ShapeDtypeStruct((groups, t_tok // SSM_CHUNK, CHUNK_W), BF16), side_shape,
                 jax.ShapeDtypeStruct((SUBLANES, late_w), F32)]
    out_specs = [tok(GROUP_WIDTH), res_spec(4), res_spec(16)] * 3 + \
                [pl.BlockSpec((groups, tm // SSM_CHUNK, CHUNK_W), lambda i: (0, i, 0)), side_spec,
                 pl.BlockSpec((SUBLANES, late_w), lambda i: (0, 0))]
    outs = pl.pallas_call(
        functools.partial(_inproj_kernel, tm=tm),
        grid=(steps,),
        in_specs=[tok(d),
                  pl.BlockSpec((None, N_MOD_EARLY, d), lambda i: (i // per_b, 0, 0)),
                  _resident((1, d)),
                  _resident((d, in_w)),
                  pl.BlockSpec((ROT_DIM, tm), lambda i: (0, i)),
                  _resident((ROT_DIM, 2 * LANES)), _resident((1, LANES)),
                  side_spec,
                  pl.BlockSpec((None, SUBLANES, kblk), lambda i: (i, 0, 0)),
                  wa_spec(0), wa_spec(1), _resident((1, late_w))],
        out_specs=out_specs,
        out_shape=out_shape,
        scratch_shapes=[pltpu.VMEM((in_w // LANES, tm, LANES), F32),
                        pltpu.VMEM((3 * GROUP_SLABS + ssm_w // LANES, tm, LANES), F32)],
        compiler_params=_cparams(("arbitrary",), "inproj"),
        name="inproj",
    )(x2, mod, g_pre, w_in_b, cos_sin, *_rope_spread(), w_side, s_blocks, w_ada, w_ada,
      b_ada[:, N_MOD_EARLY * d:])
    mod_late = outs[-1][:bsz].reshape(bsz, N_MOD - N_MOD_EARLY, d)
    return outs[:-1], mod_late


def _stat_lanes(h):
    a, b = slice(h, h + 1), slice(HEAD_DIM + h, HEAD_DIM + h + 1)
    return (a, b) if h % 2 == 0 else (b, a)


def _attn_kernel(*refs, nblk, units, side):
    if side:
        q_ref, k_ref, v_ref, wside_ref, o_ref, s_ref, wside_out_ref, sc_ref, pr_ref, mk_ref = refs
        wside_out_ref[...] = wside_ref[...].astype(BF16)
    else:
        q_ref, k_ref, v_ref, o_ref, s_ref, sc_ref, pr_ref, mk_ref = refs

    pair = 2 * ATT_BLK
    qi = lax.broadcasted_iota(jnp.int32, (pair, 2 * ATT_BLK), 0) % ATT_BLK
    kj = lax.broadcasted_iota(jnp.int32, (pair, 2 * ATT_BLK), 1)
    rel = qi - kj
    mk_ref[0] = jnp.where((rel >= 0) & (rel <= ATT_BLK), 0.0, NEG)
    mk_ref[1] = jnp.where((rel + ATT_BLK >= 0) & (rel <= 0), 0.0, NEG)
    low = lax.broadcasted_iota(jnp.int32, (ATT_BLK, LANES), 1) < HEAD_DIM
    top, bot = slice(0, ATT_BLK), slice(ATT_BLK, pair)
    ones = jnp.ones((2 * ATT_BLK, LANES), BF16)

    def coords(u):
        u = jnp.minimum(u, units - 1)
        r, i = u // nblk, u % nblk
        rows = pl.ds(pl.multiple_of(i * ATT_BLK, ATT_BLK), ATT_BLK)
        krows = pl.ds(pl.multiple_of(jnp.maximum(i - 1, 0) * ATT_BLK, ATT_BLK), 2 * ATT_BLK)
        return r, i, rows, krows

    def scores(u, slot):
        r, i, rows, krows = coords(u)
        bias = mk_ref[jnp.minimum(i, 1)]
        for jj in range(GROUP_SLABS):
            q2 = q_ref[r, rows, _slab(jj)]
            zero = jnp.zeros_like(q2)
            qp = jnp.concatenate([jnp.where(low, q2, zero), jnp.where(low, zero, q2)], axis=0)
            s = lax.dot_general(qp, k_ref[r, krows, _slab(jj)], (((1,), (1,)), ((), ())),
                                preferred_element_type=F32)
            sc_ref[slot, jj] = s + bias

    def softmax(u, slot):
        r, _, rows, _ = coords(u)
        s_ref[r, rows, :] = jnp.zeros((ATT_BLK, LANES), F32)
        for jj in range(GROUP_SLABS):
            s = sc_ref[slot, jj]
            m = jnp.max(s, axis=1, keepdims=True)
            pr_ref[slot, jj] = jnp.exp2(s - m).astype(BF16)
            s_ref[r, rows, _stat_lanes(2 * jj)[0]] = m[top]
            s_ref[r, rows, _stat_lanes(2 * jj + 1)[0]] = m[bot]

    def values(u, slot):
        r, _, rows, krows = coords(u)
        for jj in range(GROUP_SLABS):
            v_ext = jnp.concatenate([v_ref[r, krows, _slab(jj)], ones], axis=1)
            oe = jnp.dot(pr_ref[slot, jj], v_ext, preferred_element_type=F32)
            l_even, l_odd = _stat_lanes(2 * jj)[1], _stat_lanes(2 * jj + 1)[1]
            s_ref[r, rows, l_even] = oe[top, LANES:][:, l_even]
            s_ref[r, rows, l_odd] = oe[bot, LANES:][:, l_odd]
            o_ref[r, rows, _slab(jj)] = jnp.where(low, oe[top, :LANES], oe[bot, :LANES]).astype(o_ref.dtype)

    scores(0, 0)
    softmax(0, 0)
    scores(1, 1)

    def body(t, carry):
        u = UNITS_PER_TRIP * t
        for k in range(UNITS_PER_TRIP):
            scores(u + k + 2, k % 2)
            softmax(u + k + 1, (k + 1) % 2)
            values(u + k, k % 2)
        return carry

    assert UNITS_PER_TRIP % 2 == 0 and units % UNITS_PER_TRIP == 0
    lax.fori_loop(0, units // UNITS_PER_TRIP, body, 0)


def _attn_group(q, k, v, dilation, bsz, seq, w_side=None):
    n = seq // dilation
    nblk = n // ATT_BLK
    spec = lambda w: pl.BlockSpec((None, dilation, n, w), lambda b: (b, 0, 0, 0))
    win = (2 * ATT_BLK, 2 * ATT_BLK)
    ins, in_specs = [q, k, v], [spec(GROUP_WIDTH)] * 3
    out_specs = [spec(GROUP_WIDTH), spec(LANES)]
    out_shape = [jax.ShapeDtypeStruct((bsz, dilation, n, GROUP_WIDTH), BF16),
                 jax.ShapeDtypeStruct((bsz, dilation, n, LANES), F32)]
    if w_side is not None:
        side_spec, side_shape = _side_cast_spec(w_side, bsz)
        ins, in_specs = ins + [w_side], in_specs + [side_spec]
        out_specs, out_shape = out_specs + [side_spec], out_shape + [side_shape]
    return pl.pallas_call(
        functools.partial(_attn_kernel, nblk=nblk, units=dilation * nblk, side=w_side is not None),
        grid=(bsz,),
        in_specs=in_specs,
        out_specs=out_specs,
        out_shape=out_shape,
        scratch_shapes=[pltpu.VMEM((2, GROUP_SLABS) + win, F32),
                        pltpu.VMEM((2, GROUP_SLABS) + win, BF16),
                        pltpu.VMEM((2,) + win, F32)],
        compiler_params=_cparams(("parallel",), "attn"),
        name=f"attn_d{dilation}",
    )(*ins)


def _cmul(a, b):
    return a[0] * b[0] - a[1] * b[1], a[0] * b[1] + a[1] * b[0]


def _ssm_tables(a_re, a_im, log_dt, b_re, b_im, c_re, c_im, d_skip):
    hp = lax.Precision.HIGH
    f = lambda t: t.astype(F32)
    lam = (f(a_re), f(a_im))
    dt = jnp.exp(f(log_dt))[:, None]
    g = a_re.shape[0]
    taus = jnp.arange(2 * SSM_CHUNK + 1, dtype=F32)[None, :, None]
    mag = jnp.exp(lam[0][:, None, :] * dt[:, None, :] * taus)
    ang = lam[1][:, None, :] * dt[:, None, :] * taus
    apow = (mag * jnp.cos(ang), mag * jnp.sin(ang))
    a_bar = (apow[0][:, 1], apow[1][:, 1])
    den = lam[0] * lam[0] + lam[1] * lam[1]
    coef = _cmul((a_bar[0] - 1.0, a_bar[1]), (lam[0] / den, -lam[1] / den))
    b_bar = _cmul((coef[0][..., None], coef[1][..., None]), (f(b_re), f(b_im)))
    c_mat = (f(c_re), f(c_im))

    ca = _cmul((c_mat[0][:, None], c_mat[1][:, None]),
               (apow[0][:, :SSM_CHUNK, None, :], apow[1][:, :SSM_CHUNK, None, :]))
    kern = jnp.einsum('gtpn,gnq->gtpq', ca[0], b_bar[0], precision=hp) - \
        jnp.einsum('gtpn,gnq->gtpq', ca[1], b_bar[1], precision=hp)
    kern = kern.at[:, 0].add(f(d_skip)[:, :, None] * jnp.eye(SSM_P, dtype=F32))
    m_row = kern.transpose(0, 3, 1, 2).reshape(g, SSM_P, CHUNK_W)

    rev = (apow[0][:, SSM_CHUNK - 1::-1][:, :SSM_CHUNK], apow[1][:, SSM_CHUNK - 1::-1][:, :SSM_CHUNK])
    bt = (b_bar[0].transpose(0, 2, 1)[:, None], b_bar[1].transpose(0, 2, 1)[:, None])
    w4 = _cmul((rev[0][:, :, None, :], rev[1][:, :, None, :]), bt)
    w_re, w_im = (t.reshape(g, CHUNK_W, SSM_N) for t in w4)
    w_mat = jnp.concatenate([w_re, w_im], axis=-1)

    ct = (c_mat[0].transpose(0, 2, 1)[:, :, None, :], c_mat[1].transpose(0, 2, 1)[:, :, None, :])
    at = (apow[0][:, 1:SSM_CHUNK + 1].transpose(0, 2, 1)[..., None],
          apow[1][:, 1:SSM_CHUNK + 1].transpose(0, 2, 1)[..., None])
    v4 = _cmul(ct, at)
    v_re, v_im = (t.reshape(g, SSM_N, CHUNK_W) for t in v4)
    v_mat = jnp.concatenate([v_re, -v_im], axis=1)

    def mul_rows(tau):
        re, im = apow[0][:, tau], apow[1][:, tau]
        return [jnp.concatenate([re, re], -1), jnp.concatenate([-im, im], -1),
                jnp.concatenate([im, -im], -1)]
    ptab = jnp.stack(mul_rows(SSM_CHUNK) + mul_rows(2 * SSM_CHUNK) +
                     [jnp.zeros((g, LANES), F32)] * 2, axis=1)
    return m_row, w_mat.astype(BF16), v_mat.astype(BF16), ptab


def _toeplitz(m_row):
    lane = lax.broadcasted_iota(jnp.int32, m_row.shape, 1)
    blocks = [m_row]
    for s in range(1, SSM_CHUNK):
        blocks.append(jnp.where(lane >= s * SSM_P, pltpu.roll(m_row, s * SSM_P, axis=1), 0.0))
    return jnp.concatenate(blocks, axis=0).astype(BF16)


def _ssm_kernel(u_ref, m_ref, w_ref, v_ref, p_ref, y_ref, gx_ref, gz_ref, h_ref, *, groups, nchunk, bsz):
    rows = nchunk * bsz
    row = lax.broadcasted_iota(jnp.int32, (rows, LANES), 0)
    first_chunk = (row % nchunk) == 0

    def prev_chunk(a):
        return jnp.where(first_chunk, 0.0, pltpu.roll(a, 1, axis=0))

    for g in range(groups):
        ex = jnp.dot(u_ref[g], w_ref[g], preferred_element_type=F32)
        ez = pltpu.roll(ex, SSM_N, axis=1)
        exs, ezs = prev_chunk(ex), prev_chunk(ez)
        p1, p2, p3 = p_ref[g, 0:1, :], p_ref[g, 1:2, :], p_ref[g, 2:3, :]
        gx = ex + p1 * exs + p2 * ezs
        gz = ez + p1 * ezs + p3 * exs
        for b in range(bsz):
            gx_ref[g, pl.ds(b, nchunk, stride=bsz), :] = gx[b * nchunk:(b + 1) * nchunk]
            gz_ref[g, pl.ds(b, nchunk, stride=bsz), :] = gz[b * nchunk:(b + 1) * nchunk]

    y_intra = [jnp.dot(u_ref[g], _toeplitz(m_ref[g]), preferred_element_type=F32) for g in range(groups)]

    tile = (SUBLANES, LANES)
    mults = [[jnp.broadcast_to(p_ref[g, r:r + 1, :], tile) for r in (3, 4, 5)] for g in range(groups)]
    zero = jnp.zeros(tile, F32)
    state = [(zero, zero) for _ in range(groups)]
    for j in range(rows // SUBLANES):
        r0 = j * SUBLANES
        for g in range(groups):
            x, z = state[g]
            q1, q2, q3 = mults[g]
            xn = q1 * x + q2 * z + gx_ref[g, r0:r0 + SUBLANES, :]
            zn = q1 * z + q3 * x + gz_ref[g, r0:r0 + SUBLANES, :]
            h_ref[g, r0:r0 + SUBLANES, :] = xn
            state[g] = (xn, zn)

    for g in range(groups):
        h_end = jnp.concatenate([h_ref[g, pl.ds(b, nchunk, stride=bsz), :] for b in range(bsz)], axis=0)
        h_in = prev_chunk(h_end).astype(BF16)
        y = y_intra[g] + jnp.dot(h_in, v_ref[g], preferred_element_type=F32)
        y_ref[g] = y.astype(y_ref.dtype)


def _ssm(uc, tables, bsz, seq):
    m_row, w_mat, v_mat, ptab = tables
    g, rows, _ = uc.shape
    nchunk = seq // SSM_CHUNK
    assert rows == nchunk * bsz and SUBLANES % bsz == 0 and SUBLANES // bsz == 2
    gs = 4
    blk = lambda a, b: pl.BlockSpec((gs, a, b), lambda i: (i, 0, 0))
    return pl.pallas_call(
        functools.partial(_ssm_kernel, groups=gs, nchunk=nchunk, bsz=bsz),
        grid=(g // gs,),
        in_specs=[blk(rows, CHUNK_W), blk(SSM_P, CHUNK_W), blk(CHUNK_W, LANES),
                  blk(LANES, CHUNK_W), blk(SUBLANES, LANES)],
        out_specs=blk(rows, CHUNK_W),
        out_shape=jax.ShapeDtypeStruct((g, rows, CHUNK_W), BF16),
        scratch_shapes=[pltpu.VMEM((gs, rows, LANES), F32)] * 3,
        compiler_params=_cparams(("parallel",), "ssm"),
        name="ssm",
    )(uc, m_row, w_mat, v_mat, ptab)


def _rms(x):
    return x * lax.rsqrt(jnp.mean(x * x, axis=-1, keepdims=True) + EPS)


def _gelu_tanh(x):
    k = -2.0 * math.sqrt(2.0 / math.pi) * math.log2(math.e)
    e = jnp.exp2(x * (k + (k * 0.044715) * (x * x)))
    return x * (1.0 / (1.0 + e))


def _mix_kernel(o1_ref, l1_ref, o4_ref, l4_ref, o16_ref, l16_ref, y_ref, x_ref, mod_ref,
                ga_ref, gs_ref, gp_ref, ex_ref, wg_ref, bg_ref, wo_ref, wside_ref, out_ref,
                wside_out_ref, so_ref, sl_ref, sy_ref, s4_ref, *, tm):
    wside_out_ref[...] = wside_ref[...].astype(BF16)

    quarter = tm // 4

    def put16(t4, r16, start, value):
        r4, a = r16 % 4, r16 // 4
        s4_ref[t4, pl.ds(r4 * quarter + 4 * start + a, value.shape[0], stride=4), :] = value

    def from_fours(t4, dst_ref, j):
        for r4 in range(4):
            dst_ref[j, pl.ds(r4, quarter, stride=4), :] = s4_ref[t4, r4 * quarter:(r4 + 1) * quarter, :]

    for r in range(4):
        rows = pl.ds(r, quarter, stride=4)
        for jj in range(GROUP_SLABS):
            so_ref[jj, rows, :] = o4_ref[r, :, _slab(jj)].astype(F32)
        sl_ref[0, rows, :] = l4_ref[r]
    for r in range(16):
        for jj in range(GROUP_SLABS):
            put16(jj, r, 0, o16_ref[r, :, _slab(jj)].astype(F32))
        put16(GROUP_SLABS, r, 0, l16_ref[r])
    for jj in range(GROUP_SLABS):
        from_fours(jj, so_ref, GROUP_SLABS + jj)
    from_fours(GROUP_SLABS, sl_ref, 1)

    ngb, nc = y_ref.shape[0] // SLOTS, tm // SSM_CHUNK
    vs = [jnp.concatenate([y_ref[gb * SLOTS + g, :, _slab(half)].astype(F32)
                           for gb in range(ngb) for half in range(2)], axis=0)
          for g in range(SLOTS)]
    for s, o in enumerate(_slot_transpose(vs)):
        for gb in range(ngb):
            for half in range(2):
                r0 = (gb * 2 + half) * nc
                put16(GROUP_SLABS + 1 + gb, half * SLOTS + s, 0, o[r0:r0 + nc])
    for gb in range(ngb):
        from_fours(GROUP_SLABS + 1 + gb, sy_ref, gb)

    lane = lax.broadcasted_iota(jnp.int32, (tm, LANES), 1)
    even = (lane & 1) == 0
    ms, ls = [], []
    for st in (l1_ref[...], sl_ref[0], sl_ref[1]):
        sw = pltpu.roll(st, HEAD_DIM, axis=1)
        ms.append(jnp.where(even, st, sw))
        ls.append(jnp.where(even, sw, st))
    mx = jnp.maximum(jnp.maximum(ms[0], ms[1]), ms[2])
    es = [jnp.exp2(m - mx) for m in ms]
    inv = 1.0 / (es[0] * ls[0] + es[1] * ls[1] + es[2] * ls[2])
    head_lane = lane < HEADS_PER_GROUP
    wexp = [jnp.dot(jnp.where(head_lane, e * inv, 0.0).astype(BF16), ex_ref[...],
                    preferred_element_type=F32) for e in es]
    att_slabs = []
    for jj in range(GROUP_SLABS):
        a = wexp[0][:, _slab(jj)] * o1_ref[:, _slab(jj)].astype(F32)
        a = a + wexp[1][:, _slab(jj)] * so_ref[jj]
        a = a + wexp[2][:, _slab(jj)] * so_ref[GROUP_SLABS + jj]
        att_slabs.append(a)
    att = _rms(jnp.concatenate(att_slabs, axis=1)) * ga_ref[...]

    yv = jnp.concatenate([sy_ref[gb] for gb in range(sy_ref.shape[0])], axis=1)
    s = _gelu_tanh(yv)
    z = jnp.dot(s.astype(BF16), wg_ref[...], preferred_element_type=F32) + bg_ref[...]
    s = s * jax.nn.sigmoid(z)
    ssm = _rms(s) * gs_ref[...]
    aw = att.shape[1]
    mix = jnp.dot(att.astype(BF16), wo_ref[:aw, :], preferred_element_type=F32)
    mix = mix + jnp.dot(ssm.astype(BF16), wo_ref[aw:, :], preferred_element_type=F32)
    out_ref[...] = x_ref[...] + (mod_ref[0:1, :] * gp_ref[...]) * _rms(mix)


def _mix(o1, l1, o4, l4, o16, l16, yc, x2, mod, g_attn, g_ssm, g_post, w_glu_b, b_glu, w_out_b, w_side,
         bsz, seq):
    t_tok, d = x2.shape
    groups = yc.shape[0]
    sw = groups * SSM_P
    aw = GROUP_WIDTH
    tm = TOK_TILE
    per_b = seq // tm
    tok = lambda w: pl.BlockSpec((tm, w), lambda i: (i, 0))
    res = lambda dil, w: pl.BlockSpec((None, dil, tm // dil, w), lambda i: (i // per_b, 0, i % per_b, 0))
    head_of_lane = jnp.arange(aw)[None, :] // HEAD_DIM
    expand = (jnp.arange(LANES)[:, None] == head_of_lane).astype(BF16)
    side_spec, side_shape = _side_cast_spec(w_side, t_tok // tm)
    return pl.pallas_call(
        functools.partial(_mix_kernel, tm=tm),
        grid=(t_tok // tm,),
        in_specs=[tok(aw), tok(LANES), res(4, aw), res(4, LANES), res(16, aw), res(16, LANES),
                  pl.BlockSpec((groups, tm // SSM_CHUNK, CHUNK_W), lambda i: (0, i, 0)),
                  tok(d),
                  pl.BlockSpec((None, N_MOD - N_MOD_EARLY, d), lambda i: (i // per_b, 0, 0)),
                  _resident((1, aw)), _resident((1, sw)), _resident((1, d)),
                  _resident((LANES, aw)),
                  _resident((sw, sw)), _resident((1, sw)), _resident((aw + sw, d)),
                  side_spec],
        out_specs=[tok(d), side_spec],
        out_shape=[jax.ShapeDtypeStruct((t_tok, d), F32), side_shape],
        scratch_shapes=[pltpu.VMEM((2 * GROUP_SLABS, tm, LANES), F32),
                        pltpu.VMEM((2, tm, LANES), F32),
                        pltpu.VMEM((sw // LANES, tm, LANES), F32),
                        pltpu.VMEM((GROUP_SLABS + 1 + sw // LANES, tm, LANES), F32)],
        compiler_params=_cparams(("parallel",), "mix"),
        name="mix",
    )(o1, l1, o4, l4, o16, l16, yc, x2, mod, g_attn, g_ssm, g_post, expand, w_glu_b, b_glu, w_out_b, w_side)


def _mlp_kernel(x_ref, mod_ref, gpre_ref, gpost_ref, w1_ref, w2_ref, o_ref, h_ref):
    f = pl.program_id(1)
    last = pl.num_programs(1) - 1
    chunks = [slice(r, r + ROW_CHUNK) for r in range(0, x_ref.shape[0], ROW_CHUNK)]

    def partial():
        t = jnp.dot(h_ref[...], w1_ref[...], preferred_element_type=F32)
        t = jnp.square(jnp.maximum(t, 0.0)).astype(BF16)
        return jnp.dot(t, w2_ref[...], preferred_element_type=F32)

    @pl.when(f == 0)
    def _():
        scale = gpre_ref[...] * (1.0 + mod_ref[2:3, :])
        shift = mod_ref[1:2, :]
        for rows in chunks:
            h_ref[rows, :] = (_rms(x_ref[rows, :]) * scale + shift).astype(BF16)
        o_ref[...] = partial()

    @pl.when((f > 0) & (f < last))
    def _():
        o_ref[...] += partial()

    @pl.when(f == last)
    def _():
        gate = mod_ref[3:4, :] * gpost_ref[...]
        acc = o_ref[...] + partial()
        for rows in chunks:
            o_ref[rows, :] = x_ref[rows, :] + gate * _rms(acc[rows, :])


def _mlp(x2, mod, g_pre, g_post, w1_b, w2_b, seq):
    t_tok, d = x2.shape
    d_ff = w1_b.shape[1]
    tm, tf = 512, 2048
    assert d_ff // tf >= 2
    per_b = seq // tm
    return pl.pallas_call(
        _mlp_kernel,
        grid=(t_tok // tm, d_ff // tf),
        in_specs=[pl.BlockSpec((tm, d), lambda i, f: (i, 0)),
                  pl.BlockSpec((None, N_MOD - N_MOD_EARLY, d), lambda i, f: (i // per_b, 0, 0)),
                  _resident((1, d)), _resident((1, d)),
                  pl.BlockSpec((d, tf), lambda i, f: (0, f)),
                  pl.BlockSpec((tf, d), lambda i, f: (f, 0))],
        out_specs=pl.BlockSpec((tm, d), lambda i, f: (i, 0)),
        out_shape=jax.ShapeDtypeStruct((t_tok, d), F32),
        scratch_shapes=[pltpu.VMEM((tm, d), BF16)],
        compiler_params=_cparams(("parallel", "arbitrary"), "mlp"),
        name="mlp",
    )(x2, mod, g_pre, g_post, w1_b, w2_b)


def _rope_cos_sin(positions):
    freqs = ROPE_THETA ** (-jnp.arange(0, ROT_DIM, 2, dtype=F32) / ROT_DIM)
    ang = positions.reshape(-1).astype(F32)[None, :] * freqs[:, None]
    return jnp.concatenate([jnp.cos(ang), jnp.sin(ang)], axis=0)


def kernel(x, c, positions, w_ada, b_ada, g_pre_mix, g_post_mix, w_in, ssm_a_re, ssm_a_im, ssm_log_dt,
           ssm_b_re, ssm_b_im, ssm_c_re, ssm_c_im, ssm_d, w_glu, b_glu, g_attn_out, g_ssm_out, w_out,
           g_pre_mlp, g_post_mlp, w_mlp_in, w_mlp_out):
    bsz, seq, d = x.shape
    depth = w_ada.shape[0]
    x2 = x.reshape(bsz * seq, d)
    cos_sin = _rope_cos_sin(positions)
    row = lambda a: a.reshape(1, -1).astype(F32)
    for l in range(depth):
        s = jnp.zeros((SUBLANES, d), BF16).at[:bsz].set(jax.nn.silu(c).astype(BF16))
        b_ada_row = row(b_ada[l])
        mod = _adaln_early(s, w_ada[l], b_ada_row, bsz)
        (q1, q4, q16, k1, k4, k16, v1, v4, v16, uc, w_mlp_out_b), mod_late = _inproj(
            x2, mod, row(g_pre_mix[l]), w_in[l].astype(BF16), cos_sin, w_mlp_out[l],
            s, w_ada[l], b_ada_row, bsz, seq)

        nat = lambda a: a.reshape(bsz, 1, seq, a.shape[-1])
        o1, l1, w_out_b = _attn_group(nat(q1), nat(k1), nat(v1), 1, bsz, seq, w_out[l])
        o4, l4, w_glu_b = _attn_group(q4, k4, v4, 4, bsz, seq, w_glu[l])
        o16, l16 = _attn_group(q16, k16, v16, 16, bsz, seq)

        tables = _ssm_tables(ssm_a_re[l], ssm_a_im[l], ssm_log_dt[l], ssm_b_re[l], ssm_b_im[l],
                             ssm_c_re[l], ssm_c_im[l], ssm_d[l])
        yc = _ssm(uc, tables, bsz, seq)

        x2, w_mlp_in_b = _mix(o1.reshape(bsz * seq, -1), l1.reshape(bsz * seq, -1), o4, l4, o16, l16, yc,
                              x2, mod_late, row(g_attn_out[l]), row(g_ssm_out[l]), row(g_post_mix[l]),
                              w_glu_b, row(b_glu[l]), w_out_b, w_mlp_in[l],
                              bsz, seq)
        x2 = _mlp(x2, mod_late, row(g_pre_mlp[l]), row(g_post_mlp[l]), w_mlp_in_b, w_mlp_out_b, seq)
    return x2.reshape(bsz, seq, d)
```

```python
import functools
import math

import jax
import jax.numpy as jnp
from jax import lax
from jax.experimental import pallas as pl
from jax.experimental.pallas import tpu as pltpu

F32 = jnp.float32
BF16 = jnp.bfloat16

LANES = 128
SUBLANES = 8

HEAD_DIM = 64
ROT_DIM = HEAD_DIM // 4
ROPE_THETA = 500000.0
DILATIONS = (1, 4, 16)
HEADS_PER_GROUP = 6
GROUP_WIDTH = HEADS_PER_GROUP * HEAD_DIM
GROUP_SLABS = GROUP_WIDTH // LANES
ATT_BLK = 128
UNITS_PER_TRIP = 16
SSM_P = 16
SSM_N = 64
SSM_CHUNK = 16
CHUNK_W = SSM_CHUNK * SSM_P
SLOTS = LANES // SSM_P
N_MOD = 6
EPS = 1e-6
NEG = -1e30
TOK_TILE = 512
ROW_CHUNK = 128


MIB = 1024 * 1024
VMEM_MIB_V7X = 64
VMEM_MIB = dict(adaln=40, inproj=58, attn=48, ssm=48, mix=56, mlp=60)


def _cparams(sem, kernel_name):
    assert VMEM_MIB[kernel_name] < VMEM_MIB_V7X
    return pltpu.CompilerParams(dimension_semantics=sem, vmem_limit_bytes=VMEM_MIB[kernel_name] * MIB)


def _resident(shape):
    nd = len(shape)
    return pl.BlockSpec(shape, lambda *_: (0,) * nd, pipeline_mode=pl.Buffered(1))


def _slab(j):
    return slice(j * LANES, (j + 1) * LANES)


def _slot_transpose(vs):
    slot = lax.broadcasted_iota(jnp.int32, vs[0].shape, 1) // SSM_P
    vs = list(vs)
    for dist in (4, 2, 1):
        low = (slot & dist) == 0
        nxt = list(vs)
        for i in range(SLOTS):
            if i & dist == 0:
                a, b = vs[i], vs[i + dist]
                nxt[i] = jnp.where(low, a, pltpu.roll(b, dist * SSM_P, axis=1))
                nxt[i + dist] = jnp.where(low, pltpu.roll(a, LANES - dist * SSM_P, axis=1), b)
        vs = nxt
    return vs


def _adaln_kernel(s_ref, w_ref, b_ref, o_ref):
    o_ref[...] = jnp.dot(s_ref[...], w_ref[...].astype(BF16),
                         preferred_element_type=F32) + b_ref[...]


N_MOD_EARLY = 2


def _adaln_early(s, w_ada, b_ada, bsz):
    d = s.shape[1]
    n = N_MOD_EARLY * d
    tn = 1024
    out = pl.pallas_call(
        _adaln_kernel,
        grid=(n // tn,),
        in_specs=[pl.BlockSpec((SUBLANES, d), lambda j: (0, 0)),
                  pl.BlockSpec((d, tn), lambda j: (0, j)),
                  pl.BlockSpec((1, tn), lambda j: (0, j))],
        out_specs=pl.BlockSpec((SUBLANES, tn), lambda j: (0, j)),
        out_shape=jax.ShapeDtypeStruct((SUBLANES, n), F32),
        compiler_params=_cparams(("arbitrary",), "adaln"),
        name="adaln",
    )(s, w_ada, b_ada)
    return out[:bsz].reshape(bsz, N_MOD_EARLY, d)


def _inproj_kernel(x_ref, mod_ref, g_ref, w_ref, cs_ref, rx_ref, r1_ref, wside_ref,
                   sblk_ref, wa_lo_ref, wa_hi_ref, ba_ref,
                   q1_ref, q4_ref, q16_ref, k1_ref, k4_ref, k16_ref, v1_ref, v4_ref, v16_ref,
                   uc_ref, wside_out_ref, modr_ref, sc_ref, s4_ref, *, tm):
    wside_out_ref[...] = wside_ref[...].astype(BF16)

    @pl.when(pl.program_id(0) == 0)
    def _():
        modr_ref[...] = jnp.broadcast_to(ba_ref[...], modr_ref.shape)
    sblk = sblk_ref[...]
    half_w = wa_lo_ref.shape[1]
    modr_ref[:, :half_w] += jnp.dot(sblk, wa_lo_ref[...].astype(BF16), preferred_element_type=F32)
    modr_ref[:, half_w:] += jnp.dot(sblk, wa_hi_ref[...].astype(BF16), preferred_element_type=F32)

    x = x_ref[...]
    ms = jnp.mean(x * x, axis=-1, keepdims=True)
    scale = g_ref[...] * (1.0 + mod_ref[1:2, :])
    h = (x * lax.rsqrt(ms + EPS)) * scale + mod_ref[0:1, :]
    hb = h.astype(BF16)

    qk_slabs = 4 * GROUP_SLABS
    quarter = tm // 4

    def by_fours(j, t4):
        pieces = [sc_ref[j, pl.ds(r4, quarter, stride=4), :] for r4 in range(4)]
        for r4, piece in enumerate(pieces):
            s4_ref[t4, r4 * quarter:(r4 + 1) * quarter, :] = piece
        return pieces

    def rows16(t4, r16, start, size):
        r4, a = r16 % 4, r16 // 4
        return s4_ref[t4, pl.ds(r4 * quarter + 4 * start + a, size, stride=4), :]

    def scatter(j, jj, refs, val, t4=None):
        refs = dict(refs)
        if 1 in refs:
            refs[1][:, _slab(jj)] = val.astype(BF16)
        if 16 in refs:
            pieces = by_fours(j, t4)
            for r16 in range(16):
                refs[16][r16, :, _slab(jj)] = rows16(t4, r16, 0, tm // 16).astype(BF16)
        elif 4 in refs:
            pieces = [sc_ref[j, pl.ds(r4, quarter, stride=4), :] for r4 in range(4)]
        if 4 in refs:
            for r4, piece in enumerate(pieces):
                refs[4][r4, :, _slab(jj)] = piece.astype(BF16)

    pvu = jnp.dot(hb, w_ref[:, qk_slabs * LANES:], preferred_element_type=F32)
    for j in range(pvu.shape[1] // LANES):
        sc_ref[qk_slabs + j] = pvu[:, _slab(j)]
    for jj in range(GROUP_SLABS):
        scatter(qk_slabs + jj, jj, ((1, v1_ref), (4, v4_ref), (16, v16_ref)), pvu[:, _slab(jj)], t4=jj)

    u0 = qk_slabs + GROUP_SLABS
    ngb, nc = uc_ref.shape[0] // SLOTS, tm // SSM_CHUNK
    for gb in range(ngb):
        by_fours(u0 + gb, GROUP_SLABS + gb)
    vs = [jnp.concatenate([rows16(GROUP_SLABS + gb, half * SLOTS + s, cb * SUBLANES, SUBLANES)
                           for gb in range(ngb) for half in range(2) for cb in range(nc // SUBLANES)],
                          axis=0) for s in range(SLOTS)]
    for g, o in enumerate(_slot_transpose(vs)):
        for gb in range(ngb):
            r0 = gb * 2 * nc
            full = jnp.concatenate([o[r0:r0 + nc], o[r0 + nc:r0 + 2 * nc]], axis=1)
            uc_ref[gb * SLOTS + g] = full.astype(BF16)

    cs = cs_ref[...]
    cs_hi = cs.astype(BF16)
    cs_lo = (cs - cs_hi.astype(F32)).astype(BF16)
    rows_contract = (((0,), (0,)), ((), ()))
    tab = lax.dot_general(cs_hi, rx_ref[...], rows_contract, preferred_element_type=F32) + \
        lax.dot_general(cs_lo, rx_ref[...], rows_contract, preferred_element_type=F32)
    ct = tab[:, :LANES] + r1_ref[...]
    st = tab[:, LANES:]
    q_scale = math.log2(math.e) / math.sqrt(HEAD_DIM)
    ct_q, st_q = ct * q_scale, st * q_scale
    lane = lax.broadcasted_iota(jnp.int32, ct.shape, 1) % HEAD_DIM
    second_half = (lane >= ROT_DIM // 2) & (lane < ROT_DIM)
    q_refs = ((1, q1_ref),), ((4, q4_ref),), ((16, q16_ref),)
    chunk_slabs = 4
    for c0 in reversed(range(0, qk_slabs, chunk_slabs)):
        pc = jnp.dot(hb, w_ref[:, c0 * LANES:(c0 + chunk_slabs) * LANES], preferred_element_type=F32)
        for s in range(chunk_slabs):
            j = c0 + s
            grp, jj = divmod(j, GROUP_SLABS)
            t = pc[:, _slab(s)]
            partner = jnp.where(second_half,
                                pltpu.roll(t, ROT_DIM // 2, axis=1),
                                pltpu.roll(t, LANES - ROT_DIM // 2, axis=1))
            r = t * ct_q + partner * st_q if grp < 3 else t * ct + partner * st
            if grp > 0:
                sc_ref[j] = r
            scatter(j, jj, q_refs[grp] if grp < 3 else ((1, k1_ref), (4, k4_ref), (16, k16_ref)), r,
                    t4=GROUP_SLABS + ngb + (grp - 2) * GROUP_SLABS + jj if grp >= 2 else None)


def _rope_spread():
    half = ROT_DIM // 2
    dim = jnp.arange(LANES)[None, :] % HEAD_DIM
    i = jnp.arange(half)[:, None]
    first, second = dim == i, dim == i + half
    ct_rows = (first | second).astype(F32)
    st_rows = second.astype(F32) - first.astype(F32)
    zero = jnp.zeros_like(ct_rows)
    spread = jnp.concatenate([jnp.concatenate([ct_rows, zero], axis=1),
                              jnp.concatenate([zero, st_rows], axis=1)], axis=0)
    return spread.astype(BF16), (dim >= ROT_DIM).astype(F32)


def _side_cast_spec(w, steps):
    rows, cols = w.shape
    assert rows % (steps * 2 * SUBLANES) == 0
    return (pl.BlockSpec((rows // steps, cols), lambda i: (i, 0)),
            jax.ShapeDtypeStruct(w.shape, BF16))


def _inproj(x2, mod, g_pre, w_in_b, cos_sin, w_side, s, w_ada, b_ada, bsz, seq):
    t_tok, d = x2.shape
    in_w = w_in_b.shape[1]
    ssm_w = in_w - 5 * GROUP_WIDTH
    groups = ssm_w // SSM_P
    tm = TOK_TILE
    per_b = seq // tm
    steps = t_tok // tm
    tok = lambda w: pl.BlockSpec((tm, w), lambda i: (i, 0))
    side_spec, side_shape = _side_cast_spec(w_side, steps)
    kblk = d // steps
    late_w = (N_MOD - N_MOD_EARLY) * d
    assert d % steps == 0 and N_MOD_EARLY * d == late_w // 2
    s_blocks = s.reshape(SUBLANES, steps, kblk).transpose(1, 0, 2)
    wa_spec = lambda half: pl.BlockSpec((kblk, late_w // 2), lambda i: (i, 1 + half))

    def res_spec(dil):
        return pl.BlockSpec((None, dil, tm // dil, GROUP_WIDTH), lambda i: (i // per_b, 0, i % per_b, 0))

    def res_shape(dil):
        return jax.ShapeDtypeStruct((bsz, dil, seq // dil, GROUP_WIDTH), BF16)

    nat = jax.ShapeDtypeStruct((t_tok, GROUP_WIDTH), BF16)
    out_shape = [nat, res_shape(4), res_shape(16)] * 3 + \
                [jax.ShapeDtypeStruct((groups, t_tok // SSM_CHUNK, CHUNK_W), BF16), side_shape,
                 jax.ShapeDtypeStruct((SUBLANES, late_w), F32)]
    out_specs = [tok(GROUP_WIDTH), res_spec(4), res_spec(16)] * 3 + \
                [pl.BlockSpec((groups, tm // SSM_CHUNK, CHUNK_W), lambda i: (0, i, 0)), side_spec,
                 pl.BlockSpec((SUBLANES, late_w), lambda i: (0, 0))]
    outs = pl.pallas_call(
        functools.partial(_inproj_kernel, tm=tm),
        grid=(steps,),
        in_specs=[tok(d),
                  pl.BlockSpec((None, N_MOD_EARLY, d), lambda i: (i // per_b, 0, 0)),
                  _resident((1, d)),
                  _resident((d, in_w)),
                  pl.BlockSpec((ROT_DIM, tm), lambda i: (0, i)),
                  _resident((ROT_DIM, 2 * LANES)), _resident((1, LANES)),
                  side_spec,
                  pl.BlockSpec((None, SUBLANES, kblk), lambda i: (i, 0, 0)),
                  wa_spec(0), wa_spec(1), _resident((1, late_w))],
        out_specs=out_specs,
        out_shape=out_shape,
        scratch_shapes=[pltpu.VMEM((in_w // LANES, tm, LANES), F32),
                        pltpu.VMEM((3 * GROUP_SLABS + ssm_w // LANES, tm, LANES), F32)],
        compiler_params=_cparams(("arbitrary",), "inproj"),
        name="inproj",
    )(x2, mod, g_pre, w_in_b, cos_sin, *_rope_spread(), w_side, s_blocks, w_ada, w_ada,
      b_ada[:, N_MOD_EARLY * d:])
    mod_late = outs[-1][:bsz].reshape(bsz, N_MOD - N_MOD_EARLY, d)
    return outs[:-1], mod_late


def _stat_lanes(h):
    a, b = slice(h, h + 1), slice(HEAD_DIM + h, HEAD_DIM + h + 1)
    return (a, b) if h % 2 == 0 else (b, a)


def _attn_kernel(*refs, nblk, units, side):
    if side:
        q_ref, k_ref, v_ref, wside_ref, o_ref, s_ref, wside_out_ref, sc_ref, pr_ref, mk_ref = refs
        wside_out_ref[...] = wside_ref[...].astype(BF16)
    else:
        q_ref, k_ref, v_ref, o_ref, s_ref, sc_ref, pr_ref, mk_ref = refs

    pair = 2 * ATT_BLK
    qi = lax.broadcasted_iota(jnp.int32, (pair, 2 * ATT_BLK), 0) % ATT_BLK
    kj = lax.broadcasted_iota(jnp.int32, (pair, 2 * ATT_BLK), 1)
    rel = qi - kj
    mk_ref[0] = jnp.where((rel >= 0) & (rel <= ATT_BLK), 0.0, NEG)
    mk_ref[1] = jnp.where((rel + ATT_BLK >= 0) & (rel <= 0), 0.0, NEG)
    low = lax.broadcasted_iota(jnp.int32, (ATT_BLK, LANES), 1) < HEAD_DIM
    top, bot = slice(0, ATT_BLK), slice(ATT_BLK, pair)
    ones = jnp.ones((2 * ATT_BLK, LANES), BF16)

    def coords(u):
        u = jnp.minimum(u, units - 1)
        r, i = u // nblk, u % nblk
        rows = pl.ds(pl.multiple_of(i * ATT_BLK, ATT_BLK), ATT_BLK)
        krows = pl.ds(pl.multiple_of(jnp.maximum(i - 1, 0) * ATT_BLK, ATT_BLK), 2 * ATT_BLK)
        return r, i, rows, krows

    def scores(u, slot):
        r, i, rows, krows = coords(u)
        bias = mk_ref[jnp.minimum(i, 1)]
        for jj in range(GROUP_SLABS):
            q2 = q_ref[r, rows, _slab(jj)]
            zero = jnp.zeros_like(q2)
            qp = jnp.concatenate([jnp.where(low, q2, zero), jnp.where(low, zero, q2)], axis=0)
            s = lax.dot_general(qp, k_ref[r, krows, _slab(jj)], (((1,), (1,)), ((), ())),
                                preferred_element_type=F32)
            sc_ref[slot, jj] = s + bias

    def softmax(u, slot):
        r, _, rows, _ = coords(u)
        s_ref[r, rows, :] = jnp.zeros((ATT_BLK, LANES), F32)
        for jj in range(GROUP_SLABS):
            s = sc_ref[slot, jj]
            m = jnp.max(s, axis=1, keepdims=True)
            pr_ref[slot, jj] = jnp.exp2(s - m).astype(BF16)
            s_ref[r, rows, _stat_lanes(2 * jj)[0]] = m[top]
            s_ref[r, rows, _stat_lanes(2 * jj + 1)[0]] = m[bot]

    def values(u, slot):
        r, _, rows, krows = coords(u)
        for jj in range(GROUP_SLABS):
            v_ext = jnp.concatenate([v_ref[r, krows, _slab(jj)], ones], axis=1)
            oe = jnp.dot(pr_ref[slot, jj], v_ext, preferred_element_type=F32)
            l_even, l_odd = _stat_lanes(2 * jj)[1], _stat_lanes(2 * jj + 1)[1]
            s_ref[r, rows, l_even] = oe[top, LANES:][:, l_even]
            s_ref[r, rows, l_odd] = oe[bot, LANES:][:, l_odd]
            o_ref[r, rows, _slab(jj)] = jnp.where(low, oe[top, :LANES], oe[bot, :LANES]).astype(o_ref.dtype)

    scores(0, 0)
    softmax(0, 0)
    scores(1, 1)

    def body(t, carry):
        u = UNITS_PER_TRIP * t
        for k in range(UNITS_PER_TRIP):
            scores(u + k + 2, k % 2)
            softmax(u + k + 1, (k + 1) % 2)
            values(u + k, k % 2)
        return carry

    assert UNITS_PER_TRIP % 2 == 0 and units % UNITS_PER_TRIP == 0
    lax.fori_loop(0, units // UNITS_PER_TRIP, body, 0)


def _attn_group(q, k, v, dilation, bsz, seq, w_side=None):
    n = seq // dilation
    nblk = n // ATT_BLK
    spec = lambda w: pl.BlockSpec((None, dilation, n, w), lambda b: (b, 0, 0, 0))
    win = (2 * ATT_BLK, 2 * ATT_BLK)
    ins, in_specs = [q, k, v], [spec(GROUP_WIDTH)] * 3
    out_specs = [spec(GROUP_WIDTH), spec(LANES)]
    out_shape = [jax.ShapeDtypeStruct((bsz, dilation, n, GROUP_WIDTH), BF16),
                 jax.ShapeDtypeStruct((bsz, dilation, n, LANES), F32)]
    if w_side is not None:
        side_spec, side_shape = _side_cast_spec(w_side, bsz)
        ins, in_specs = ins + [w_side], in_specs + [side_spec]
        out_specs, out_shape = out_specs + [side_spec], out_shape + [side_shape]
    return pl.pallas_call(
        functools.partial(_attn_kernel, nblk=nblk, units=dilation * nblk, side=w_side is not None),
        grid=(bsz,),
        in_specs=in_specs,
        out_specs=out_specs,
        out_shape=out_shape,
        scratch_shapes=[pltpu.VMEM((2, GROUP_SLABS) + win, F32),
                        pltpu.VMEM((2, GROUP_SLABS) + win, BF16),
                        pltpu.VMEM((2,) + win, F32)],
        compiler_params=_cparams(("parallel",), "attn"),
        name=f"attn_d{dilation}",
    )(*ins)


def _cmul(a, b):
    return a[0] * b[0] - a[1] * b[1], a[0] * b[1] + a[1] * b[0]


def _ssm_tables(a_re, a_im, log_dt, b_re, b_im, c_re, c_im, d_skip):
    f = lambda t: t.astype(F32)
    lam = (f(a_re), f(a_im))
    dt = jnp.exp(f(log_dt))[:, None]
    g = a_re.shape[0]
    taus = jnp.arange(2 * SSM_CHUNK + 1, dtype=F32)[None, :, None]
    mag = jnp.exp(lam[0][:, None, :] * dt[:, None, :] * taus)
    ang = lam[1][:, None, :] * dt[:, None, :] * taus
    apow = (mag * jnp.cos(ang), mag * jnp.sin(ang))
    a_bar = (apow[0][:, 1], apow[1][:, 1])
    den = lam[0] * lam[0] + lam[1] * lam[1]
    coef = _cmul((a_bar[0] - 1.0, a_bar[1]), (lam[0] / den, -lam[1] / den))
    b_bar = _cmul((coef[0][..., None], coef[1][..., None]), (f(b_re), f(b_im)))
    c_mat = (f(c_re), f(c_im))

    pair = lambda lo, hi: jnp.concatenate([lo, hi], axis=-1)
    r_re, r_im = (t[:, SSM_CHUNK - 1::-1][:, :SSM_CHUNK] for t in apow)
    n_re, n_im = (t[:, 1:SSM_CHUNK + 1] for t in apow)
    k_re, k_im = (t[:, :SSM_CHUNK] for t in apow)
    bt_re, bt_im = (t.transpose(0, 2, 1) for t in b_bar)
    fac = jnp.stack([pair(r_re, r_re), pair(-r_im, r_im),
                     pair(n_re, -n_re), pair(-n_im, -n_im),
                     pair(k_re, k_re), pair(-k_im, k_im),
                     pair(bt_re, bt_im), pair(c_mat[0], c_mat[1]),
                     pair(bt_re, -bt_im)], axis=1)
    d_lane = jnp.pad(f(d_skip), ((0, 0), (0, CHUNK_W - SSM_P)))[:, None, :]

    def mul_rows(tau):
        re, im = apow[0][:, tau], apow[1][:, tau]
        return [pair(re, re), pair(-im, im), pair(im, -im)]
    ptab = jnp.stack(mul_rows(SSM_CHUNK) + mul_rows(2 * SSM_CHUNK) +
                     [jnp.zeros((g, LANES), F32)] * 2, axis=1)
    return fac, d_lane, ptab


def _chunk_matrices(f_ref, d_ref, g):
    b, c, b_conj = f_ref[g, 6], f_ref[g, 7], f_ref[g, 8]
    swapped = lambda x: pltpu.roll(x, SSM_N, axis=1)

    def outer(k, x):
        a1, a2, xs = f_ref[g, k], f_ref[g, k + 1], swapped(x)
        return jnp.concatenate([a1[i:i + 1] * x + a2[i:i + 1] * xs for i in range(SSM_CHUNK)], axis=0)

    w = outer(0, b).astype(BF16)
    v = outer(2, c).T.astype(BF16)
    ca = outer(4, c)
    m_row = lax.dot_general(b_conj, ca, (((1,), (1,)), ((), ())), precision=lax.Precision.HIGHEST,
                            preferred_element_type=F32)
    q = lax.broadcasted_iota(jnp.int32, m_row.shape, 0)
    lane = lax.broadcasted_iota(jnp.int32, m_row.shape, 1)
    m_row = m_row + jnp.where(lane == q, d_ref[g], 0.0)
    return w, v, _toeplitz(m_row)


def _toeplitz(m_row):
    lane = lax.broadcasted_iota(jnp.int32, m_row.shape, 1)
    blocks = [m_row]
    for s in range(1, SSM_CHUNK):
        blocks.append(jnp.where(lane >= s * SSM_P, pltpu.roll(m_row, s * SSM_P, axis=1), 0.0))
    return jnp.concatenate(blocks, axis=0).astype(BF16)


def _ssm_kernel(u_ref, f_ref, d_ref, p_ref, y_ref, gx_ref, gz_ref, h_ref, *, groups, nchunk, bsz):
    rows = nchunk * bsz
    row = lax.broadcasted_iota(jnp.int32, (rows, LANES), 0)
    first_chunk = (row % nchunk) == 0

    def prev_chunk(a):
        return jnp.where(first_chunk, 0.0, pltpu.roll(a, 1, axis=0))

    mats = [_chunk_matrices(f_ref, d_ref, g) for g in range(groups)]

    for g in range(groups):
        ex = jnp.dot(u_ref[g], mats[g][0], preferred_element_type=F32)
        ez = pltpu.roll(ex, SSM_N, axis=1)
        exs, ezs = prev_chunk(ex), prev_chunk(ez)
        p1, p2, p3 = p_ref[g, 0:1, :], p_ref[g, 1:2, :], p_ref[g, 2:3, :]
        gx = ex + p1 * exs + p2 * ezs
        gz = ez + p1 * ezs + p3 * exs
        for b in range(bsz):
            gx_ref[g, pl.ds(b, nchunk, stride=bsz), :] = gx[b * nchunk:(b + 1) * nchunk]
            gz_ref[g, pl.ds(b, nchunk, stride=bsz), :] = gz[b * nchunk:(b + 1) * nchunk]

    y_intra = [jnp.dot(u_ref[g], mats[g][2], preferred_element_type=F32) for g in range(groups)]

    tile = (SUBLANES, LANES)
    mults = [[jnp.broadcast_to(p_ref[g, r:r + 1, :], tile) for r in (3, 4, 5)] for g in range(groups)]
    zero = jnp.zeros(tile, F32)
    state = [(zero, zero) for _ in range(groups)]
    for j in range(rows // SUBLANES):
        r0 = j * SUBLANES
        for g in range(groups):
            x, z = state[g]
            q1, q2, q3 = mults[g]
            xn = q1 * x + q2 * z + gx_ref[g, r0:r0 + SUBLANES, :]
            zn = q1 * z + q3 * x + gz_ref[g, r0:r0 + SUBLANES, :]
            h_ref[g, r0:r0 + SUBLANES, :] = xn
            state[g] = (xn, zn)

    for g in range(groups):
        h_end = jnp.concatenate([h_ref[g, pl.ds(b, nchunk, stride=bsz), :] for b in range(bsz)], axis=0)
        h_in = prev_chunk(h_end).astype(BF16)
        y = y_intra[g] + jnp.dot(h_in, mats[g][1], preferred_element_type=F32)
        y_ref[g] = y.astype(y_ref.dtype)


def _ssm(uc, tables, bsz, seq):
    fac, d_lane, ptab = tables
    g, rows, _ = uc.shape
    nchunk = seq // SSM_CHUNK
    assert rows == nchunk * bsz and SUBLANES % bsz == 0 and SUBLANES // bsz == 2
    gs = 4
    blk = lambda a, b: pl.BlockSpec((gs, a, b), lambda i: (i, 0, 0))
    return pl.pallas_call(
        functools.partial(_ssm_kernel, groups=gs, nchunk=nchunk, bsz=bsz),
        grid=(g // gs,),
        in_specs=[blk(rows, CHUNK_W),
                  pl.BlockSpec((gs,) + fac.shape[1:], lambda i: (i, 0, 0, 0)),
                  blk(1, CHUNK_W), blk(SUBLANES, LANES)],
        out_specs=blk(rows, CHUNK_W),
        out_shape=jax.ShapeDtypeStruct((g, rows, CHUNK_W), BF16),
        scratch_shapes=[pltpu.VMEM((gs, rows, LANES), F32)] * 3,
        compiler_params=_cparams(("parallel",), "ssm"),
        name="ssm",
    )(uc, fac, d_lane, ptab)


def _rms(x):
    return x * lax.rsqrt(jnp.mean(x * x, axis=-1, keepdims=True) + EPS)


def _gelu_tanh(x):
    k = -2.0 * math.sqrt(2.0 / math.pi) * math.log2(math.e)
    e = jnp.exp2(x * (k + (k * 0.044715) * (x * x)))
    return x * (1.0 / (1.0 + e))


def _mix_kernel(o1_ref, l1_ref, o4_ref, l4_ref, o16_ref, l16_ref, y_ref, x_ref, mod_ref,
                ga_ref, gs_ref, gp_ref, ex_ref, wg_ref, bg_ref, wo_ref, wside_ref, out_ref,
                wside_out_ref, so_ref, sl_ref, sy_ref, s4_ref, *, tm):
    wside_out_ref[...] = wside_ref[...].astype(BF16)

    quarter = tm // 4

    def put16(t4, r16, start, value):
        r4, a = r16 % 4, r16 // 4
        s4_ref[t4, pl.ds(r4 * quarter + 4 * start + a, value.shape[0], stride=4), :] = value

    def from_fours(t4, dst_ref, j):
        for r4 in range(4):
            dst_ref[j, pl.ds(r4, quarter, stride=4), :] = s4_ref[t4, r4 * quarter:(r4 + 1) * quarter, :]

    for r in range(4):
        rows = pl.ds(r, quarter, stride=4)
        for jj in range(GROUP_SLABS):
            so_ref[jj, rows, :] = o4_ref[r, :, _slab(jj)].astype(F32)
        sl_ref[0, rows, :] = l4_ref[r]
    for r in range(16):
        for jj in range(GROUP_SLABS):
            put16(jj, r, 0, o16_ref[r, :, _slab(jj)].astype(F32))
        put16(GROUP_SLABS, r, 0, l16_ref[r])
    for jj in range(GROUP_SLABS):
        from_fours(jj, so_ref, GROUP_SLABS + jj)
    from_fours(GROUP_SLABS, sl_ref, 1)

    ngb, nc = y_ref.shape[0] // SLOTS, tm // SSM_CHUNK
    vs = [jnp.concatenate([y_ref[gb * SLOTS + g, :, _slab(half)].astype(F32)
                           for gb in range(ngb) for half in range(2)], axis=0)
          for g in range(SLOTS)]
    for s, o in enumerate(_slot_transpose(vs)):
        for gb in range(ngb):
            for half in range(2):
                r0 = (gb * 2 + half) * nc
                put16(GROUP_SLABS + 1 + gb, half * SLOTS + s, 0, o[r0:r0 + nc])
    for gb in range(ngb):
        from_fours(GROUP_SLABS + 1 + gb, sy_ref, gb)

    lane = lax.broadcasted_iota(jnp.int32, (tm, LANES), 1)
    even = (lane & 1) == 0
    ms, ls = [], []
    for st in (l1_ref[...], sl_ref[0], sl_ref[1]):
        sw = pltpu.roll(st, HEAD_DIM, axis=1)
        ms.append(jnp.where(even, st, sw))
        ls.append(jnp.where(even, sw, st))
    mx = jnp.maximum(jnp.maximum(ms[0], ms[1]), ms[2])
    es = [jnp.exp2(m - mx) for m in ms]
    inv = 1.0 / (es[0] * ls[0] + es[1] * ls[1] + es[2] * ls[2])
    head_lane = lane < HEADS_PER_GROUP
    wexp = [jnp.dot(jnp.where(head_lane, e * inv, 0.0).astype(BF16), ex_ref[...],
                    preferred_element_type=F32) for e in es]
    att_slabs = []
    for jj in range(GROUP_SLABS):
        a = wexp[0][:, _slab(jj)] * o1_ref[:, _slab(jj)].astype(F32)
        a = a + wexp[1][:, _slab(jj)] * so_ref[jj]
        a = a + wexp[2][:, _slab(jj)] * so_ref[GROUP_SLABS + jj]
        att_slabs.append(a)
    att = _rms(jnp.concatenate(att_slabs, axis=1)) * ga_ref[...]

    yv = jnp.concatenate([sy_ref[gb] for gb in range(sy_ref.shape[0])], axis=1)
    s = _gelu_tanh(yv)
    z = jnp.dot(s.astype(BF16), wg_ref[...], preferred_element_type=F32) + bg_ref[...]
    s = s * jax.nn.sigmoid(z)
    ssm = _rms(s) * gs_ref[...]
    aw = att.shape[1]
    mix = jnp.dot(att.astype(BF16), wo_ref[:aw, :], preferred_element_type=F32)
    mix = mix + jnp.dot(ssm.astype(BF16), wo_ref[aw:, :], preferred_element_type=F32)
    out_ref[...] = x_ref[...] + (mod_ref[0:1, :] * gp_ref[...]) * _rms(mix)


def _mix(o1, l1, o4, l4, o16, l16, yc, x2, mod, g_attn, g_ssm, g_post, w_glu_b, b_glu, w_out_b, w_side,
         bsz, seq):
    t_tok, d = x2.shape
    groups = yc.shape[0]
    sw = groups * SSM_P
    aw = GROUP_WIDTH
    tm = TOK_TILE
    per_b = seq // tm
    tok = lambda w: pl.BlockSpec((tm, w), lambda i: (i, 0))
    res = lambda dil, w: pl.BlockSpec((None, dil, tm // dil, w), lambda i: (i // per_b, 0, i % per_b, 0))
    head_of_lane = jnp.arange(aw)[None, :] // HEAD_DIM
    expand = (jnp.arange(LANES)[:, None] == head_of_lane).astype(BF16)
    side_spec, side_shape = _side_cast_spec(w_side, t_tok // tm)
    return pl.pallas_call(
        functools.partial(_mix_kernel, tm=tm),
        grid=(t_tok // tm,),
        in_specs=[tok(aw), tok(LANES), res(4, aw), res(4, LANES), res(16, aw), res(16, LANES),
                  pl.BlockSpec((groups, tm // SSM_CHUNK, CHUNK_W), lambda i: (0, i, 0)),
                  tok(d),
                  pl.BlockSpec((None, N_MOD - N_MOD_EARLY, d), lambda i: (i // per_b, 0, 0)),
                  _resident((1, aw)), _resident((1, sw)), _resident((1, d)),
                  _resident((LANES, aw)),
                  _resident((sw, sw)), _resident((1, sw)), _resident((aw + sw, d)),
                  side_spec],
        out_specs=[tok(d), side_spec],
        out_shape=[jax.ShapeDtypeStruct((t_tok, d), F32), side_shape],
        scratch_shapes=[pltpu.VMEM((2 * GROUP_SLABS, tm, LANES), F32),
                        pltpu.VMEM((2, tm, LANES), F32),
                        pltpu.VMEM((sw // LANES, tm, LANES), F32),
                        pltpu.VMEM((GROUP_SLABS + 1 + sw // LANES, tm, LANES), F32)],
        compiler_params=_cparams(("parallel",), "mix"),
        name="mix",
    )(o1, l1, o4, l4, o16, l16, yc, x2, mod, g_attn, g_ssm, g_post, expand, w_glu_b, b_glu, w_out_b, w_side)


def _mlp_kernel(x_ref, mod_ref, gpre_ref, gpost_ref, w1_ref, w2_ref, o_ref, h_ref):
    f = pl.program_id(1)
    last = pl.num_programs(1) - 1
    chunks = [slice(r, r + ROW_CHUNK) for r in range(0, x_ref.shape[0], ROW_CHUNK)]

    def partial():
        t = jnp.dot(h_ref[...], w1_ref[...], preferred_element_type=F32)
        t = jnp.square(jnp.maximum(t, 0.0)).astype(BF16)
        return jnp.dot(t, w2_ref[...], preferred_element_type=F32)

    @pl.when(f == 0)
    def _():
        scale = gpre_ref[...] * (1.0 + mod_ref[2:3, :])
        shift = mod_ref[1:2, :]
        for rows in chunks:
            h_ref[rows, :] = (_rms(x_ref[rows, :]) * scale + shift).astype(BF16)
        o_ref[...] = partial()

    @pl.when((f > 0) & (f < last))
    def _():
        o_ref[...] += partial()

    @pl.when(f == last)
    def _():
        gate = mod_ref[3:4, :] * gpost_ref[...]
        acc = o_ref[...] + partial()
        for rows in chunks:
            o_ref[rows, :] = x_ref[rows, :] + gate * _rms(acc[rows, :])


def _mlp(x2, mod, g_pre, g_post, w1_b, w2_b, seq):
    t_tok, d = x2.shape
    d_ff = w1_b.shape[1]
    tm, tf = 512, 2048
    assert d_ff // tf >= 2
    per_b = seq // tm
    return pl.pallas_call(
        _mlp_kernel,
        grid=(t_tok // tm, d_ff // tf),
        in_specs=[pl.BlockSpec((tm, d), lambda i, f: (i, 0)),
                  pl.BlockSpec((None, N_MOD - N_MOD_EARLY, d), lambda i, f: (i // per_b, 0, 0)),
                  _resident((1, d)), _resident((1, d)),
                  pl.BlockSpec((d, tf), lambda i, f: (0, f)),
                  pl.BlockSpec((tf, d), lambda i, f: (f, 0))],
        out_specs=pl.BlockSpec((tm, d), lambda i, f: (i, 0)),
        out_shape=jax.ShapeDtypeStruct((t_tok, d), F32),
        scratch_shapes=[pltpu.VMEM((tm, d), BF16)],
        compiler_params=_cparams(("parallel", "arbitrary"), "mlp"),
        name="mlp",
    )(x2, mod, g_pre, g_post, w1_b, w2_b)


def _rope_cos_sin(positions):
    freqs = ROPE_THETA ** (-jnp.arange(0, ROT_DIM, 2, dtype=F32) / ROT_DIM)
    ang = positions.reshape(-1).astype(F32)[None, :] * freqs[:, None]
    return jnp.concatenate([jnp.cos(ang), jnp.sin(ang)], axis=0)


def kernel(x, c, positions, w_ada, b_ada, g_pre_mix, g_post_mix, w_in, ssm_a_re, ssm_a_im, ssm_log_dt,
           ssm_b_re, ssm_b_im, ssm_c_re, ssm_c_im, ssm_d, w_glu, b_glu, g_attn_out, g_ssm_out, w_out,
           g_pre_mlp, g_post_mlp, w_mlp_in, w_mlp_out):
    bsz, seq, d = x.shape
    depth = w_ada.shape[0]
    x2 = x.reshape(bsz * seq, d)
    cos_sin = _rope_cos_sin(positions)
    row = lambda a: a.reshape(1, -1).astype(F32)
    for l in range(depth):
        s = jnp.zeros((SUBLANES, d), BF16).at[:bsz].set(jax.nn.silu(c).astype(BF16))
        b_ada_row = row(b_ada[l])
        mod = _adaln_early(s, w_ada[l], b_ada_row, bsz)
        (q1, q4, q16, k1, k4, k16, v1, v4, v16, uc, w_mlp_out_b), mod_late = _inproj(
            x2, mod, row(g_pre_mix[l]), w_in[l].astype(BF16), cos_sin, w_mlp_out[l],
            s, w_ada[l], b_ada_row, bsz, seq)

        nat = lambda a: a.reshape(bsz, 1, seq, a.shape[-1])
        o1, l1, w_out_b = _attn_group(nat(q1), nat(k1), nat(v1), 1, bsz, seq, w_out[l])
        o4, l4, w_glu_b = _attn_group(q4, k4, v4, 4, bsz, seq, w_glu[l])
        o16, l16 = _attn_group(q16, k16, v16, 16, bsz, seq)

        tables = _ssm_tables(ssm_a_re[l], ssm_a_im[l], ssm_log_dt[l], ssm_b_re[l], ssm_b_im[l],
                             ssm_c_re[l], ssm_c_im[l], ssm_d[l])
        yc = _ssm(uc, tables, bsz, seq)

        x2, w_mlp_in_b = _mix(o1.reshape(bsz * seq, -1), l1.reshape(bsz * seq, -1), o4, l4, o16, l16, yc,
                              x2, mod_late, row(g_attn_out[l]), row(g_ssm_out[l]), row(g_post_mix[l]),
                              w_glu_b, row(b_glu[l]), w_out_b, w_mlp_in[l],
                              bsz, seq)
        x2 = _mlp(x2, mod_late, row(g_pre_mlp[l]), row(g_post_mlp[l]), w_mlp_in_b, w_mlp_out_b, seq)
    return x2.reshape(bsz, seq, d)
```

```python
import functools
import math

import jax
import jax.numpy as jnp
from jax import lax
from jax.experimental import pallas as pl
from jax.experimental.pallas import tpu as pltpu

F32 = jnp.float32
BF16 = jnp.bfloat16

LANES = 128
SUBLANES = 8

HEAD_DIM = 64
ROT_DIM = HEAD_DIM // 4
ROPE_THETA = 500000.0
DILATIONS = (1, 4, 16)
HEADS_PER_GROUP = 6
GROUP_WIDTH = HEADS_PER_GROUP * HEAD_DIM
GROUP_SLABS = GROUP_WIDTH // LANES
ATT_BLK = 128
UNITS_PER_TRIP = 32
SSM_P = 16
SSM_N = 64
SSM_CHUNK = 16
CHUNK_W = SSM_CHUNK * SSM_P
SLOTS = LANES // SSM_P
N_MOD = 6
EPS = 1e-6
NEG = -1e30
TOK_TILE = 512
ROW_CHUNK = 128


MIB = 1024 * 1024
VMEM_MIB_V7X = 64
VMEM_MIB = dict(adaln=40, inproj=58, attn=48, ssm=48, mix=56, mlp=60)


def _cparams(sem, kernel_name):
    assert VMEM_MIB[kernel_name] < VMEM_MIB_V7X
    return pltpu.CompilerParams(dimension_semantics=sem, vmem_limit_bytes=VMEM_MIB[kernel_name] * MIB)


def _resident(shape):
    nd = len(shape)
    return pl.BlockSpec(shape, lambda *_: (0,) * nd, pipeline_mode=pl.Buffered(1))


def _slab(j):
    return slice(j * LANES, (j + 1) * LANES)


def _slot_transpose(vs):
    slot = lax.broadcasted_iota(jnp.int32, vs[0].shape, 1) // SSM_P
    vs = list(vs)
    for dist in (4, 2, 1):
        low = (slot & dist) == 0
        nxt = list(vs)
        for i in range(SLOTS):
            if i & dist == 0:
                a, b = vs[i], vs[i + dist]
                nxt[i] = jnp.where(low, a, pltpu.roll(b, dist * SSM_P, axis=1))
                nxt[i + dist] = jnp.where(low, pltpu.roll(a, LANES - dist * SSM_P, axis=1), b)
        vs = nxt
    return vs


def _adaln_kernel(s_ref, w_ref, b_ref, o_ref):
    o_ref[...] = jnp.dot(s_ref[...], w_ref[...].astype(BF16),
                         preferred_element_type=F32) + b_ref[...]


N_MOD_EARLY = 2


def _adaln_early(s, w_ada, b_ada, bsz):
    d = s.shape[1]
    n = N_MOD_EARLY * d
    tn = 1024
    out = pl.pallas_call(
        _adaln_kernel,
        grid=(n // tn,),
        in_specs=[pl.BlockSpec((SUBLANES, d), lambda j: (0, 0)),
                  pl.BlockSpec((d, tn), lambda j: (0, j)),
                  pl.BlockSpec((1, tn), lambda j: (0, j))],
        out_specs=pl.BlockSpec((SUBLANES, tn), lambda j: (0, j)),
        out_shape=jax.ShapeDtypeStruct((SUBLANES, n), F32),
        compiler_params=_cparams(("arbitrary",), "adaln"),
        name="adaln",
    )(s, w_ada, b_ada)
    return out[:bsz].reshape(bsz, N_MOD_EARLY, d)


def _inproj_kernel(x_ref, mod_ref, g_ref, w_ref, cs_ref, rx_ref, r1_ref, wside_ref,
                   sblk_ref, wa_lo_ref, wa_hi_ref, ba_ref,
                   q1_ref, q4_ref, q16_ref, k1_ref, k4_ref, k16_ref, v1_ref, v4_ref, v16_ref,
                   uc_ref, wside_out_ref, modr_ref, sc_ref, s4_ref, *, tm):
    wside_out_ref[...] = wside_ref[...].astype(BF16)

    @pl.when(pl.program_id(0) == 0)
    def _():
        modr_ref[...] = jnp.broadcast_to(ba_ref[...], modr_ref.shape)
    sblk = sblk_ref[...]
    half_w = wa_lo_ref.shape[1]
    modr_ref[:, :half_w] += jnp.dot(sblk, wa_lo_ref[...].astype(BF16), preferred_element_type=F32)
    modr_ref[:, half_w:] += jnp.dot(sblk, wa_hi_ref[...].astype(BF16), preferred_element_type=F32)

    x = x_ref[...]
    ms = jnp.mean(x * x, axis=-1, keepdims=True)
    scale = g_ref[...] * (1.0 + mod_ref[1:2, :])
    h = (x * lax.rsqrt(ms + EPS)) * scale + mod_ref[0:1, :]
    hb = h.astype(BF16)

    qk_slabs = 4 * GROUP_SLABS
    quarter = tm // 4

    def by_fours(j, t4):
        pieces = [sc_ref[j, pl.ds(r4, quarter, stride=4), :] for r4 in range(4)]
        for r4, piece in enumerate(pieces):
            s4_ref[t4, r4 * quarter:(r4 + 1) * quarter, :] = piece
        return pieces

    def rows16(t4, r16, start, size):
        r4, a = r16 % 4, r16 // 4
        return s4_ref[t4, pl.ds(r4 * quarter + 4 * start + a, size, stride=4), :]

    def scatter(j, jj, refs, val, t4=None):
        refs = dict(refs)
        if 1 in refs:
            refs[1][:, _slab(jj)] = val.astype(BF16)
        if 16 in refs:
            pieces = by_fours(j, t4)
            for r16 in range(16):
                refs[16][r16, :, _slab(jj)] = rows16(t4, r16, 0, tm // 16).astype(BF16)
        elif 4 in refs:
            pieces = [sc_ref[j, pl.ds(r4, quarter, stride=4), :] for r4 in range(4)]
        if 4 in refs:
            for r4, piece in enumerate(pieces):
                refs[4][r4, :, _slab(jj)] = piece.astype(BF16)

    pvu = jnp.dot(hb, w_ref[:, qk_slabs * LANES:], preferred_element_type=F32)
    for j in range(pvu.shape[1] // LANES):
        sc_ref[qk_slabs + j] = pvu[:, _slab(j)]
    for jj in range(GROUP_SLABS):
        scatter(qk_slabs + jj, jj, ((1, v1_ref), (4, v4_ref), (16, v16_ref)), pvu[:, _slab(jj)], t4=jj)

    u0 = qk_slabs + GROUP_SLABS
    ngb, nc = uc_ref.shape[0] // SLOTS, tm // SSM_CHUNK
    for gb in range(ngb):
        by_fours(u0 + gb, GROUP_SLABS + gb)
    vs = [jnp.concatenate([rows16(GROUP_SLABS + gb, half * SLOTS + s, cb * SUBLANES, SUBLANES)
                           for gb in range(ngb) for half in range(2) for cb in range(nc // SUBLANES)],
                          axis=0) for s in range(SLOTS)]
    for g, o in enumerate(_slot_transpose(vs)):
        for gb in range(ngb):
            r0 = gb * 2 * nc
            full = jnp.concatenate([o[r0:r0 + nc], o[r0 + nc:r0 + 2 * nc]], axis=1)
            uc_ref[gb * SLOTS + g] = full.astype(BF16)

    cs = cs_ref[...]
    cs_hi = cs.astype(BF16)
    cs_lo = (cs - cs_hi.astype(F32)).astype(BF16)
    rows_contract = (((0,), (0,)), ((), ()))
    tab = lax.dot_general(cs_hi, rx_ref[...], rows_contract, preferred_element_type=F32) + \
        lax.dot_general(cs_lo, rx_ref[...], rows_contract, preferred_element_type=F32)
    ct = tab[:, :LANES] + r1_ref[...]
    st = tab[:, LANES:]
    q_scale = math.log2(math.e) / math.sqrt(HEAD_DIM)
    ct_q, st_q = ct * q_scale, st * q_scale
    lane = lax.broadcasted_iota(jnp.int32, ct.shape, 1) % HEAD_DIM
    second_half = (lane >= ROT_DIM // 2) & (lane < ROT_DIM)
    q_refs = ((1, q1_ref),), ((4, q4_ref),), ((16, q16_ref),)
    chunk_slabs = 4
    for c0 in reversed(range(0, qk_slabs, chunk_slabs)):
        pc = jnp.dot(hb, w_ref[:, c0 * LANES:(c0 + chunk_slabs) * LANES], preferred_element_type=F32)
        for s in range(chunk_slabs):
            j = c0 + s
            grp, jj = divmod(j, GROUP_SLABS)
            t = pc[:, _slab(s)]
            partner = jnp.where(second_half,
                                pltpu.roll(t, ROT_DIM // 2, axis=1),
                                pltpu.roll(t, LANES - ROT_DIM // 2, axis=1))
            r = t * ct_q + partner * st_q if grp < 3 else t * ct + partner * st
            if grp > 0:
                sc_ref[j] = r
            scatter(j, jj, q_refs[grp] if grp < 3 else ((1, k1_ref), (4, k4_ref), (16, k16_ref)), r,
                    t4=GROUP_SLABS + ngb + (grp - 2) * GROUP_SLABS + jj if grp >= 2 else None)


def _rope_spread():
    half = ROT_DIM // 2
    dim = jnp.arange(LANES)[None, :] % HEAD_DIM
    i = jnp.arange(half)[:, None]
    first, second = dim == i, dim == i + half
    ct_rows = (first | second).astype(F32)
    st_rows = second.astype(F32) - first.astype(F32)
    zero = jnp.zeros_like(ct_rows)
    spread = jnp.concatenate([jnp.concatenate([ct_rows, zero], axis=1),
                              jnp.concatenate([zero, st_rows], axis=1)], axis=0)
    return spread.astype(BF16), (dim >= ROT_DIM).astype(F32)


def _side_cast_spec(w, steps):
    rows, cols = w.shape
    assert rows % (steps * 2 * SUBLANES) == 0
    return (pl.BlockSpec((rows // steps, cols), lambda i: (i, 0)),
            jax.ShapeDtypeStruct(w.shape, BF16))


def _inproj(x2, mod, g_pre, w_in_b, cos_sin, w_side, s, w_ada, b_ada, bsz, seq):
    t_tok, d = x2.shape
    in_w = w_in_b.shape[1]
    ssm_w = in_w - 5 * GROUP_WIDTH
    groups = ssm_w // SSM_P
    tm = TOK_TILE
    per_b = seq // tm
    steps = t_tok // tm
    tok = lambda w: pl.BlockSpec((tm, w), lambda i: (i, 0))
    side_spec, side_shape = _side_cast_spec(w_side, steps)
    kblk = d // steps
    late_w = (N_MOD - N_MOD_EARLY) * d
    assert d % steps == 0 and N_MOD_EARLY * d == late_w // 2
    s_blocks = s.reshape(SUBLANES, steps, kblk).transpose(1, 0, 2)
    wa_spec = lambda half: pl.BlockSpec((kblk, late_w // 2), lambda i: (i, 1 + half))

    def res_spec(dil):
        return pl.BlockSpec((None, dil, tm // dil, GROUP_WIDTH), lambda i: (i // per_b, 0, i % per_b, 0))

    def res_shape(dil):
        return jax.ShapeDtypeStruct((bsz, dil, seq // dil, GROUP_WIDTH), BF16)

    nat = jax.ShapeDtypeStruct((t_tok, GROUP_WIDTH), BF16)
    out_shape = [nat, res_shape(4), res_shape(16)] * 3 + \
                [jax.ShapeDtypeStruct((groups, t_tok // SSM_CHUNK, CHUNK_W), BF16), side_shape,
                 jax.ShapeDtypeStruct((SUBLANES, late_w), F32)]
    out_specs = [tok(GROUP_WIDTH), res_spec(4), res_spec(16)] * 3 + \
                [pl.BlockSpec((groups, tm // SSM_CHUNK, CHUNK_W), lambda i: (0, i, 0)), side_spec,
                 pl.BlockSpec((SUBLANES, late_w), lambda i: (0, 0))]
    outs = pl.pallas_call(
        functools.partial(_inproj_kernel, tm=tm),
        grid=(steps,),
        in_specs=[tok(d),
                  pl.BlockSpec((None, N_MOD_EARLY, d), lambda i: (i // per_b, 0, 0)),
                  _resident((1, d)),
                  _resident((d, in_w)),
                  pl.BlockSpec((ROT_DIM, tm), lambda i: (0, i)),
                  _resident((ROT_DIM, 2 * LANES)), _resident((1, LANES)),
                  side_spec,
                  pl.BlockSpec((None, SUBLANES, kblk), lambda i: (i, 0, 0)),
                  wa_spec(0), wa_spec(1), _resident((1, late_w))],
        out_specs=out_specs,
        out_shape=out_shape,
        scratch_shapes=[pltpu.VMEM((in_w // LANES, tm, LANES), F32),
                        pltpu.VMEM((3 * GROUP_SLABS + ssm_w // LANES, tm, LANES), F32)],
        compiler_params=_cparams(("arbitrary",), "inproj"),
        name="inproj",
    )(x2, mod, g_pre, w_in_b, cos_sin, *_rope_spread(), w_side, s_blocks, w_ada, w_ada,
      b_ada[:, N_MOD_EARLY * d:])
    mod_late = outs[-1][:bsz].reshape(bsz, N_MOD - N_MOD_EARLY, d)
    return outs[:-1], mod_late


def _stat_lanes(h):
    a, b = slice(h, h + 1), slice(HEAD_DIM + h, HEAD_DIM + h + 1)
    return (a, b) if h % 2 == 0 else (b, a)


def _attn_kernel(*refs, nblk, units, side):
    if side:
        q_ref, k_ref, v_ref, wside_ref, o_ref, s_ref, wside_out_ref, sc_ref, pr_ref, mk_ref = refs
        wside_out_ref[...] = wside_ref[...].astype(BF16)
    else:
        q_ref, k_ref, v_ref, o_ref, s_ref, sc_ref, pr_ref, mk_ref = refs

    pair = 2 * ATT_BLK
    qi = lax.broadcasted_iota(jnp.int32, (pair, 2 * ATT_BLK), 0) % ATT_BLK
    kj = lax.broadcasted_iota(jnp.int32, (pair, 2 * ATT_BLK), 1)
    rel = qi - kj
    mk_ref[0] = jnp.where((rel >= 0) & (rel <= ATT_BLK), 0.0, NEG)
    mk_ref[1] = jnp.where((rel + ATT_BLK >= 0) & (rel <= 0), 0.0, NEG)
    low = lax.broadcasted_iota(jnp.int32, (ATT_BLK, LANES), 1) < HEAD_DIM
    top, bot = slice(0, ATT_BLK), slice(ATT_BLK, pair)
    ones = jnp.ones((2 * ATT_BLK, LANES), BF16)

    def coords(u):
        u = jnp.minimum(u, units - 1)
        r, i = u // nblk, u % nblk
        rows = pl.ds(pl.multiple_of(i * ATT_BLK, ATT_BLK), ATT_BLK)
        krows = pl.ds(pl.multiple_of(jnp.maximum(i - 1, 0) * ATT_BLK, ATT_BLK), 2 * ATT_BLK)
        return r, i, rows, krows

    def scores(u, slot):
        r, i, rows, krows = coords(u)
        bias = mk_ref[jnp.minimum(i, 1)]
        for jj in range(GROUP_SLABS):
            q2 = q_ref[r, rows, _slab(jj)]
            zero = jnp.zeros_like(q2)
            qp = jnp.concatenate([jnp.where(low, q2, zero), jnp.where(low, zero, q2)], axis=0)
            s = lax.dot_general(qp, k_ref[r, krows, _slab(jj)], (((1,), (1,)), ((), ())),
                                preferred_element_type=F32)
            sc_ref[slot, jj] = s + bias

    def softmax(u, slot):
        r, _, rows, _ = coords(u)
        s_ref[r, rows, :] = jnp.zeros((ATT_BLK, LANES), F32)
        for jj in range(GROUP_SLABS):
            s = sc_ref[slot, jj]
            m = jnp.max(s, axis=1, keepdims=True)
            pr_ref[slot, jj] = jnp.exp2(s - m).astype(BF16)
            s_ref[r, rows, _stat_lanes(2 * jj)[0]] = m[top]
            s_ref[r, rows, _stat_lanes(2 * jj + 1)[0]] = m[bot]

    def values(u, slot):
        r, _, rows, krows = coords(u)
        for jj in range(GROUP_SLABS):
            v_ext = jnp.concatenate([v_ref[r, krows, _slab(jj)], ones], axis=1)
            oe = jnp.dot(pr_ref[slot, jj], v_ext, preferred_element_type=F32)
            l_even, l_odd = _stat_lanes(2 * jj)[1], _stat_lanes(2 * jj + 1)[1]
            s_ref[r, rows, l_even] = oe[top, LANES:][:, l_even]
            s_ref[r, rows, l_odd] = oe[bot, LANES:][:, l_odd]
            o_ref[r, rows, _slab(jj)] = jnp.where(low, oe[top, :LANES], oe[bot, :LANES]).astype(o_ref.dtype)

    scores(0, 0)
    softmax(0, 0)
    scores(1, 1)

    def body(t, carry):
        u = UNITS_PER_TRIP * t
        for k in range(UNITS_PER_TRIP):
            scores(u + k + 2, k % 2)
            softmax(u + k + 1, (k + 1) % 2)
            values(u + k, k % 2)
        return carry

    assert UNITS_PER_TRIP % 2 == 0 and units % UNITS_PER_TRIP == 0
    lax.fori_loop(0, units // UNITS_PER_TRIP, body, 0)


def _attn_group(q, k, v, dilation, bsz, seq, w_side=None):
    n = seq // dilation
    nblk = n // ATT_BLK
    spec = lambda w: pl.BlockSpec((None, dilation, n, w), lambda b: (b, 0, 0, 0))
    win = (2 * ATT_BLK, 2 * ATT_BLK)
    ins, in_specs = [q, k, v], [spec(GROUP_WIDTH)] * 3
    out_specs = [spec(GROUP_WIDTH), spec(LANES)]
    out_shape = [jax.ShapeDtypeStruct((bsz, dilation, n, GROUP_WIDTH), BF16),
                 jax.ShapeDtypeStruct((bsz, dilation, n, LANES), F32)]
    if w_side is not None:
        side_spec, side_shape = _side_cast_spec(w_side, bsz)
        ins, in_specs = ins + [w_side], in_specs + [side_spec]
        out_specs, out_shape = out_specs + [side_spec], out_shape + [side_shape]
    return pl.pallas_call(
        functools.partial(_attn_kernel, nblk=nblk, units=dilation * nblk, side=w_side is not None),
        grid=(bsz,),
        in_specs=in_specs,
        out_specs=out_specs,
        out_shape=out_shape,
        scratch_shapes=[pltpu.VMEM((2, GROUP_SLABS) + win, F32),
                        pltpu.VMEM((2, GROUP_SLABS) + win, BF16),
                        pltpu.VMEM((2,) + win, F32)],
        compiler_params=_cparams(("parallel",), "attn"),
        name=f"attn_d{dilation}",
    )(*ins)


def _cmul(a, b):
    return a[0] * b[0] - a[1] * b[1], a[0] * b[1] + a[1] * b[0]


def _ssm_tables(a_re, a_im, log_dt, b_re, b_im, c_re, c_im, d_skip):
    f = lambda t: t.astype(F32)
    lam = (f(a_re), f(a_im))
    dt = jnp.exp(f(log_dt))[:, None]
    g = a_re.shape[0]
    taus = jnp.arange(2 * SSM_CHUNK + 1, dtype=F32)[None, :, None]
    mag = jnp.exp(lam[0][:, None, :] * dt[:, None, :] * taus)
    ang = lam[1][:, None, :] * dt[:, None, :] * taus
    apow = (mag * jnp.cos(ang), mag * jnp.sin(ang))
    a_bar = (apow[0][:, 1], apow[1][:, 1])
    den = lam[0] * lam[0] + lam[1] * lam[1]
    coef = _cmul((a_bar[0] - 1.0, a_bar[1]), (lam[0] / den, -lam[1] / den))
    b_bar = _cmul((coef[0][..., None], coef[1][..., None]), (f(b_re), f(b_im)))
    c_mat = (f(c_re), f(c_im))

    pair = lambda lo, hi: jnp.concatenate([lo, hi], axis=-1)
    r_re, r_im = (t[:, SSM_CHUNK - 1::-1][:, :SSM_CHUNK] for t in apow)
    n_re, n_im = (t[:, 1:SSM_CHUNK + 1] for t in apow)
    k_re, k_im = (t[:, :SSM_CHUNK] for t in apow)
    bt_re, bt_im = (t.transpose(0, 2, 1) for t in b_bar)
    fac = jnp.stack([pair(r_re, r_re), pair(-r_im, r_im),
                     pair(n_re, -n_re), pair(-n_im, -n_im),
                     pair(k_re, k_re), pair(-k_im, k_im),
                     pair(bt_re, bt_im), pair(c_mat[0], c_mat[1]),
                     pair(bt_re, -bt_im)], axis=1)
    d_lane = jnp.pad(f(d_skip), ((0, 0), (0, CHUNK_W - SSM_P)))[:, None, :]

    def mul_rows(tau):
        re, im = apow[0][:, tau], apow[1][:, tau]
        return [pair(re, re), pair(-im, im), pair(im, -im)]
    ptab = jnp.stack(mul_rows(SSM_CHUNK) + mul_rows(2 * SSM_CHUNK) +
                     [jnp.zeros((g, LANES), F32)] * 2, axis=1)
    return fac, d_lane, ptab


def _chunk_matrices(f_ref, d_ref, g):
    b, c, b_conj = f_ref[g, 6], f_ref[g, 7], f_ref[g, 8]
    swapped = lambda x: pltpu.roll(x, SSM_N, axis=1)

    def outer(k, x):
        a1, a2, xs = f_ref[g, k], f_ref[g, k + 1], swapped(x)
        return jnp.concatenate([a1[i:i + 1] * x + a2[i:i + 1] * xs for i in range(SSM_CHUNK)], axis=0)

    w = outer(0, b).astype(BF16)
    v = outer(2, c).T.astype(BF16)
    ca = outer(4, c)
    m_row = lax.dot_general(b_conj, ca, (((1,), (1,)), ((), ())), precision=lax.Precision.HIGHEST,
                            preferred_element_type=F32)
    q = lax.broadcasted_iota(jnp.int32, m_row.shape, 0)
    lane = lax.broadcasted_iota(jnp.int32, m_row.shape, 1)
    m_row = m_row + jnp.where(lane == q, d_ref[g], 0.0)
    return w, v, _toeplitz(m_row)


def _toeplitz(m_row):
    lane = lax.broadcasted_iota(jnp.int32, m_row.shape, 1)
    blocks = [m_row]
    for s in range(1, SSM_CHUNK):
        blocks.append(jnp.where(lane >= s * SSM_P, pltpu.roll(m_row, s * SSM_P, axis=1), 0.0))
    return jnp.concatenate(blocks, axis=0).astype(BF16)


def _ssm_kernel(u_ref, f_ref, d_ref, p_ref, y_ref, gx_ref, gz_ref, h_ref, *, groups, nchunk, bsz):
    rows = nchunk * bsz
    row = lax.broadcasted_iota(jnp.int32, (rows, LANES), 0)
    first_chunk = (row % nchunk) == 0

    def prev_chunk(a):
        return jnp.where(first_chunk, 0.0, pltpu.roll(a, 1, axis=0))

    mats = [_chunk_matrices(f_ref, d_ref, g) for g in range(groups)]

    for g in range(groups):
        ex = jnp.dot(u_ref[g], mats[g][0], preferred_element_type=F32)
        ez = pltpu.roll(ex, SSM_N, axis=1)
        exs, ezs = prev_chunk(ex), prev_chunk(ez)
        p1, p2, p3 = p_ref[g, 0:1, :], p_ref[g, 1:2, :], p_ref[g, 2:3, :]
        gx = ex + p1 * exs + p2 * ezs
        gz = ez + p1 * ezs + p3 * exs
        for b in range(bsz):
            gx_ref[g, pl.ds(b, nchunk, stride=bsz), :] = gx[b * nchunk:(b + 1) * nchunk]
            gz_ref[g, pl.ds(b, nchunk, stride=bsz), :] = gz[b * nchunk:(b + 1) * nchunk]

    y_intra = [jnp.dot(u_ref[g], mats[g][2], preferred_element_type=F32) for g in range(groups)]

    tile = (SUBLANES, LANES)
    mults = [[jnp.broadcast_to(p_ref[g, r:r + 1, :], tile) for r in (3, 4, 5)] for g in range(groups)]
    zero = jnp.zeros(tile, F32)
    state = [(zero, zero) for _ in range(groups)]
    for j in range(rows // SUBLANES):
        r0 = j * SUBLANES
        for g in range(groups):
            x, z = state[g]
            q1, q2, q3 = mults[g]
            xn = q1 * x + q2 * z + gx_ref[g, r0:r0 + SUBLANES, :]
            zn = q1 * z + q3 * x + gz_ref[g, r0:r0 + SUBLANES, :]
            h_ref[g, r0:r0 + SUBLANES, :] = xn
            state[g] = (xn, zn)

    for g in range(groups):
        h_end = jnp.concatenate([h_ref[g, pl.ds(b, nchunk, stride=bsz), :] for b in range(bsz)], axis=0)
        h_in = prev_chunk(h_end).astype(BF16)
        y = y_intra[g] + jnp.dot(h_in, mats[g][1], preferred_element_type=F32)
        y_ref[g] = y.astype(y_ref.dtype)


def _ssm(uc, tables, bsz, seq):
    fac, d_lane, ptab = tables
    g, rows, _ = uc.shape
    nchunk = seq // SSM_CHUNK
    assert rows == nchunk * bsz and SUBLANES % bsz == 0 and SUBLANES // bsz == 2
    gs = 4
    blk = lambda a, b: pl.BlockSpec((gs, a, b), lambda i: (i, 0, 0))
    return pl.pallas_call(
        functools.partial(_ssm_kernel, groups=gs, nchunk=nchunk, bsz=bsz),
        grid=(g // gs,),
        in_specs=[blk(rows, CHUNK_W),
                  pl.BlockSpec((gs,) + fac.shape[1:], lambda i: (i, 0, 0, 0)),
                  blk(1, CHUNK_W), blk(SUBLANES, LANES)],
        out_specs=blk(rows, CHUNK_W),
        out_shape=jax.ShapeDtypeStruct((g, rows, CHUNK_W), BF16),
        scratch_shapes=[pltpu.VMEM((gs, rows, LANES), F32)] * 3,
        compiler_params=_cparams(("parallel",), "ssm"),
        name="ssm",
    )(uc, fac, d_lane, ptab)


def _rms(x):
    return x * lax.rsqrt(jnp.mean(x * x, axis=-1, keepdims=True) + EPS)


def _gelu_tanh(x):
    k = -2.0 * math.sqrt(2.0 / math.pi) * math.log2(math.e)
    e = jnp.exp2(x * (k + (k * 0.044715) * (x * x)))
    return x * (1.0 / (1.0 + e))


def _mix_kernel(o1_ref, l1_ref, o4_ref, l4_ref, o16_ref, l16_ref, y_ref, x_ref, mod_ref,
                ga_ref, gs_ref, gp_ref, ex_ref, wg_ref, bg_ref, wo_ref, wside_ref, out_ref,
                wside_out_ref, so_ref, sl_ref, sy_ref, s4_ref, *, tm):
    wside_out_ref[...] = wside_ref[...].astype(BF16)

    quarter = tm // 4

    def put16(t4, r16, start, value):
        r4, a = r16 % 4, r16 // 4
        s4_ref[t4, pl.ds(r4 * quarter + 4 * start + a, value.shape[0], stride=4), :] = value

    def from_fours(t4, dst_ref, j):
        for r4 in range(4):
            dst_ref[j, pl.ds(r4, quarter, stride=4), :] = s4_ref[t4, r4 * quarter:(r4 + 1) * quarter, :]

    for r in range(4):
        rows = pl.ds(r, quarter, stride=4)
        for jj in range(GROUP_SLABS):
            so_ref[jj, rows, :] = o4_ref[r, :, _slab(jj)].astype(F32)
        sl_ref[0, rows, :] = l4_ref[r]
    for r in range(16):
        for jj in range(GROUP_SLABS):
            put16(jj, r, 0, o16_ref[r, :, _slab(jj)].astype(F32))
        put16(GROUP_SLABS, r, 0, l16_ref[r])
    for jj in range(GROUP_SLABS):
        from_fours(jj, so_ref, GROUP_SLABS + jj)
    from_fours(GROUP_SLABS, sl_ref, 1)

    ngb, nc = y_ref.shape[0] // SLOTS, tm // SSM_CHUNK
    vs = [jnp.concatenate([y_ref[gb * SLOTS + g, :, _slab(half)].astype(F32)
                           for gb in range(ngb) for half in range(2)], axis=0)
          for g in range(SLOTS)]
    for s, o in enumerate(_slot_transpose(vs)):
        for gb in range(ngb):
            for half in range(2):
                r0 = (gb * 2 + half) * nc
                put16(GROUP_SLABS + 1 + gb, half * SLOTS + s, 0, o[r0:r0 + nc])
    for gb in range(ngb):
        from_fours(GROUP_SLABS + 1 + gb, sy_ref, gb)

    lane = lax.broadcasted_iota(jnp.int32, (tm, LANES), 1)
    even = (lane & 1) == 0
    ms, ls = [], []
    for st in (l1_ref[...], sl_ref[0], sl_ref[1]):
        sw = pltpu.roll(st, HEAD_DIM, axis=1)
        ms.append(jnp.where(even, st, sw))
        ls.append(jnp.where(even, sw, st))
    mx = jnp.maximum(jnp.maximum(ms[0], ms[1]), ms[2])
    es = [jnp.exp2(m - mx) for m in ms]
    inv = 1.0 / (es[0] * ls[0] + es[1] * ls[1] + es[2] * ls[2])
    head_lane = lane < HEADS_PER_GROUP
    wexp = [jnp.dot(jnp.where(head_lane, e * inv, 0.0).astype(BF16), ex_ref[...],
                    preferred_element_type=F32) for e in es]
    att_slabs = []
    for jj in range(GROUP_SLABS):
        a = wexp[0][:, _slab(jj)] * o1_ref[:, _slab(jj)].astype(F32)
        a = a + wexp[1][:, _slab(jj)] * so_ref[jj]
        a = a + wexp[2][:, _slab(jj)] * so_ref[GROUP_SLABS + jj]
        att_slabs.append(a)
    att = _rms(jnp.concatenate(att_slabs, axis=1)) * ga_ref[...]

    yv = jnp.concatenate([sy_ref[gb] for gb in range(sy_ref.shape[0])], axis=1)
    s = _gelu_tanh(yv)
    z = jnp.dot(s.astype(BF16), wg_ref[...], preferred_element_type=F32) + bg_ref[...]
    s = s * jax.nn.sigmoid(z)
    ssm = _rms(s) * gs_ref[...]
    aw = att.shape[1]
    mix = jnp.dot(att.astype(BF16), wo_ref[:aw, :], preferred_element_type=F32)
    mix = mix + jnp.dot(ssm.astype(BF16), wo_ref[aw:, :], preferred_element_type=F32)
    out_ref[...] = x_ref[...] + (mod_ref[0:1, :] * gp_ref[...]) * _rms(mix)


def _mix(o1, l1, o4, l4, o16, l16, yc, x2, mod, g_attn, g_ssm, g_post, w_glu_b, b_glu, w_out_b, w_side,
         bsz, seq):
    t_tok, d = x2.shape
    groups = yc.shape[0]
    sw = groups * SSM_P
    aw = GROUP_WIDTH
    tm = TOK_TILE
    per_b = seq // tm
    tok = lambda w: pl.BlockSpec((tm, w), lambda i: (i, 0))
    res = lambda dil, w: pl.BlockSpec((None, dil, tm // dil, w), lambda i: (i // per_b, 0, i % per_b, 0))
    head_of_lane = jnp.arange(aw)[None, :] // HEAD_DIM
    expand = (jnp.arange(LANES)[:, None] == head_of_lane).astype(BF16)
    side_spec, side_shape = _side_cast_spec(w_side, t_tok // tm)
    return pl.pallas_call(
        functools.partial(_mix_kernel, tm=tm),
        grid=(t_tok // tm,),
        in_specs=[tok(aw), tok(LANES), res(4, aw), res(4, LANES), res(16, aw), res(16, LANES),
                  pl.BlockSpec((groups, tm // SSM_CHUNK, CHUNK_W), lambda i: (0, i, 0)),
                  tok(d),
                  pl.BlockSpec((None, N_MOD - N_MOD_EARLY, d), lambda i: (i // per_b, 0, 0)),
                  _resident((1, aw)), _resident((1, sw)), _resident((1, d)),
                  _resident((LANES, aw)),
                  _resident((sw, sw)), _resident((1, sw)), _resident((aw + sw, d)),
                  side_spec],
        out_specs=[tok(d), side_spec],
        out_shape=[jax.ShapeDtypeStruct((t_tok, d), F32), side_shape],
        scratch_shapes=[pltpu.VMEM((2 * GROUP_SLABS, tm, LANES), F32),
                        pltpu.VMEM((2, tm, LANES), F32),
                        pltpu.VMEM((sw // LANES, tm, LANES), F32),
                        pltpu.VMEM((GROUP_SLABS + 1 + sw // LANES, tm, LANES), F32)],
        compiler_params=_cparams(("parallel",), "mix"),
        name="mix",
    )(o1, l1, o4, l4, o16, l16, yc, x2, mod, g_attn, g_ssm, g_post, expand, w_glu_b, b_glu, w_out_b, w_side)


def _mlp_kernel(x_ref, mod_ref, gpre_ref, gpost_ref, w1_ref, w2_ref, o_ref, h_ref):
    f = pl.program_id(1)
    last = pl.num_programs(1) - 1
    chunks = [slice(r, r + ROW_CHUNK) for r in range(0, x_ref.shape[0], ROW_CHUNK)]

    def partial():
        t = jnp.dot(h_ref[...], w1_ref[...], preferred_element_type=F32)
        t = jnp.square(jnp.maximum(t, 0.0)).astype(BF16)
        return jnp.dot(t, w2_ref[...], preferred_element_type=F32)

    @pl.when(f == 0)
    def _():
        scale = gpre_ref[...] * (1.0 + mod_ref[2:3, :])
        shift = mod_ref[1:2, :]
        for rows in chunks:
            h_ref[rows, :] = (_rms(x_ref[rows, :]) * scale + shift).astype(BF16)
        o_ref[...] = partial()

    @pl.when((f > 0) & (f < last))
    def _():
        o_ref[...] += partial()

    @pl.when(f == last)
    def _():
        gate = mod_ref[3:4, :] * gpost_ref[...]
        acc = o_ref[...] + partial()
        for rows in chunks:
            o_ref[rows, :] = x_ref[rows, :] + gate * _rms(acc[rows, :])


def _mlp(x2, mod, g_pre, g_post, w1_b, w2_b, seq):
    t_tok, d = x2.shape
    d_ff = w1_b.shape[1]
    tm, tf = 512, 2048
    assert d_ff // tf >= 2
    per_b = seq // tm
    return pl.pallas_call(
        _mlp_kernel,
        grid=(t_tok // tm, d_ff // tf),
        in_specs=[pl.BlockSpec((tm, d), lambda i, f: (i, 0)),
                  pl.BlockSpec((None, N_MOD - N_MOD_EARLY, d), lambda i, f: (i // per_b, 0, 0)),
                  _resident((1, d)), _resident((1, d)),
                  pl.BlockSpec((d, tf), lambda i, f: (0, f)),
                  pl.BlockSpec((tf, d), lambda i, f: (f, 0))],
        out_specs=pl.BlockSpec((tm, d), lambda i, f: (i, 0)),
        out_shape=jax.ShapeDtypeStruct((t_tok, d), F32),
        scratch_shapes=[pltpu.VMEM((tm, d), BF16)],
        compiler_params=_cparams(("parallel", "arbitrary"), "mlp"),
        name="mlp",
    )(x2, mod, g_pre, g_post, w1_b, w2_b)


def _rope_cos_sin(positions):
    freqs = ROPE_THETA ** (-jnp.arange(0, ROT_DIM, 2, dtype=F32) / ROT_DIM)
    ang = positions.reshape(-1).astype(F32)[None, :] * freqs[:, None]
    return jnp.concatenate([jnp.cos(ang), jnp.sin(ang)], axis=0)


def kernel(x, c, positions, w_ada, b_ada, g_pre_mix, g_post_mix, w_in, ssm_a_re, ssm_a_im, ssm_log_dt,
           ssm_b_re, ssm_b_im, ssm_c_re, ssm_c_im, ssm_d, w_glu, b_glu, g_attn_out, g_ssm_out, w_out,
           g_pre_mlp, g_post_mlp, w_mlp_in, w_mlp_out):
    bsz, seq, d = x.shape
    depth = w_ada.shape[0]
    x2 = x.reshape(bsz * seq, d)
    cos_sin = _rope_cos_sin(positions)
    row = lambda a: a.reshape(1, -1).astype(F32)
    for l in range(depth):
        s = jnp.zeros((SUBLANES, d), BF16).at[:bsz].set(jax.nn.silu(c).astype(BF16))
        b_ada_row = row(b_ada[l])
        mod = _adaln_early(s, w_ada[l], b_ada_row, bsz)
        (q1, q4, q16, k1, k4, k16, v1, v4, v16, uc, w_mlp_out_b), mod_late = _inproj(
            x2, mod, row(g_pre_mix[l]), w_in[l].astype(BF16), cos_sin, w_mlp_out[l],
            s, w_ada[l], b_ada_row, bsz, seq)

        nat = lambda a: a.reshape(bsz, 1, seq, a.shape[-1])
        o1, l1, w_out_b = _attn_group(nat(q1), nat(k1), nat(v1), 1, bsz, seq, w_out[l])
        o4, l4, w_glu_b = _attn_group(q4, k4, v4, 4, bsz, seq, w_glu[l])
        o16, l16 = _attn_group(q16, k16, v16, 16, bsz, seq)

        tables = _ssm_tables(ssm_a_re[l], ssm_a_im[l], ssm_log_dt[l], ssm_b_re[l], ssm_b_im[l],
                             ssm_c_re[l], ssm_c_im[l], ssm_d[l])
        yc = _ssm(uc, tables, bsz, seq)

        x2, w_mlp_in_b = _mix(o1.reshape(bsz * seq, -1), l1.reshape(bsz * seq, -1), o4, l4, o16, l16, yc,
                              x2, mod_late, row(g_attn_out[l]), row(g_ssm_out[l]), row(g_post_mix[l]),
                              w_glu_b, row(b_glu[l]), w_out_b, w_mlp_in[l],
                              bsz, seq)
        x2 = _mlp(x2, mod_late, row(g_pre_mlp[l]), row(g_post_mlp[l]), w_mlp_in_b, w_mlp_out_b, seq)
    return x2.reshape(bsz, seq, d)
```

```python
import functools
import math

import jax
import jax.numpy as jnp
from jax import lax
from jax.experimental import pallas as pl
from jax.experimental.pallas import tpu as pltpu

F32 = jnp.float32
BF16 = jnp.bfloat16

LANES = 128
SUBLANES = 8

HEAD_DIM = 64
ROT_DIM = HEAD_DIM // 4
ROPE_THETA = 500000.0
DILATIONS = (1, 4, 16)
HEADS_PER_GROUP = 6
GROUP_WIDTH = HEADS_PER_GROUP * HEAD_DIM
GROUP_SLABS = GROUP_WIDTH // LANES
ATT_BLK = 128
UNITS_PER_TRIP = 32
SSM_P = 16
SSM_N = 64
SSM_CHUNK = 16
CHUNK_W = SSM_CHUNK * SSM_P
SLOTS = LANES // SSM_P
N_MOD = 6
EPS = 1e-6
NEG = -1e30
TOK_TILE = 512
ROW_CHUNK = 128


MIB = 1024 * 1024
VMEM_MIB_V7X = 64
VMEM_MIB = dict(adaln=40, inproj=58, attn=48, ssm=48, mix=56, mlp=60)


def _cparams(sem, kernel_name):
    assert VMEM_MIB[kernel_name] < VMEM_MIB_V7X
    return pltpu.CompilerParams(dimension_semantics=sem, vmem_limit_bytes=VMEM_MIB[kernel_name] * MIB)


def _resident(shape):
    nd = len(shape)
    return pl.BlockSpec(shape, lambda *_: (0,) * nd, pipeline_mode=pl.Buffered(1))


def _slab(j):
    return slice(j * LANES, (j + 1) * LANES)


def _slot_transpose(vs):
    slot = lax.broadcasted_iota(jnp.int32, vs[0].shape, 1) // SSM_P
    vs = list(vs)
    for dist in (4, 2, 1):
        low = (slot & dist) == 0
        nxt = list(vs)
        for i in range(SLOTS):
            if i & dist == 0:
                a, b = vs[i], vs[i + dist]
                nxt[i] = jnp.where(low, a, pltpu.roll(b, dist * SSM_P, axis=1))
                nxt[i + dist] = jnp.where(low, pltpu.roll(a, LANES - dist * SSM_P, axis=1), b)
        vs = nxt
    return vs


def _adaln_kernel(s_ref, w_ref, b_ref, o_ref):
    o_ref[...] = jnp.dot(s_ref[...], w_ref[...].astype(BF16),
                         preferred_element_type=F32) + b_ref[...]


N_MOD_EARLY = 2


def _adaln_early(s, w_ada, b_ada, bsz):
    d = s.shape[1]
    n = N_MOD_EARLY * d
    tn = 1024
    out = pl.pallas_call(
        _adaln_kernel,
        grid=(n // tn,),
        in_specs=[pl.BlockSpec((SUBLANES, d), lambda j: (0, 0)),
                  pl.BlockSpec((d, tn), lambda j: (0, j)),
                  pl.BlockSpec((1, tn), lambda j: (0, j))],
        out_specs=pl.BlockSpec((SUBLANES, tn), lambda j: (0, j)),
        out_shape=jax.ShapeDtypeStruct((SUBLANES, n), F32),
        compiler_params=_cparams(("arbitrary",), "adaln"),
        name="adaln",
    )(s, w_ada, b_ada)
    return out[:bsz].reshape(bsz, N_MOD_EARLY, d)


def _inproj_kernel(x_ref, mod_ref, g_ref, w_ref, cs_ref, rx_ref, r1_ref, wside_ref,
                   sblk_ref, wa_lo_ref, wa_hi_ref, ba_ref,
                   q1_ref, q4_ref, q16_ref, k1_ref, k4_ref, k16_ref, v1_ref, v4_ref, v16_ref,
                   uc_ref, wside_out_ref, modr_ref, sc_ref, s4_ref, *, tm):
    wside_out_ref[...] = wside_ref[...].astype(BF16)

    @pl.when(pl.program_id(0) == 0)
    def _():
        modr_ref[...] = jnp.broadcast_to(ba_ref[...], modr_ref.shape)
    sblk = sblk_ref[...]
    half_w = wa_lo_ref.shape[1]
    modr_ref[:, :half_w] += jnp.dot(sblk, wa_lo_ref[...].astype(BF16), preferred_element_type=F32)
    modr_ref[:, half_w:] += jnp.dot(sblk, wa_hi_ref[...].astype(BF16), preferred_element_type=F32)

    x = x_ref[...]
    ms = jnp.mean(x * x, axis=-1, keepdims=True)
    scale = g_ref[...] * (1.0 + mod_ref[1:2, :])
    h = (x * lax.rsqrt(ms + EPS)) * scale + mod_ref[0:1, :]
    hb = h.astype(BF16)

    qk_slabs = 4 * GROUP_SLABS
    quarter = tm // 4

    def by_fours(j, t4):
        pieces = [sc_ref[j, pl.ds(r4, quarter, stride=4), :] for r4 in range(4)]
        for r4, piece in enumerate(pieces):
            s4_ref[t4, r4 * quarter:(r4 + 1) * quarter, :] = piece
        return pieces

    def rows16(t4, r16, start, size):
        r4, a = r16 % 4, r16 // 4
        return s4_ref[t4, pl.ds(r4 * quarter + 4 * start + a, size, stride=4), :]

    def scatter(j, jj, refs, val, t4=None):
        refs = dict(refs)
        if 1 in refs:
            refs[1][:, _slab(jj)] = val.astype(BF16)
        if 16 in refs:
            pieces = by_fours(j, t4)
            for r16 in range(16):
                refs[16][r16, :, _slab(jj)] = rows16(t4, r16, 0, tm // 16).astype(BF16)
        elif 4 in refs:
            pieces = [sc_ref[j, pl.ds(r4, quarter, stride=4), :] for r4 in range(4)]
        if 4 in refs:
            for r4, piece in enumerate(pieces):
                refs[4][r4, :, _slab(jj)] = piece.astype(BF16)

    pvu = jnp.dot(hb, w_ref[:, qk_slabs * LANES:], preferred_element_type=F32)
    for j in range(pvu.shape[1] // LANES):
        sc_ref[qk_slabs + j] = pvu[:, _slab(j)]
    for jj in range(GROUP_SLABS):
        scatter(qk_slabs + jj, jj, ((1, v1_ref), (4, v4_ref), (16, v16_ref)), pvu[:, _slab(jj)], t4=jj)

    u0 = qk_slabs + GROUP_SLABS
    ngb, nc = uc_ref.shape[0] // SLOTS, tm // SSM_CHUNK
    for gb in range(ngb):
        by_fours(u0 + gb, GROUP_SLABS + gb)
    vs = [jnp.concatenate([rows16(GROUP_SLABS + gb, half * SLOTS + s, cb * SUBLANES, SUBLANES)
                           for gb in range(ngb) for half in range(2) for cb in range(nc // SUBLANES)],
                          axis=0) for s in range(SLOTS)]
    for g, o in enumerate(_slot_transpose(vs)):
        for gb in range(ngb):
            r0 = gb * 2 * nc
            full = jnp.concatenate([o[r0:r0 + nc], o[r0 + nc:r0 + 2 * nc]], axis=1)
            uc_ref[gb * SLOTS + g] = full.astype(BF16)

    cs = cs_ref[...]
    cs_hi = cs.astype(BF16)
    cs_lo = (cs - cs_hi.astype(F32)).astype(BF16)
    rows_contract = (((0,), (0,)), ((), ()))
    tab = lax.dot_general(cs_hi, rx_ref[...], rows_contract, preferred_element_type=F32) + \
        lax.dot_general(cs_lo, rx_ref[...], rows_contract, preferred_element_type=F32)
    ct = tab[:, :LANES] + r1_ref[...]
    st = tab[:, LANES:]
    q_scale = math.log2(math.e) / math.sqrt(HEAD_DIM)
    ct_q, st_q = ct * q_scale, st * q_scale
    lane = lax.broadcasted_iota(jnp.int32, ct.shape, 1) % HEAD_DIM
    second_half = (lane >= ROT_DIM // 2) & (lane < ROT_DIM)
    q_refs = ((1, q1_ref),), ((4, q4_ref),), ((16, q16_ref),)
    chunk_slabs = 4
    for c0 in reversed(range(0, qk_slabs, chunk_slabs)):
        pc = jnp.dot(hb, w_ref[:, c0 * LANES:(c0 + chunk_slabs) * LANES], preferred_element_type=F32)
        for s in range(chunk_slabs):
            j = c0 + s
            grp, jj = divmod(j, GROUP_SLABS)
            t = pc[:, _slab(s)]
            partner = jnp.where(second_half,
                                pltpu.roll(t, ROT_DIM // 2, axis=1),
                                pltpu.roll(t, LANES - ROT_DIM // 2, axis=1))
            r = t * ct_q + partner * st_q if grp < 3 else t * ct + partner * st
            if grp > 0:
                sc_ref[j] = r
            scatter(j, jj, q_refs[grp] if grp < 3 else ((1, k1_ref), (4, k4_ref), (16, k16_ref)), r,
                    t4=GROUP_SLABS + ngb + (grp - 2) * GROUP_SLABS + jj if grp >= 2 else None)


def _rope_spread():
    half = ROT_DIM // 2
    dim = jnp.arange(LANES)[None, :] % HEAD_DIM
    i = jnp.arange(half)[:, None]
    first, second = dim == i, dim == i + half
    ct_rows = (first | second).astype(F32)
    st_rows = second.astype(F32) - first.astype(F32)
    zero = jnp.zeros_like(ct_rows)
    spread = jnp.concatenate([jnp.concatenate([ct_rows, zero], axis=1),
                              jnp.concatenate([zero, st_rows], axis=1)], axis=0)
    return spread.astype(BF16), (dim >= ROT_DIM).astype(F32)


def _side_cast_spec(w, steps):
    rows, cols = w.shape
    assert rows % (steps * 2 * SUBLANES) == 0
    return (pl.BlockSpec((rows // steps, cols), lambda i: (i, 0)),
            jax.ShapeDtypeStruct(w.shape, BF16))


def _inproj(x2, mod, g_pre, w_in_b, cos_sin, w_side, s, w_ada, b_ada, bsz, seq):
    t_tok, d = x2.shape
    in_w = w_in_b.shape[1]
    ssm_w = in_w - 5 * GROUP_WIDTH
    groups = ssm_w // SSM_P
    tm = TOK_TILE
    per_b = seq // tm
    steps = t_tok // tm
    tok = lambda w: pl.BlockSpec((tm, w), lambda i: (i, 0))
    side_spec, side_shape = _side_cast_spec(w_side, steps)
    kblk = d // steps
    late_w = (N_MOD - N_MOD_EARLY) * d
    assert d % steps == 0 and N_MOD_EARLY * d == late_w // 2
    s_blocks = s.reshape(SUBLANES, steps, kblk).transpose(1, 0, 2)
    wa_spec = lambda half: pl.BlockSpec((kblk, late_w // 2), lambda i: (i, 1 + half))

    def res_spec(dil):
        return pl.BlockSpec((None, dil, tm // dil, GROUP_WIDTH), lambda i: (i // per_b, 0, i % per_b, 0))

    def res_shape(dil):
        return jax.ShapeDtypeStruct((bsz, dil, seq // dil, GROUP_WIDTH), BF16)

    nat = jax.ShapeDtypeStruct((t_tok, GROUP_WIDTH), BF16)
    out_shape = [nat, res_shape(4), res_shape(16)] * 3 + \
                [jax.ShapeDtypeStruct((groups, t_tok // SSM_CHUNK, CHUNK_W), BF16), side_shape,
                 jax.ShapeDtypeStruct((SUBLANES, late_w), F32)]
    out_specs = [tok(GROUP_WIDTH), res_spec(4), res_spec(16)] * 3 + \
                [pl.BlockSpec((groups, tm // SSM_CHUNK, CHUNK_W), lambda i: (0, i, 0)), side_spec,
                 pl.BlockSpec((SUBLANES, late_w), lambda i: (0, 0))]
    outs = pl.pallas_call(
        functools.partial(_inproj_kernel, tm=tm),
        grid=(steps,),
        in_specs=[tok(d),
                  pl.BlockSpec((None, N_MOD_EARLY, d), lambda i: (i // per_b, 0, 0)),
                  _resident((1, d)),
                  _resident((d, in_w)),
                  pl.BlockSpec((ROT_DIM, tm), lambda i: (0, i)),
                  _resident((ROT_DIM, 2 * LANES)), _resident((1, LANES)),
                  side_spec,
                  pl.BlockSpec((None, SUBLANES, kblk), lambda i: (i, 0, 0)),
                  wa_spec(0), wa_spec(1), _resident((1, late_w))],
        out_specs=out_specs,
        out_shape=out_shape,
        scratch_shapes=[pltpu.VMEM((in_w // LANES, tm, LANES), F32),
                        pltpu.VMEM((3 * GROUP_SLABS + ssm_w // LANES, tm, LANES), F32)],
        compiler_params=_cparams(("arbitrary",), "inproj"),
        name="inproj",
    )(x2, mod, g_pre, w_in_b, cos_sin, *_rope_spread(), w_side, s_blocks, w_ada, w_ada,
      b_ada[:, N_MOD_EARLY * d:])
    mod_late = outs[-1][:bsz].reshape(bsz, N_MOD - N_MOD_EARLY, d)
    return outs[:-1], mod_late


def _stat_lanes(h):
    a, b = slice(h, h + 1), slice(HEAD_DIM + h, HEAD_DIM + h + 1)
    return (a, b) if h % 2 == 0 else (b, a)


def _attn_kernel(*refs, nblk, units, side):
    if side:
        q_ref, k_ref, v_ref, wside_ref, o_ref, s_ref, wside_out_ref, sc_ref, pr_ref, mk_ref = refs
        wside_out_ref[...] = wside_ref[...].astype(BF16)
    else:
        q_ref, k_ref, v_ref, o_ref, s_ref, sc_ref, pr_ref, mk_ref = refs

    pair = 2 * ATT_BLK
    qi = lax.broadcasted_iota(jnp.int32, (pair, 2 * ATT_BLK), 0) % ATT_BLK
    kj = lax.broadcasted_iota(jnp.int32, (pair, 2 * ATT_BLK), 1)
    rel = qi - kj
    mk_ref[0] = jnp.where((rel >= 0) & (rel <= ATT_BLK), 0.0, NEG)
    mk_ref[1] = jnp.where((rel + ATT_BLK >= 0) & (rel <= 0), 0.0, NEG)
    low = lax.broadcasted_iota(jnp.int32, (ATT_BLK, LANES), 1) < HEAD_DIM
    top, bot = slice(0, ATT_BLK), slice(ATT_BLK, pair)
    ones = jnp.ones((2 * ATT_BLK, LANES), BF16)

    def coords(u):
        u = jnp.minimum(u, units - 1)
        r, i = u // nblk, u % nblk
        rows = pl.ds(pl.multiple_of(i * ATT_BLK, ATT_BLK), ATT_BLK)
        krows = pl.ds(pl.multiple_of(jnp.maximum(i - 1, 0) * ATT_BLK, ATT_BLK), 2 * ATT_BLK)
        return r, i, rows, krows

    def scores(u, slot):
        r, i, rows, krows = coords(u)
        bias = mk_ref[jnp.minimum(i, 1)]
        for jj in range(GROUP_SLABS):
            q2 = q_ref[r, rows, _slab(jj)]
            zero = jnp.zeros_like(q2)
            qp = jnp.concatenate([jnp.where(low, q2, zero), jnp.where(low, zero, q2)], axis=0)
            s = lax.dot_general(qp, k_ref[r, krows, _slab(jj)], (((1,), (1,)), ((), ())),
                                preferred_element_type=F32)
            sc_ref[slot, jj] = s + bias

    def softmax(u, slot):
        r, _, rows, _ = coords(u)
        s_ref[r, rows, :] = jnp.zeros((ATT_BLK, LANES), F32)
        for jj in range(GROUP_SLABS):
            s = sc_ref[slot, jj]
            m = jnp.max(s, axis=1, keepdims=True)
            pr_ref[slot, jj] = jnp.exp2(s - m).astype(BF16)
            s_ref[r, rows, _stat_lanes(2 * jj)[0]] = m[top]
            s_ref[r, rows, _stat_lanes(2 * jj + 1)[0]] = m[bot]

    def values(u, slot):
        r, _, rows, krows = coords(u)
        for jj in range(GROUP_SLABS):
            v_ext = jnp.concatenate([v_ref[r, krows, _slab(jj)], ones], axis=1)
            oe = jnp.dot(pr_ref[slot, jj], v_ext, preferred_element_type=F32)
            l_even, l_odd = _stat_lanes(2 * jj)[1], _stat_lanes(2 * jj + 1)[1]
            s_ref[r, rows, l_even] = oe[top, LANES:][:, l_even]
            s_ref[r, rows, l_odd] = oe[bot, LANES:][:, l_odd]
            o_ref[r, rows, _slab(jj)] = jnp.where(low, oe[top, :LANES], oe[bot, :LANES]).astype(o_ref.dtype)

    scores(0, 0)
    softmax(0, 0)
    scores(1, 1)

    def body(t, carry):
        u = UNITS_PER_TRIP * t
        for k in range(UNITS_PER_TRIP):
            scores(u + k + 2, k % 2)
            softmax(u + k + 1, (k + 1) % 2)
            values(u + k, k % 2)
        return carry

    assert UNITS_PER_TRIP % 2 == 0 and units % UNITS_PER_TRIP == 0
    lax.fori_loop(0, units // UNITS_PER_TRIP, body, 0)


def _attn_group(q, k, v, dilation, bsz, seq, w_side=None):
    n = seq // dilation
    nblk = n // ATT_BLK
    spec = lambda w: pl.BlockSpec((None, dilation, n, w), lambda b: (b, 0, 0, 0))
    win = (2 * ATT_BLK, 2 * ATT_BLK)
    ins, in_specs = [q, k, v], [spec(GROUP_WIDTH)] * 3
    out_specs = [spec(GROUP_WIDTH), spec(LANES)]
    out_shape = [jax.ShapeDtypeStruct((bsz, dilation, n, GROUP_WIDTH), BF16),
                 jax.ShapeDtypeStruct((bsz, dilation, n, LANES), F32)]
    if w_side is not None:
        side_spec, side_shape = _side_cast_spec(w_side, bsz)
        ins, in_specs = ins + [w_side], in_specs + [side_spec]
        out_specs, out_shape = out_specs + [side_spec], out_shape + [side_shape]
    return pl.pallas_call(
        functools.partial(_attn_kernel, nblk=nblk, units=dilation * nblk, side=w_side is not None),
        grid=(bsz,),
        in_specs=in_specs,
        out_specs=out_specs,
        out_shape=out_shape,
        scratch_shapes=[pltpu.VMEM((2, GROUP_SLABS) + win, F32),
                        pltpu.VMEM((2, GROUP_SLABS) + win, BF16),
                        pltpu.VMEM((2,) + win, F32)],
        compiler_params=_cparams(("parallel",), "attn"),
        name=f"attn_d{dilation}",
    )(*ins)


def _cmul(a, b):
    return a[0] * b[0] - a[1] * b[1], a[0] * b[1] + a[1] * b[0]


def _ssm_tables(a_re, a_im, log_dt, b_re, b_im, c_re, c_im, d_skip):
    f = lambda t: t.astype(F32)
    lam = (f(a_re), f(a_im))
    dt = jnp.exp(f(log_dt))[:, None]
    g = a_re.shape[0]
    taus = jnp.arange(2 * SSM_CHUNK + 1, dtype=F32)[None, :, None]
    mag = jnp.exp(lam[0][:, None, :] * dt[:, None, :] * taus)
    ang = lam[1][:, None, :] * dt[:, None, :] * taus
    apow = (mag * jnp.cos(ang), mag * jnp.sin(ang))
    a_bar = (apow[0][:, 1], apow[1][:, 1])
    den = lam[0] * lam[0] + lam[1] * lam[1]
    coef = _cmul((a_bar[0] - 1.0, a_bar[1]), (lam[0] / den, -lam[1] / den))
    b_bar = _cmul((coef[0][..., None], coef[1][..., None]), (f(b_re), f(b_im)))
    c_mat = (f(c_re), f(c_im))

    pair = lambda lo, hi: jnp.concatenate([lo, hi], axis=-1)
    r_re, r_im = (t[:, SSM_CHUNK - 1::-1][:, :SSM_CHUNK] for t in apow)
    n_re, n_im = (t[:, 1:SSM_CHUNK + 1] for t in apow)
    k_re, k_im = (t[:, :SSM_CHUNK] for t in apow)
    bt_re, bt_im = (t.transpose(0, 2, 1) for t in b_bar)
    fac = jnp.stack([pair(r_re, r_re), pair(-r_im, r_im),
                     pair(n_re, -n_re), pair(-n_im, -n_im),
                     pair(k_re, k_re), pair(-k_im, k_im),
                     pair(bt_re, bt_im), pair(c_mat[0], c_mat[1]),
                     pair(bt_re, -bt_im)], axis=1)
    d_lane = jnp.pad(f(d_skip), ((0, 0), (0, CHUNK_W - SSM_P)))[:, None, :]

    def mul_rows(tau):
        re, im = apow[0][:, tau], apow[1][:, tau]
        return [pair(re, re), pair(-im, im), pair(im, -im)]
    ptab = jnp.stack(mul_rows(SSM_CHUNK) + mul_rows(2 * SSM_CHUNK) +
                     [jnp.zeros((g, LANES), F32)] * 2, axis=1)
    return fac, d_lane, ptab


def _chunk_matrices(f_ref, d_ref, g):
    b, c, b_conj = f_ref[g, 6], f_ref[g, 7], f_ref[g, 8]
    swapped = lambda x: pltpu.roll(x, SSM_N, axis=1)

    def outer(k, x):
        a1, a2, xs = f_ref[g, k], f_ref[g, k + 1], swapped(x)
        return jnp.concatenate([a1[i:i + 1] * x + a2[i:i + 1] * xs for i in range(SSM_CHUNK)], axis=0)

    w = outer(0, b).astype(BF16)
    v = outer(2, c).T.astype(BF16)
    ca = outer(4, c)
    m_row = lax.dot_general(b_conj, ca, (((1,), (1,)), ((), ())), precision=lax.Precision.HIGHEST,
                            preferred_element_type=F32)
    q = lax.broadcasted_iota(jnp.int32, m_row.shape, 0)
    lane = lax.broadcasted_iota(jnp.int32, m_row.shape, 1)
    m_row = m_row + jnp.where(lane == q, d_ref[g], 0.0)
    return w, v, _toeplitz(m_row)


def _toeplitz(m_row):
    lane = lax.broadcasted_iota(jnp.int32, m_row.shape, 1)
    blocks = [m_row]
    for s in range(1, SSM_CHUNK):
        blocks.append(jnp.where(lane >= s * SSM_P, pltpu.roll(m_row, s * SSM_P, axis=1), 0.0))
    return jnp.concatenate(blocks, axis=0).astype(BF16)


def _ssm_kernel(u_ref, f_ref, d_ref, p_ref, y_ref, gx_ref, gz_ref, h_ref, *, groups, nchunk, bsz):
    rows = nchunk * bsz
    row = lax.broadcasted_iota(jnp.int32, (rows, LANES), 0)
    first_chunk = (row % nchunk) == 0

    def prev_chunk(a):
        return jnp.where(first_chunk, 0.0, pltpu.roll(a, 1, axis=0))

    mats = [_chunk_matrices(f_ref, d_ref, g) for g in range(groups)]

    for g in range(groups):
        ex = jnp.dot(u_ref[g], mats[g][0], preferred_element_type=F32)
        ez = pltpu.roll(ex, SSM_N, axis=1)
        exs, ezs = prev_chunk(ex), prev_chunk(ez)
        p1, p2, p3 = p_ref[g, 0:1, :], p_ref[g, 1:2, :], p_ref[g, 2:3, :]
        gx = ex + p1 * exs + p2 * ezs
        gz = ez + p1 * ezs + p3 * exs
        for b in range(bsz):
            gx_ref[g, pl.ds(b, nchunk, stride=bsz), :] = gx[b * nchunk:(b + 1) * nchunk]
            gz_ref[g, pl.ds(b, nchunk, stride=bsz), :] = gz[b * nchunk:(b + 1) * nchunk]

    y_intra = [jnp.dot(u_ref[g], mats[g][2], preferred_element_type=F32) for g in range(groups)]

    tile = (SUBLANES, LANES)
    mults = [[jnp.broadcast_to(p_ref[g, r:r + 1, :], tile) for r in (3, 4, 5)] for g in range(groups)]
    zero = jnp.zeros(tile, F32)
    state = [(zero, zero) for _ in range(groups)]
    for j in range(rows // SUBLANES):
        r0 = j * SUBLANES
        for g in range(groups):
            x, z = state[g]
            q1, q2, q3 = mults[g]
            xn = q1 * x + q2 * z + gx_ref[g, r0:r0 + SUBLANES, :]
            zn = q1 * z + q3 * x + gz_ref[g, r0:r0 + SUBLANES, :]
            h_ref[g, r0:r0 + SUBLANES, :] = xn
            state[g] = (xn, zn)

    for g in range(groups):
        h_end = jnp.concatenate([h_ref[g, pl.ds(b, nchunk, stride=bsz), :] for b in range(bsz)], axis=0)
        h_in = prev_chunk(h_end).astype(BF16)
        y = y_intra[g] + jnp.dot(h_in, mats[g][1], preferred_element_type=F32)
        y_ref[g] = y.astype(y_ref.dtype)


def _ssm(uc, tables, bsz, seq):
    fac, d_lane, ptab = tables
    g, rows, _ = uc.shape
    nchunk = seq // SSM_CHUNK
    assert rows == nchunk * bsz and SUBLANES % bsz == 0 and SUBLANES // bsz == 2
    gs = 4
    blk = lambda a, b: pl.BlockSpec((gs, a, b), lambda i: (i, 0, 0))
    return pl.pallas_call(
        functools.partial(_ssm_kernel, groups=gs, nchunk=nchunk, bsz=bsz),
        grid=(g // gs,),
        in_specs=[blk(rows, CHUNK_W),
                  pl.BlockSpec((gs,) + fac.shape[1:], lambda i: (i, 0, 0, 0)),
                  blk(1, CHUNK_W), blk(SUBLANES, LANES)],
        out_specs=blk(rows, CHUNK_W),
        out_shape=jax.ShapeDtypeStruct((g, rows, CHUNK_W), BF16),
        scratch_shapes=[pltpu.VMEM((gs, rows, LANES), F32)] * 3,
        compiler_params=_cparams(("parallel",), "ssm"),
        name="ssm",
    )(uc, fac, d_lane, ptab)


def _rms(x):
    return x * lax.rsqrt(jnp.mean(x * x, axis=-1, keepdims=True) + EPS)


def _gelu_tanh(x):
    k = -2.0 * math.sqrt(2.0 / math.pi) * math.log2(math.e)
    e = jnp.exp2(x * (k + (k * 0.044715) * (x * x)))
    return x * (1.0 / (1.0 + e))


def _mix_kernel(o1_ref, l1_ref, o4_ref, l4_ref, o16_ref, l16_ref, y_ref, x_ref, mod_ref,
                ga_ref, gs_ref, gp_ref, ex_ref, wg_ref, bg_ref, wo_ref, wside_ref, out_ref,
                wside_out_ref, so_ref, sl_ref, sy_ref, s4_ref, *, tm):
    wside_out_ref[...] = wside_ref[...].astype(BF16)

    quarter = tm // 4

    def put16(t4, r16, start, value):
        r4, a = r16 % 4, r16 // 4
        s4_ref[t4, pl.ds(r4 * quarter + 4 * start + a, value.shape[0], stride=4), :] = value

    def from_fours(t4, dst_ref, j):
        for r4 in range(4):
            dst_ref[j, pl.ds(r4, quarter, stride=4), :] = s4_ref[t4, r4 * quarter:(r4 + 1) * quarter, :]

    for r in range(4):
        rows = pl.ds(r, quarter, stride=4)
        for jj in range(GROUP_SLABS):
            so_ref[jj, rows, :] = o4_ref[r, :, _slab(jj)].astype(F32)
        sl_ref[0, rows, :] = l4_ref[r]
    for r in range(16):
        for jj in range(GROUP_SLABS):
            put16(jj, r, 0, o16_ref[r, :, _slab(jj)].astype(F32))
        put16(GROUP_SLABS, r, 0, l16_ref[r])
    for jj in range(GROUP_SLABS):
        from_fours(jj, so_ref, GROUP_SLABS + jj)
    from_fours(GROUP_SLABS, sl_ref, 1)

    ngb, nc = y_ref.shape[0] // SLOTS, tm // SSM_CHUNK
    vs = [jnp.concatenate([y_ref[gb * SLOTS + g, :, _slab(half)].astype(F32)
                           for gb in range(ngb) for half in range(2)], axis=0)
          for g in range(SLOTS)]
    for s, o in enumerate(_slot_transpose(vs)):
        for gb in range(ngb):
            for half in range(2):
                r0 = (gb * 2 + half) * nc
                put16(GROUP_SLABS + 1 + gb, half * SLOTS + s, 0, o[r0:r0 + nc])
    for gb in range(ngb):
        from_fours(GROUP_SLABS + 1 + gb, sy_ref, gb)

    lane = lax.broadcasted_iota(jnp.int32, (tm, LANES), 1)
    even = (lane & 1) == 0
    ms, ls = [], []
    for st in (l1_ref[...], sl_ref[0], sl_ref[1]):
        sw = pltpu.roll(st, HEAD_DIM, axis=1)
        ms.append(jnp.where(even, st, sw))
        ls.append(jnp.where(even, sw, st))
    mx = jnp.maximum(jnp.maximum(ms[0], ms[1]), ms[2])
    es = [jnp.exp2(m - mx) for m in ms]
    inv = 1.0 / (es[0] * ls[0] + es[1] * ls[1] + es[2] * ls[2])
    head_lane = lane < HEADS_PER_GROUP
    wexp = [jnp.dot(jnp.where(head_lane, e * inv, 0.0).astype(BF16), ex_ref[...],
                    preferred_element_type=F32) for e in es]
    att_slabs = []
    for jj in range(GROUP_SLABS):
        a = wexp[0][:, _slab(jj)] * o1_ref[:, _slab(jj)].astype(F32)
        a = a + wexp[1][:, _slab(jj)] * so_ref[jj]
        a = a + wexp[2][:, _slab(jj)] * so_ref[GROUP_SLABS + jj]
        att_slabs.append(a)
    att = _rms(jnp.concatenate(att_slabs, axis=1)) * ga_ref[...]

    yv = jnp.concatenate([sy_ref[gb] for gb in range(sy_ref.shape[0])], axis=1)
    s = _gelu_tanh(yv)
    z = jnp.dot(s.astype(BF16), wg_ref[...], preferred_element_type=F32) + bg_ref[...]
    s = s * jax.nn.sigmoid(z)
    ssm = _rms(s) * gs_ref[...]
    both = jnp.concatenate([att.astype(BF16), ssm.astype(BF16)], axis=1)
    mix = jnp.dot(both, wo_ref[...], preferred_element_type=F32)
    out_ref[...] = x_ref[...] + (mod_ref[0:1, :] * gp_ref[...]) * _rms(mix)


def _mix(o1, l1, o4, l4, o16, l16, yc, x2, mod, g_attn, g_ssm, g_post, w_glu_b, b_glu, w_out_b, w_side,
         bsz, seq):
    t_tok, d = x2.shape
    groups = yc.shape[0]
    sw = groups * SSM_P
    aw = GROUP_WIDTH
    tm = TOK_TILE
    per_b = seq // tm
    tok = lambda w: pl.BlockSpec((tm, w), lambda i: (i, 0))
    res = lambda dil, w: pl.BlockSpec((None, dil, tm // dil, w), lambda i: (i // per_b, 0, i % per_b, 0))
    head_of_lane = jnp.arange(aw)[None, :] // HEAD_DIM
    expand = (jnp.arange(LANES)[:, None] == head_of_lane).astype(BF16)
    side_spec, side_shape = _side_cast_spec(w_side, t_tok // tm)
    return pl.pallas_call(
        functools.partial(_mix_kernel, tm=tm),
        grid=(t_tok // tm,),
        in_specs=[tok(aw), tok(LANES), res(4, aw), res(4, LANES), res(16, aw), res(16, LANES),
                  pl.BlockSpec((groups, tm // SSM_CHUNK, CHUNK_W), lambda i: (0, i, 0)),
                  tok(d),
                  pl.BlockSpec((None, N_MOD - N_MOD_EARLY, d), lambda i: (i // per_b, 0, 0)),
                  _resident((1, aw)), _resident((1, sw)), _resident((1, d)),
                  _resident((LANES, aw)),
                  _resident((sw, sw)), _resident((1, sw)), _resident((aw + sw, d)),
                  side_spec],
        out_specs=[tok(d), side_spec],
        out_shape=[jax.ShapeDtypeStruct((t_tok, d), F32), side_shape],
        scratch_shapes=[pltpu.VMEM((2 * GROUP_SLABS, tm, LANES), F32),
                        pltpu.VMEM((2, tm, LANES), F32),
                        pltpu.VMEM((sw // LANES, tm, LANES), F32),
                        pltpu.VMEM((GROUP_SLABS + 1 + sw // LANES, tm, LANES), F32)],
        compiler_params=_cparams(("parallel",), "mix"),
        name="mix",
    )(o1, l1, o4, l4, o16, l16, yc, x2, mod, g_attn, g_ssm, g_post, expand, w_glu_b, b_glu, w_out_b, w_side)


def _mlp_kernel(x_ref, mod_ref, gpre_ref, gpost_ref, w1_ref, w2_ref, o_ref, h_ref):
    f = pl.program_id(1)
    last = pl.num_programs(1) - 1
    chunks = [slice(r, r + ROW_CHUNK) for r in range(0, x_ref.shape[0], ROW_CHUNK)]

    def partial():
        t = jnp.dot(h_ref[...], w1_ref[...], preferred_element_type=F32)
        t = jnp.square(jnp.maximum(t, 0.0)).astype(BF16)
        return jnp.dot(t, w2_ref[...], preferred_element_type=F32)

    @pl.when(f == 0)
    def _():
        scale = gpre_ref[...] * (1.0 + mod_ref[2:3, :])
        shift = mod_ref[1:2, :]
        for rows in chunks:
            h_ref[rows, :] = (_rms(x_ref[rows, :]) * scale + shift).astype(BF16)
        o_ref[...] = partial()

    @pl.when((f > 0) & (f < last))
    def _():
        o_ref[...] += partial()

    @pl.when(f == last)
    def _():
        gate = mod_ref[3:4, :] * gpost_ref[...]
        acc = o_ref[...] + partial()
        for rows in chunks:
            o_ref[rows, :] = x_ref[rows, :] + gate * _rms(acc[rows, :])


def _mlp(x2, mod, g_pre, g_post, w1_b, w2_b, seq):
    t_tok, d = x2.shape
    d_ff = w1_b.shape[1]
    tm, tf = 512, 2048
    assert d_ff // tf >= 2
    per_b = seq // tm
    return pl.pallas_call(
        _mlp_kernel,
        grid=(t_tok // tm, d_ff // tf),
        in_specs=[pl.BlockSpec((tm, d), lambda i, f: (i, 0)),
                  pl.BlockSpec((None, N_MOD - N_MOD_EARLY, d), lambda i, f: (i // per_b, 0, 0)),
                  _resident((1, d)), _resident((1, d)),
                  pl.BlockSpec((d, tf), lambda i, f: (0, f)),
                  pl.BlockSpec((tf, d), lambda i, f: (f, 0))],
        out_specs=pl.BlockSpec((tm, d), lambda i, f: (i, 0)),
        out_shape=jax.ShapeDtypeStruct((t_tok, d), F32),
        scratch_shapes=[pltpu.VMEM((tm, d), BF16)],
        compiler_params=_cparams(("parallel", "arbitrary"), "mlp"),
        name="mlp",
    )(x2, mod, g_pre, g_post, w1_b, w2_b)


def _rope_cos_sin(positions):
    freqs = ROPE_THETA ** (-jnp.arange(0, ROT_DIM, 2, dtype=F32) / ROT_DIM)
    ang = positions.reshape(-1).astype(F32)[None, :] * freqs[:, None]
    return jnp.concatenate([jnp.cos(ang), jnp.sin(ang)], axis=0)


def kernel(x, c, positions, w_ada, b_ada, g_pre_mix, g_post_mix, w_in, ssm_a_re, ssm_a_im, ssm_log_dt,
           ssm_b_re, ssm_b_im, ssm_c_re, ssm_c_im, ssm_d, w_glu, b_glu, g_attn_out, g_ssm_out, w_out,
           g_pre_mlp, g_post_mlp, w_mlp_in, w_mlp_out):
    bsz, seq, d = x.shape
    depth = w_ada.shape[0]
    x2 = x.reshape(bsz * seq, d)
    cos_sin = _rope_cos_sin(positions)
    row = lambda a: a.reshape(1, -1).astype(F32)
    for l in range(depth):
        s = jnp.zeros((SUBLANES, d), BF16).at[:bsz].set(jax.nn.silu(c).astype(BF16))
        b_ada_row = row(b_ada[l])
        mod = _adaln_early(s, w_ada[l], b_ada_row, bsz)
        (q1, q4, q16, k1, k4, k16, v1, v4, v16, uc, w_mlp_out_b), mod_late = _inproj(
            x2, mod, row(g_pre_mix[l]), w_in[l].astype(BF16), cos_sin, w_mlp_out[l],
            s, w_ada[l], b_ada_row, bsz, seq)

        nat = lambda a: a.reshape(bsz, 1, seq, a.shape[-1])
        o1, l1, w_out_b = _attn_group(nat(q1), nat(k1), nat(v1), 1, bsz, seq, w_out[l])
        o4, l4, w_glu_b = _attn_group(q4, k4, v4, 4, bsz, seq, w_glu[l])
        o16, l16 = _attn_group(q16, k16, v16, 16, bsz, seq)

        tables = _ssm_tables(ssm_a_re[l], ssm_a_im[l], ssm_log_dt[l], ssm_b_re[l], ssm_b_im[l],
                             ssm_c_re[l], ssm_c_im[l], ssm_d[l])
        yc = _ssm(uc, tables, bsz, seq)

        x2, w_mlp_in_b = _mix(o1.reshape(bsz * seq, -1), l1.reshape(bsz * seq, -1), o4, l4, o16, l16, yc,
                              x2, mod_late, row(g_attn_out[l]), row(g_ssm_out[l]), row(g_post_mix[l]),
                              w_glu_b, row(b_glu[l]), w_out_b, w_mlp_in[l],
                              bsz, seq)
        x2 = _mlp(x2, mod_late, row(g_pre_mlp[l]), row(g_post_mlp[l]), w_mlp_in_b, w_mlp_out_b, seq)
    return x2.reshape(bsz, seq, d)
```

```python
import functools
import math

import jax
import jax.numpy as jnp
from jax import lax
from jax.experimental import pallas as pl
from jax.experimental.pallas import tpu as pltpu

F32 = jnp.float32
BF16 = jnp.bfloat16

LANES = 128
SUBLANES = 8

HEAD_DIM = 64
ROT_DIM = HEAD_DIM // 4
ROPE_THETA = 500000.0
DILATIONS = (1, 4, 16)
HEADS_PER_GROUP = 6
GROUP_WIDTH = HEADS_PER_GROUP * HEAD_DIM
GROUP_SLABS = GROUP_WIDTH // LANES
ATT_BLK = 128
UNITS_PER_TRIP = 32
SSM_P = 16
SSM_N = 64
SSM_CHUNK = 16
CHUNK_W = SSM_CHUNK * SSM_P
SLOTS = LANES // SSM_P
N_MOD = 6
EPS = 1e-6
NEG = -1e30
TOK_TILE = 512
ROW_CHUNK = 128


MIB = 1024 * 1024
VMEM_MIB_V7X = 64
VMEM_MIB = dict(adaln=40, inproj=58, attn=48, ssm=48, mix=56, mlp=60)


def _cparams(sem, kernel_name):
    assert VMEM_MIB[kernel_name] < VMEM_MIB_V7X
    return pltpu.CompilerParams(dimension_semantics=sem, vmem_limit_bytes=VMEM_MIB[kernel_name] * MIB)


def _resident(shape):
    nd = len(shape)
    return pl.BlockSpec(shape, lambda *_: (0,) * nd, pipeline_mode=pl.Buffered(1))


def _slab(j):
    return slice(j * LANES, (j + 1) * LANES)


def _slot_transpose(vs):
    slot = lax.broadcasted_iota(jnp.int32, vs[0].shape, 1) // SSM_P
    vs = list(vs)
    for dist in (4, 2, 1):
        low = (slot & dist) == 0
        nxt = list(vs)
        for i in range(SLOTS):
            if i & dist == 0:
                a, b = vs[i], vs[i + dist]
                nxt[i] = jnp.where(low, a, pltpu.roll(b, dist * SSM_P, axis=1))
                nxt[i + dist] = jnp.where(low, pltpu.roll(a, LANES - dist * SSM_P, axis=1), b)
        vs = nxt
    return vs


def _adaln_kernel(s_ref, w_ref, b_ref, o_ref):
    o_ref[...] = jnp.dot(s_ref[...], w_ref[...].astype(BF16),
                         preferred_element_type=F32) + b_ref[...]


N_MOD_EARLY = 2


def _adaln_early(s, w_ada, b_ada, bsz):
    d = s.shape[1]
    n = N_MOD_EARLY * d
    tn = 1024
    out = pl.pallas_call(
        _adaln_kernel,
        grid=(n // tn,),
        in_specs=[pl.BlockSpec((SUBLANES, d), lambda j: (0, 0)),
                  pl.BlockSpec((d, tn), lambda j: (0, j)),
                  pl.BlockSpec((1, tn), lambda j: (0, j))],
        out_specs=pl.BlockSpec((SUBLANES, tn), lambda j: (0, j)),
        out_shape=jax.ShapeDtypeStruct((SUBLANES, n), F32),
        compiler_params=_cparams(("arbitrary",), "adaln"),
        name="adaln",
    )(s, w_ada, b_ada)
    return out[:bsz].reshape(bsz, N_MOD_EARLY, d)


def _inproj_kernel(x_ref, mod_ref, g_ref, w_ref, cs_ref, rx_ref, r1_ref, wside_ref,
                   sblk_ref, wa_lo_ref, wa_hi_ref, ba_ref,
                   q1_ref, q4_ref, q16_ref, k1_ref, k4_ref, k16_ref, v1_ref, v4_ref, v16_ref,
                   uc_ref, wside_out_ref, modr_ref, sc_ref, s4_ref, *, tm):
    wside_out_ref[...] = wside_ref[...].astype(BF16)

    @pl.when(pl.program_id(0) == 0)
    def _():
        modr_ref[...] = jnp.broadcast_to(ba_ref[...], modr_ref.shape)
    sblk = sblk_ref[...]
    half_w = wa_lo_ref.shape[1]
    modr_ref[:, :half_w] += jnp.dot(sblk, wa_lo_ref[...].astype(BF16), preferred_element_type=F32)
    modr_ref[:, half_w:] += jnp.dot(sblk, wa_hi_ref[...].astype(BF16), preferred_element_type=F32)

    x = x_ref[...]
    ms = jnp.mean(x * x, axis=-1, keepdims=True)
    scale = g_ref[...] * (1.0 + mod_ref[1:2, :])
    h = (x * lax.rsqrt(ms + EPS)) * scale + mod_ref[0:1, :]
    hb = h.astype(BF16)

    qk_slabs = 4 * GROUP_SLABS
    quarter = tm // 4

    def by_fours(j, t4):
        pieces = [sc_ref[j, pl.ds(r4, quarter, stride=4), :] for r4 in range(4)]
        for r4, piece in enumerate(pieces):
            s4_ref[t4, r4 * quarter:(r4 + 1) * quarter, :] = piece
        return pieces

    def rows16(t4, r16, start, size):
        r4, a = r16 % 4, r16 // 4
        return s4_ref[t4, pl.ds(r4 * quarter + 4 * start + a, size, stride=4), :]

    def scatter(j, jj, refs, val, t4=None):
        refs = dict(refs)
        if 1 in refs:
            refs[1][:, _slab(jj)] = val.astype(BF16)
        if 16 in refs:
            pieces = by_fours(j, t4)
            for r16 in range(16):
                refs[16][r16, :, _slab(jj)] = rows16(t4, r16, 0, tm // 16).astype(BF16)
        elif 4 in refs:
            pieces = [sc_ref[j, pl.ds(r4, quarter, stride=4), :] for r4 in range(4)]
        if 4 in refs:
            for r4, piece in enumerate(pieces):
                refs[4][r4, :, _slab(jj)] = piece.astype(BF16)

    u0 = qk_slabs + GROUP_SLABS
    ngb, nc = uc_ref.shape[0] // SLOTS, tm // SSM_CHUNK

    def to_chunk_major(gbs):
        for gb in gbs:
            by_fours(u0 + gb, GROUP_SLABS + gb)
        vs = [jnp.concatenate([rows16(GROUP_SLABS + gb, half * SLOTS + s, cb * SUBLANES, SUBLANES)
                               for gb in gbs for half in range(2) for cb in range(nc // SUBLANES)],
                              axis=0) for s in range(SLOTS)]
        for g, o in enumerate(_slot_transpose(vs)):
            for k, gb in enumerate(gbs):
                r0 = k * 2 * nc
                full = jnp.concatenate([o[r0:r0 + nc], o[r0 + nc:r0 + 2 * nc]], axis=1)
                uc_ref[gb * SLOTS + g] = full.astype(BF16)

    first_u = 1
    pu = jnp.dot(hb, w_ref[:, (u0 + first_u) * LANES:], preferred_element_type=F32)
    for gb in range(first_u, ngb):
        sc_ref[u0 + gb] = pu[:, _slab(gb - first_u)]
    to_chunk_major(range(first_u, ngb))

    cs = cs_ref[...]
    cs_hi = cs.astype(BF16)
    cs_lo = (cs - cs_hi.astype(F32)).astype(BF16)
    rows_contract = (((0,), (0,)), ((), ()))
    tab = lax.dot_general(jnp.concatenate([cs_hi, cs_lo], axis=0), rx_ref[...], rows_contract,
                          preferred_element_type=F32)
    ct = tab[:, :LANES] + r1_ref[...]
    st = tab[:, LANES:]
    q_scale = math.log2(math.e) / math.sqrt(HEAD_DIM)
    ct_q, st_q = ct * q_scale, st * q_scale
    lane = lax.broadcasted_iota(jnp.int32, ct.shape, 1) % HEAD_DIM
    second_half = (lane >= ROT_DIM // 2) & (lane < ROT_DIM)
    q_refs = ((1, q1_ref),), ((4, q4_ref),), ((16, q16_ref),)
    chunk_slabs = 4
    for c0 in reversed(range(0, qk_slabs, chunk_slabs)):
        pc = jnp.dot(hb, w_ref[:, c0 * LANES:(c0 + chunk_slabs) * LANES], preferred_element_type=F32)
        for s in range(chunk_slabs):
            j = c0 + s
            grp, jj = divmod(j, GROUP_SLABS)
            t = pc[:, _slab(s)]
            partner = jnp.where(second_half,
                                pltpu.roll(t, ROT_DIM // 2, axis=1),
                                pltpu.roll(t, LANES - ROT_DIM // 2, axis=1))
            r = t * ct_q + partner * st_q if grp < 3 else t * ct + partner * st
            if grp > 0:
                sc_ref[j] = r
            scatter(j, jj, q_refs[grp] if grp < 3 else ((1, k1_ref), (4, k4_ref), (16, k16_ref)), r,
                    t4=GROUP_SLABS + ngb + (grp - 2) * GROUP_SLABS + jj if grp >= 2 else None)

    pv = jnp.dot(hb, w_ref[:, qk_slabs * LANES:(u0 + first_u) * LANES], preferred_element_type=F32)
    for j in range(GROUP_SLABS + first_u):
        sc_ref[qk_slabs + j] = pv[:, _slab(j)]
    for jj in range(GROUP_SLABS):
        scatter(qk_slabs + jj, jj, ((1, v1_ref), (4, v4_ref), (16, v16_ref)), pv[:, _slab(jj)], t4=jj)
    to_chunk_major(range(first_u))


def _rope_spread():
    half = ROT_DIM // 2
    dim = jnp.arange(LANES)[None, :] % HEAD_DIM
    i = jnp.arange(half)[:, None]
    first, second = dim == i, dim == i + half
    ct_rows = (first | second).astype(F32)
    st_rows = second.astype(F32) - first.astype(F32)
    zero = jnp.zeros_like(ct_rows)
    spread = jnp.concatenate([jnp.concatenate([ct_rows, zero], axis=1),
                              jnp.concatenate([zero, st_rows], axis=1)], axis=0)
    return jnp.concatenate([spread, spread], axis=0).astype(BF16), (dim >= ROT_DIM).astype(F32)


def _side_cast_spec(w, steps):
    rows, cols = w.shape
    assert rows % (steps * 2 * SUBLANES) == 0
    return (pl.BlockSpec((rows // steps, cols), lambda i: (i, 0)),
            jax.ShapeDtypeStruct(w.shape, BF16))


def _inproj(x2, mod, g_pre, w_in_b, cos_sin, w_side, s, w_ada, b_ada, bsz, seq):
    t_tok, d = x2.shape
    in_w = w_in_b.shape[1]
    ssm_w = in_w - 5 * GROUP_WIDTH
    groups = ssm_w // SSM_P
    tm = TOK_TILE
    per_b = seq // tm
    steps = t_tok // tm
    tok = lambda w: pl.BlockSpec((tm, w), lambda i: (i, 0))
    side_spec, side_shape = _side_cast_spec(w_side, steps)
    kblk = d // steps
    late_w = (N_MOD - N_MOD_EARLY) * d
    assert d % steps == 0 and N_MOD_EARLY * d == late_w // 2
    s_blocks = s.reshape(SUBLANES, steps, kblk).transpose(1, 0, 2)
    wa_spec = lambda half: pl.BlockSpec((kblk, late_w // 2), lambda i: (i, 1 + half))

    def res_spec(dil):
        return pl.BlockSpec((None, dil, tm // dil, GROUP_WIDTH), lambda i: (i // per_b, 0, i % per_b, 0))

    def res_shape(dil):
        return jax.ShapeDtypeStruct((bsz, dil, seq // dil, GROUP_WIDTH), BF16)

    nat = jax.ShapeDtypeStruct((t_tok, GROUP_WIDTH), BF16)
    out_shape = [nat, res_shape(4), res_shape(16)] * 3 + \
                [jax.ShapeDtypeStruct((groups, t_tok // SSM_CHUNK, CHUNK_W), BF16), side_shape,
                 jax.ShapeDtypeStruct((SUBLANES, late_w), F32)]
    out_specs = [tok(GROUP_WIDTH), res_spec(4), res_spec(16)] * 3 + \
                [pl.BlockSpec((groups, tm // SSM_CHUNK, CHUNK_W), lambda i: (0, i, 0)), side_spec,
                 pl.BlockSpec((SUBLANES, late_w), lambda i: (0, 0))]
    outs = pl.pallas_call(
        functools.partial(_inproj_kernel, tm=tm),
        grid=(steps,),
        in_specs=[tok(d),
                  pl.BlockSpec((None, N_MOD_EARLY, d), lambda i: (i // per_b, 0, 0)),
                  _resident((1, d)),
                  _resident((d, in_w)),
                  pl.BlockSpec((ROT_DIM, tm), lambda i: (0, i)),
                  _resident((2 * ROT_DIM, 2 * LANES)), _resident((1, LANES)),
                  side_spec,
                  pl.BlockSpec((None, SUBLANES, kblk), lambda i: (i, 0, 0)),
                  wa_spec(0), wa_spec(1), _resident((1, late_w))],
        out_specs=out_specs,
        out_shape=out_shape,
        scratch_shapes=[pltpu.VMEM((in_w // LANES, tm, LANES), F32),
                        pltpu.VMEM((3 * GROUP_SLABS + ssm_w // LANES, tm, LANES), F32)],
        compiler_params=_cparams(("arbitrary",), "inproj"),
        name="inproj",
    )(x2, mod, g_pre, w_in_b, cos_sin, *_rope_spread(), w_side, s_blocks, w_ada, w_ada,
      b_ada[:, N_MOD_EARLY * d:])
    mod_late = outs[-1][:bsz].reshape(bsz, N_MOD - N_MOD_EARLY, d)
    return outs[:-1], mod_late


def _stat_lanes(h):
    a, b = slice(h, h + 1), slice(HEAD_DIM + h, HEAD_DIM + h + 1)
    return (a, b) if h % 2 == 0 else (b, a)


def _attn_kernel(*refs, nblk, units, side):
    if side:
        q_ref, k_ref, v_ref, wside_ref, o_ref, s_ref, wside_out_ref, sc_ref, pr_ref, mk_ref = refs
        wside_out_ref[...] = wside_ref[...].astype(BF16)
    else:
        q_ref, k_ref, v_ref, o_ref, s_ref, sc_ref, pr_ref, mk_ref = refs

    pair = 2 * ATT_BLK
    qi = lax.broadcasted_iota(jnp.int32, (pair, 2 * ATT_BLK), 0) % ATT_BLK
    kj = lax.broadcasted_iota(jnp.int32, (pair, 2 * ATT_BLK), 1)
    rel = qi - kj
    mk_ref[0] = jnp.where((rel >= 0) & (rel <= ATT_BLK), 0.0, NEG)
    mk_ref[1] = jnp.where((rel + ATT_BLK >= 0) & (rel <= 0), 0.0, NEG)
    low = lax.broadcasted_iota(jnp.int32, (ATT_BLK, LANES), 1) < HEAD_DIM
    top, bot = slice(0, ATT_BLK), slice(ATT_BLK, pair)
    ones = jnp.ones((2 * ATT_BLK, LANES), BF16)

    def coords(u):
        u = jnp.minimum(u, units - 1)
        r, i = u // nblk, u % nblk
        rows = pl.ds(pl.multiple_of(i * ATT_BLK, ATT_BLK), ATT_BLK)
        krows = pl.ds(pl.multiple_of(jnp.maximum(i - 1, 0) * ATT_BLK, ATT_BLK), 2 * ATT_BLK)
        return r, i, rows, krows

    def scores(u, slot):
        r, i, rows, krows = coords(u)
        bias = mk_ref[jnp.minimum(i, 1)]
        for jj in range(GROUP_SLABS):
            q2 = q_ref[r, rows, _slab(jj)]
            zero = jnp.zeros_like(q2)
            qp = jnp.concatenate([jnp.where(low, q2, zero), jnp.where(low, zero, q2)], axis=0)
            s = lax.dot_general(qp, k_ref[r, krows, _slab(jj)], (((1,), (1,)), ((), ())),
                                preferred_element_type=F32)
            sc_ref[slot, jj] = s + bias

    def softmax(u, slot):
        r, _, rows, _ = coords(u)
        s_ref[r, rows, :] = jnp.zeros((ATT_BLK, LANES), F32)
        for jj in range(GROUP_SLABS):
            s = sc_ref[slot, jj]
            m = jnp.max(s, axis=1, keepdims=True)
            pr_ref[slot, jj] = jnp.exp2(s - m).astype(BF16)
            s_ref[r, rows, _stat_lanes(2 * jj)[0]] = m[top]
            s_ref[r, rows, _stat_lanes(2 * jj + 1)[0]] = m[bot]

    def values(u, slot):
        r, _, rows, krows = coords(u)
        for jj in range(GROUP_SLABS):
            v_ext = jnp.concatenate([v_ref[r, krows, _slab(jj)], ones], axis=1)
            oe = jnp.dot(pr_ref[slot, jj], v_ext, preferred_element_type=F32)
            l_even, l_odd = _stat_lanes(2 * jj)[1], _stat_lanes(2 * jj + 1)[1]
            s_ref[r, rows, l_even] = oe[top, LANES:][:, l_even]
            s_ref[r, rows, l_odd] = oe[bot, LANES:][:, l_odd]
            o_ref[r, rows, _slab(jj)] = jnp.where(low, oe[top, :LANES], oe[bot, :LANES]).astype(o_ref.dtype)

    scores(0, 0)
    softmax(0, 0)
    scores(1, 1)

    def body(t, carry):
        u = UNITS_PER_TRIP * t
        for k in range(UNITS_PER_TRIP):
            scores(u + k + 2, k % 2)
            softmax(u + k + 1, (k + 1) % 2)
            values(u + k, k % 2)
        return carry

    assert UNITS_PER_TRIP % 2 == 0 and units % UNITS_PER_TRIP == 0
    lax.fori_loop(0, units // UNITS_PER_TRIP, body, 0)


def _attn_group(q, k, v, dilation, bsz, seq, w_side=None):
    n = seq // dilation
    nblk = n // ATT_BLK
    spec = lambda w: pl.BlockSpec((None, dilation, n, w), lambda b: (b, 0, 0, 0))
    win = (2 * ATT_BLK, 2 * ATT_BLK)
    ins, in_specs = [q, k, v], [spec(GROUP_WIDTH)] * 3
    out_specs = [spec(GROUP_WIDTH), spec(LANES)]
    out_shape = [jax.ShapeDtypeStruct((bsz, dilation, n, GROUP_WIDTH), BF16),
                 jax.ShapeDtypeStruct((bsz, dilation, n, LANES), F32)]
    if w_side is not None:
        side_spec, side_shape = _side_cast_spec(w_side, bsz)
        ins, in_specs = ins + [w_side], in_specs + [side_spec]
        out_specs, out_shape = out_specs + [side_spec], out_shape + [side_shape]
    return pl.pallas_call(
        functools.partial(_attn_kernel, nblk=nblk, units=dilation * nblk, side=w_side is not None),
        grid=(bsz,),
        in_specs=in_specs,
        out_specs=out_specs,
        out_shape=out_shape,
        scratch_shapes=[pltpu.VMEM((2, GROUP_SLABS) + win, F32),
                        pltpu.VMEM((2, GROUP_SLABS) + win, BF16),
                        pltpu.VMEM((2,) + win, F32)],
        compiler_params=_cparams(("parallel",), "attn"),
        name=f"attn_d{dilation}",
    )(*ins)


def _cmul(a, b):
    return a[0] * b[0] - a[1] * b[1], a[0] * b[1] + a[1] * b[0]


def _ssm_tables(a_re, a_im, log_dt, b_re, b_im, c_re, c_im, d_skip):
    f = lambda t: t.astype(F32)
    lam = (f(a_re), f(a_im))
    dt = jnp.exp(f(log_dt))[:, None]
    g = a_re.shape[0]
    taus = jnp.arange(2 * SSM_CHUNK + 1, dtype=F32)[None, :, None]
    mag = jnp.exp(lam[0][:, None, :] * dt[:, None, :] * taus)
    ang = lam[1][:, None, :] * dt[:, None, :] * taus
    apow = (mag * jnp.cos(ang), mag * jnp.sin(ang))
    a_bar = (apow[0][:, 1], apow[1][:, 1])
    den = lam[0] * lam[0] + lam[1] * lam[1]
    coef = _cmul((a_bar[0] - 1.0, a_bar[1]), (lam[0] / den, -lam[1] / den))
    b_bar = _cmul((coef[0][..., None], coef[1][..., None]), (f(b_re), f(b_im)))
    c_mat = (f(c_re), f(c_im))

    pair = lambda lo, hi: jnp.concatenate([lo, hi], axis=-1)
    r_re, r_im = (t[:, SSM_CHUNK - 1::-1][:, :SSM_CHUNK] for t in apow)
    n_re, n_im = (t[:, 1:SSM_CHUNK + 1] for t in apow)
    k_re, k_im = (t[:, :SSM_CHUNK] for t in apow)
    bt_re, bt_im = (t.transpose(0, 2, 1) for t in b_bar)
    fac = jnp.stack([pair(r_re, r_re), pair(-r_im, r_im),
                     pair(n_re, -n_re), pair(-n_im, -n_im),
                     pair(k_re, k_re), pair(-k_im, k_im),
                     pair(bt_re, bt_im), pair(c_mat[0], c_mat[1]),
                     pair(bt_re, -bt_im)], axis=1)
    d_lane = jnp.pad(f(d_skip), ((0, 0), (0, CHUNK_W - SSM_P)))[:, None, :]

    def mul_rows(tau):
        re, im = apow[0][:, tau], apow[1][:, tau]
        return [pair(re, re), pair(-im, im), pair(im, -im)]
    ptab = jnp.stack(mul_rows(SSM_CHUNK) + mul_rows(2 * SSM_CHUNK) +
                     [jnp.zeros((g, LANES), F32)] * 2, axis=1)
    return fac, d_lane, ptab


def _chunk_matrices(f_ref, d_ref, g):
    b, c, b_conj = f_ref[g, 6], f_ref[g, 7], f_ref[g, 8]
    swapped = lambda x: pltpu.roll(x, SSM_N, axis=1)

    def outer(k, x):
        a1, a2, xs = f_ref[g, k], f_ref[g, k + 1], swapped(x)
        return jnp.concatenate([a1[i:i + 1] * x + a2[i:i + 1] * xs for i in range(SSM_CHUNK)], axis=0)

    w = outer(0, b).astype(BF16)
    v = outer(2, c).T.astype(BF16)
    ca = outer(4, c)
    m_row = lax.dot_general(b_conj, ca, (((1,), (1,)), ((), ())), precision=lax.Precision.HIGHEST,
                            preferred_element_type=F32)
    q = lax.broadcasted_iota(jnp.int32, m_row.shape, 0)
    lane = lax.broadcasted_iota(jnp.int32, m_row.shape, 1)
    m_row = m_row + jnp.where(lane == q, d_ref[g], 0.0)
    return w, v, _toeplitz(m_row)


def _toeplitz(m_row):
    lane = lax.broadcasted_iota(jnp.int32, m_row.shape, 1)
    blocks = [m_row]
    for s in range(1, SSM_CHUNK):
        blocks.append(jnp.where(lane >= s * SSM_P, pltpu.roll(m_row, s * SSM_P, axis=1), 0.0))
    return jnp.concatenate(blocks, axis=0).astype(BF16)


def _ssm_kernel(u_ref, f_ref, d_ref, p_ref, y_ref, gx_ref, gz_ref, h_ref, *, groups, nchunk, bsz):
    rows = nchunk * bsz
    row = lax.broadcasted_iota(jnp.int32, (rows, LANES), 0)
    first_chunk = (row % nchunk) == 0

    def prev_chunk(a):
        return jnp.where(first_chunk, 0.0, pltpu.roll(a, 1, axis=0))

    mats = [_chunk_matrices(f_ref, d_ref, g) for g in range(groups)]

    for g in range(groups):
        ex = jnp.dot(u_ref[g], mats[g][0], preferred_element_type=F32)
        ez = pltpu.roll(ex, SSM_N, axis=1)
        exs, ezs = prev_chunk(ex), prev_chunk(ez)
        p1, p2, p3 = p_ref[g, 0:1, :], p_ref[g, 1:2, :], p_ref[g, 2:3, :]
        gx = ex + p1 * exs + p2 * ezs
        gz = ez + p1 * ezs + p3 * exs
        for b in range(bsz):
            gx_ref[g, pl.ds(b, nchunk, stride=bsz), :] = gx[b * nchunk:(b + 1) * nchunk]
            gz_ref[g, pl.ds(b, nchunk, stride=bsz), :] = gz[b * nchunk:(b + 1) * nchunk]

    y_intra = [jnp.dot(u_ref[g], mats[g][2], preferred_element_type=F32) for g in range(groups)]

    tile = (SUBLANES, LANES)
    mults = [[jnp.broadcast_to(p_ref[g, r:r + 1, :], tile) for r in (3, 4, 5)] for g in range(groups)]
    zero = jnp.zeros(tile, F32)
    state = [(zero, zero) for _ in range(groups)]
    for j in range(rows // SUBLANES):
        r0 = j * SUBLANES
        for g in range(groups):
            x, z = state[g]
            q1, q2, q3 = mults[g]
            xn = q1 * x + q2 * z + gx_ref[g, r0:r0 + SUBLANES, :]
            zn = q1 * z + q3 * x + gz_ref[g, r0:r0 + SUBLANES, :]
            h_ref[g, r0:r0 + SUBLANES, :] = xn
            state[g] = (xn, zn)

    for g in range(groups):
        h_end = jnp.concatenate([h_ref[g, pl.ds(b, nchunk, stride=bsz), :] for b in range(bsz)], axis=0)
        h_in = prev_chunk(h_end).astype(BF16)
        y = y_intra[g] + jnp.dot(h_in, mats[g][1], preferred_element_type=F32)
        y_ref[g] = y.astype(y_ref.dtype)


def _ssm(uc, tables, bsz, seq):
    fac, d_lane, ptab = tables
    g, rows, _ = uc.shape
    nchunk = seq // SSM_CHUNK
    assert rows == nchunk * bsz and SUBLANES % bsz == 0 and SUBLANES // bsz == 2
    gs = 4
    blk = lambda a, b: pl.BlockSpec((gs, a, b), lambda i: (i, 0, 0))
    return pl.pallas_call(
        functools.partial(_ssm_kernel, groups=gs, nchunk=nchunk, bsz=bsz),
        grid=(g // gs,),
        in_specs=[blk(rows, CHUNK_W),
                  pl.BlockSpec((gs,) + fac.shape[1:], lambda i: (i, 0, 0, 0)),
                  blk(1, CHUNK_W), blk(SUBLANES, LANES)],
        out_specs=blk(rows, CHUNK_W),
        out_shape=jax.ShapeDtypeStruct((g, rows, CHUNK_W), BF16),
        scratch_shapes=[pltpu.VMEM((gs, rows, LANES), F32)] * 3,
        compiler_params=_cparams(("parallel",), "ssm"),
        name="ssm",
    )(uc, fac, d_lane, ptab)


def _rms(x):
    return x * lax.rsqrt(jnp.mean(x * x, axis=-1, keepdims=True) + EPS)


def _gelu_tanh(x):
    k = -2.0 * math.sqrt(2.0 / math.pi) * math.log2(math.e)
    e = jnp.exp2(x * (k + (k * 0.044715) * (x * x)))
    return x * (1.0 / (1.0 + e))


def _mix_kernel(o1_ref, l1_ref, o4_ref, l4_ref, o16_ref, l16_ref, y_ref, x_ref, mod_ref,
                ga_ref, gs_ref, gp_ref, ex_ref, wg_ref, bg_ref, wo_ref, wside_ref, out_ref,
                wside_out_ref, so_ref, sl_ref, sy_ref, s4_ref, *, tm):
    wside_out_ref[...] = wside_ref[...].astype(BF16)

    quarter = tm // 4

    def put16(t4, r16, start, value):
        r4, a = r16 % 4, r16 // 4
        s4_ref[t4, pl.ds(r4 * quarter + 4 * start + a, value.shape[0], stride=4), :] = value

    def from_fours(t4, dst_ref, j):
        for r4 in range(4):
            dst_ref[j, pl.ds(r4, quarter, stride=4), :] = s4_ref[t4, r4 * quarter:(r4 + 1) * quarter, :]

    for r in range(4):
        rows = pl.ds(r, quarter, stride=4)
        for jj in range(GROUP_SLABS):
            so_ref[jj, rows, :] = o4_ref[r, :, _slab(jj)].astype(F32)
        sl_ref[0, rows, :] = l4_ref[r]
    for r in range(16):
        for jj in range(GROUP_SLABS):
            put16(jj, r, 0, o16_ref[r, :, _slab(jj)].astype(F32))
        put16(GROUP_SLABS, r, 0, l16_ref[r])
    for jj in range(GROUP_SLABS):
        from_fours(jj, so_ref, GROUP_SLABS + jj)
    from_fours(GROUP_SLABS, sl_ref, 1)

    ngb, nc = y_ref.shape[0] // SLOTS, tm // SSM_CHUNK
    vs = [jnp.concatenate([y_ref[gb * SLOTS + g, :, _slab(half)].astype(F32)
                           for gb in range(ngb) for half in range(2)], axis=0)
          for g in range(SLOTS)]
    for s, o in enumerate(_slot_transpose(vs)):
        for gb in range(ngb):
            for half in range(2):
                r0 = (gb * 2 + half) * nc
                put16(GROUP_SLABS + 1 + gb, half * SLOTS + s, 0, o[r0:r0 + nc])
    for gb in range(ngb):
        from_fours(GROUP_SLABS + 1 + gb, sy_ref, gb)

    lane = lax.broadcasted_iota(jnp.int32, (tm, LANES), 1)
    even = (lane & 1) == 0
    ms, ls = [], []
    for st in (l1_ref[...], sl_ref[0], sl_ref[1]):
        sw = pltpu.roll(st, HEAD_DIM, axis=1)
        ms.append(jnp.where(even, st, sw))
        ls.append(jnp.where(even, sw, st))
    mx = jnp.maximum(jnp.maximum(ms[0], ms[1]), ms[2])
    es = [jnp.exp2(m - mx) for m in ms]
    inv = 1.0 / (es[0] * ls[0] + es[1] * ls[1] + es[2] * ls[2])
    head_lane = lane < HEADS_PER_GROUP
    wexp = [jnp.dot(jnp.where(head_lane, e * inv, 0.0).astype(BF16), ex_ref[...],
                    preferred_element_type=F32) for e in es]
    att_slabs = []
    for jj in range(GROUP_SLABS):
        a = wexp[0][:, _slab(jj)] * o1_ref[:, _slab(jj)].astype(F32)
        a = a + wexp[1][:, _slab(jj)] * so_ref[jj]
        a = a + wexp[2][:, _slab(jj)] * so_ref[GROUP_SLABS + jj]
        att_slabs.append(a)
    att = _rms(jnp.concatenate(att_slabs, axis=1)) * ga_ref[...]

    yv = jnp.concatenate([sy_ref[gb] for gb in range(sy_ref.shape[0])], axis=1)
    s = _gelu_tanh(yv)
    z = jnp.dot(s.astype(BF16), wg_ref[...], preferred_element_type=F32) + bg_ref[...]
    s = s * jax.nn.sigmoid(z)
    ssm = _rms(s) * gs_ref[...]
    both = jnp.concatenate([att.astype(BF16), ssm.astype(BF16)], axis=1)
    mix = jnp.dot(both, wo_ref[...], preferred_element_type=F32)
    out_ref[...] = x_ref[...] + (mod_ref[0:1, :] * gp_ref[...]) * _rms(mix)


def _mix(o1, l1, o4, l4, o16, l16, yc, x2, mod, g_attn, g_ssm, g_post, w_glu_b, b_glu, w_out_b, w_side,
         bsz, seq):
    t_tok, d = x2.shape
    groups = yc.shape[0]
    sw = groups * SSM_P
    aw = GROUP_WIDTH
    tm = TOK_TILE
    per_b = seq // tm
    tok = lambda w: pl.BlockSpec((tm, w), lambda i: (i, 0))
    res = lambda dil, w: pl.BlockSpec((None, dil, tm // dil, w), lambda i: (i // per_b, 0, i % per_b, 0))
    head_of_lane = jnp.arange(aw)[None, :] // HEAD_DIM
    expand = (jnp.arange(LANES)[:, None] == head_of_lane).astype(BF16)
    side_spec, side_shape = _side_cast_spec(w_side, t_tok // tm)
    return pl.pallas_call(
        functools.partial(_mix_kernel, tm=tm),
        grid=(t_tok // tm,),
        in_specs=[tok(aw), tok(LANES), res(4, aw), res(4, LANES), res(16, aw), res(16, LANES),
                  pl.BlockSpec((groups, tm // SSM_CHUNK, CHUNK_W), lambda i: (0, i, 0)),
                  tok(d),
                  pl.BlockSpec((None, N_MOD - N_MOD_EARLY, d), lambda i: (i // per_b, 0, 0)),
                  _resident((1, aw)), _resident((1, sw)), _resident((1, d)),
                  _resident((LANES, aw)),
                  _resident((sw, sw)), _resident((1, sw)), _resident((aw + sw, d)),
                  side_spec],
        out_specs=[tok(d), side_spec],
        out_shape=[jax.ShapeDtypeStruct((t_tok, d), F32), side_shape],
        scratch_shapes=[pltpu.VMEM((2 * GROUP_SLABS, tm, LANES), F32),
                        pltpu.VMEM((2, tm, LANES), F32),
                        pltpu.VMEM((sw // LANES, tm, LANES), F32),
                        pltpu.VMEM((GROUP_SLABS + 1 + sw // LANES, tm, LANES), F32)],
        compiler_params=_cparams(("parallel",), "mix"),
        name="mix",
    )(o1, l1, o4, l4, o16, l16, yc, x2, mod, g_attn, g_ssm, g_post, expand, w_glu_b, b_glu, w_out_b, w_side)


def _mlp_kernel(x_ref, mod_ref, gpre_ref, gpost_ref, w1_ref, w2_ref, o_ref, h_ref):
    f = pl.program_id(1)
    last = pl.num_programs(1) - 1
    chunks = [slice(r, r + ROW_CHUNK) for r in range(0, x_ref.shape[0], ROW_CHUNK)]

    def partial():
        t = jnp.dot(h_ref[...], w1_ref[...], preferred_element_type=F32)
        t = jnp.square(jnp.maximum(t, 0.0)).astype(BF16)
        return jnp.dot(t, w2_ref[...], preferred_element_type=F32)

    @pl.when(f == 0)
    def _():
        scale = gpre_ref[...] * (1.0 + mod_ref[2:3, :])
        shift = mod_ref[1:2, :]
        for rows in chunks:
            h_ref[rows, :] = (_rms(x_ref[rows, :]) * scale + shift).astype(BF16)
        o_ref[...] = partial()

    @pl.when((f > 0) & (f < last))
    def _():
        o_ref[...] += partial()

    @pl.when(f == last)
    def _():
        gate = mod_ref[3:4, :] * gpost_ref[...]
        acc = o_ref[...] + partial()
        for rows in chunks:
            o_ref[rows, :] = x_ref[rows, :] + gate * _rms(acc[rows, :])


def _mlp(x2, mod, g_pre, g_post, w1_b, w2_b, seq):
    t_tok, d = x2.shape
    d_ff = w1_b.shape[1]
    tm, tf = 512, 2048
    assert d_ff // tf >= 2
    per_b = seq // tm
    return pl.pallas_call(
        _mlp_kernel,
        grid=(t_tok // tm, d_ff // tf),
        in_specs=[pl.BlockSpec((tm, d), lambda i, f: (i, 0)),
                  pl.BlockSpec((None, N_MOD - N_MOD_EARLY, d), lambda i, f: (i // per_b, 0, 0)),
                  _resident((1, d)), _resident((1, d)),
                  pl.BlockSpec((d, tf), lambda i, f: (0, f)),
                  pl.BlockSpec((tf, d), lambda i, f: (f, 0))],
        out_specs=pl.BlockSpec((tm, d), lambda i, f: (i, 0)),
        out_shape=jax.ShapeDtypeStruct((t_tok, d), F32),
        scratch_shapes=[pltpu.VMEM((tm, d), BF16)],
        compiler_params=_cparams(("parallel", "arbitrary"), "mlp"),
        name="mlp",
    )(x2, mod, g_pre, g_post, w1_b, w2_b)


def _rope_cos_sin(positions):
    freqs = ROPE_THETA ** (-jnp.arange(0, ROT_DIM, 2, dtype=F32) / ROT_DIM)
    ang = positions.reshape(-1).astype(F32)[None, :] * freqs[:, None]
    return jnp.concatenate([jnp.cos(ang), jnp.sin(ang)], axis=0)


def kernel(x, c, positions, w_ada, b_ada, g_pre_mix, g_post_mix, w_in, ssm_a_re, ssm_a_im, ssm_log_dt,
           ssm_b_re, ssm_b_im, ssm_c_re, ssm_c_im, ssm_d, w_glu, b_glu, g_attn_out, g_ssm_out, w_out,
           g_pre_mlp, g_post_mlp, w_mlp_in, w_mlp_out):
    bsz, seq, d = x.shape
    depth = w_ada.shape[0]
    x2 = x.reshape(bsz * seq, d)
    cos_sin = _rope_cos_sin(positions)
    row = lambda a: a.reshape(1, -1).astype(F32)
    for l in range(depth):
        s = jnp.zeros((SUBLANES, d), BF16).at[:bsz].set(jax.nn.silu(c).astype(BF16))
        b_ada_row = row(b_ada[l])
        mod = _adaln_early(s, w_ada[l], b_ada_row, bsz)
        (q1, q4, q16, k1, k4, k16, v1, v4, v16, uc, w_mlp_out_b), mod_late = _inproj(
            x2, mod, row(g_pre_mix[l]), w_in[l].astype(BF16), cos_sin, w_mlp_out[l],
            s, w_ada[l], b_ada_row, bsz, seq)

        nat = lambda a: a.reshape(bsz, 1, seq, a.shape[-1])
        o1, l1, w_out_b = _attn_group(nat(q1), nat(k1), nat(v1), 1, bsz, seq, w_out[l])
        o4, l4, w_glu_b = _attn_group(q4, k4, v4, 4, bsz, seq, w_glu[l])
        o16, l16 = _attn_group(q16, k16, v16, 16, bsz, seq)

        tables = _ssm_tables(ssm_a_re[l], ssm_a_im[l], ssm_log_dt[l], ssm_b_re[l], ssm_b_im[l],
                             ssm_c_re[l], ssm_c_im[l], ssm_d[l])
        yc = _ssm(uc, tables, bsz, seq)

        x2, w_mlp_in_b = _mix(o1.reshape(bsz * seq, -1), l1.reshape(bsz * seq, -1), o4, l4, o16, l16, yc,
                              x2, mod_late, row(g_attn_out[l]), row(g_ssm_out[l]), row(g_post_mix[l]),
                              w_glu_b, row(b_glu[l]), w_out_b, w_mlp_in[l],
                              bsz, seq)
        x2 = _mlp(x2, mod_late, row(g_pre_mlp[l]), row(g_post_mlp[l]), w_mlp_in_b, w_mlp_out_b, seq)
    return x2.reshape(bsz, seq, d)
```
